```python
import jax, jax.numpy as jnp
from jax import lax
import numpy as np

D_MODEL = 2048
BATCH = 8
SEQ = 8192
DEPTH = 4

GRID_W = 64
GLA_HEADS = 4
GLA_DK = D_MODEL // 4
GLA_DV = D_MODEL // 2
GLA_HK = GLA_DK // GLA_HEADS
GLA_HV = GLA_DV // GLA_HEADS
GLA_GATE_RANK = 16
GLA_TAU = 16.0
GLA_CHUNK = 64
NA_HD = 64
NA_WIDTH = D_MODEL // 4
NA_HEADS = NA_WIDTH // NA_HD
NA_ROWS = 8
NA_COLS = 16
NA_QB_W = 16
NA_KB_W = NA_QB_W + NA_COLS
MEM_TOKENS = 256
MEM_HEADS = 4
MEM_WIDTH = D_MODEL // 4
MEM_HD = MEM_WIDTH // MEM_HEADS
MIX_WIDTH = GLA_DV + NA_WIDTH + MEM_WIDTH
IN_SPLITS = (GLA_DK, GLA_DK, GLA_DV, GLA_DV, GLA_GATE_RANK, GLA_GATE_RANK,
             NA_WIDTH, NA_WIDTH, NA_WIDTH, MEM_WIDTH)
IN_COLS = sum(IN_SPLITS)
D_FF = ((8 * D_MODEL // 3 + 255) // 256) * 256
RMS_EPS = 1e-6

kernel_name = "hybrid_gla_natten_mem_encoder"


def rms_norm(x, g):
    xf = x.astype(jnp.float32)
    y = xf * lax.rsqrt(jnp.mean(xf * xf, axis=-1, keepdims=True) + RMS_EPS)
    return (y * g.astype(jnp.float32)).astype(x.dtype)


def head_rms_norm(t, g, n_heads):
    B, S, W = t.shape
    y = rms_norm(t.reshape(B, S, n_heads, W // n_heads), g.reshape(n_heads, W // n_heads))
    return y.reshape(B, S, W)


def split_heads(t, n_heads):
    B, S, W = t.shape
    return t.reshape(B, S, n_heads, W // n_heads).transpose(0, 2, 1, 3)


def merge_heads(t):
    B, H, S, d = t.shape
    return t.transpose(0, 2, 1, 3).reshape(B, S, H * d)


def gla_chunked(q, k, v, log_a, strict):
    B, H, S, dk = q.shape
    dv = v.shape[-1]
    C = GLA_CHUNK
    N = S // C
    qc = q.astype(jnp.float32).reshape(B, H, N, C, dk)
    kc = k.astype(jnp.float32).reshape(B, H, N, C, dk)
    vc = v.astype(jnp.float32).reshape(B, H, N, C, dv)
    cum = jnp.cumsum(log_a.astype(jnp.float32).reshape(B, H, N, C, dk), axis=3)
    last = cum[:, :, :, -1:, :]
    q_e = qc * jnp.exp(cum)
    k_e = kc * jnp.exp(-cum)
    k_end = kc * jnp.exp(last - cum)
    scores = jnp.einsum('bhnqd,bhnkd->bhnqk', q_e, k_e)
    mask = np.tril(np.ones((C, C), dtype=bool), k=-1 if strict else 0)
    scores = jnp.where(mask, scores, 0.0)
    o_intra = jnp.einsum('bhnqk,bhnke->bhnqe', scores, vc)
    kv = jnp.einsum('bhnkd,bhnke->nbhde', k_end, vc)
    decay = jnp.exp(last[:, :, :, 0, :]).transpose(2, 0, 1, 3)

    def step(state, inp):
        dec, kv_n = inp
        return dec[..., None] * state + kv_n, state

    _, s_prev = lax.scan(step, jnp.zeros((B, H, dk, dv), jnp.float32), (decay, kv))
    o_inter = jnp.einsum('bhnqd,nbhde->bhnqe', q_e, s_prev)
    return (o_intra + o_inter).reshape(B, H, S, dv)


def gla_group(q_in, k_in, v_in, r_in, gf_in, gb_in, wg2_f, bg_f, wg2_b, bg_b, out_norm):
    q = split_heads(q_in, GLA_HEADS) * (GLA_HK ** -0.5)
    k = split_heads(k_in, GLA_HEADS)
    v = split_heads(v_in, GLA_HEADS)
    z_f = gf_in.astype(jnp.float32) @ wg2_f.astype(jnp.float32) + bg_f.astype(jnp.float32)
    z_b = gb_in.astype(jnp.float32) @ wg2_b.astype(jnp.float32) + bg_b.astype(jnp.float32)
    la_f = split_heads(jax.nn.log_sigmoid(z_f) / GLA_TAU, GLA_HEADS)
    la_b = split_heads(jax.nn.log_sigmoid(z_b) / GLA_TAU, GLA_HEADS)
    o_f = gla_chunked(q, k, v, la_f, strict=False)
    flip = lambda t: jnp.flip(t, axis=2)
    o_b = flip(gla_chunked(flip(q), flip(k), flip(v), flip(la_b), strict=True))
    o = merge_heads(o_f + o_b).astype(q_in.dtype)
    o = head_rms_norm(o, out_norm, GLA_HEADS)
    return o * jax.nn.silu(r_in)


def natten2d(q, k, v, rpb):
    B, H, S, dh = q.shape
    rows = S // GRID_W
    kr = min(NA_ROWS, rows)
    nj = GRID_W // NA_QB_W
    r = np.arange(rows)
    rs = np.clip(r - kr // 2, 0, rows - kr)
    c0 = np.arange(nj) * NA_QB_W
    kc0 = np.clip(c0 - NA_COLS // 2, 0, GRID_W - NA_KB_W)
    key_rows = rs[:, None] + np.arange(kr)[None, :]
    key_cols = kc0[:, None] + np.arange(NA_KB_W)[None, :]
    nk = kr * NA_KB_W
    idx = (key_rows[:, None, :, None] * GRID_W + key_cols[None, :, None, :]).reshape(-1)
    k_blk = jnp.take(k, idx, axis=2).reshape(B, H, rows, nj, nk, dh)
    v_blk = jnp.take(v, idx, axis=2).reshape(B, H, rows, nj, nk, dh)
    q_blk = q.reshape(B, H, rows, nj, NA_QB_W, dh)
    qcol = c0[:, None] + np.arange(NA_QB_W)[None, :]
    cs = np.clip(qcol - NA_COLS // 2, 0, GRID_W - NA_COLS)
    in_win = (key_cols[:, None, :] >= cs[:, :, None]) & (key_cols[:, None, :] < cs[:, :, None] + NA_COLS)
    mask = np.broadcast_to(in_win[:, :, None, :], (nj, NA_QB_W, kr, NA_KB_W)).reshape(nj, NA_QB_W, nk)
    dr = key_rows - r[:, None] + (NA_ROWS - 1)
    dc = np.clip(key_cols[:, None, :] - qcol[:, :, None], -(NA_COLS - 1), NA_COLS - 1) + (NA_COLS - 1)
    bias = rpb[:, dr[:, None, None, :, None], dc[None, :, :, None, :]].reshape(H, rows, nj, NA_QB_W, nk)
    s = jnp.einsum('bhrjqd,bhrjkd->bhrjqk', q_blk, k_blk).astype(jnp.float32) * (dh ** -0.5)
    s = s + bias[None].astype(jnp.float32)
    s = jnp.where(mask, s, -jnp.inf)
    p = jax.nn.softmax(s, axis=-1).astype(v.dtype)
    o = jnp.einsum('bhrjqk,bhrjkd->bhrjqd', p, v_blk)
    return o.reshape(B, H, S, dh)


def natten_group(q_in, k_in, v_in, q_norm, k_norm, rpb, out_norm):
    q = rms_norm(split_heads(q_in, NA_HEADS), q_norm)
    k = rms_norm(split_heads(k_in, NA_HEADS), k_norm)
    v = split_heads(v_in, NA_HEADS)
    o = merge_heads(natten2d(q, k, v, rpb))
    return head_rms_norm(o, out_norm, NA_HEADS)


def memory_group(q_in, mem, mem_norm, mem_wkv, q_norm, k_norm, out_norm):
    q = rms_norm(split_heads(q_in, MEM_HEADS), q_norm)
    kv = rms_norm(mem, mem_norm) @ mem_wkv
    k_m, v_m = jnp.split(kv, 2, axis=-1)
    k_m = rms_norm(split_heads(k_m, MEM_HEADS), k_norm)
    v_m = split_heads(v_m, MEM_HEADS)
    s = jnp.einsum('bhqd,bhkd->bhqk', q, k_m).astype(jnp.float32) * (MEM_HD ** -0.5)
    p = jax.nn.softmax(s, axis=-1).astype(v_m.dtype)
    o = merge_heads(jnp.einsum('bhqk,bhkd->bhqd', p, v_m))
    return head_rms_norm(o, out_norm, MEM_HEADS)


def _fwd_setup_inputs(seed: int = 0) -> dict:
    key = jax.random.key(seed)
    ks = jax.random.split(key, 22)
    f32 = jnp.float32

    def nrm(k, shape, scale):
        return jax.random.normal(k, shape, f32) * scale

    def gain(k, shape):
        return 1.0 + 0.05 * jax.random.normal(k, shape, f32)

    out_scale = (2 * DEPTH) ** -0.5
    return {
        "x": nrm(ks[0], (BATCH, SEQ, D_MODEL), 1.0),
        "mem": nrm(ks[1], (BATCH, MEM_TOKENS, D_MODEL), 1.0),
        "attn_norm": gain(ks[2], (DEPTH, D_MODEL)),
        "w_in": nrm(ks[3], (DEPTH, D_MODEL, IN_COLS), D_MODEL ** -0.5),
        "gla_wg2_f": nrm(ks[4], (DEPTH, GLA_GATE_RANK, GLA_DK), GLA_GATE_RANK ** -0.5),
        "gla_bg_f": nrm(ks[5], (DEPTH, GLA_DK), 0.1),
        "gla_wg2_b": nrm(ks[6], (DEPTH, GLA_GATE_RANK, GLA_DK), GLA_GATE_RANK ** -0.5),
        "gla_bg_b": nrm(ks[7], (DEPTH, GLA_DK), 0.1),
        "gla_out_norm": gain(ks[8], (DEPTH, GLA_DV)),
        "na_q_norm": gain(ks[9], (DEPTH, NA_HD)),
        "na_k_norm": gain(ks[10], (DEPTH, NA_HD)),
        "na_rpb": nrm(ks[11], (DEPTH, NA_HEADS, 2 * NA_ROWS - 1, 2 * NA_COLS - 1), 0.1),
        "na_out_norm": gain(ks[12], (DEPTH, NA_WIDTH)),
        "mem_norm": gain(ks[13], (DEPTH, D_MODEL)),
        "mem_wkv": nrm(ks[14], (DEPTH, D_MODEL, 2 * MEM_WIDTH), D_MODEL ** -0.5),
        "mem_q_norm": gain(ks[15], (DEPTH, MEM_HD)),
        "mem_k_norm": gain(ks[16], (DEPTH, MEM_HD)),
        "mem_out_norm": gain(ks[17], (DEPTH, MEM_WIDTH)),
        "w_out": nrm(ks[18], (DEPTH, MIX_WIDTH, D_MODEL), MIX_WIDTH ** -0.5 * out_scale),
        "ffn_norm": gain(ks[19], (DEPTH, D_MODEL)),
        "ffn_w13": nrm(ks[20], (DEPTH, D_MODEL, 2 * D_FF), D_MODEL ** -0.5),
        "ffn_w2": nrm(ks[21], (DEPTH, D_FF, D_MODEL), D_FF ** -0.5 * out_scale),
    }


def _fwd_reference(x, mem, attn_norm, w_in, gla_wg2_f, gla_bg_f, gla_wg2_b, gla_bg_b, gla_out_norm,
              na_q_norm, na_k_norm, na_rpb, na_out_norm, mem_norm, mem_wkv, mem_q_norm, mem_k_norm,
              mem_out_norm, w_out, ffn_norm, ffn_w13, ffn_w2):
    split_points = np.cumsum(IN_SPLITS)[:-1].tolist()
    for l in range(DEPTH):
        xn = rms_norm(x, attn_norm[l])
        proj = xn @ w_in[l]
        (g_q, g_k, g_v, g_r, g_f, g_b, n_q, n_k, n_v, m_q) = jnp.split(proj, split_points, axis=-1)
        y_gla = gla_group(g_q, g_k, g_v, g_r, g_f, g_b, gla_wg2_f[l], gla_bg_f[l],
                          gla_wg2_b[l], gla_bg_b[l], gla_out_norm[l])
        y_na = natten_group(n_q, n_k, n_v, na_q_norm[l], na_k_norm[l], na_rpb[l], na_out_norm[l])
        y_mem = memory_group(m_q, mem, mem_norm[l], mem_wkv[l], mem_q_norm[l], mem_k_norm[l],
                             mem_out_norm[l])
        x = x + jnp.concatenate([y_gla, y_na, y_mem], axis=-1) @ w_out[l]
        h = rms_norm(x, ffn_norm[l])
        gate, up = jnp.split(h @ ffn_w13[l], 2, axis=-1)
        x = x + (jax.nn.silu(gate) * up) @ ffn_w2[l]
    return x


import jax as _jax
import jax.numpy as _jnp

TWIN_FORMAT = 'train_step'
FWD_PARAMS = ['x', 'mem', 'attn_norm', 'w_in', 'gla_wg2_f', 'gla_bg_f', 'gla_wg2_b', 'gla_bg_b', 'gla_out_norm', 'na_q_norm', 'na_k_norm', 'na_rpb', 'na_out_norm', 'mem_norm', 'mem_wkv', 'mem_q_norm', 'mem_k_norm', 'mem_out_norm', 'w_out', 'ffn_norm', 'ffn_w13', 'ffn_w2']
TWIN_WEIGHTS = ['attn_norm', 'w_in', 'gla_wg2_f', 'gla_bg_f', 'gla_wg2_b', 'gla_bg_b', 'gla_out_norm', 'na_q_norm', 'na_k_norm', 'na_rpb', 'na_out_norm', 'mem_norm', 'mem_wkv', 'mem_q_norm', 'mem_k_norm', 'mem_out_norm', 'w_out', 'ffn_norm', 'ffn_w13', 'ffn_w2']
TWIN_DIFF_INPUT = 'x'
TWIN_INPUTS = ['x', 'mem', 'attn_norm', 'w_in', 'gla_wg2_f', 'gla_bg_f', 'gla_wg2_b', 'gla_bg_b', 'gla_out_norm', 'na_q_norm', 'na_k_norm', 'na_rpb', 'na_out_norm', 'mem_norm', 'mem_wkv', 'mem_q_norm', 'mem_k_norm', 'mem_out_norm', 'w_out', 'ffn_norm', 'ffn_w13', 'ffn_w2', 'loss_target', 'm_attn_norm', 'm_w_in', 'm_gla_wg2_f', 'm_gla_bg_f', 'm_gla_wg2_b', 'm_gla_bg_b', 'm_gla_out_norm', 'm_na_q_norm', 'm_na_k_norm', 'm_na_rpb', 'm_na_out_norm', 'm_mem_norm', 'm_mem_wkv', 'm_mem_q_norm', 'm_mem_k_norm', 'm_mem_out_norm', 'm_w_out', 'm_ffn_norm', 'm_ffn_w13', 'm_ffn_w2', 'v_attn_norm', 'v_w_in', 'v_gla_wg2_f', 'v_gla_bg_f', 'v_gla_wg2_b', 'v_gla_bg_b', 'v_gla_out_norm', 'v_na_q_norm', 'v_na_k_norm', 'v_na_rpb', 'v_na_out_norm', 'v_mem_norm', 'v_mem_wkv', 'v_mem_q_norm', 'v_mem_k_norm', 'v_mem_out_norm', 'v_w_out', 'v_ffn_norm', 'v_ffn_w13', 'v_ffn_w2']
TWIN_OUTPUTS = ['loss', 'grad_x', 'grad_attn_norm', 'grad_w_in', 'grad_gla_wg2_f', 'grad_gla_bg_f', 'grad_gla_wg2_b', 'grad_gla_bg_b', 'grad_gla_out_norm', 'grad_na_q_norm', 'grad_na_k_norm', 'grad_na_rpb', 'grad_na_out_norm', 'grad_mem_norm', 'grad_mem_wkv', 'grad_mem_q_norm', 'grad_mem_k_norm', 'grad_mem_out_norm', 'grad_w_out', 'grad_ffn_norm', 'grad_ffn_w13', 'grad_ffn_w2', 'delta_attn_norm', 'delta_w_in', 'delta_gla_wg2_f', 'delta_gla_bg_f', 'delta_gla_wg2_b', 'delta_gla_bg_b', 'delta_gla_out_norm', 'delta_na_q_norm', 'delta_na_k_norm', 'delta_na_rpb', 'delta_na_out_norm', 'delta_mem_norm', 'delta_mem_wkv', 'delta_mem_q_norm', 'delta_mem_k_norm', 'delta_mem_out_norm', 'delta_w_out', 'delta_ffn_norm', 'delta_ffn_w13', 'delta_ffn_w2', 'new_m_attn_norm', 'new_m_w_in', 'new_m_gla_wg2_f', 'new_m_gla_bg_f', 'new_m_gla_wg2_b', 'new_m_gla_bg_b', 'new_m_gla_out_norm', 'new_m_na_q_norm', 'new_m_na_k_norm', 'new_m_na_rpb', 'new_m_na_out_norm', 'new_m_mem_norm', 'new_m_mem_wkv', 'new_m_mem_q_norm', 'new_m_mem_k_norm', 'new_m_mem_out_norm', 'new_m_w_out', 'new_m_ffn_norm', 'new_m_ffn_w13', 'new_m_ffn_w2', 'new_v_attn_norm', 'new_v_w_in', 'new_v_gla_wg2_f', 'new_v_gla_bg_f', 'new_v_gla_wg2_b', 'new_v_gla_bg_b', 'new_v_gla_out_norm', 'new_v_na_q_norm', 'new_v_na_k_norm', 'new_v_na_rpb', 'new_v_na_out_norm', 'new_v_mem_norm', 'new_v_mem_wkv', 'new_v_mem_q_norm', 'new_v_mem_k_norm', 'new_v_mem_out_norm', 'new_v_w_out', 'new_v_ffn_norm', 'new_v_ffn_w13', 'new_v_ffn_w2']
TWIN_LEAF_KINDS = {'loss': 'loss', 'grad_x': 'grad_x', 'grad_attn_norm': 'grad_w', 'grad_w_in': 'grad_w', 'grad_gla_wg2_f': 'grad_w', 'grad_gla_bg_f': 'grad_w', 'grad_gla_wg2_b': 'grad_w', 'grad_gla_bg_b': 'grad_w', 'grad_gla_out_norm': 'grad_w', 'grad_na_q_norm': 'grad_w', 'grad_na_k_norm': 'grad_w', 'grad_na_rpb': 'grad_w', 'grad_na_out_norm': 'grad_w', 'grad_mem_norm': 'grad_w', 'grad_mem_wkv': 'grad_w', 'grad_mem_q_norm': 'grad_w', 'grad_mem_k_norm': 'grad_w', 'grad_mem_out_norm': 'grad_w', 'grad_w_out': 'grad_w', 'grad_ffn_norm': 'grad_w', 'grad_ffn_w13': 'grad_w', 'grad_ffn_w2': 'grad_w', 'delta_attn_norm': 'delta_w', 'delta_w_in': 'delta_w', 'delta_gla_wg2_f': 'delta_w', 'delta_gla_bg_f': 'delta_w', 'delta_gla_wg2_b': 'delta_w', 'delta_gla_bg_b': 'delta_w', 'delta_gla_out_norm': 'delta_w', 'delta_na_q_norm': 'delta_w', 'delta_na_k_norm': 'delta_w', 'delta_na_rpb': 'delta_w', 'delta_na_out_norm': 'delta_w', 'delta_mem_norm': 'delta_w', 'delta_mem_wkv': 'delta_w', 'delta_mem_q_norm': 'delta_w', 'delta_mem_k_norm': 'delta_w', 'delta_mem_out_norm': 'delta_w', 'delta_w_out': 'delta_w', 'delta_ffn_norm': 'delta_w', 'delta_ffn_w13': 'delta_w', 'delta_ffn_w2': 'delta_w', 'new_m_attn_norm': 'new_m', 'new_m_w_in': 'new_m', 'new_m_gla_wg2_f': 'new_m', 'new_m_gla_bg_f': 'new_m', 'new_m_gla_wg2_b': 'new_m', 'new_m_gla_bg_b': 'new_m', 'new_m_gla_out_norm': 'new_m', 'new_m_na_q_norm': 'new_m', 'new_m_na_k_norm': 'new_m', 'new_m_na_rpb': 'new_m', 'new_m_na_out_norm': 'new_m', 'new_m_mem_norm': 'new_m', 'new_m_mem_wkv': 'new_m', 'new_m_mem_q_norm': 'new_m', 'new_m_mem_k_norm': 'new_m', 'new_m_mem_out_norm': 'new_m', 'new_m_w_out': 'new_m', 'new_m_ffn_norm': 'new_m', 'new_m_ffn_w13': 'new_m', 'new_m_ffn_w2': 'new_m', 'new_v_attn_norm': 'new_v', 'new_v_w_in': 'new_v', 'new_v_gla_wg2_f': 'new_v', 'new_v_gla_bg_f': 'new_v', 'new_v_gla_wg2_b': 'new_v', 'new_v_gla_bg_b': 'new_v', 'new_v_gla_out_norm': 'new_v', 'new_v_na_q_norm': 'new_v', 'new_v_na_k_norm': 'new_v', 'new_v_na_rpb': 'new_v', 'new_v_na_out_norm': 'new_v', 'new_v_mem_norm': 'new_v', 'new_v_mem_wkv': 'new_v', 'new_v_mem_q_norm': 'new_v', 'new_v_mem_k_norm': 'new_v', 'new_v_mem_out_norm': 'new_v', 'new_v_w_out': 'new_v', 'new_v_ffn_norm': 'new_v', 'new_v_ffn_w13': 'new_v', 'new_v_ffn_w2': 'new_v'}


def _forward(args):
    return _fwd_reference(*[args[k] for k in FWD_PARAMS])


def _output_shape():
    def fwd():
        inp = _fwd_setup_inputs(0)
        return _fwd_reference(*[inp[k] for k in FWD_PARAMS])
    out = _jax.eval_shape(fwd)
    return out.shape, out.dtype

N_MICROBATCH = 1
ADAM_LR = 0.001
ADAM_B1 = 0.9
ADAM_B2 = 0.999
ADAM_EPS = 1e-08
ADAM_WD = 0.01
ADAM_STEP = 10
PER_EXAMPLE_BATCH_AXIS = {'x': 0, 'mem': 0, 'loss_target': 0}
SHARED_INPUTS = []
_WEIGHT_DTYPES = {'attn_norm': _jnp.float32, 'w_in': _jnp.float32, 'gla_wg2_f': _jnp.float32, 'gla_bg_f': _jnp.float32, 'gla_wg2_b': _jnp.float32, 'gla_bg_b': _jnp.float32, 'gla_out_norm': _jnp.float32, 'na_q_norm': _jnp.float32, 'na_k_norm': _jnp.float32, 'na_rpb': _jnp.float32, 'na_out_norm': _jnp.float32, 'mem_norm': _jnp.float32, 'mem_wkv': _jnp.float32, 'mem_q_norm': _jnp.float32, 'mem_k_norm': _jnp.float32, 'mem_out_norm': _jnp.float32, 'w_out': _jnp.float32, 'ffn_norm': _jnp.float32, 'ffn_w13': _jnp.float32, 'ffn_w2': _jnp.float32}
MOMENT_SCALE = {'attn_norm': 8.800692e-01, 'w_in': 2.504843e-01, 'gla_wg2_f': 7.267757e-03, 'gla_bg_f': 2.426527e-02, 'gla_wg2_b': 6.949223e-03, 'gla_bg_b': 2.425794e-02, 'gla_out_norm': 1.417527e+00, 'na_q_norm': 2.032587e-01, 'na_k_norm': 2.035471e-01, 'na_rpb': 2.360049e-02, 'na_out_norm': 4.242240e+00, 'mem_norm': 5.565649e-01, 'mem_wkv': 7.894968e-01, 'mem_q_norm': 2.928087e-01, 'mem_k_norm': 2.885574e-01, 'mem_out_norm': 4.259116e+00, 'w_out': 2.126191e+00, 'ffn_norm': 3.092338e+00, 'ffn_w13': 4.304885e-02, 'ffn_w2': 2.180872e-01}


def _to_microbatches(a, axis):
    t = _jnp.moveaxis(a, axis, 0)
    t = t.reshape((N_MICROBATCH, t.shape[0] // N_MICROBATCH) + t.shape[1:])
    return _jnp.moveaxis(t, 1, axis + 1)


def setup_inputs(seed: int = 0) -> dict:
    inp = _fwd_setup_inputs(seed)
    key = _jax.random.fold_in(_jax.random.key(seed), 7919)
    shape, _ = _output_shape()
    out = dict(inp)
    out["loss_target"] = _jax.random.normal(_jax.random.fold_in(key, 0), shape, _jnp.float32)
    for i, name in enumerate(TWIN_WEIGHTS):
        w = inp[name].astype(_jnp.float32)
        if MOMENT_SCALE is None:
            s = _jnp.sqrt(_jnp.mean(_jnp.square(w)) + 1e-30)
        else:
            s = MOMENT_SCALE[name]
        km, kv = _jax.random.split(_jax.random.fold_in(key, i + 1))
        out[name] = w
        out["m_" + name] = s * _jax.random.normal(km, w.shape, _jnp.float32)
        out["v_" + name] = (s * s) * _jax.random.uniform(kv, w.shape, _jnp.float32, 0.5, 1.5)
    if N_MICROBATCH > 1:
        for name, axis in PER_EXAMPLE_BATCH_AXIS.items():
            out[name] = _to_microbatches(out[name], axis)
    return {'x': out['x'], 'mem': out['mem'], 'attn_norm': out['attn_norm'], 'w_in': out['w_in'], 'gla_wg2_f': out['gla_wg2_f'], 'gla_bg_f': out['gla_bg_f'], 'gla_wg2_b': out['gla_wg2_b'], 'gla_bg_b': out['gla_bg_b'], 'gla_out_norm': out['gla_out_norm'], 'na_q_norm': out['na_q_norm'], 'na_k_norm': out['na_k_norm'], 'na_rpb': out['na_rpb'], 'na_out_norm': out['na_out_norm'], 'mem_norm': out['mem_norm'], 'mem_wkv': out['mem_wkv'], 'mem_q_norm': out['mem_q_norm'], 'mem_k_norm': out['mem_k_norm'], 'mem_out_norm': out['mem_out_norm'], 'w_out': out['w_out'], 'ffn_norm': out['ffn_norm'], 'ffn_w13': out['ffn_w13'], 'ffn_w2': out['ffn_w2'], 'loss_target': out['loss_target'], 'm_attn_norm': out['m_attn_norm'], 'm_w_in': out['m_w_in'], 'm_gla_wg2_f': out['m_gla_wg2_f'], 'm_gla_bg_f': out['m_gla_bg_f'], 'm_gla_wg2_b': out['m_gla_wg2_b'], 'm_gla_bg_b': out['m_gla_bg_b'], 'm_gla_out_norm': out['m_gla_out_norm'], 'm_na_q_norm': out['m_na_q_norm'], 'm_na_k_norm': out['m_na_k_norm'], 'm_na_rpb': out['m_na_rpb'], 'm_na_out_norm': out['m_na_out_norm'], 'm_mem_norm': out['m_mem_norm'], 'm_mem_wkv': out['m_mem_wkv'], 'm_mem_q_norm': out['m_mem_q_norm'], 'm_mem_k_norm': out['m_mem_k_norm'], 'm_mem_out_norm': out['m_mem_out_norm'], 'm_w_out': out['m_w_out'], 'm_ffn_norm': out['m_ffn_norm'], 'm_ffn_w13': out['m_ffn_w13'], 'm_ffn_w2': out['m_ffn_w2'], 'v_attn_norm': out['v_attn_norm'], 'v_w_in': out['v_w_in'], 'v_gla_wg2_f': out['v_gla_wg2_f'], 'v_gla_bg_f': out['v_gla_bg_f'], 'v_gla_wg2_b': out['v_gla_wg2_b'], 'v_gla_bg_b': out['v_gla_bg_b'], 'v_gla_out_norm': out['v_gla_out_norm'], 'v_na_q_norm': out['v_na_q_norm'], 'v_na_k_norm': out['v_na_k_norm'], 'v_na_rpb': out['v_na_rpb'], 'v_na_out_norm': out['v_na_out_norm'], 'v_mem_norm': out['v_mem_norm'], 'v_mem_wkv': out['v_mem_wkv'], 'v_mem_q_norm': out['v_mem_q_norm'], 'v_mem_k_norm': out['v_mem_k_norm'], 'v_mem_out_norm': out['v_mem_out_norm'], 'v_w_out': out['v_w_out'], 'v_ffn_norm': out['v_ffn_norm'], 'v_ffn_w13': out['v_ffn_w13'], 'v_ffn_w2': out['v_ffn_w2']}


def _loss(weights, diff, rest, loss_target):
    with _jax.named_scope("forward"):
        args = {**rest, TWIN_DIFF_INPUT: diff, **{k: w.astype(_WEIGHT_DTYPES[k]) for k, w in weights.items()}}
        y = _forward(args)
    with _jax.named_scope("loss_head"):
        err = _jnp.square(y.astype(_jnp.float32) - loss_target)
        return 0.5 * _jnp.sum(_jnp.mean(err, axis=-1)) if err.ndim else 0.5 * err


def _adamw(w, g, m, v):
    m = ADAM_B1 * m + (1.0 - ADAM_B1) * g
    v = ADAM_B2 * v + (1.0 - ADAM_B2) * _jnp.square(g)
    m_hat = m / (1.0 - ADAM_B1 ** ADAM_STEP)
    v_hat = v / (1.0 - ADAM_B2 ** ADAM_STEP)
    delta = -ADAM_LR * (m_hat / (_jnp.sqrt(v_hat) + ADAM_EPS) + ADAM_WD * w)
    return delta, m, v


def reference(x, mem, attn_norm, w_in, gla_wg2_f, gla_bg_f, gla_wg2_b, gla_bg_b, gla_out_norm, na_q_norm, na_k_norm, na_rpb, na_out_norm, mem_norm, mem_wkv, mem_q_norm, mem_k_norm, mem_out_norm, w_out, ffn_norm, ffn_w13, ffn_w2, loss_target, m_attn_norm, m_w_in, m_gla_wg2_f, m_gla_bg_f, m_gla_wg2_b, m_gla_bg_b, m_gla_out_norm, m_na_q_norm, m_na_k_norm, m_na_rpb, m_na_out_norm, m_mem_norm, m_mem_wkv, m_mem_q_norm, m_mem_k_norm, m_mem_out_norm, m_w_out, m_ffn_norm, m_ffn_w13, m_ffn_w2, v_attn_norm, v_w_in, v_gla_wg2_f, v_gla_bg_f, v_gla_wg2_b, v_gla_bg_b, v_gla_out_norm, v_na_q_norm, v_na_k_norm, v_na_rpb, v_na_out_norm, v_mem_norm, v_mem_wkv, v_mem_q_norm, v_mem_k_norm, v_mem_out_norm, v_w_out, v_ffn_norm, v_ffn_w13, v_ffn_w2):
    given = dict(x=x, mem=mem, attn_norm=attn_norm, w_in=w_in, gla_wg2_f=gla_wg2_f, gla_bg_f=gla_bg_f, gla_wg2_b=gla_wg2_b, gla_bg_b=gla_bg_b, gla_out_norm=gla_out_norm, na_q_norm=na_q_norm, na_k_norm=na_k_norm, na_rpb=na_rpb, na_out_norm=na_out_norm, mem_norm=mem_norm, mem_wkv=mem_wkv, mem_q_norm=mem_q_norm, mem_k_norm=mem_k_norm, mem_out_norm=mem_out_norm, w_out=w_out, ffn_norm=ffn_norm, ffn_w13=ffn_w13, ffn_w2=ffn_w2, loss_target=loss_target, m_attn_norm=m_attn_norm, m_w_in=m_w_in, m_gla_wg2_f=m_gla_wg2_f, m_gla_bg_f=m_gla_bg_f, m_gla_wg2_b=m_gla_wg2_b, m_gla_bg_b=m_gla_bg_b, m_gla_out_norm=m_gla_out_norm, m_na_q_norm=m_na_q_norm, m_na_k_norm=m_na_k_norm, m_na_rpb=m_na_rpb, m_na_out_norm=m_na_out_norm, m_mem_norm=m_mem_norm, m_mem_wkv=m_mem_wkv, m_mem_q_norm=m_mem_q_norm, m_mem_k_norm=m_mem_k_norm, m_mem_out_norm=m_mem_out_norm, m_w_out=m_w_out, m_ffn_norm=m_ffn_norm, m_ffn_w13=m_ffn_w13, m_ffn_w2=m_ffn_w2, v_attn_norm=v_attn_norm, v_w_in=v_w_in, v_gla_wg2_f=v_gla_wg2_f, v_gla_bg_f=v_gla_bg_f, v_gla_wg2_b=v_gla_wg2_b, v_gla_bg_b=v_gla_bg_b, v_gla_out_norm=v_gla_out_norm, v_na_q_norm=v_na_q_norm, v_na_k_norm=v_na_k_norm, v_na_rpb=v_na_rpb, v_na_out_norm=v_na_out_norm, v_mem_norm=v_mem_norm, v_mem_wkv=v_mem_wkv, v_mem_q_norm=v_mem_q_norm, v_mem_k_norm=v_mem_k_norm, v_mem_out_norm=v_mem_out_norm, v_w_out=v_w_out, v_ffn_norm=v_ffn_norm, v_ffn_w13=v_ffn_w13, v_ffn_w2=v_ffn_w2)
    weights = {n: given[n] for n in TWIN_WEIGHTS}
    shared = {n: given[n] for n in SHARED_INPUTS}
    per_example = {n: given[n] for n in ['x', 'mem']}
    grad_fn = _jax.value_and_grad(_loss, argnums=(0, 1))

    def one_microbatch(ex, loss_target):
        ex = dict(ex)
        diff = ex.pop(TWIN_DIFF_INPUT)
        return grad_fn(weights, diff, {**shared, **ex}, loss_target)

    if N_MICROBATCH == 1:
        loss, (grad_w, grad_x) = one_microbatch(per_example, given["loss_target"])
    else:
        def body(carry, xs):
            loss_sum, grad_sum = carry
            l_k, (gw_k, gx_k) = one_microbatch(xs[0], xs[1])
            with _jax.named_scope("update"):
                return (loss_sum + l_k, _jax.tree.map(_jnp.add, grad_sum, gw_k)), gx_k

        init = (_jnp.zeros((), _jnp.float32), _jax.tree.map(_jnp.zeros_like, weights))
        (loss, grad_w), grad_x = _jax.lax.scan(body, init, (per_example, given["loss_target"]))
    with _jax.named_scope("update"):
        delta_w, new_m, new_v = {}, {}, {}
        for n in TWIN_WEIGHTS:
            delta_w[n], new_m[n], new_v[n] = _adamw(weights[n], grad_w[n], given["m_" + n], given["v_" + n])
    return (loss, grad_x, *[grad_w[n] for n in TWIN_WEIGHTS], *[delta_w[n] for n in TWIN_WEIGHTS],
            *[new_m[n] for n in TWIN_WEIGHTS], *[new_v[n] for n in TWIN_WEIGHTS])
```

```python
import functools

import numpy as np
import jax
import jax.numpy as jnp
from jax import lax
from jax.experimental import pallas as pl
from jax.experimental.pallas import tpu as pltpu

F32, BF16 = jnp.float32, jnp.bfloat16
HI = lax.Precision.HIGHEST
SDS = jax.ShapeDtypeStruct
MESH = pl.DeviceIdType.MESH

D_MODEL = 2048
GRID_W = 64
GLA_H, GLA_HK, GLA_HV, GLA_RANK, GLA_TAU, GLA_C = 4, 128, 256, 16, 16.0, 64
GLA_DK, GLA_DV = GLA_H * GLA_HK, GLA_H * GLA_HV
NA_H, NA_HD, NA_ROWS, NA_COLS = 8, 64, 8, 16
NA_W = NA_H * NA_HD
MEM_H, MEM_HD, MEM_TOK = 4, 128, 256
MEM_W = MEM_H * MEM_HD
D_FF = 5632
IN_COLS = 5152
RMS_EPS = 1e-6
ADAM_LR, ADAM_B1, ADAM_B2, ADAM_EPS, ADAM_WD, ADAM_STEP = 0.001, 0.9, 0.999, 1e-08, 0.01, 10
N_DEV = 8

PC = 5376
COL_R, COL_NQ, COL_NK, COL_NV, COL_MQ, COL_TAIL = 2048, 3072, 3584, 4096, 4608, 5120
ORIG_GATE0 = 3072
ORIG_AFTER_GATE = 3104

GLA_G = 8
VMEM_LIMIT = 56 * 1024 * 1024
NEG = -1e30


def _cparams(sem):
    return pltpu.CompilerParams(dimension_semantics=sem, vmem_limit_bytes=VMEM_LIMIT)


def _fit(n, pref, unit=128):
    if n <= pref:
        return n
    t = (pref // unit) * unit
    while t >= unit:
        if n % t == 0:
            return t
        t -= unit
    return n


def _bdot(a, b, dims):
    return lax.dot_general(a.astype(BF16), b.astype(BF16), (dims, ((), ())), preferred_element_type=F32)


NN, NT, TN = ((1,), (0,)), ((1,), (1,)), ((0,), (0,))


def matmul(a, b, mode, name, out_dtype=F32, res=None, tm=1024, tn=1024, tk=512, exact=False):
    if mode == "nn":
        (M, K), (_, N) = a.shape, b.shape
    elif mode == "nt":
        (M, K), (N, _) = a.shape, b.shape
    else:
        (K, M), (_, N) = a.shape, b.shape
    tm, tn, tk = _fit(M, tm, 8 if mode != "tn" else 128), _fit(N, tn), _fit(K, tk, 128 if mode != "tn" else 16)
    nk = K // tk
    dims = {"nn": NN, "nt": NT, "tn": TN}[mode]

    def body(*refs):
        if res is None:
            a_ref, b_ref, o_ref = refs[:3]
            r_ref = None
            acc = refs[3] if nk > 1 else None
        else:
            a_ref, b_ref, r_ref, o_ref = refs[:4]
            acc = refs[4] if nk > 1 else None
        if exact:
            part = lax.dot_general(a_ref[...], b_ref[...], (dims, ((), ())), preferred_element_type=F32, precision=HI)
        else:
            part = _bdot(a_ref[...], b_ref[...], dims)

        def finish(val):
            if r_ref is not None:
                val = val + r_ref[...]
            o_ref[...] = val.astype(out_dtype)

        if nk == 1:
            finish(part)
        else:
            kk = pl.program_id(2)

            @pl.when(kk == 0)
            def _():
                acc[...] = part

            @pl.when(kk > 0)
            def _():
                acc[...] += part

            @pl.when(kk == nk - 1)
            def _():
                finish(acc[...])

    a_spec = pl.BlockSpec((tk, tm), lambda i, j, k: (k, i)) if mode == "tn" else pl.BlockSpec((tm, tk), lambda i, j, k: (i, k))
    b_spec = pl.BlockSpec((tn, tk), lambda i, j, k: (j, k)) if mode == "nt" else pl.BlockSpec((tk, tn), lambda i, j, k: (k, j))
    o_spec = pl.BlockSpec((tm, tn), lambda i, j, k: (i, j))
    in_specs, args = [a_spec, b_spec], [a, b]
    if res is not None:
        in_specs.append(o_spec)
        args.append(res)
    return pl.pallas_call(
        body, grid=(M // tm, N // tn, nk), in_specs=in_specs, out_specs=o_spec, out_shape=SDS((M, N), out_dtype),
        scratch_shapes=[pltpu.VMEM((tm, tn), F32)] if nk > 1 else [],
        compiler_params=_cparams(("parallel", "parallel", "arbitrary")), name=name)(*args)


def rms_fwd(x, g, name):
    T, D = x.shape
    tm = _fit(T, 512, 8)

    def body(x_ref, g_ref, o_ref):
        xv = x_ref[...]
        r = lax.rsqrt(jnp.mean(xv * xv, axis=-1, keepdims=True) + RMS_EPS)
        o_ref[...] = (xv * r * g_ref[...]).astype(BF16)

    return pl.pallas_call(
        body, grid=(T // tm,), in_specs=[pl.BlockSpec((tm, D), lambda i: (i, 0)), pl.BlockSpec((1, D), lambda i: (0, 0))],
        out_specs=pl.BlockSpec((tm, D), lambda i: (i, 0)), out_shape=SDS((T, D), BF16),
        compiler_params=_cparams(("parallel",)), name=name)(x, g)


def rms_bwd(x, g, dy, dres, name):
    T, D = x.shape
    tm = _fit(T, 256, 8)
    has_res = dres is not None

    def body(*refs):
        if has_res:
            x_ref, g_ref, dy_ref, dres_ref, dx_ref, dxb_ref, dg_ref = refs
        else:
            x_ref, g_ref, dy_ref, dx_ref, dxb_ref, dg_ref = refs
        xv, dyv = x_ref[...], dy_ref[...]
        r = lax.rsqrt(jnp.mean(xv * xv, axis=-1, keepdims=True) + RMS_EPS)
        xh = xv * r
        dxh = dyv * g_ref[...]
        dx = r * (dxh - xh * jnp.mean(dxh * xh, axis=-1, keepdims=True))
        if has_res:
            dx = dx + dres_ref[...]
        dx_ref[...] = dx
        dxb_ref[...] = dx.astype(BF16)

        @pl.when(pl.program_id(0) == 0)
        def _():
            dg_ref[...] = jnp.zeros_like(dg_ref)

        dg_ref[...] += jnp.sum(dyv * xh, axis=0, keepdims=True)

    row = pl.BlockSpec((tm, D), lambda i: (i, 0))
    vec = pl.BlockSpec((1, D), lambda i: (0, 0))
    args = [x, g, dy] + ([dres] if has_res else [])
    return pl.pallas_call(
        body, grid=(T // tm,), in_specs=[row, vec, row] + ([row] if has_res else []),
        out_specs=[row, row, vec], out_shape=[SDS((T, D), F32), SDS((T, D), BF16), SDS((1, D), F32)],
        compiler_params=_cparams(("arbitrary",)), name=name)(*args)


def swiglu_fwd(gu, name):
    T, F2 = gu.shape
    Fh = F2 // 2
    tm, tf = _fit(T, 512, 8), _fit(Fh, 512)
    nf = Fh // tf

    def body(g_ref, u_ref, o_ref):
        gv = g_ref[...]
        o_ref[...] = (gv * jax.nn.sigmoid(gv) * u_ref[...]).astype(BF16)

    return pl.pallas_call(
        body, grid=(T // tm, nf),
        in_specs=[pl.BlockSpec((tm, tf), lambda i, j: (i, j)), pl.BlockSpec((tm, tf), lambda i, j: (i, j + nf))],
        out_specs=pl.BlockSpec((tm, tf), lambda i, j: (i, j)), out_shape=SDS((T, Fh), BF16),
        compiler_params=_cparams(("parallel", "parallel")), name=name)(gu, gu)


def swiglu_bwd(gu, da, name):
    T, F2 = gu.shape
    Fh = F2 // 2
    tm, tf = _fit(T, 512, 8), _fit(Fh, 512)
    nf = Fh // tf

    def body(g_ref, u_ref, da_ref, o_ref):
        gv, uv, dav = g_ref[...], u_ref[...], da_ref[...]
        sg = jax.nn.sigmoid(gv)
        d_gate = dav * uv * (sg * (1.0 + gv * (1.0 - sg)))
        d_up = dav * gv * sg
        o_ref[...] = jnp.where(pl.program_id(1) < nf, d_gate, d_up).astype(BF16)

    return pl.pallas_call(
        body, grid=(T // tm, 2 * nf),
        in_specs=[pl.BlockSpec((tm, tf), lambda i, j: (i, j % nf)), pl.BlockSpec((tm, tf), lambda i, j: (i, j % nf + nf)),
                  pl.BlockSpec((tm, tf), lambda i, j: (i, j % nf))],
        out_specs=pl.BlockSpec((tm, tf), lambda i, j: (i, j)), out_shape=SDS((T, F2), BF16),
        compiler_params=_cparams(("parallel", "parallel")), name=name)(gu, gu, da)


def loss_bwd(y, tgt, name):
    T, D = y.shape
    tm = _fit(T, 512, 8)

    def body(y_ref, t_ref, dy_ref, dyb_ref, l_ref):
        e = y_ref[...] - t_ref[...]
        dy = e * (1.0 / D)
        dy_ref[...] = dy
        dyb_ref[...] = dy.astype(BF16)

        @pl.when(pl.program_id(0) == 0)
        def _():
            l_ref[...] = jnp.zeros_like(l_ref)

        l_ref[...] += jnp.sum(jnp.sum(e * e, axis=1, keepdims=True), axis=0, keepdims=True)

    row = pl.BlockSpec((tm, D), lambda i: (i, 0))
    return pl.pallas_call(
        body, grid=(T // tm,), in_specs=[row, row], out_specs=[row, row, pl.BlockSpec((8, 128), lambda i: (0, 0))],
        out_shape=[SDS((T, D), F32), SDS((T, D), BF16), SDS((8, 128), F32)],
        compiler_params=_cparams(("arbitrary",)), name=name)(y, tgt)


def _log_sigmoid(z):
    return jnp.minimum(z, 0.0) - jnp.log(1.0 + jnp.exp(-jnp.abs(z)))


def _gla_chunk(q, k, v, tail, wg, bg, s_prev, cmat, mask):
    z = jnp.dot(tail, wg, preferred_element_type=F32, precision=HI) + bg
    la = _log_sigmoid(z) * (1.0 / GLA_TAU)
    cum = jnp.dot(cmat, la, preferred_element_type=F32, precision=HI)
    last = jnp.sum(la, axis=0, keepdims=True)
    q_e = q * jnp.exp(cum) * (GLA_HK ** -0.5)
    k_e = k * jnp.exp(-cum)
    k_end = k * jnp.exp(last - cum)
    sc = jnp.where(mask > 0.5, _bdot(q_e, k_e, NT), 0.0)
    o = _bdot(sc, v, NN) + _bdot(q_e, s_prev, NN)
    kv = _bdot(k_end, v, TN)
    s_new = s_prev * jnp.transpose(jnp.exp(last)) + kv
    return o, s_new


def _gla_group(q, k, v, tail, wg, bg, s, cmat, mask, backward_dir):
    n = q.shape[0] // GLA_C
    outs = [None] * n
    for g in (range(n - 1, -1, -1) if backward_dir else range(n)):
        sl = slice(g * GLA_C, (g + 1) * GLA_C)
        outs[g], s = _gla_chunk(q[sl], k[sl], v[sl], tail[sl], wg, bg, s, cmat, mask)
    return jnp.concatenate(outs, axis=0), s


def _gla_consts(backward_dir):
    i = np.arange(GLA_C)
    if backward_dir:
        return (i[None, :] >= i[:, None]).astype(np.float32), (i[None, :] > i[:, None]).astype(np.float32)
    return (i[None, :] <= i[:, None]).astype(np.float32), (i[None, :] <= i[:, None]).astype(np.float32)


def _gla_in_specs(GC, nmap):
    return [
        pl.BlockSpec((GC, GLA_HK), lambda h, n: (nmap(n), h)),
        pl.BlockSpec((GC, GLA_HK), lambda h, n: (nmap(n), GLA_H + h)),
        pl.BlockSpec((GC, GLA_HV), lambda h, n: (nmap(n), GLA_DK * 2 // GLA_HV + h)),
        pl.BlockSpec((GC, 128), lambda h, n: (nmap(n), COL_TAIL // 128)),
        pl.BlockSpec((None, 128, 128), lambda h, n: (h, 0, 0)),
        pl.BlockSpec((None, 1, 128), lambda h, n: (h, 0, 0)),
        pl.BlockSpec((GLA_C, GLA_C), lambda h, n: (0, 0)),
        pl.BlockSpec((GLA_C, GLA_C), lambda h, n: (0, 0)),
    ]


def gla_fwd(proj, wgpad, bg, backward_dir, name):
    T = proj.shape[0]
    GC = min(GLA_G * GLA_C, T)
    NG = T // GC
    cmat, mask = _gla_consts(backward_dir)
    nmap = (lambda n: NG - 1 - n) if backward_dir else (lambda n: n)

    def body(q_ref, k_ref, v_ref, t_ref, wg_ref, bg_ref, c_ref, m_ref, o_ref, ss_ref, s_scr):
        @pl.when(pl.program_id(1) == 0)
        def _():
            s_scr[...] = jnp.zeros_like(s_scr)

        s0 = s_scr[...]
        ss_ref[...] = s0
        o, s1 = _gla_group(q_ref[...], k_ref[...], v_ref[...], t_ref[...], wg_ref[...], bg_ref[...], s0,
                           c_ref[...], m_ref[...], backward_dir)
        o_ref[...] = o
        s_scr[...] = s1

    return pl.pallas_call(
        body, grid=(GLA_H, NG), in_specs=_gla_in_specs(GC, nmap),
        out_specs=[pl.BlockSpec((GC, GLA_HV), lambda h, n: (nmap(n), h)),
                   pl.BlockSpec((None, None, GLA_HK, GLA_HV), lambda h, n: (h, nmap(n), 0, 0))],
        out_shape=[SDS((T, GLA_DV), F32), SDS((GLA_H, NG, GLA_HK, GLA_HV), F32)],
        scratch_shapes=[pltpu.VMEM((GLA_HK, GLA_HV), F32)],
        compiler_params=_cparams(("parallel", "arbitrary")), name=name)(proj, proj, proj, proj, wgpad, bg, cmat, mask)


def gla_bwd(proj, wgpad, bg, ssave, do, backward_dir, name):
    T = proj.shape[0]
    GC = min(GLA_G * GLA_C, T)
    NG = T // GC
    cmat, mask = _gla_consts(backward_dir)
    nmap = (lambda n: n) if backward_dir else (lambda n: NG - 1 - n)

    def body(q_ref, k_ref, v_ref, t_ref, wg_ref, bg_ref, c_ref, m_ref, ss_ref, do_ref,
             dq_ref, dk_ref, dv_ref, dt_ref, dwg_ref, dbg_ref, ds_scr):
        first = pl.program_id(1) == 0

        @pl.when(first)
        def _():
            ds_scr[...] = jnp.zeros_like(ds_scr)
            dwg_ref[...] = jnp.zeros_like(dwg_ref)
            dbg_ref[...] = jnp.zeros_like(dbg_ref)

        cm, mk = c_ref[...], m_ref[...]
        fn = lambda q, k, v, t, wg, b, s: _gla_group(q, k, v, t, wg, b, s, cm, mk, backward_dir)
        _, vjp = jax.vjp(fn, q_ref[...], k_ref[...], v_ref[...], t_ref[...], wg_ref[...], bg_ref[...], ss_ref[...])
        dq, dk, dv, dt, dwg, dbg, ds = vjp((do_ref[...], ds_scr[...]))
        dq_ref[...] = dq
        dk_ref[...] = dk
        dv_ref[...] = dv
        dt_ref[...] = dt
        dwg_ref[...] += dwg
        dbg_ref[...] += dbg
        ds_scr[...] = ds

    in_specs = _gla_in_specs(GC, nmap) + [
        pl.BlockSpec((None, None, GLA_HK, GLA_HV), lambda h, n: (h, nmap(n), 0, 0)),
        pl.BlockSpec((GC, GLA_HV), lambda h, n: (nmap(n), h)),
    ]
    out_specs = [
        pl.BlockSpec((GC, GLA_HK), lambda h, n: (nmap(n), h)),
        pl.BlockSpec((GC, GLA_HK), lambda h, n: (nmap(n), h)),
        pl.BlockSpec((GC, GLA_HV), lambda h, n: (nmap(n), h)),
        pl.BlockSpec((None, GC, 128), lambda h, n: (h, nmap(n), 0)),
        pl.BlockSpec((None, 128, 128), lambda h, n: (h, 0, 0)),
        pl.BlockSpec((None, 1, 128), lambda h, n: (h, 0, 0)),
    ]
    out_shape = [SDS((T, GLA_DK), F32), SDS((T, GLA_DK), F32), SDS((T, GLA_DV), F32), SDS((GLA_H, T, 128), F32),
                 SDS((GLA_H, 128, 128), F32), SDS((GLA_H, 1, 128), F32)]
    return pl.pallas_call(
        body, grid=(GLA_H, NG), in_specs=in_specs, out_specs=out_specs, out_shape=out_shape,
        scratch_shapes=[pltpu.VMEM((GLA_HK, GLA_HV), F32)],
        compiler_params=_cparams(("parallel", "arbitrary")), name=name)(proj, proj, proj, proj, wgpad, bg, cmat, mask, ssave, do)


def _block_diag(width, hd):
    i = np.arange(width)
    return ((i[:, None] // hd) == (i[None, :] // hd)).astype(np.float32) / hd


def _norm_heads(t, hd):
    outs = []
    for h in range(t.shape[1] // hd):
        th = t[:, h * hd:(h + 1) * hd]
        outs.append(th * lax.rsqrt(jnp.mean(th * th, axis=-1, keepdims=True) + RMS_EPS))
    return jnp.concatenate(outs, axis=1)


def _norm_bd(t, bd):
    return t * lax.rsqrt(jnp.dot(t * t, bd, preferred_element_type=F32, precision=HI) + RMS_EPS)


def _pre_fn(nq, nk, mq, gq, gk, gm, bd):
    return _norm_bd(nq, bd) * gq, _norm_bd(nk, bd) * gk, _norm_heads(mq, MEM_HD) * gm


def pre_fwd(proj, gq, gk, gm, name):
    T = proj.shape[0]
    tm = _fit(T, 512, 8)
    bd = _block_diag(NA_W, NA_HD)

    def body(nq_ref, nk_ref, nv_ref, mq_ref, gq_ref, gk_ref, gm_ref, bd_ref, q_ref, k_ref, v_ref, m_ref):
        qn, kn, mn = _pre_fn(nq_ref[...], nk_ref[...], mq_ref[...], gq_ref[...], gk_ref[...], gm_ref[...], bd_ref[...])
        q_ref[...] = qn.astype(BF16)
        k_ref[...] = kn.astype(BF16)
        v_ref[...] = nv_ref[...].astype(BF16)
        m_ref[...] = mn.astype(BF16)

    col = lambda c0: pl.BlockSpec((tm, 512), lambda i: (i, c0 // 512))
    vec = pl.BlockSpec((1, 512), lambda i: (0, 0))
    row = pl.BlockSpec((tm, 512), lambda i: (i, 0))
    return pl.pallas_call(
        body, grid=(T // tm,),
        in_specs=[col(COL_NQ), col(COL_NK), col(COL_NV), col(COL_MQ), vec, vec, vec, pl.BlockSpec((NA_W, NA_W), lambda i: (0, 0))],
        out_specs=[row] * 4, out_shape=[SDS((T, 512), BF16)] * 4,
        compiler_params=_cparams(("parallel",)), name=name)(proj, proj, proj, proj, gq, gk, gm, bd)


def pre_bwd(proj, gq, gk, gm, d_qn, d_kn, d_nv, d_mn, dq_f, dq_b, dk_f, dk_b, dv_f, dv_b, d_r, dt_f, dt_b, name):
    T = proj.shape[0]
    tm = _fit(T, 256, 8)
    bd = _block_diag(NA_W, NA_HD)

    def body(nq_ref, nk_ref, mq_ref, gq_ref, gk_ref, gm_ref, bd_ref, dqn_ref, dkn_ref, dnv_ref, dmn_ref,
             dqf_ref, dqb_ref, dkf_ref, dkb_ref, dvf_ref, dvb_ref, dr_ref, dtf_ref, dtb_ref,
             o_ref, dgq_ref, dgk_ref, dgm_ref):
        bdv = bd_ref[...]
        fn = lambda a, b, c, x, y, z: _pre_fn(a, b, c, x, y, z, bdv)
        _, vjp = jax.vjp(fn, nq_ref[...], nk_ref[...], mq_ref[...], gq_ref[...], gk_ref[...], gm_ref[...])
        d_nq, d_nk, d_mq, dgq, dgk, dgm = vjp((dqn_ref[...], dkn_ref[...], dmn_ref[...]))
        o_ref[:, 0:512] = (dqf_ref[...] + dqb_ref[...]).astype(BF16)
        o_ref[:, 512:1024] = (dkf_ref[...] + dkb_ref[...]).astype(BF16)
        o_ref[:, 1024:2048] = (dvf_ref[...] + dvb_ref[...]).astype(BF16)
        o_ref[:, COL_R:COL_R + 1024] = dr_ref[...].astype(BF16)
        o_ref[:, COL_NQ:COL_NQ + 512] = d_nq.astype(BF16)
        o_ref[:, COL_NK:COL_NK + 512] = d_nk.astype(BF16)
        o_ref[:, COL_NV:COL_NV + 512] = dnv_ref[...].astype(BF16)
        o_ref[:, COL_MQ:COL_MQ + 512] = d_mq.astype(BF16)
        dt = dtf_ref[0] + dtb_ref[0]
        for h in range(1, GLA_H):
            dt = dt + dtf_ref[h] + dtb_ref[h]
        o_ref[:, COL_TAIL:COL_TAIL + 128] = dt.astype(BF16)
        o_ref[:, COL_TAIL + 128:PC] = jnp.zeros((tm, PC - COL_TAIL - 128), BF16)

        @pl.when(pl.program_id(0) == 0)
        def _():
            dgq_ref[...] = jnp.zeros_like(dgq_ref)
            dgk_ref[...] = jnp.zeros_like(dgk_ref)
            dgm_ref[...] = jnp.zeros_like(dgm_ref)

        dgq_ref[...] += dgq
        dgk_ref[...] += dgk
        dgm_ref[...] += dgm

    col = lambda c0: pl.BlockSpec((tm, 512), lambda i: (i, c0 // 512))
    vec = pl.BlockSpec((1, 512), lambda i: (0, 0))
    r512 = pl.BlockSpec((tm, 512), lambda i: (i, 0))
    r1024 = pl.BlockSpec((tm, 1024), lambda i: (i, 0))
    tl = pl.BlockSpec((GLA_H, tm, 128), lambda i: (0, i, 0))
    in_specs = [col(COL_NQ), col(COL_NK), col(COL_MQ), vec, vec, vec, pl.BlockSpec((NA_W, NA_W), lambda i: (0, 0)),
                r512, r512, r512, r512, r512, r512, r512, r512, r1024, r1024, r1024, tl, tl]
    return pl.pallas_call(
        body, grid=(T // tm,), in_specs=in_specs,
        out_specs=[pl.BlockSpec((tm, PC), lambda i: (i, 0)), vec, vec, vec],
        out_shape=[SDS((T, PC), BF16), SDS((1, 512), F32), SDS((1, 512), F32), SDS((1, 512), F32)],
        compiler_params=_cparams(("arbitrary",)), name=name)(
            proj, proj, proj, gq, gk, gm, bd, d_qn, d_kn, d_nv, d_mn, dq_f, dq_b, dk_f, dk_b, dv_f, dv_b, d_r, dt_f, dt_b)


def _post_fn(o_f, o_b, r, o_na, o_mem, g_gla, g_na, g_mem, bd):
    y_gla = _norm_heads(o_f + o_b, GLA_HV) * g_gla * (r * jax.nn.sigmoid(r))
    y_na = _norm_bd(o_na, bd) * g_na
    y_mem = _norm_heads(o_mem, MEM_HD) * g_mem
    return jnp.concatenate([y_gla, y_na, y_mem], axis=1)


def post_fwd(o_f, o_b, proj, o_na, o_mem, g_gla, g_na, g_mem, name):
    T = proj.shape[0]
    tm = _fit(T, 256, 8)
    bd = _block_diag(NA_W, NA_HD)

    def body(of_ref, ob_ref, r_ref, ona_ref, omem_ref, gg_ref, gn_ref, gm_ref, bd_ref, y_ref):
        y_ref[...] = _post_fn(of_ref[...], ob_ref[...], r_ref[...], ona_ref[...], omem_ref[...],
                              gg_ref[...], gn_ref[...], gm_ref[...], bd_ref[...]).astype(BF16)

    r1024 = pl.BlockSpec((tm, 1024), lambda i: (i, 0))
    r512 = pl.BlockSpec((tm, 512), lambda i: (i, 0))
    in_specs = [r1024, r1024, pl.BlockSpec((tm, 1024), lambda i: (i, COL_R // 1024)), r512, r512,
                pl.BlockSpec((1, 1024), lambda i: (0, 0)), pl.BlockSpec((1, 512), lambda i: (0, 0)),
                pl.BlockSpec((1, 512), lambda i: (0, 0)), pl.BlockSpec((NA_W, NA_W), lambda i: (0, 0))]
    return pl.pallas_call(
        body, grid=(T // tm,), in_specs=in_specs, out_specs=pl.BlockSpec((tm, D_MODEL), lambda i: (i, 0)),
        out_shape=SDS((T, D_MODEL), BF16), compiler_params=_cparams(("parallel",)), name=name)(
            o_f, o_b, proj, o_na, o_mem, g_gla, g_na, g_mem, bd)


def post_bwd(o_f, o_b, proj, o_na, o_mem, g_gla, g_na, g_mem, dy, name):
    T = proj.shape[0]
    tm = _fit(T, 256, 8)
    bd = _block_diag(NA_W, NA_HD)

    def body(of_ref, ob_ref, r_ref, ona_ref, omem_ref, gg_ref, gn_ref, gm_ref, bd_ref, dy_ref,
             do_ref, dr_ref, dna_ref, dmem_ref, dgg_ref, dgn_ref, dgm_ref):
        bdv = bd_ref[...]
        fn = lambda o, r, a, m, x, y, z: _post_fn(o, 0.0, r, a, m, x, y, z, bdv)
        _, vjp = jax.vjp(fn, of_ref[...] + ob_ref[...], r_ref[...], ona_ref[...], omem_ref[...],
                         gg_ref[...], gn_ref[...], gm_ref[...])
        d_o, d_r, d_na, d_mem, dgg, dgn, dgm = vjp(dy_ref[...])
        do_ref[...] = d_o
        dr_ref[...] = d_r
        dna_ref[...] = d_na
        dmem_ref[...] = d_mem

        @pl.when(pl.program_id(0) == 0)
        def _():
            dgg_ref[...] = jnp.zeros_like(dgg_ref)
            dgn_ref[...] = jnp.zeros_like(dgn_ref)
            dgm_ref[...] = jnp.zeros_like(dgm_ref)

        dgg_ref[...] += dgg
        dgn_ref[...] += dgn
        dgm_ref[...] += dgm

    r1024 = pl.BlockSpec((tm, 1024), lambda i: (i, 0))
    r512 = pl.BlockSpec((tm, 512), lambda i: (i, 0))
    v1024 = pl.BlockSpec((1, 1024), lambda i: (0, 0))
    v512 = pl.BlockSpec((1, 512), lambda i: (0, 0))
    in_specs = [r1024, r1024, pl.BlockSpec((tm, 1024), lambda i: (i, COL_R // 1024)), r512, r512, v1024, v512, v512,
                pl.BlockSpec((NA_W, NA_W), lambda i: (0, 0)), pl.BlockSpec((tm, D_MODEL), lambda i: (i, 0))]
    return pl.pallas_call(
        body, grid=(T // tm,), in_specs=in_specs, out_specs=[r1024, r1024, r512, r512, v1024, v512, v512],
        out_shape=[SDS((T, 1024), F32), SDS((T, 1024), F32), SDS((T, 512), F32), SDS((T, 512), F32),
                   SDS((1, 1024), F32), SDS((1, 512), F32), SDS((1, 512), F32)],
        compiler_params=_cparams(("arbitrary",)), name=name)(o_f, o_b, proj, o_na, o_mem, g_gla, g_na, g_mem, bd, dy)


NA_RB = 8


def _na_row_scores(q_ref, k_ref, v_ref, tb_ref, rb, j, n_rows):
    r = rb * NA_RB + j
    rs = jnp.clip(r - NA_ROWS // 2, 0, n_rows - NA_ROWS)
    dr0 = rs - r + (NA_ROWS - 1)
    tok = pl.ds(pl.multiple_of(rs * GRID_W, GRID_W), NA_ROWS * GRID_W)
    q = q_ref[j * GRID_W:(j + 1) * GRID_W, :]
    kk, vv = k_ref[tok, :], v_ref[tok, :]
    bias = jnp.concatenate([tb_ref[dr0 + 2 * i] for i in range(NA_ROWS // 2)], axis=1)
    s = _bdot(q, kk, NT) * (NA_HD ** -0.5) + bias
    m = jnp.max(s, axis=1, keepdims=True)
    p = jnp.exp(s - m)
    l = jnp.sum(p, axis=1, keepdims=True)
    return q, kk, vv, p, l, tok, dr0


def natten_fwd(q, k, v, tb2, name):
    H, T, hd = q.shape
    n_rows = T // GRID_W
    rbt = NA_RB * GRID_W

    def body(q_ref, k_ref, v_ref, tb_ref, o_ref):
        rb = pl.program_id(1)
        for j in range(NA_RB):
            _, _, vv, p, l, _, _ = _na_row_scores(q_ref, k_ref, v_ref, tb_ref, rb, j, n_rows)
            o_ref[j * GRID_W:(j + 1) * GRID_W, :] = _bdot(p, vv, NN) / l

    whole = pl.BlockSpec((None, T, hd), lambda h, r: (h, 0, 0))
    blk = pl.BlockSpec((None, rbt, hd), lambda h, r: (h, r, 0))
    return pl.pallas_call(
        body, grid=(H, n_rows // NA_RB),
        in_specs=[blk, whole, whole, pl.BlockSpec((None, 2 * NA_ROWS - 2, GRID_W, 2 * GRID_W), lambda h, r: (h, 0, 0, 0))],
        out_specs=blk, out_shape=SDS((H, T, hd), F32),
        compiler_params=_cparams(("parallel", "arbitrary")), name=name)(q, k, v, tb2)


def natten_bwd(q, k, v, tb2, do, name):
    H, T, hd = q.shape
    n_rows = T // GRID_W
    rbt = NA_RB * GRID_W
    scale = NA_HD ** -0.5

    def body(q_ref, k_ref, v_ref, tb_ref, do_ref, dq_ref, dk_ref, dv_ref, dtb_ref):
        rb = pl.program_id(1)

        @pl.when(rb == 0)
        def _():
            dk_ref[...] = jnp.zeros_like(dk_ref)
            dv_ref[...] = jnp.zeros_like(dv_ref)
            dtb_ref[...] = jnp.zeros_like(dtb_ref)

        for j in range(NA_RB):
            qv, kk, vv, p, l, tok, dr0 = _na_row_scores(q_ref, k_ref, v_ref, tb_ref, rb, j, n_rows)
            p = p / l
            dov = do_ref[j * GRID_W:(j + 1) * GRID_W, :]
            dp = _bdot(dov, vv, NT)
            ds = p * (dp - jnp.sum(dp * p, axis=1, keepdims=True))
            dq_ref[j * GRID_W:(j + 1) * GRID_W, :] = _bdot(ds, kk, NN) * scale
            dk_ref[tok, :] += _bdot(ds, qv, TN) * scale
            dv_ref[tok, :] += _bdot(p, dov, TN)
            for i in range(NA_ROWS // 2):
                dtb_ref[dr0 + 2 * i] += ds[:, 2 * GRID_W * i:2 * GRID_W * (i + 1)]

    whole = pl.BlockSpec((None, T, hd), lambda h, r: (h, 0, 0))
    blk = pl.BlockSpec((None, rbt, hd), lambda h, r: (h, r, 0))
    tbs = pl.BlockSpec((None, 2 * NA_ROWS - 2, GRID_W, 2 * GRID_W), lambda h, r: (h, 0, 0, 0))
    return pl.pallas_call(
        body, grid=(H, n_rows // NA_RB), in_specs=[blk, whole, whole, tbs, blk],
        out_specs=[blk, whole, whole, tbs],
        out_shape=[SDS((H, T, hd), F32), SDS((H, T, hd), F32), SDS((H, T, hd), F32), SDS(tb2.shape, F32)],
        compiler_params=_cparams(("parallel", "arbitrary")), name=name)(q, k, v, tb2, do)


def _rpb_expand_consts():
    qc = np.arange(GRID_W)[:, None]
    kc = np.arange(GRID_W)[None, :]
    cs = np.clip(qc - NA_COLS // 2, 0, GRID_W - NA_COLS)
    inside = (kc >= cs) & (kc < cs + NA_COLS)
    dc = np.clip(kc - qc, -(NA_COLS - 1), NA_COLS - 1) + (NA_COLS - 1)
    e = np.zeros((128, GRID_W * GRID_W), np.float32)
    flat = (qc * GRID_W + kc)
    e[dc[inside], flat[inside]] = 1.0
    neg = np.where(inside, 0.0, NEG).astype(np.float32).reshape(1, -1)
    return e, neg


def _rpb_fold_consts():
    sa = np.zeros((NA_H * 15, NA_H * 14), np.float32)
    sb = np.zeros((NA_H * 15, NA_H * 14), np.float32)
    for h in range(NA_H):
        for d in range(14):
            sa[h * 15 + d, h * 14 + d] = 1.0
            sb[h * 15 + d + 1, h * 14 + d] = 1.0
    return sa, sb


def rpb_table(rpb, name):
    e, neg = _rpb_expand_consts()
    rp = jnp.pad(rpb.reshape(NA_H * 15, 31), ((0, 0), (0, 128 - 31)))

    def body(r_ref, e_ref, n_ref, o_ref):
        o_ref[...] = jnp.dot(r_ref[...], e_ref[...], preferred_element_type=F32, precision=HI) + n_ref[...]

    t = pl.pallas_call(body, out_shape=SDS((NA_H * 15, GRID_W * GRID_W), F32), name=name)(rp, e, neg)
    t = t.reshape(NA_H, 15, GRID_W, GRID_W)
    return jnp.concatenate([t[:, :14], t[:, 1:]], axis=-1)


def rpb_table_bwd(dtb2, name):
    e, _ = _rpb_expand_consts()
    sa, sb = _rpb_fold_consts()
    a = dtb2[..., :GRID_W].reshape(NA_H * 14, GRID_W * GRID_W)
    b = dtb2[..., GRID_W:].reshape(NA_H * 14, GRID_W * GRID_W)

    def body(a_ref, b_ref, e_ref, sa_ref, sb_ref, o_ref):
        ev = e_ref[...]
        pa = lax.dot_general(a_ref[...], ev, (NT, ((), ())), preferred_element_type=F32, precision=HI)
        pb = lax.dot_general(b_ref[...], ev, (NT, ((), ())), preferred_element_type=F32, precision=HI)
        o_ref[...] = (jnp.dot(sa_ref[...], pa, preferred_element_type=F32, precision=HI)
                      + jnp.dot(sb_ref[...], pb, preferred_element_type=F32, precision=HI))

    d = pl.pallas_call(body, out_shape=SDS((NA_H * 15, 128), F32), name=name)(a, b, e, sa, sb)
    return d[:, :31].reshape(NA_H, 15, 31)


def _kprep_fn(kv, gk):
    return _norm_heads(kv[:, :MEM_W], MEM_HD) * gk, kv[:, MEM_W:]


def mem_kprep(kv, gk, name):
    def body(kv_ref, g_ref, k_ref, v_ref):
        kn, vv = _kprep_fn(kv_ref[...], g_ref[...])
        k_ref[...] = kn.astype(BF16)
        v_ref[...] = vv.astype(BF16)

    return pl.pallas_call(body, out_shape=[SDS((MEM_TOK, MEM_W), BF16)] * 2, name=name)(kv, gk)


def mem_kprep_bwd(kv, gk, dk, dv, name):
    def body(kv_ref, g_ref, dk_ref, dv_ref, dkv_ref, dg_ref):
        _, vjp = jax.vjp(_kprep_fn, kv_ref[...], g_ref[...])
        dkv, dg = vjp((dk_ref[...], dv_ref[...]))
        dkv_ref[...] = dkv.astype(BF16)
        dg_ref[...] = dg

    return pl.pallas_call(body, out_shape=[SDS((MEM_TOK, 2 * MEM_W), BF16), SDS((1, MEM_W), F32)], name=name)(kv, gk, dk, dv)


def _mem_probs(q_ref, k_ref, h):
    hs = slice(h * MEM_HD, (h + 1) * MEM_HD)
    qh, kh = q_ref[:, hs], k_ref[:, hs]
    s = _bdot(qh, kh, NT) * (MEM_HD ** -0.5)
    p = jnp.exp(s - jnp.max(s, axis=1, keepdims=True))
    return hs, qh, kh, p, jnp.sum(p, axis=1, keepdims=True)


def mem_attn_fwd(q, km, vm, name):
    T = q.shape[0]
    tm = _fit(T, 512, 8)

    def body(q_ref, k_ref, v_ref, o_ref):
        for h in range(MEM_H):
            hs, _, _, p, l = _mem_probs(q_ref, k_ref, h)
            o_ref[:, hs] = _bdot(p, v_ref[:, hs], NN) / l

    row = pl.BlockSpec((tm, MEM_W), lambda i: (i, 0))
    full = pl.BlockSpec((MEM_TOK, MEM_W), lambda i: (0, 0))
    return pl.pallas_call(body, grid=(T // tm,), in_specs=[row, full, full], out_specs=row, out_shape=SDS((T, MEM_W), F32),
                          compiler_params=_cparams(("parallel",)), name=name)(q, km, vm)


def mem_attn_bwd(q, km, vm, do, name):
    T = q.shape[0]
    tm = _fit(T, 512, 8)
    scale = MEM_HD ** -0.5

    def body(q_ref, k_ref, v_ref, do_ref, dq_ref, dk_ref, dv_ref):
        @pl.when(pl.program_id(0) == 0)
        def _():
            dk_ref[...] = jnp.zeros_like(dk_ref)
            dv_ref[...] = jnp.zeros_like(dv_ref)

        for h in range(MEM_H):
            hs, qh, kh, p, l = _mem_probs(q_ref, k_ref, h)
            p = p / l
            dov = do_ref[:, hs]
            dp = _bdot(dov, v_ref[:, hs], NT)
            ds = p * (dp - jnp.sum(dp * p, axis=1, keepdims=True))
            dq_ref[:, hs] = _bdot(ds, kh, NN) * scale
            dk_ref[:, hs] += _bdot(ds, qh, TN) * scale
            dv_ref[:, hs] += _bdot(p, dov, TN)

    row = pl.BlockSpec((tm, MEM_W), lambda i: (i, 0))
    full = pl.BlockSpec((MEM_TOK, MEM_W), lambda i: (0, 0))
    return pl.pallas_call(
        body, grid=(T // tm,), in_specs=[row, full, full, row], out_specs=[row, full, full],
        out_shape=[SDS((T, MEM_W), F32), SDS((MEM_TOK, MEM_W), F32), SDS((MEM_TOK, MEM_W), F32)],
        compiler_params=_cparams(("arbitrary",)), name=name)(q, km, vm, do)


def sum_slots(a, name, rows=2048):
    n, R, _ = a.shape
    tr = _fit(R, rows, 8)

    def body(a_ref, o_ref):
        s = a_ref[0]
        for i in range(1, n):
            s = s + a_ref[i]
        o_ref[...] = s

    return pl.pallas_call(body, grid=(R // tr,), in_specs=[pl.BlockSpec((n, tr, 128), lambda i: (0, i, 0))],
                          out_specs=pl.BlockSpec((tr, 128), lambda i: (i, 0)), out_shape=SDS((R, 128), F32),
                          compiler_params=_cparams(("parallel",)), name=name)(a)


def pair_sum(g, land, c_idx, name, rows=5584):
    _, R, _ = g.shape
    tr = _fit(R, rows, 8)

    def body(c_ref, g_ref, l_ref, o_ref):
        o_ref[...] = g_ref[...] + l_ref[...]

    blk = pl.BlockSpec((None, tr, 128), lambda k, i, c: (k, i, 0))
    gs = pltpu.PrefetchScalarGridSpec(
        num_scalar_prefetch=1, grid=(4, R // tr),
        in_specs=[pl.BlockSpec((None, tr, 128), lambda k, i, c: (2 * k + c[0], i, 0)), blk], out_specs=blk)
    return pl.pallas_call(body, grid_spec=gs, out_shape=SDS((4, R, 128), F32),
                          compiler_params=_cparams(("parallel", "parallel")), name=name)(c_idx, g, land)


def adamw(w, g, m, v, name):
    R, Cc = w.shape
    tr = _fit(R, max(8, (262144 // max(Cc, 128)) // 8 * 8), 8)
    c1 = 1.0 - ADAM_B1 ** ADAM_STEP
    c2 = 1.0 - ADAM_B2 ** ADAM_STEP

    def body(w_ref, g_ref, m_ref, v_ref, d_ref, mo_ref, vo_ref):
        gv = g_ref[...]
        m2 = ADAM_B1 * m_ref[...] + (1.0 - ADAM_B1) * gv
        v2 = ADAM_B2 * v_ref[...] + (1.0 - ADAM_B2) * (gv * gv)
        d_ref[...] = -ADAM_LR * ((m2 / c1) / (jnp.sqrt(v2 / c2) + ADAM_EPS) + ADAM_WD * w_ref[...])
        mo_ref[...] = m2
        vo_ref[...] = v2

    blk = pl.BlockSpec((tr, Cc), lambda i: (i, 0))
    return pl.pallas_call(body, grid=(R // tr,), in_specs=[blk] * 4, out_specs=[blk] * 3, out_shape=[SDS((R, Cc), F32)] * 3,
                          compiler_params=_cparams(("parallel",)), name=name)(w, g, m, v)


ANY = pl.BlockSpec(memory_space=pl.ANY)


def _my_pos():
    return lax.axis_index("x"), lax.axis_index("y"), lax.axis_index("c")


def all_gather(x_shard, name):
    def body(x_ref, out_ref, send_sems, recv_sems, local_sem):
        x, y, c = _my_pos()
        me, sibling = (x, y, c), (x, y, 1 - c)
        chips = [(1 - x, y), (x, 1 - y), (1 - x, 1 - y)]

        def slot(px, py, pc):
            return out_ref.at[4 * px + 2 * py + pc]

        def copy(k, block, to, src=None):
            return pltpu.make_async_remote_copy(
                src_ref=slot(*block) if src is None else src, dst_ref=slot(*block),
                send_sem=send_sems.at[k], recv_sem=recv_sems.at[k], device_id=to, device_id_type=MESH)

        mine = pltpu.make_async_copy(x_ref, slot(*me), local_sem)
        mine.start()
        first = [copy(0, me, sibling, src=x_ref)]
        first += [copy(1 + j, me, (*chip, c), src=x_ref) for j, chip in enumerate(chips)]
        for cp in first:
            cp.start()
        passed = [copy(4 + j, (*chip, c), sibling) for j, chip in enumerate(chips)]
        for j, chip in enumerate(chips):
            copy(1 + j, (*chip, c), me).wait_recv()
            passed[j].start()
        copy(0, sibling, me).wait_recv()
        for j, chip in enumerate(chips):
            copy(4 + j, (*chip, 1 - c), me).wait_recv()
        for cp in first + passed:
            cp.wait_send()
        mine.wait()

    return pl.pallas_call(
        body, out_shape=SDS((N_DEV,) + x_shard.shape, x_shard.dtype), in_specs=[ANY], out_specs=ANY,
        scratch_shapes=[pltpu.SemaphoreType.DMA((7,)), pltpu.SemaphoreType.DMA((7,)), pltpu.SemaphoreType.DMA],
        name=name)(x_shard)


def pair_exchange(g, name):
    _, R, L = g.shape

    def body(g_ref, land_ref, send_sems, recv_sems):
        x, y, c = _my_pos()
        sibling = (x, y, 1 - c)
        copies = [pltpu.make_async_remote_copy(
            src_ref=g_ref.at[2 * k + (1 - c)], dst_ref=land_ref.at[k], send_sem=send_sems.at[k], recv_sem=recv_sems.at[k],
            device_id=sibling, device_id_type=MESH) for k in range(4)]
        for cp in copies:
            cp.start()
        for cp in copies:
            cp.wait_recv()
        for cp in copies:
            cp.wait_send()

    return pl.pallas_call(
        body, out_shape=SDS((4, R, L), g.dtype), in_specs=[ANY], out_specs=ANY,
        scratch_shapes=[pltpu.SemaphoreType.DMA((4,)), pltpu.SemaphoreType.DMA((4,))], name=name)(g)


def chip_exchange(s, name):
    _, R, L = s.shape

    def body(s_ref, land_ref, send_sems, recv_sems, local_sem):
        x, y, c = _my_pos()
        my_chip = 2 * x + y
        chips = [(1 - x, y), (x, 1 - y), (1 - x, 1 - y)]
        own = pltpu.make_async_copy(s_ref.at[my_chip], land_ref.at[my_chip], local_sem)
        own.start()
        sends = [pltpu.make_async_remote_copy(
            src_ref=s_ref.at[2 * px + py], dst_ref=land_ref.at[my_chip], send_sem=send_sems.at[j], recv_sem=recv_sems.at[j],
            device_id=(px, py, c), device_id_type=MESH) for j, (px, py) in enumerate(chips)]
        for cp in sends:
            cp.start()
        for j, (px, py) in enumerate(chips):
            pltpu.make_async_remote_copy(
                src_ref=s_ref.at[my_chip], dst_ref=land_ref.at[2 * px + py], send_sem=send_sems.at[j], recv_sem=recv_sems.at[j],
                device_id=(px, py, c), device_id_type=MESH).wait_recv()
        for cp in sends:
            cp.wait_send()
        own.wait()

    return pl.pallas_call(
        body, out_shape=SDS((4, R, L), s.dtype), in_specs=[ANY], out_specs=ANY,
        scratch_shapes=[pltpu.SemaphoreType.DMA((3,)), pltpu.SemaphoreType.DMA((3,)), pltpu.SemaphoreType.DMA],
        name=name)(s)


SHARDED = (("ffn_w13", (2048, 1408), 1), ("ffn_w2", (704, 2048), 0), ("w_out", (256, 2048), 0), ("mem_wkv", (256, 1024), 0),
           ("w_in", (2048, 644), 1), ("gla_wg2_f", (16, 64), 1), ("gla_wg2_b", (16, 64), 1))
SHARD_ELEMS = sum(a * b for _, (a, b), _ in SHARDED)
SHARD_ROWS = SHARD_ELEMS // 128


def pack_shards(shards):
    return jnp.concatenate([shards[n].reshape(-1) for n, _, _ in SHARDED]).reshape(SHARD_ROWS, 128)


def unpack_gathered(buf):
    flat = buf.reshape(N_DEV, SHARD_ELEMS)
    out, off = {}, 0
    for n, (a, b), axis in SHARDED:
        seg = flat[:, off:off + a * b].reshape(N_DEV, a, b)
        off += a * b
        out[n] = seg.reshape(N_DEV * a, b) if axis == 0 else seg.transpose(1, 0, 2).reshape(a, N_DEV * b)
    return out


def pack_full_grads(grads):
    segs = []
    for n, (a, b), axis in SHARDED:
        g = grads[n]
        g = g.reshape(N_DEV, a, b) if axis == 0 else g.reshape(a, N_DEV, b).transpose(1, 0, 2)
        segs.append(g.reshape(N_DEV, a * b))
    return jnp.concatenate(segs, axis=1).reshape(N_DEV, SHARD_ROWS, 128)


def unpack_shard_grads(flat):
    flat = flat.reshape(-1)
    out, off = {}, 0
    for n, (a, b), _ in SHARDED:
        out[n] = flat[off:off + a * b].reshape(a, b)
        off += a * b
    return out


def permute_w_in(w):
    return jnp.concatenate([w[:, :ORIG_GATE0], w[:, ORIG_AFTER_GATE:], w[:, ORIG_GATE0:ORIG_AFTER_GATE],
                            jnp.zeros((w.shape[0], PC - IN_COLS), w.dtype)], axis=1)


def unpermute_w_in(g):
    return jnp.concatenate([g[:, :ORIG_GATE0], g[:, COL_TAIL:COL_TAIL + 2 * GLA_RANK], g[:, ORIG_GATE0:COL_TAIL]], axis=1)


def pad_gate_weight(wg2, backward_dir):
    r0 = GLA_RANK if backward_dir else 0
    w = wg2.astype(F32).reshape(GLA_RANK, GLA_H, GLA_HK).transpose(1, 0, 2)
    return jnp.pad(w, ((0, 0), (r0, 128 - GLA_RANK - r0), (0, 0)))


def unpad_gate_grad(dw, backward_dir):
    r0 = GLA_RANK if backward_dir else 0
    return dw[:, r0:r0 + GLA_RANK, :].transpose(1, 0, 2).reshape(GLA_RANK, GLA_DK)


def to_heads(t):
    T = t.shape[0]
    return t.reshape(T, NA_H, NA_HD).transpose(1, 0, 2)


def from_heads(t):
    return t.transpose(1, 0, 2).reshape(t.shape[1], NA_W)


REPLICATED = ("attn_norm", "gla_bg_f", "gla_bg_b", "gla_out_norm", "na_q_norm", "na_k_norm", "na_rpb", "na_out_norm",
              "mem_norm", "mem_q_norm", "mem_k_norm", "mem_out_norm", "ffn_norm")
WEIGHTS = ("attn_norm", "w_in", "gla_wg2_f", "gla_bg_f", "gla_wg2_b", "gla_bg_b", "gla_out_norm", "na_q_norm", "na_k_norm",
           "na_rpb", "na_out_norm", "mem_norm", "mem_wkv", "mem_q_norm", "mem_k_norm", "mem_out_norm", "w_out", "ffn_norm",
           "ffn_w13", "ffn_w2")


def fold_heads(dg, n_heads, name):
    hd = dg.shape[1] // n_heads
    fold = (np.arange(dg.shape[1])[:, None] % hd == np.arange(128)[None, :]).astype(np.float32)
    out = matmul(jnp.pad(dg, ((0, 7), (0, 0))), fold, "nn", name, exact=True)
    return out[0, :hd]


def layer_fwd(x, mem_n_in, p, W, l):
    tag = f"l{l}_"
    row = lambda v: v.reshape(1, -1)
    sv = {"x": x}
    sv["xn"] = rms_fwd(x, row(p["attn_norm"]), tag + "attn_rms")
    proj = matmul(sv["xn"], W["w_in"], "nn", tag + "proj", tn=768, tk=2048)
    sv["proj"] = proj
    sv["wg_f"], sv["wg_b"] = pad_gate_weight(W["gla_wg2_f"], False), pad_gate_weight(W["gla_wg2_b"], True)
    sv["bg_f"], sv["bg_b"] = p["gla_bg_f"].reshape(GLA_H, 1, GLA_HK), p["gla_bg_b"].reshape(GLA_H, 1, GLA_HK)
    sv["o_f"], sv["s_f"] = gla_fwd(proj, sv["wg_f"], sv["bg_f"], False, tag + "gla_f")
    sv["o_b"], sv["s_b"] = gla_fwd(proj, sv["wg_b"], sv["bg_b"], True, tag + "gla_b")
    sv["gq"], sv["gk"] = jnp.tile(row(p["na_q_norm"]), (1, NA_H)), jnp.tile(row(p["na_k_norm"]), (1, NA_H))
    sv["gmq"], sv["gmk"] = jnp.tile(row(p["mem_q_norm"]), (1, MEM_H)), jnp.tile(row(p["mem_k_norm"]), (1, MEM_H))
    qn, kn, vn, mqn = pre_fwd(proj, sv["gq"], sv["gk"], sv["gmq"], tag + "pre")
    sv["q_hm"], sv["k_hm"], sv["v_hm"], sv["mqn"] = to_heads(qn), to_heads(kn), to_heads(vn), mqn
    sv["tb2"] = rpb_table(p["na_rpb"], tag + "rpb_table")
    sv["o_na"] = from_heads(natten_fwd(sv["q_hm"], sv["k_hm"], sv["v_hm"], sv["tb2"], tag + "natten"))
    sv["mem_n"] = rms_fwd(mem_n_in, row(p["mem_norm"]), tag + "mem_rms")
    sv["kv"] = matmul(sv["mem_n"], W["mem_wkv"], "nn", tag + "mem_kv", tn=512, tk=2048)
    sv["km"], sv["vm"] = mem_kprep(sv["kv"], sv["gmk"], tag + "mem_kprep")
    sv["o_mem"] = mem_attn_fwd(mqn, sv["km"], sv["vm"], tag + "mem_attn")
    sv["ycat"] = post_fwd(sv["o_f"], sv["o_b"], proj, sv["o_na"], sv["o_mem"], row(p["gla_out_norm"]),
                          row(p["na_out_norm"]), row(p["mem_out_norm"]), tag + "post")
    x1 = matmul(sv["ycat"], W["w_out"], "nn", tag + "out_proj", res=x, tk=2048)
    sv["x1"] = x1
    sv["h"] = rms_fwd(x1, row(p["ffn_norm"]), tag + "ffn_rms")
    sv["gu"] = matmul(sv["h"], W["ffn_w13"], "nn", tag + "ffn_up", tk=2048)
    sv["a"] = swiglu_fwd(sv["gu"], tag + "swiglu")
    x2 = matmul(sv["a"], W["ffn_w2"], "nn", tag + "ffn_down", res=x1)
    return x2, sv


def layer_bwd(dx2, dx2_b, mem_n_in, p, W, sv, l):
    tag = f"l{l}_b_"
    row = lambda v: v.reshape(1, -1)
    gw, gs = {}, {}
    da = matmul(dx2_b, W["ffn_w2"], "nt", tag + "d_a", tn=512, tk=2048)
    gw["ffn_w2"] = matmul(sv["a"], dx2_b, "tn", tag + "dw2", tm=1408, tn=2048, tk=512)
    dgu = swiglu_bwd(sv["gu"], da, tag + "swiglu")
    dh = matmul(dgu, W["ffn_w13"], "nt", tag + "d_h", tk=1024)
    gw["ffn_w13"] = matmul(sv["h"], dgu, "tn", tag + "dw13", tm=2048, tn=1024, tk=512)
    dx1, dx1_b, dg = rms_bwd(sv["x1"], row(p["ffn_norm"]), dh, dx2, tag + "ffn_rms")
    gs["ffn_norm"] = dg[0]
    dycat = matmul(dx1_b, W["w_out"], "nt", tag + "d_ycat", tk=2048)
    gw["w_out"] = matmul(sv["ycat"], dx1_b, "tn", tag + "dw_out", tm=2048, tn=1024, tk=512)
    d_o, d_r, d_ona, d_omem, dgg, dgn, dgm = post_bwd(
        sv["o_f"], sv["o_b"], sv["proj"], sv["o_na"], sv["o_mem"], row(p["gla_out_norm"]), row(p["na_out_norm"]),
        row(p["mem_out_norm"]), dycat, tag + "post")
    gs["gla_out_norm"], gs["na_out_norm"], gs["mem_out_norm"] = dgg[0], dgn[0], dgm[0]
    dq_f, dk_f, dv_f, dt_f, dwg_f, dbg_f = gla_bwd(sv["proj"], sv["wg_f"], sv["bg_f"], sv["s_f"], d_o, False, tag + "gla_f")
    dq_b, dk_b, dv_b, dt_b, dwg_b, dbg_b = gla_bwd(sv["proj"], sv["wg_b"], sv["bg_b"], sv["s_b"], d_o, True, tag + "gla_b")
    gw["gla_wg2_f"], gw["gla_wg2_b"] = unpad_gate_grad(dwg_f, False), unpad_gate_grad(dwg_b, True)
    gs["gla_bg_f"], gs["gla_bg_b"] = dbg_f.reshape(-1), dbg_b.reshape(-1)
    dq_hm, dk_hm, dv_hm, dtb2 = natten_bwd(sv["q_hm"], sv["k_hm"], sv["v_hm"], sv["tb2"], to_heads(d_ona), tag + "natten")
    gs["na_rpb"] = rpb_table_bwd(dtb2, tag + "rpb_table")
    d_mqn, dkm, dvm = mem_attn_bwd(sv["mqn"], sv["km"], sv["vm"], d_omem, tag + "mem_attn")
    dkv, dgmk = mem_kprep_bwd(sv["kv"], sv["gmk"], dkm, dvm, tag + "mem_kprep")
    gs["mem_k_norm"] = fold_heads(dgmk, MEM_H, tag + "fold_mk")
    gw["mem_wkv"] = matmul(sv["mem_n"], dkv, "tn", tag + "dw_kv", tm=2048, tn=1024, tk=256)
    d_memn = matmul(dkv, W["mem_wkv"], "nt", tag + "d_memn", tk=1024)
    _, _, dg = rms_bwd(mem_n_in, row(p["mem_norm"]), d_memn, None, tag + "mem_rms")
    gs["mem_norm"] = dg[0]
    dproj, dgq, dgk, dgmq = pre_bwd(sv["proj"], sv["gq"], sv["gk"], sv["gmq"], from_heads(dq_hm), from_heads(dk_hm),
                                    from_heads(dv_hm), d_mqn, dq_f, dq_b, dk_f, dk_b, dv_f, dv_b, d_r, dt_f, dt_b, tag + "pre")
    gs["na_q_norm"] = fold_heads(dgq, NA_H, tag + "fold_q")
    gs["na_k_norm"] = fold_heads(dgk, NA_H, tag + "fold_k")
    gs["mem_q_norm"] = fold_heads(dgmq, MEM_H, tag + "fold_mq")
    dxn = matmul(dproj, W["w_in"], "nt", tag + "d_xn", tk=768)
    gw["w_in"] = unpermute_w_in(matmul(sv["xn"], dproj, "tn", tag + "dw_in", tm=2048, tn=768, tk=512))
    dx, dx_b, dg = rms_bwd(sv["x"], row(p["attn_norm"]), dxn, dx1, tag + "attn_rms")
    gs["attn_norm"] = dg[0]
    return dx, dx_b, gw, gs


def gather_layer_weights(shards_bf16, l):
    full = unpack_gathered(all_gather(pack_shards(shards_bf16), f"l{l}_gather_weights"))
    full["w_in"] = permute_w_in(full["w_in"])
    return full


def reduce_scatter_layer(gw, c_idx, l):
    g = pack_full_grads(gw)
    land1 = pair_exchange(g, f"l{l}_rs_pair_exchange")
    chip = pair_sum(g, land1, c_idx, f"l{l}_rs_pair_sum")
    land2 = chip_exchange(chip, f"l{l}_rs_chip_exchange")
    return unpack_shard_grads(sum_slots(land2, f"l{l}_rs_chip_sum", rows=5584))


def kernel(x, mem, attn_norm, w_in, gla_wg2_f, gla_bg_f, gla_wg2_b, gla_bg_b, gla_out_norm, na_q_norm, na_k_norm, na_rpb, na_out_norm, mem_norm, mem_wkv, mem_q_norm, mem_k_norm, mem_out_norm, w_out, ffn_norm, ffn_w13, ffn_w2, loss_target, m_attn_norm, m_w_in, m_gla_wg2_f, m_gla_bg_f, m_gla_wg2_b, m_gla_bg_b, m_gla_out_norm, m_na_q_norm, m_na_k_norm, m_na_rpb, m_na_out_norm, m_mem_norm, m_mem_wkv, m_mem_q_norm, m_mem_k_norm, m_mem_out_norm, m_w_out, m_ffn_norm, m_ffn_w13, m_ffn_w2, v_attn_norm, v_w_in, v_gla_wg2_f, v_gla_bg_f, v_gla_wg2_b, v_gla_bg_b, v_gla_out_norm, v_na_q_norm, v_na_k_norm, v_na_rpb, v_na_out_norm, v_mem_norm, v_mem_wkv, v_mem_q_norm, v_mem_k_norm, v_mem_out_norm, v_w_out, v_ffn_norm, v_ffn_w13, v_ffn_w2):
    w = dict(attn_norm=attn_norm, w_in=w_in, gla_wg2_f=gla_wg2_f, gla_bg_f=gla_bg_f, gla_wg2_b=gla_wg2_b, gla_bg_b=gla_bg_b,
             gla_out_norm=gla_out_norm, na_q_norm=na_q_norm, na_k_norm=na_k_norm, na_rpb=na_rpb, na_out_norm=na_out_norm,
             mem_norm=mem_norm, mem_wkv=mem_wkv, mem_q_norm=mem_q_norm, mem_k_norm=mem_k_norm, mem_out_norm=mem_out_norm,
             w_out=w_out, ffn_norm=ffn_norm, ffn_w13=ffn_w13, ffn_w2=ffn_w2)
    mom = dict(attn_norm=m_attn_norm, w_in=m_w_in, gla_wg2_f=m_gla_wg2_f, gla_bg_f=m_gla_bg_f, gla_wg2_b=m_gla_wg2_b,
               gla_bg_b=m_gla_bg_b, gla_out_norm=m_gla_out_norm, na_q_norm=m_na_q_norm, na_k_norm=m_na_k_norm, na_rpb=m_na_rpb,
               na_out_norm=m_na_out_norm, mem_norm=m_mem_norm, mem_wkv=m_mem_wkv, mem_q_norm=m_mem_q_norm,
               mem_k_norm=m_mem_k_norm, mem_out_norm=m_mem_out_norm, w_out=m_w_out, ffn_norm=m_ffn_norm, ffn_w13=m_ffn_w13,
               ffn_w2=m_ffn_w2)
    var = dict(attn_norm=v_attn_norm, w_in=v_w_in, gla_wg2_f=v_gla_wg2_f, gla_bg_f=v_gla_bg_f, gla_wg2_b=v_gla_wg2_b,
               gla_bg_b=v_gla_bg_b, gla_out_norm=v_gla_out_norm, na_q_norm=v_na_q_norm, na_k_norm=v_na_k_norm, na_rpb=v_na_rpb,
               na_out_norm=v_na_out_norm, mem_norm=v_mem_norm, mem_wkv=v_mem_wkv, mem_q_norm=v_mem_q_norm,
               mem_k_norm=v_mem_k_norm, mem_out_norm=v_mem_out_norm, w_out=v_w_out, ffn_norm=v_ffn_norm, ffn_w13=v_ffn_w13,
               ffn_w2=v_ffn_w2)
    depth = attn_norm.shape[0]
    T = x.shape[1]
    xs, mem0, tgt = x.reshape(T, D_MODEL), mem.reshape(MEM_TOK, D_MODEL), loss_target.reshape(T, D_MODEL)
    c_idx = lax.axis_index("c").astype(jnp.int32).reshape(1)

    W = [gather_layer_weights({n: w[n][l].astype(BF16) for n, _, _ in SHARDED}, l) for l in range(depth)]
    P = [{n: w[n][l] for n in REPLICATED} for l in range(depth)]

    saved = []
    h = xs
    for l in range(depth):
        h, sv = layer_fwd(h, mem0, P[l], W[l], l)
        saved.append(sv)
    dy, dy_b, lsum = loss_bwd(h, tgt, "loss")
    loss = lax.psum(lsum[0, 0], ("x", "y", "c")) * (0.5 / D_MODEL)

    g_shard, g_small = [None] * depth, [None] * depth
    for l in range(depth - 1, -1, -1):
        dy, dy_b, gw, g_small[l] = layer_bwd(dy, dy_b, mem0, P[l], W[l], saved[l], l)
        g_shard[l] = reduce_scatter_layer(gw, c_idx, l)
        saved[l] = None
    grad_x = dy.reshape(x.shape)

    small = jnp.concatenate([g_small[l][n].reshape(-1) for l in range(depth) for n in REPLICATED])
    n_small = small.shape[0]
    rows = -(-n_small // 1024) * 8
    small = jnp.pad(small, (0, rows * 128 - n_small)).reshape(rows, 128)
    small = sum_slots(all_gather(small, "gather_small_grads"), "sum_small_grads").reshape(-1)
    grads, off = {}, 0
    per_layer = {n: [] for n in REPLICATED}
    for l in range(depth):
        for n in REPLICATED:
            sz = int(np.prod(w[n].shape[1:]))
            per_layer[n].append(small[off:off + sz].reshape(w[n].shape[1:]))
            off += sz
    for n in REPLICATED:
        grads[n] = jnp.stack(per_layer[n])
    for n, _, _ in SHARDED:
        grads[n] = jnp.stack([g_shard[l][n] for l in range(depth)])

    delta, new_m, new_v = {}, {}, {}
    for n in WEIGHTS:
        shp = w[n].shape
        two_d = (shp[0], int(np.prod(shp[1:]))) if n in REPLICATED else (int(np.prod(shp[:-1])), shp[-1])
        d_, m_, v_ = adamw(w[n].reshape(two_d), grads[n].reshape(two_d), mom[n].reshape(two_d), var[n].reshape(two_d),
                           "adamw_" + n)
        delta[n], new_m[n], new_v[n] = d_.reshape(shp), m_.reshape(shp), v_.reshape(shp)

    return (loss, grad_x, *[grads[n] for n in WEIGHTS], *[delta[n] for n in WEIGHTS], *[new_m[n] for n in WEIGHTS],
            *[new_v[n] for n in WEIGHTS])
```

```python
import functools

import numpy as np
import jax
import jax.numpy as jnp
from jax import lax
from jax.experimental import pallas as pl
from jax.experimental.pallas import tpu as pltpu

F32, BF16 = jnp.float32, jnp.bfloat16
HI = lax.Precision.HIGHEST
SDS = jax.ShapeDtypeStruct
MESH = pl.DeviceIdType.MESH

D_MODEL = 2048
GRID_W = 64
GLA_H, GLA_HK, GLA_HV, GLA_RANK, GLA_TAU, GLA_C = 4, 128, 256, 16, 16.0, 64
GLA_DK, GLA_DV = GLA_H * GLA_HK, GLA_H * GLA_HV
NA_H, NA_HD, NA_ROWS, NA_COLS = 8, 64, 8, 16
NA_W = NA_H * NA_HD
MEM_H, MEM_HD, MEM_TOK = 4, 128, 256
MEM_W = MEM_H * MEM_HD
D_FF = 5632
IN_COLS = 5152
RMS_EPS = 1e-6
ADAM_LR, ADAM_B1, ADAM_B2, ADAM_EPS, ADAM_WD, ADAM_STEP = 0.001, 0.9, 0.999, 1e-08, 0.01, 10
N_DEV = 8

PC = 5376
COL_R, COL_NQ, COL_NK, COL_NV, COL_MQ, COL_TAIL = 2048, 3072, 3584, 4096, 4608, 5120
ORIG_GATE0 = 3072
ORIG_AFTER_GATE = 3104

GLA_G = 8
VMEM_LIMIT = 56 * 1024 * 1024
NEG = -1e30


def _cparams(sem):
    return pltpu.CompilerParams(dimension_semantics=sem, vmem_limit_bytes=VMEM_LIMIT)


def _fit(n, pref, unit=128):
    if n <= pref:
        return n
    t = (pref // unit) * unit
    while t >= unit:
        if n % t == 0:
            return t
        t -= unit
    return n


def _bdot(a, b, dims):
    return lax.dot_general(a.astype(BF16), b.astype(BF16), (dims, ((), ())), preferred_element_type=F32)


NN, NT, TN = ((1,), (0,)), ((1,), (1,)), ((0,), (0,))


@functools.partial(jax.custom_vjp, nondiff_argnums=(2,))
def _bdot_vjp(a, b, dims):
    return _bdot(a, b, dims)


def _bdot_vjp_fwd(a, b, dims):
    return _bdot(a, b, dims), (a, b)


def _bdot_vjp_bwd(dims, res, ct):
    a, b = res
    if dims == NN:
        return _bdot(ct, b, NT), _bdot(a, ct, TN)
    if dims == NT:
        return _bdot(ct, b, NN), _bdot(ct, a, TN)
    return _bdot(b, ct, NT), _bdot(a, ct, NN)


_bdot_vjp.defvjp(_bdot_vjp_fwd, _bdot_vjp_bwd)


def _split_dot(c, x, dims):
    hi = x.astype(BF16)
    lo = (x - hi.astype(F32)).astype(BF16)
    cb = c.astype(BF16)
    return (lax.dot_general(cb, hi, (dims, ((), ())), preferred_element_type=F32)
            + lax.dot_general(cb, lo, (dims, ((), ())), preferred_element_type=F32))


@jax.custom_vjp
def _tri_sum(cmat, x):
    return _split_dot(cmat, x, NN)


def _tri_sum_fwd(cmat, x):
    return _split_dot(cmat, x, NN), cmat


def _tri_sum_bwd(cmat, ct):
    return jnp.zeros_like(cmat), _split_dot(cmat, ct, TN)


_tri_sum.defvjp(_tri_sum_fwd, _tri_sum_bwd)


def matmul(a, b, mode, name, out_dtype=F32, res=None, tm=1024, tn=1024, tk=512, exact=False,
           a_halves=False, b_halves=False, b_blocked=False, out_blocked=False):
    if mode == "nn":
        M, K = a.shape
        N = b.shape[0] * b.shape[2] if b_blocked else b.shape[1]
        if b_blocked:
            tn = b.shape[2]
    elif mode == "nt":
        M, K = (a.shape[1], 2 * a.shape[2]) if a_halves else a.shape
        N = b.shape[1] if b_blocked else b.shape[0]
        if b_blocked:
            tk = b.shape[2]
    else:
        K, M = a.shape
        N = 2 * b.shape[2] if b_halves else b.shape[1]
    tm, tn, tk = _fit(M, tm, 8 if mode != "tn" else 128), _fit(N, tn), _fit(K, tk, 128 if mode != "tn" else 16)
    nk = K // tk
    dims = {"nn": NN, "nt": NT, "tn": TN}[mode]

    def body(*refs):
        if res is None:
            a_ref, b_ref, o_ref = refs[:3]
            r_ref = None
            acc = refs[3] if nk > 1 else None
        else:
            a_ref, b_ref, r_ref, o_ref = refs[:4]
            acc = refs[4] if nk > 1 else None
        if exact:
            part = lax.dot_general(a_ref[...], b_ref[...], (dims, ((), ())), preferred_element_type=F32, precision=HI)
        else:
            part = _bdot(a_ref[...], b_ref[...], dims)

        def finish(val):
            if r_ref is not None:
                val = val + r_ref[...]
            o_ref[...] = val.astype(out_dtype)

        if nk == 1:
            finish(part)
        else:
            kk = pl.program_id(2)

            @pl.when(kk == 0)
            def _():
                acc[...] = part

            @pl.when(kk > 0)
            def _():
                acc[...] += part

            @pl.when(kk == nk - 1)
            def _():
                finish(acc[...])

    if mode == "tn":
        a_spec = pl.BlockSpec((tk, tm), lambda i, j, k: (k, i))
    elif a_halves:
        nh = K // 2 // tk
        a_spec = pl.BlockSpec((None, tm, tk), lambda i, j, k: (k // nh, i, k % nh))
    else:
        a_spec = pl.BlockSpec((tm, tk), lambda i, j, k: (i, k))
    if b_blocked:
        b_spec = (pl.BlockSpec((None, tk, tn), lambda i, j, k: (j, k, 0)) if mode == "nn"
                  else pl.BlockSpec((None, tn, tk), lambda i, j, k: (k, j, 0)))
    elif b_halves:
        nh = N // 2 // tn
        b_spec = pl.BlockSpec((None, tk, tn), lambda i, j, k: (j // nh, k, j % nh))
    elif mode == "nt":
        b_spec = pl.BlockSpec((tn, tk), lambda i, j, k: (j, k))
    else:
        b_spec = pl.BlockSpec((tk, tn), lambda i, j, k: (k, j))
    if out_blocked:
        o_spec, o_shape = pl.BlockSpec((None, tm, tn), lambda i, j, k: (j, i, 0)), SDS((N // tn, M, tn), out_dtype)
    else:
        o_spec, o_shape = pl.BlockSpec((tm, tn), lambda i, j, k: (i, j)), SDS((M, N), out_dtype)
    in_specs, args = [a_spec, b_spec], [a, b]
    if res is not None:
        in_specs.append(o_spec)
        args.append(res)
    return pl.pallas_call(
        body, grid=(M // tm, N // tn, nk), in_specs=in_specs, out_specs=o_spec, out_shape=o_shape,
        scratch_shapes=[pltpu.VMEM((tm, tn), F32)] if nk > 1 else [],
        compiler_params=_cparams(("parallel", "parallel", "arbitrary")), name=name)(*args)


def rms_fwd(x, g, name):
    T, D = x.shape
    tm = _fit(T, 512, 8)

    def body(x_ref, g_ref, o_ref):
        xv = x_ref[...]
        r = lax.rsqrt(jnp.mean(xv * xv, axis=-1, keepdims=True) + RMS_EPS)
        o_ref[...] = (xv * r * g_ref[...]).astype(BF16)

    return pl.pallas_call(
        body, grid=(T // tm,), in_specs=[pl.BlockSpec((tm, D), lambda i: (i, 0)), pl.BlockSpec((1, D), lambda i: (0, 0))],
        out_specs=pl.BlockSpec((tm, D), lambda i: (i, 0)), out_shape=SDS((T, D), BF16),
        compiler_params=_cparams(("parallel",)), name=name)(x, g)


def rms_bwd(x, g, dy, dres, name):
    T, D = x.shape
    tm = _fit(T, 256, 8)
    has_res = dres is not None

    def body(*refs):
        if has_res:
            x_ref, g_ref, dy_ref, dres_ref, dx_ref, dxb_ref, dg_ref = refs
        else:
            x_ref, g_ref, dy_ref, dx_ref, dxb_ref, dg_ref = refs
        xv, dyv = x_ref[...], dy_ref[...]
        r = lax.rsqrt(jnp.mean(xv * xv, axis=-1, keepdims=True) + RMS_EPS)
        xh = xv * r
        dxh = dyv * g_ref[...]
        dx = r * (dxh - xh * jnp.mean(dxh * xh, axis=-1, keepdims=True))
        if has_res:
            dx = dx + dres_ref[...]
        dx_ref[...] = dx
        dxb_ref[...] = dx.astype(BF16)

        @pl.when(pl.program_id(0) == 0)
        def _():
            dg_ref[...] = jnp.zeros_like(dg_ref)

        dg_ref[...] += jnp.sum(dyv * xh, axis=0, keepdims=True)

    row = pl.BlockSpec((tm, D), lambda i: (i, 0))
    vec = pl.BlockSpec((1, D), lambda i: (0, 0))
    args = [x, g, dy] + ([dres] if has_res else [])
    return pl.pallas_call(
        body, grid=(T // tm,), in_specs=[row, vec, row] + ([row] if has_res else []),
        out_specs=[row, row, vec], out_shape=[SDS((T, D), F32), SDS((T, D), BF16), SDS((1, D), F32)],
        compiler_params=_cparams(("arbitrary",)), name=name)(*args)


def swiglu_fwd(gu, name):
    T, F2 = gu.shape
    Fh = F2 // 2
    tm, tf = _fit(T, 512, 8), _fit(Fh, 512)
    nf = Fh // tf

    def body(g_ref, u_ref, o_ref):
        gv = g_ref[...]
        o_ref[...] = (gv * jax.nn.sigmoid(gv) * u_ref[...]).astype(BF16)

    return pl.pallas_call(
        body, grid=(T // tm, nf),
        in_specs=[pl.BlockSpec((tm, tf), lambda i, j: (i, j)), pl.BlockSpec((tm, tf), lambda i, j: (i, j + nf))],
        out_specs=pl.BlockSpec((tm, tf), lambda i, j: (i, j)), out_shape=SDS((T, Fh), BF16),
        compiler_params=_cparams(("parallel", "parallel")), name=name)(gu, gu)


def ffn_down_bwd(dy_b, w2, gu, name):
    T, D = dy_b.shape
    Fh = w2.shape[0]
    tm, tf = _fit(T, 1024, 8), _fit(Fh, 512)
    nf = Fh // tf

    def body(dy_ref, w_ref, g_ref, u_ref, o_ref):
        dav = _bdot(dy_ref[...], w_ref[...], NT)
        gv, uv = g_ref[...], u_ref[...]
        sg = jax.nn.sigmoid(gv)
        o_ref[0] = (dav * uv * (sg * (1.0 + gv * (1.0 - sg)))).astype(BF16)
        o_ref[1] = (dav * gv * sg).astype(BF16)

    return pl.pallas_call(
        body, grid=(T // tm, nf),
        in_specs=[pl.BlockSpec((tm, D), lambda i, j: (i, 0)), pl.BlockSpec((tf, D), lambda i, j: (j, 0)),
                  pl.BlockSpec((tm, tf), lambda i, j: (i, j)), pl.BlockSpec((tm, tf), lambda i, j: (i, j + nf))],
        out_specs=pl.BlockSpec((2, tm, tf), lambda i, j: (0, i, j)), out_shape=SDS((2, T, Fh), BF16),
        compiler_params=_cparams(("parallel", "parallel")), name=name)(dy_b, w2, gu, gu)


def loss_bwd(y, tgt, name):
    T, D = y.shape
    tm = _fit(T, 512, 8)

    def body(y_ref, t_ref, dy_ref, dyb_ref, l_ref):
        e = y_ref[...] - t_ref[...]
        dy = e * (1.0 / D)
        dy_ref[...] = dy
        dyb_ref[...] = dy.astype(BF16)

        @pl.when(pl.program_id(0) == 0)
        def _():
            l_ref[...] = jnp.zeros_like(l_ref)

        l_ref[...] += jnp.sum(jnp.sum(e * e, axis=1, keepdims=True), axis=0, keepdims=True)

    row = pl.BlockSpec((tm, D), lambda i: (i, 0))
    return pl.pallas_call(
        body, grid=(T // tm,), in_specs=[row, row], out_specs=[row, row, pl.BlockSpec((8, 128), lambda i: (0, 0))],
        out_shape=[SDS((T, D), F32), SDS((T, D), BF16), SDS((8, 128), F32)],
        compiler_params=_cparams(("arbitrary",)), name=name)(y, tgt)


def _log_sigmoid(z):
    return jnp.minimum(z, 0.0) - jnp.log(1.0 + jnp.exp(-jnp.abs(z)))


def _gla_chunk(q, k, v, tail, wg, bg, s_prev, cmat, mask):
    z = _bdot_vjp(tail, wg, NN) + bg
    la = _log_sigmoid(z) * (1.0 / GLA_TAU)
    cum = _tri_sum(cmat, la)
    last = jnp.sum(la, axis=0, keepdims=True)
    q_e = q * jnp.exp(cum) * (GLA_HK ** -0.5)
    k_e = k * jnp.exp(-cum)
    k_end = k * jnp.exp(last - cum)
    sc = jnp.where(mask > 0.5, _bdot_vjp(q_e, k_e, NT), 0.0)
    o = _bdot_vjp(sc, v, NN) + _bdot_vjp(q_e, s_prev, NN)
    kv = _bdot_vjp(k_end, v, TN)
    s_new = s_prev * jnp.transpose(jnp.exp(last)) + kv
    return o, s_new


def _gla_group(q, k, v, tail, wg, bg, s, cmat, mask, backward_dir):
    n = q.shape[0] // GLA_C
    outs = [None] * n
    for g in (range(n - 1, -1, -1) if backward_dir else range(n)):
        sl = slice(g * GLA_C, (g + 1) * GLA_C)
        outs[g], s = _gla_chunk(q[sl], k[sl], v[sl], tail[sl], wg, bg, s, cmat, mask)
    return jnp.concatenate(outs, axis=0), s


def _gla_consts(backward_dir):
    i = np.arange(GLA_C)
    if backward_dir:
        return (i[None, :] >= i[:, None]).astype(np.float32), (i[None, :] > i[:, None]).astype(np.float32)
    return (i[None, :] <= i[:, None]).astype(np.float32), (i[None, :] <= i[:, None]).astype(np.float32)


def _gla_in_specs(GC, nmap):
    return [
        pl.BlockSpec((GC, GLA_HK), lambda h, n: (nmap(n), h)),
        pl.BlockSpec((GC, GLA_HK), lambda h, n: (nmap(n), GLA_H + h)),
        pl.BlockSpec((GC, GLA_HV), lambda h, n: (nmap(n), GLA_DK * 2 // GLA_HV + h)),
        pl.BlockSpec((GC, 128), lambda h, n: (nmap(n), COL_TAIL // 128)),
        pl.BlockSpec((None, 128, 128), lambda h, n: (h, 0, 0)),
        pl.BlockSpec((None, 1, 128), lambda h, n: (h, 0, 0)),
        pl.BlockSpec((GLA_C, GLA_C), lambda h, n: (0, 0)),
        pl.BlockSpec((GLA_C, GLA_C), lambda h, n: (0, 0)),
    ]


def gla_fwd(proj, wgpad, bg, backward_dir, name):
    T = proj.shape[0]
    GC = min(GLA_G * GLA_C, T)
    NG = T // GC
    cmat, mask = _gla_consts(backward_dir)
    nmap = (lambda n: NG - 1 - n) if backward_dir else (lambda n: n)

    def body(q_ref, k_ref, v_ref, t_ref, wg_ref, bg_ref, c_ref, m_ref, o_ref, ss_ref, s_scr):
        @pl.when(pl.program_id(1) == 0)
        def _():
            s_scr[...] = jnp.zeros_like(s_scr)

        s0 = s_scr[...]
        ss_ref[...] = s0
        o, s1 = _gla_group(q_ref[...], k_ref[...], v_ref[...], t_ref[...], wg_ref[...], bg_ref[...], s0,
                           c_ref[...], m_ref[...], backward_dir)
        o_ref[...] = o
        s_scr[...] = s1

    return pl.pallas_call(
        body, grid=(GLA_H, NG), in_specs=_gla_in_specs(GC, nmap),
        out_specs=[pl.BlockSpec((GC, GLA_HV), lambda h, n: (nmap(n), h)),
                   pl.BlockSpec((None, None, GLA_HK, GLA_HV), lambda h, n: (h, nmap(n), 0, 0))],
        out_shape=[SDS((T, GLA_DV), F32), SDS((GLA_H, NG, GLA_HK, GLA_HV), F32)],
        scratch_shapes=[pltpu.VMEM((GLA_HK, GLA_HV), F32)],
        compiler_params=_cparams(("parallel", "arbitrary")), name=name)(proj, proj, proj, proj, wgpad, bg, cmat, mask)


def gla_bwd(proj, wgpad, bg, ssave, do, backward_dir, name):
    T = proj.shape[0]
    GC = min(GLA_G * GLA_C, T)
    NG = T // GC
    cmat, mask = _gla_consts(backward_dir)
    nmap = (lambda n: n) if backward_dir else (lambda n: NG - 1 - n)

    def body(q_ref, k_ref, v_ref, t_ref, wg_ref, bg_ref, c_ref, m_ref, ss_ref, do_ref,
             dq_ref, dk_ref, dv_ref, dt_ref, dwg_ref, dbg_ref, ds_scr):
        first = pl.program_id(1) == 0

        @pl.when(first)
        def _():
            ds_scr[...] = jnp.zeros_like(ds_scr)
            dwg_ref[...] = jnp.zeros_like(dwg_ref)
            dbg_ref[...] = jnp.zeros_like(dbg_ref)

        cm, mk = c_ref[...], m_ref[...]
        fn = lambda q, k, v, t, wg, b, s: _gla_group(q, k, v, t, wg, b, s, cm, mk, backward_dir)
        _, vjp = jax.vjp(fn, q_ref[...], k_ref[...], v_ref[...], t_ref[...], wg_ref[...], bg_ref[...], ss_ref[...])
        dq, dk, dv, dt, dwg, dbg, ds = vjp((do_ref[...], ds_scr[...]))
        dq_ref[...] = dq
        dk_ref[...] = dk
        dv_ref[...] = dv
        dt_ref[...] = dt
        dwg_ref[...] += dwg
        dbg_ref[...] += dbg
        ds_scr[...] = ds

    in_specs = _gla_in_specs(GC, nmap) + [
        pl.BlockSpec((None, None, GLA_HK, GLA_HV), lambda h, n: (h, nmap(n), 0, 0)),
        pl.BlockSpec((GC, GLA_HV), lambda h, n: (nmap(n), h)),
    ]
    out_specs = [
        pl.BlockSpec((GC, GLA_HK), lambda h, n: (nmap(n), h)),
        pl.BlockSpec((GC, GLA_HK), lambda h, n: (nmap(n), h)),
        pl.BlockSpec((GC, GLA_HV), lambda h, n: (nmap(n), h)),
        pl.BlockSpec((None, GC, 128), lambda h, n: (h, nmap(n), 0)),
        pl.BlockSpec((None, 128, 128), lambda h, n: (h, 0, 0)),
        pl.BlockSpec((None, 1, 128), lambda h, n: (h, 0, 0)),
    ]
    out_shape = [SDS((T, GLA_DK), F32), SDS((T, GLA_DK), F32), SDS((T, GLA_DV), F32), SDS((GLA_H, T, 128), F32),
                 SDS((GLA_H, 128, 128), F32), SDS((GLA_H, 1, 128), F32)]
    return pl.pallas_call(
        body, grid=(GLA_H, NG), in_specs=in_specs, out_specs=out_specs, out_shape=out_shape,
        scratch_shapes=[pltpu.VMEM((GLA_HK, GLA_HV), F32)],
        compiler_params=_cparams(("parallel", "arbitrary")), name=name)(proj, proj, proj, proj, wgpad, bg, cmat, mask, ssave, do)


def _block_diag(width, hd):
    i = np.arange(width)
    return ((i[:, None] // hd) == (i[None, :] // hd)).astype(np.float32) / hd


def _norm_heads(t, hd):
    outs = []
    for h in range(t.shape[1] // hd):
        th = t[:, h * hd:(h + 1) * hd]
        outs.append(th * lax.rsqrt(jnp.mean(th * th, axis=-1, keepdims=True) + RMS_EPS))
    return jnp.concatenate(outs, axis=1)


def _norm_bd(t, bd):
    return t * lax.rsqrt(jnp.dot(t * t, bd, preferred_element_type=F32, precision=HI) + RMS_EPS)


def _pre_fn(nq, nk, mq, gq, gk, gm, bd):
    return _norm_bd(nq, bd) * gq, _norm_bd(nk, bd) * gk, _norm_heads(mq, MEM_HD) * gm


def pre_fwd(proj, gq, gk, gm, name):
    T = proj.shape[0]
    tm = _fit(T, 512, 8)
    bd = _block_diag(NA_W, NA_HD)

    def body(nq_ref, nk_ref, nv_ref, mq_ref, gq_ref, gk_ref, gm_ref, bd_ref, q_ref, k_ref, v_ref, m_ref):
        qn, kn, mn = _pre_fn(nq_ref[...], nk_ref[...], mq_ref[...], gq_ref[...], gk_ref[...], gm_ref[...], bd_ref[...])
        q_ref[...] = qn.astype(BF16)
        k_ref[...] = kn.astype(BF16)
        v_ref[...] = nv_ref[...].astype(BF16)
        m_ref[...] = mn.astype(BF16)

    col = lambda c0: pl.BlockSpec((tm, 512), lambda i: (i, c0 // 512))
    vec = pl.BlockSpec((1, 512), lambda i: (0, 0))
    row = pl.BlockSpec((tm, 512), lambda i: (i, 0))
    return pl.pallas_call(
        body, grid=(T // tm,),
        in_specs=[col(COL_NQ), col(COL_NK), col(COL_NV), col(COL_MQ), vec, vec, vec, pl.BlockSpec((NA_W, NA_W), lambda i: (0, 0))],
        out_specs=[row] * 4, out_shape=[SDS((T, 512), BF16)] * 4,
        compiler_params=_cparams(("parallel",)), name=name)(proj, proj, proj, proj, gq, gk, gm, bd)


def pre_bwd(proj, gq, gk, gm, d_qn, d_kn, d_nv, d_mn, dq_f, dq_b, dk_f, dk_b, dv_f, dv_b, d_r, dt_f, dt_b, name):
    T = proj.shape[0]
    tm = _fit(T, 256, 8)
    bd = _block_diag(NA_W, NA_HD)

    def body(nq_ref, nk_ref, mq_ref, gq_ref, gk_ref, gm_ref, bd_ref, dqn_ref, dkn_ref, dnv_ref, dmn_ref,
             dqf_ref, dqb_ref, dkf_ref, dkb_ref, dvf_ref, dvb_ref, dr_ref, dtf_ref, dtb_ref,
             o_ref, dgq_ref, dgk_ref, dgm_ref):
        bdv = bd_ref[...]
        fn = lambda a, b, c, x, y, z: _pre_fn(a, b, c, x, y, z, bdv)
        _, vjp = jax.vjp(fn, nq_ref[...], nk_ref[...], mq_ref[...], gq_ref[...], gk_ref[...], gm_ref[...])
        d_nq, d_nk, d_mq, dgq, dgk, dgm = vjp((dqn_ref[...], dkn_ref[...], dmn_ref[...]))
        o_ref[:, 0:512] = (dqf_ref[...] + dqb_ref[...]).astype(BF16)
        o_ref[:, 512:1024] = (dkf_ref[...] + dkb_ref[...]).astype(BF16)
        o_ref[:, 1024:2048] = (dvf_ref[...] + dvb_ref[...]).astype(BF16)
        o_ref[:, COL_R:COL_R + 1024] = dr_ref[...].astype(BF16)
        o_ref[:, COL_NQ:COL_NQ + 512] = d_nq.astype(BF16)
        o_ref[:, COL_NK:COL_NK + 512] = d_nk.astype(BF16)
        o_ref[:, COL_NV:COL_NV + 512] = dnv_ref[...].astype(BF16)
        o_ref[:, COL_MQ:COL_MQ + 512] = d_mq.astype(BF16)
        dt = dtf_ref[0] + dtb_ref[0]
        for h in range(1, GLA_H):
            dt = dt + dtf_ref[h] + dtb_ref[h]
        o_ref[:, COL_TAIL:COL_TAIL + 128] = dt.astype(BF16)
        o_ref[:, COL_TAIL + 128:PC] = jnp.zeros((tm, PC - COL_TAIL - 128), BF16)

        @pl.when(pl.program_id(0) == 0)
        def _():
            dgq_ref[...] = jnp.zeros_like(dgq_ref)
            dgk_ref[...] = jnp.zeros_like(dgk_ref)
            dgm_ref[...] = jnp.zeros_like(dgm_ref)

        dgq_ref[...] += dgq
        dgk_ref[...] += dgk
        dgm_ref[...] += dgm

    col = lambda c0: pl.BlockSpec((tm, 512), lambda i: (i, c0 // 512))
    vec = pl.BlockSpec((1, 512), lambda i: (0, 0))
    r512 = pl.BlockSpec((tm, 512), lambda i: (i, 0))
    r1024 = pl.BlockSpec((tm, 1024), lambda i: (i, 0))
    tl = pl.BlockSpec((GLA_H, tm, 128), lambda i: (0, i, 0))
    in_specs = [col(COL_NQ), col(COL_NK), col(COL_MQ), vec, vec, vec, pl.BlockSpec((NA_W, NA_W), lambda i: (0, 0)),
                r512, r512, r512, r512, r512, r512, r512, r512, r1024, r1024, r1024, tl, tl]
    return pl.pallas_call(
        body, grid=(T // tm,), in_specs=in_specs,
        out_specs=[pl.BlockSpec((tm, PC), lambda i: (i, 0)), vec, vec, vec],
        out_shape=[SDS((T, PC), BF16), SDS((1, 512), F32), SDS((1, 512), F32), SDS((1, 512), F32)],
        compiler_params=_cparams(("arbitrary",)), name=name)(
            proj, proj, proj, gq, gk, gm, bd, d_qn, d_kn, d_nv, d_mn, dq_f, dq_b, dk_f, dk_b, dv_f, dv_b, d_r, dt_f, dt_b)


def _post_fn(o_f, o_b, r, o_na, o_mem, g_gla, g_na, g_mem, bd):
    y_gla = _norm_heads(o_f + o_b, GLA_HV) * g_gla * (r * jax.nn.sigmoid(r))
    y_na = _norm_bd(o_na, bd) * g_na
    y_mem = _norm_heads(o_mem, MEM_HD) * g_mem
    return jnp.concatenate([y_gla, y_na, y_mem], axis=1)


def post_fwd(o_f, o_b, proj, o_na, o_mem, g_gla, g_na, g_mem, name):
    T = proj.shape[0]
    tm = _fit(T, 256, 8)
    bd = _block_diag(NA_W, NA_HD)

    def body(of_ref, ob_ref, r_ref, ona_ref, omem_ref, gg_ref, gn_ref, gm_ref, bd_ref, y_ref):
        y_ref[...] = _post_fn(of_ref[...], ob_ref[...], r_ref[...], ona_ref[...], omem_ref[...],
                              gg_ref[...], gn_ref[...], gm_ref[...], bd_ref[...]).astype(BF16)

    r1024 = pl.BlockSpec((tm, 1024), lambda i: (i, 0))
    r512 = pl.BlockSpec((tm, 512), lambda i: (i, 0))
    in_specs = [r1024, r1024, pl.BlockSpec((tm, 1024), lambda i: (i, COL_R // 1024)), r512, r512,
                pl.BlockSpec((1, 1024), lambda i: (0, 0)), pl.BlockSpec((1, 512), lambda i: (0, 0)),
                pl.BlockSpec((1, 512), lambda i: (0, 0)), pl.BlockSpec((NA_W, NA_W), lambda i: (0, 0))]
    return pl.pallas_call(
        body, grid=(T // tm,), in_specs=in_specs, out_specs=pl.BlockSpec((tm, D_MODEL), lambda i: (i, 0)),
        out_shape=SDS((T, D_MODEL), BF16), compiler_params=_cparams(("parallel",)), name=name)(
            o_f, o_b, proj, o_na, o_mem, g_gla, g_na, g_mem, bd)


def post_bwd(o_f, o_b, proj, o_na, o_mem, g_gla, g_na, g_mem, dy, name):
    T = proj.shape[0]
    tm = _fit(T, 256, 8)
    bd = _block_diag(NA_W, NA_HD)

    def body(of_ref, ob_ref, r_ref, ona_ref, omem_ref, gg_ref, gn_ref, gm_ref, bd_ref, dy_ref,
             do_ref, dr_ref, dna_ref, dmem_ref, dgg_ref, dgn_ref, dgm_ref):
        bdv = bd_ref[...]
        fn = lambda o, r, a, m, x, y, z: _post_fn(o, 0.0, r, a, m, x, y, z, bdv)
        _, vjp = jax.vjp(fn, of_ref[...] + ob_ref[...], r_ref[...], ona_ref[...], omem_ref[...],
                         gg_ref[...], gn_ref[...], gm_ref[...])
        d_o, d_r, d_na, d_mem, dgg, dgn, dgm = vjp(dy_ref[...])
        do_ref[...] = d_o
        dr_ref[...] = d_r
        dna_ref[...] = d_na
        dmem_ref[...] = d_mem

        @pl.when(pl.program_id(0) == 0)
        def _():
            dgg_ref[...] = jnp.zeros_like(dgg_ref)
            dgn_ref[...] = jnp.zeros_like(dgn_ref)
            dgm_ref[...] = jnp.zeros_like(dgm_ref)

        dgg_ref[...] += dgg
        dgn_ref[...] += dgn
        dgm_ref[...] += dgm

    r1024 = pl.BlockSpec((tm, 1024), lambda i: (i, 0))
    r512 = pl.BlockSpec((tm, 512), lambda i: (i, 0))
    v1024 = pl.BlockSpec((1, 1024), lambda i: (0, 0))
    v512 = pl.BlockSpec((1, 512), lambda i: (0, 0))
    in_specs = [r1024, r1024, pl.BlockSpec((tm, 1024), lambda i: (i, COL_R // 1024)), r512, r512, v1024, v512, v512,
                pl.BlockSpec((NA_W, NA_W), lambda i: (0, 0)), pl.BlockSpec((tm, D_MODEL), lambda i: (i, 0))]
    return pl.pallas_call(
        body, grid=(T // tm,), in_specs=in_specs, out_specs=[r1024, r1024, r512, r512, v1024, v512, v512],
        out_shape=[SDS((T, 1024), F32), SDS((T, 1024), F32), SDS((T, 512), F32), SDS((T, 512), F32),
                   SDS((1, 1024), F32), SDS((1, 512), F32), SDS((1, 512), F32)],
        compiler_params=_cparams(("arbitrary",)), name=name)(o_f, o_b, proj, o_na, o_mem, g_gla, g_na, g_mem, bd, dy)


NA_RB = 8


def _na_row_scores(q_ref, k_ref, v_ref, tb_ref, rb, j, n_rows):
    r = rb * NA_RB + j
    rs = jnp.clip(r - NA_ROWS // 2, 0, n_rows - NA_ROWS)
    dr0 = rs - r + (NA_ROWS - 1)
    tok = pl.ds(pl.multiple_of(rs * GRID_W, GRID_W), NA_ROWS * GRID_W)
    q = q_ref[j * GRID_W:(j + 1) * GRID_W, :]
    kk, vv = k_ref[tok, :], v_ref[tok, :]
    bias = jnp.concatenate([tb_ref[dr0 + 2 * i] for i in range(NA_ROWS // 2)], axis=1)
    s = _bdot(q, kk, NT) * (NA_HD ** -0.5) + bias
    m = jnp.max(s, axis=1, keepdims=True)
    p = jnp.exp(s - m)
    l = jnp.sum(p, axis=1, keepdims=True)
    return q, kk, vv, p, l, tok, dr0


def natten_fwd(q, k, v, tb2, name):
    H, T, hd = q.shape
    n_rows = T // GRID_W
    rbt = NA_RB * GRID_W

    def body(q_ref, k_ref, v_ref, tb_ref, o_ref):
        rb = pl.program_id(1)
        for j in range(NA_RB):
            _, _, vv, p, l, _, _ = _na_row_scores(q_ref, k_ref, v_ref, tb_ref, rb, j, n_rows)
            o_ref[j * GRID_W:(j + 1) * GRID_W, :] = _bdot(p, vv, NN) / l

    whole = pl.BlockSpec((None, T, hd), lambda h, r: (h, 0, 0))
    blk = pl.BlockSpec((None, rbt, hd), lambda h, r: (h, r, 0))
    return pl.pallas_call(
        body, grid=(H, n_rows // NA_RB),
        in_specs=[blk, whole, whole, pl.BlockSpec((None, 2 * NA_ROWS - 2, GRID_W, 2 * GRID_W), lambda h, r: (h, 0, 0, 0))],
        out_specs=blk, out_shape=SDS((H, T, hd), F32),
        compiler_params=_cparams(("parallel", "arbitrary")), name=name)(q, k, v, tb2)


def natten_bwd(q, k, v, tb2, do, name):
    H, T, hd = q.shape
    n_rows = T // GRID_W
    rbt = NA_RB * GRID_W
    scale = NA_HD ** -0.5

    def body(q_ref, k_ref, v_ref, tb_ref, do_ref, dq_ref, dk_ref, dv_ref, dtb_ref):
        rb = pl.program_id(1)

        @pl.when(rb == 0)
        def _():
            dk_ref[...] = jnp.zeros_like(dk_ref)
            dv_ref[...] = jnp.zeros_like(dv_ref)
            dtb_ref[...] = jnp.zeros_like(dtb_ref)

        for j in range(NA_RB):
            qv, kk, vv, p, l, tok, dr0 = _na_row_scores(q_ref, k_ref, v_ref, tb_ref, rb, j, n_rows)
            p = p / l
            dov = do_ref[j * GRID_W:(j + 1) * GRID_W, :]
            dp = _bdot(dov, vv, NT)
            ds = p * (dp - jnp.sum(dp * p, axis=1, keepdims=True))
            dq_ref[j * GRID_W:(j + 1) * GRID_W, :] = _bdot(ds, kk, NN) * scale
            dk_ref[tok, :] += _bdot(ds, qv, TN) * scale
            dv_ref[tok, :] += _bdot(p, dov, TN)
            for i in range(NA_ROWS // 2):
                dtb_ref[dr0 + 2 * i] += ds[:, 2 * GRID_W * i:2 * GRID_W * (i + 1)]

    whole = pl.BlockSpec((None, T, hd), lambda h, r: (h, 0, 0))
    blk = pl.BlockSpec((None, rbt, hd), lambda h, r: (h, r, 0))
    tbs = pl.BlockSpec((None, 2 * NA_ROWS - 2, GRID_W, 2 * GRID_W), lambda h, r: (h, 0, 0, 0))
    return pl.pallas_call(
        body, grid=(H, n_rows // NA_RB), in_specs=[blk, whole, whole, tbs, blk],
        out_specs=[blk, whole, whole, tbs],
        out_shape=[SDS((H, T, hd), F32), SDS((H, T, hd), F32), SDS((H, T, hd), F32), SDS(tb2.shape, F32)],
        compiler_params=_cparams(("parallel", "arbitrary")), name=name)(q, k, v, tb2, do)


def _rpb_expand_consts():
    qc = np.arange(GRID_W)[:, None]
    kc = np.arange(GRID_W)[None, :]
    cs = np.clip(qc - NA_COLS // 2, 0, GRID_W - NA_COLS)
    inside = (kc >= cs) & (kc < cs + NA_COLS)
    dc = np.clip(kc - qc, -(NA_COLS - 1), NA_COLS - 1) + (NA_COLS - 1)
    e = np.zeros((128, GRID_W * GRID_W), np.float32)
    flat = (qc * GRID_W + kc)
    e[dc[inside], flat[inside]] = 1.0
    neg = np.where(inside, 0.0, NEG).astype(np.float32).reshape(1, -1)
    return e, neg


def _rpb_fold_consts():
    sa = np.zeros((NA_H * 15, NA_H * 14), np.float32)
    sb = np.zeros((NA_H * 15, NA_H * 14), np.float32)
    for h in range(NA_H):
        for d in range(14):
            sa[h * 15 + d, h * 14 + d] = 1.0
            sb[h * 15 + d + 1, h * 14 + d] = 1.0
    return sa, sb


def rpb_table(rpb, name):
    e, neg = _rpb_expand_consts()
    rp = jnp.pad(rpb.reshape(NA_H * 15, 31), ((0, 0), (0, 128 - 31)))

    def body(r_ref, e_ref, n_ref, o_ref):
        o_ref[...] = jnp.dot(r_ref[...], e_ref[...], preferred_element_type=F32, precision=HI) + n_ref[...]

    t = pl.pallas_call(body, out_shape=SDS((NA_H * 15, GRID_W * GRID_W), F32), name=name)(rp, e, neg)
    t = t.reshape(NA_H, 15, GRID_W, GRID_W)
    return jnp.concatenate([t[:, :14], t[:, 1:]], axis=-1)


def rpb_table_bwd(dtb2, name):
    e, _ = _rpb_expand_consts()
    sa, sb = _rpb_fold_consts()
    a = dtb2[..., :GRID_W].reshape(NA_H * 14, GRID_W * GRID_W)
    b = dtb2[..., GRID_W:].reshape(NA_H * 14, GRID_W * GRID_W)

    def body(a_ref, b_ref, e_ref, sa_ref, sb_ref, o_ref):
        ev = e_ref[...]
        pa = lax.dot_general(a_ref[...], ev, (NT, ((), ())), preferred_element_type=F32, precision=HI)
        pb = lax.dot_general(b_ref[...], ev, (NT, ((), ())), preferred_element_type=F32, precision=HI)
        o_ref[...] = (jnp.dot(sa_ref[...], pa, preferred_element_type=F32, precision=HI)
                      + jnp.dot(sb_ref[...], pb, preferred_element_type=F32, precision=HI))

    d = pl.pallas_call(body, out_shape=SDS((NA_H * 15, 128), F32), name=name)(a, b, e, sa, sb)
    return d[:, :31].reshape(NA_H, 15, 31)


def _kprep_fn(kv, gk):
    return _norm_heads(kv[:, :MEM_W], MEM_HD) * gk, kv[:, MEM_W:]


def mem_kprep(kv, gk, name):
    def body(kv_ref, g_ref, k_ref, v_ref):
        kn, vv = _kprep_fn(kv_ref[...], g_ref[...])
        k_ref[...] = kn.astype(BF16)
        v_ref[...] = vv.astype(BF16)

    return pl.pallas_call(body, out_shape=[SDS((MEM_TOK, MEM_W), BF16)] * 2, name=name)(kv, gk)


def mem_kprep_bwd(kv, gk, dk, dv, name):
    def body(kv_ref, g_ref, dk_ref, dv_ref, dkv_ref, dg_ref):
        _, vjp = jax.vjp(_kprep_fn, kv_ref[...], g_ref[...])
        dkv, dg = vjp((dk_ref[...], dv_ref[...]))
        dkv_ref[...] = dkv.astype(BF16)
        dg_ref[...] = dg

    return pl.pallas_call(body, out_shape=[SDS((MEM_TOK, 2 * MEM_W), BF16), SDS((1, MEM_W), F32)], name=name)(kv, gk, dk, dv)


def _mem_probs(q_ref, k_ref, h):
    hs = slice(h * MEM_HD, (h + 1) * MEM_HD)
    qh, kh = q_ref[:, hs], k_ref[:, hs]
    s = _bdot(qh, kh, NT) * (MEM_HD ** -0.5)
    p = jnp.exp(s - jnp.max(s, axis=1, keepdims=True))
    return hs, qh, kh, p, jnp.sum(p, axis=1, keepdims=True)


def mem_attn_fwd(q, km, vm, name):
    T = q.shape[0]
    tm = _fit(T, 512, 8)

    def body(q_ref, k_ref, v_ref, o_ref):
        for h in range(MEM_H):
            hs, _, _, p, l = _mem_probs(q_ref, k_ref, h)
            o_ref[:, hs] = _bdot(p, v_ref[:, hs], NN) / l

    row = pl.BlockSpec((tm, MEM_W), lambda i: (i, 0))
    full = pl.BlockSpec((MEM_TOK, MEM_W), lambda i: (0, 0))
    return pl.pallas_call(body, grid=(T // tm,), in_specs=[row, full, full], out_specs=row, out_shape=SDS((T, MEM_W), F32),
                          compiler_params=_cparams(("parallel",)), name=name)(q, km, vm)


def mem_attn_bwd(q, km, vm, do, name):
    T = q.shape[0]
    tm = _fit(T, 512, 8)
    scale = MEM_HD ** -0.5

    def body(q_ref, k_ref, v_ref, do_ref, dq_ref, dk_ref, dv_ref):
        @pl.when(pl.program_id(0) == 0)
        def _():
            dk_ref[...] = jnp.zeros_like(dk_ref)
            dv_ref[...] = jnp.zeros_like(dv_ref)

        for h in range(MEM_H):
            hs, qh, kh, p, l = _mem_probs(q_ref, k_ref, h)
            p = p / l
            dov = do_ref[:, hs]
            dp = _bdot(dov, v_ref[:, hs], NT)
            ds = p * (dp - jnp.sum(dp * p, axis=1, keepdims=True))
            dq_ref[:, hs] = _bdot(ds, kh, NN) * scale
            dk_ref[:, hs] += _bdot(ds, qh, TN) * scale
            dv_ref[:, hs] += _bdot(p, dov, TN)

    row = pl.BlockSpec((tm, MEM_W), lambda i: (i, 0))
    full = pl.BlockSpec((MEM_TOK, MEM_W), lambda i: (0, 0))
    return pl.pallas_call(
        body, grid=(T // tm,), in_specs=[row, full, full, row], out_specs=[row, full, full],
        out_shape=[SDS((T, MEM_W), F32), SDS((MEM_TOK, MEM_W), F32), SDS((MEM_TOK, MEM_W), F32)],
        compiler_params=_cparams(("arbitrary",)), name=name)(q, km, vm, do)


def sum_slots(a, name, rows=2048):
    n, R, _ = a.shape
    tr = _fit(R, rows, 16)

    def body(a_ref, o_ref):
        s = a_ref[0].astype(F32)
        for i in range(1, n):
            s = s + a_ref[i].astype(F32)
        o_ref[...] = s

    return pl.pallas_call(body, grid=(R // tr,), in_specs=[pl.BlockSpec((n, tr, 128), lambda i: (0, i, 0))],
                          out_specs=pl.BlockSpec((tr, 128), lambda i: (i, 0)), out_shape=SDS((R, 128), F32),
                          compiler_params=_cparams(("parallel",)), name=name)(a)


def pair_sum(g, land, c_idx, name, rows=4096):
    _, R, _ = g.shape
    tr = _fit(R, rows, 16)

    def body(c_ref, g_ref, l_ref, o_ref):
        o_ref[...] = (g_ref[...].astype(F32) + l_ref[...].astype(F32)).astype(o_ref.dtype)

    blk = pl.BlockSpec((None, tr, 128), lambda k, i, c: (k, i, 0))
    gs = pltpu.PrefetchScalarGridSpec(
        num_scalar_prefetch=1, grid=(4, R // tr),
        in_specs=[pl.BlockSpec((None, tr, 128), lambda k, i, c: (2 * k + c[0], i, 0)), blk], out_specs=blk)
    return pl.pallas_call(body, grid_spec=gs, out_shape=SDS((4, R, 128), g.dtype),
                          compiler_params=_cparams(("parallel", "parallel")), name=name)(c_idx, g, land)


def adamw(w, g, m, v, name):
    R, Cc = w.shape
    tr = _fit(R, max(8, (262144 // max(Cc, 128)) // 8 * 8), 8)
    c1 = 1.0 - ADAM_B1 ** ADAM_STEP
    c2 = 1.0 - ADAM_B2 ** ADAM_STEP

    def body(w_ref, g_ref, m_ref, v_ref, d_ref, mo_ref, vo_ref):
        gv = g_ref[...]
        m2 = ADAM_B1 * m_ref[...] + (1.0 - ADAM_B1) * gv
        v2 = ADAM_B2 * v_ref[...] + (1.0 - ADAM_B2) * (gv * gv)
        d_ref[...] = -ADAM_LR * ((m2 / c1) / (jnp.sqrt(v2 / c2) + ADAM_EPS) + ADAM_WD * w_ref[...])
        mo_ref[...] = m2
        vo_ref[...] = v2

    blk = pl.BlockSpec((tr, Cc), lambda i: (i, 0))
    return pl.pallas_call(body, grid=(R // tr,), in_specs=[blk] * 4, out_specs=[blk] * 3, out_shape=[SDS((R, Cc), F32)] * 3,
                          compiler_params=_cparams(("parallel",)), name=name)(w, g, m, v)


ANY = pl.BlockSpec(memory_space=pl.ANY)


def _my_pos():
    return lax.axis_index("x"), lax.axis_index("y"), lax.axis_index("c")


def all_gather(x_shard, name):
    def body(x_ref, out_ref, send_sems, recv_sems, local_sem):
        x, y, c = _my_pos()
        me, sibling = (x, y, c), (x, y, 1 - c)
        chips = [(1 - x, y), (x, 1 - y), (1 - x, 1 - y)]

        def slot(px, py, pc):
            return out_ref.at[4 * px + 2 * py + pc]

        def copy(k, block, to, src=None):
            return pltpu.make_async_remote_copy(
                src_ref=slot(*block) if src is None else src, dst_ref=slot(*block),
                send_sem=send_sems.at[k], recv_sem=recv_sems.at[k], device_id=to, device_id_type=MESH)

        mine = pltpu.make_async_copy(x_ref, slot(*me), local_sem)
        mine.start()
        first = [copy(0, me, sibling, src=x_ref)]
        first += [copy(1 + j, me, (*chip, c), src=x_ref) for j, chip in enumerate(chips)]
        for cp in first:
            cp.start()
        passed = [copy(4 + j, (*chip, c), sibling) for j, chip in enumerate(chips)]
        for j, chip in enumerate(chips):
            copy(1 + j, (*chip, c), me).wait_recv()
            passed[j].start()
        copy(0, sibling, me).wait_recv()
        for j, chip in enumerate(chips):
            copy(4 + j, (*chip, 1 - c), me).wait_recv()
        for cp in first + passed:
            cp.wait_send()
        mine.wait()

    return pl.pallas_call(
        body, out_shape=SDS((N_DEV,) + x_shard.shape, x_shard.dtype), in_specs=[ANY], out_specs=ANY,
        scratch_shapes=[pltpu.SemaphoreType.DMA((7,)), pltpu.SemaphoreType.DMA((7,)), pltpu.SemaphoreType.DMA],
        name=name)(x_shard)


def pair_exchange(g, name):
    _, R, L = g.shape

    def body(g_ref, land_ref, send_sems, recv_sems):
        x, y, c = _my_pos()
        sibling = (x, y, 1 - c)
        copies = [pltpu.make_async_remote_copy(
            src_ref=g_ref.at[2 * k + (1 - c)], dst_ref=land_ref.at[k], send_sem=send_sems.at[k], recv_sem=recv_sems.at[k],
            device_id=sibling, device_id_type=MESH) for k in range(4)]
        for cp in copies:
            cp.start()
        for cp in copies:
            cp.wait_recv()
        for cp in copies:
            cp.wait_send()

    return pl.pallas_call(
        body, out_shape=SDS((4, R, L), g.dtype), in_specs=[ANY], out_specs=ANY,
        scratch_shapes=[pltpu.SemaphoreType.DMA((4,)), pltpu.SemaphoreType.DMA((4,))], name=name)(g)


def chip_exchange(s, name):
    _, R, L = s.shape

    def body(s_ref, land_ref, send_sems, recv_sems, local_sem):
        x, y, c = _my_pos()
        my_chip = 2 * x + y
        chips = [(1 - x, y), (x, 1 - y), (1 - x, 1 - y)]
        own = pltpu.make_async_copy(s_ref.at[my_chip], land_ref.at[my_chip], local_sem)
        own.start()
        sends = [pltpu.make_async_remote_copy(
            src_ref=s_ref.at[2 * px + py], dst_ref=land_ref.at[my_chip], send_sem=send_sems.at[j], recv_sem=recv_sems.at[j],
            device_id=(px, py, c), device_id_type=MESH) for j, (px, py) in enumerate(chips)]
        for cp in sends:
            cp.start()
        for j, (px, py) in enumerate(chips):
            pltpu.make_async_remote_copy(
                src_ref=s_ref.at[my_chip], dst_ref=land_ref.at[2 * px + py], send_sem=send_sems.at[j], recv_sem=recv_sems.at[j],
                device_id=(px, py, c), device_id_type=MESH).wait_recv()
        for cp in sends:
            cp.wait_send()
        own.wait()

    return pl.pallas_call(
        body, out_shape=SDS((4, R, L), s.dtype), in_specs=[ANY], out_specs=ANY,
        scratch_shapes=[pltpu.SemaphoreType.DMA((3,)), pltpu.SemaphoreType.DMA((3,)), pltpu.SemaphoreType.DMA],
        name=name)(s)


W_IN_SHARD, W_IN_PACKED = 644, 768
SHARDED = (("ffn_w13", 2048, 1408, 1, 1408), ("ffn_w2", 704, 2048, 0, 2048), ("w_out", 256, 2048, 0, 2048),
           ("mem_wkv", 256, 1024, 0, 1024), ("w_in", 2048, W_IN_SHARD, 1, W_IN_PACKED), ("gla_wg2_f", 16, 64, 1, 64),
           ("gla_wg2_b", 16, 64, 1, 64))
SHARD_NAMES = tuple(s[0] for s in SHARDED)
SHARD_ELEMS = sum(a * bp for _, a, _, _, bp in SHARDED)
PACK_ROWS = -(-SHARD_ELEMS // (128 * 4096)) * 4096


def _to_pack(segs, lead):
    pad = jnp.zeros(lead + (PACK_ROWS * 128 - SHARD_ELEMS,), segs[0].dtype)
    return jnp.concatenate(segs + [pad], axis=-1).reshape(lead + (PACK_ROWS, 128))


def _permuted_ranges(c0, c1):
    res = []
    for o0, o1, p0 in ((0, ORIG_GATE0, 0), (ORIG_GATE0, ORIG_AFTER_GATE, COL_TAIL), (ORIG_AFTER_GATE, IN_COLS, ORIG_GATE0)):
        lo, hi = max(c0, o0), min(c1, o1)
        if lo < hi:
            res.append((p0 + lo - o0, p0 + hi - o0))
    return res


def assemble_w_in(blocks):
    placed = []
    for d in range(N_DEV):
        c = d * W_IN_SHARD
        for p0, p1 in _permuted_ranges(c, c + W_IN_SHARD):
            placed.append((p0, blocks[d][:, c - d * W_IN_SHARD:c - d * W_IN_SHARD + (p1 - p0)]))
            c += p1 - p0
    placed.sort(key=lambda t: t[0])
    return jnp.concatenate([t[1] for t in placed] + [jnp.zeros((blocks.shape[1], PC - IN_COLS), blocks.dtype)], axis=1)


def split_w_in_grad(g):
    pad = jnp.zeros((g.shape[0], W_IN_PACKED - W_IN_SHARD), g.dtype)
    return jnp.stack([jnp.concatenate([g[:, p0:p1] for p0, p1 in _permuted_ranges(d * W_IN_SHARD, (d + 1) * W_IN_SHARD)]
                                      + [pad], axis=1) for d in range(N_DEV)])


def pack_shards(shards):
    segs = [jnp.pad(shards[n], ((0, 0), (0, bp - b))).reshape(-1) for n, _, b, _, bp in SHARDED]
    return _to_pack(segs, ())


def unpack_gathered(buf):
    flat = buf.reshape(N_DEV, PACK_ROWS * 128)
    out, off = {}, 0
    for n, a, b, axis, bp in SHARDED:
        seg = flat[:, off:off + a * bp].reshape(N_DEV, a, bp)
        off += a * bp
        if n == "ffn_w13":
            out[n] = seg
        elif n == "w_in":
            out[n] = assemble_w_in(seg)
        else:
            out[n] = seg.reshape(N_DEV * a, b) if axis == 0 else seg.transpose(1, 0, 2).reshape(a, N_DEV * b)
    return out


def pack_full_grads(grads):
    segs = []
    for n, a, b, axis, bp in SHARDED:
        g = grads[n].astype(BF16)
        if n == "w_in":
            g = split_w_in_grad(g)
        elif n != "ffn_w13":
            g = g.reshape(N_DEV, a, b) if axis == 0 else g.reshape(a, N_DEV, b).transpose(1, 0, 2)
        segs.append(g.reshape(N_DEV, a * bp))
    return _to_pack(segs, (N_DEV,))


def unpack_shard_grads(flat):
    flat = flat.reshape(-1)
    out, off = {}, 0
    for n, a, b, _, bp in SHARDED:
        out[n] = flat[off:off + a * bp].reshape(a, bp)[:, :b]
        off += a * bp
    return out


def pad_gate_weight(wg2, backward_dir):
    r0 = GLA_RANK if backward_dir else 0
    w = wg2.astype(F32).reshape(GLA_RANK, GLA_H, GLA_HK).transpose(1, 0, 2)
    return jnp.pad(w, ((0, 0), (r0, 128 - GLA_RANK - r0), (0, 0)))


def unpad_gate_grad(dw, backward_dir):
    r0 = GLA_RANK if backward_dir else 0
    return dw[:, r0:r0 + GLA_RANK, :].transpose(1, 0, 2).reshape(GLA_RANK, GLA_DK)


def to_heads(t):
    T = t.shape[0]
    return t.reshape(T, NA_H, NA_HD).transpose(1, 0, 2)


def from_heads(t):
    return t.transpose(1, 0, 2).reshape(t.shape[1], NA_W)


REPLICATED = ("attn_norm", "gla_bg_f", "gla_bg_b", "gla_out_norm", "na_q_norm", "na_k_norm", "na_rpb", "na_out_norm",
              "mem_norm", "mem_q_norm", "mem_k_norm", "mem_out_norm", "ffn_norm")
WEIGHTS = ("attn_norm", "w_in", "gla_wg2_f", "gla_bg_f", "gla_wg2_b", "gla_bg_b", "gla_out_norm", "na_q_norm", "na_k_norm",
           "na_rpb", "na_out_norm", "mem_norm", "mem_wkv", "mem_q_norm", "mem_k_norm", "mem_out_norm", "w_out", "ffn_norm",
           "ffn_w13", "ffn_w2")


def fold_heads(dg, n_heads, name):
    hd = dg.shape[1] // n_heads
    fold = (np.arange(dg.shape[1])[:, None] % hd == np.arange(128)[None, :]).astype(np.float32)
    out = matmul(jnp.pad(dg, ((0, 7), (0, 0))), fold, "nn", name, exact=True)
    return out[0, :hd]


def layer_fwd(x, mem_n_in, p, W, l):
    tag = f"l{l}_"
    row = lambda v: v.reshape(1, -1)
    sv = {"x": x}
    sv["xn"] = rms_fwd(x, row(p["attn_norm"]), tag + "attn_rms")
    proj = matmul(sv["xn"], W["w_in"], "nn", tag + "proj", tn=768, tk=2048)
    sv["proj"] = proj
    sv["wg_f"], sv["wg_b"] = pad_gate_weight(W["gla_wg2_f"], False), pad_gate_weight(W["gla_wg2_b"], True)
    sv["bg_f"], sv["bg_b"] = p["gla_bg_f"].reshape(GLA_H, 1, GLA_HK), p["gla_bg_b"].reshape(GLA_H, 1, GLA_HK)
    sv["o_f"], sv["s_f"] = gla_fwd(proj, sv["wg_f"], sv["bg_f"], False, tag + "gla_f")
    sv["o_b"], sv["s_b"] = gla_fwd(proj, sv["wg_b"], sv["bg_b"], True, tag + "gla_b")
    sv["gq"], sv["gk"] = jnp.tile(row(p["na_q_norm"]), (1, NA_H)), jnp.tile(row(p["na_k_norm"]), (1, NA_H))
    sv["gmq"], sv["gmk"] = jnp.tile(row(p["mem_q_norm"]), (1, MEM_H)), jnp.tile(row(p["mem_k_norm"]), (1, MEM_H))
    qn, kn, vn, mqn = pre_fwd(proj, sv["gq"], sv["gk"], sv["gmq"], tag + "pre")
    sv["q_hm"], sv["k_hm"], sv["v_hm"], sv["mqn"] = to_heads(qn), to_heads(kn), to_heads(vn), mqn
    sv["tb2"] = rpb_table(p["na_rpb"], tag + "rpb_table")
    sv["o_na"] = from_heads(natten_fwd(sv["q_hm"], sv["k_hm"], sv["v_hm"], sv["tb2"], tag + "natten"))
    sv["mem_n"] = rms_fwd(mem_n_in, row(p["mem_norm"]), tag + "mem_rms")
    sv["kv"] = matmul(sv["mem_n"], W["mem_wkv"], "nn", tag + "mem_kv", tn=512, tk=2048)
    sv["km"], sv["vm"] = mem_kprep(sv["kv"], sv["gmk"], tag + "mem_kprep")
    sv["o_mem"] = mem_attn_fwd(mqn, sv["km"], sv["vm"], tag + "mem_attn")
    sv["ycat"] = post_fwd(sv["o_f"], sv["o_b"], proj, sv["o_na"], sv["o_mem"], row(p["gla_out_norm"]),
                          row(p["na_out_norm"]), row(p["mem_out_norm"]), tag + "post")
    x1 = matmul(sv["ycat"], W["w_out"], "nn", tag + "out_proj", res=x, tk=2048)
    sv["x1"] = x1
    sv["h"] = rms_fwd(x1, row(p["ffn_norm"]), tag + "ffn_rms")
    sv["gu"] = matmul(sv["h"], W["ffn_w13"], "nn", tag + "ffn_up", tk=2048, b_blocked=True)
    sv["a"] = swiglu_fwd(sv["gu"], tag + "swiglu")
    x2 = matmul(sv["a"], W["ffn_w2"], "nn", tag + "ffn_down", res=x1)
    return x2, sv


def layer_bwd(dx2, dx2_b, mem_n_in, p, W, sv, l):
    tag = f"l{l}_b_"
    row = lambda v: v.reshape(1, -1)
    gw, gs = {}, {}
    gw["ffn_w2"] = matmul(sv["a"], dx2_b, "tn", tag + "dw2", out_dtype=BF16, tm=1408, tn=1024, tk=512)
    dgu = ffn_down_bwd(dx2_b, W["ffn_w2"], sv["gu"], tag + "d_swiglu")
    dh = matmul(dgu, W["ffn_w13"], "nt", tag + "d_h", a_halves=True, b_blocked=True)
    gw["ffn_w13"] = matmul(sv["h"], dgu, "tn", tag + "dw13", out_dtype=BF16, tm=2048, tn=1408, tk=512,
                           b_halves=True, out_blocked=True)
    dx1, dx1_b, dg = rms_bwd(sv["x1"], row(p["ffn_norm"]), dh, dx2, tag + "ffn_rms")
    gs["ffn_norm"] = dg[0]
    dycat = matmul(dx1_b, W["w_out"], "nt", tag + "d_ycat", tk=2048)
    gw["w_out"] = matmul(sv["ycat"], dx1_b, "tn", tag + "dw_out", out_dtype=BF16, tm=2048, tn=1024, tk=512)
    d_o, d_r, d_ona, d_omem, dgg, dgn, dgm = post_bwd(
        sv["o_f"], sv["o_b"], sv["proj"], sv["o_na"], sv["o_mem"], row(p["gla_out_norm"]), row(p["na_out_norm"]),
        row(p["mem_out_norm"]), dycat, tag + "post")
    gs["gla_out_norm"], gs["na_out_norm"], gs["mem_out_norm"] = dgg[0], dgn[0], dgm[0]
    dq_f, dk_f, dv_f, dt_f, dwg_f, dbg_f = gla_bwd(sv["proj"], sv["wg_f"], sv["bg_f"], sv["s_f"], d_o, False, tag + "gla_f")
    dq_b, dk_b, dv_b, dt_b, dwg_b, dbg_b = gla_bwd(sv["proj"], sv["wg_b"], sv["bg_b"], sv["s_b"], d_o, True, tag + "gla_b")
    gw["gla_wg2_f"], gw["gla_wg2_b"] = unpad_gate_grad(dwg_f, False), unpad_gate_grad(dwg_b, True)
    gs["gla_bg_f"], gs["gla_bg_b"] = dbg_f.reshape(-1), dbg_b.reshape(-1)
    dq_hm, dk_hm, dv_hm, dtb2 = natten_bwd(sv["q_hm"], sv["k_hm"], sv["v_hm"], sv["tb2"], to_heads(d_ona), tag + "natten")
    gs["na_rpb"] = rpb_table_bwd(dtb2, tag + "rpb_table")
    d_mqn, dkm, dvm = mem_attn_bwd(sv["mqn"], sv["km"], sv["vm"], d_omem, tag + "mem_attn")
    dkv, dgmk = mem_kprep_bwd(sv["kv"], sv["gmk"], dkm, dvm, tag + "mem_kprep")
    gs["mem_k_norm"] = fold_heads(dgmk, MEM_H, tag + "fold_mk")
    gw["mem_wkv"] = matmul(sv["mem_n"], dkv, "tn", tag + "dw_kv", out_dtype=BF16, tm=2048, tn=1024, tk=256)
    d_memn = matmul(dkv, W["mem_wkv"], "nt", tag + "d_memn", tk=1024)
    _, _, dg = rms_bwd(mem_n_in, row(p["mem_norm"]), d_memn, None, tag + "mem_rms")
    gs["mem_norm"] = dg[0]
    dproj, dgq, dgk, dgmq = pre_bwd(sv["proj"], sv["gq"], sv["gk"], sv["gmq"], from_heads(dq_hm), from_heads(dk_hm),
                                    from_heads(dv_hm), d_mqn, dq_f, dq_b, dk_f, dk_b, dv_f, dv_b, d_r, dt_f, dt_b, tag + "pre")
    gs["na_q_norm"] = fold_heads(dgq, NA_H, tag + "fold_q")
    gs["na_k_norm"] = fold_heads(dgk, NA_H, tag + "fold_k")
    gs["mem_q_norm"] = fold_heads(dgmq, MEM_H, tag + "fold_mq")
    dxn = matmul(dproj, W["w_in"], "nt", tag + "d_xn", tk=768)
    gw["w_in"] = matmul(sv["xn"], dproj, "tn", tag + "dw_in", out_dtype=BF16, tm=2048, tn=768, tk=512)
    dx, dx_b, dg = rms_bwd(sv["x"], row(p["attn_norm"]), dxn, dx1, tag + "attn_rms")
    gs["attn_norm"] = dg[0]
    return dx, dx_b, gw, gs


def gather_layer_weights(shards_bf16, l):
    return unpack_gathered(all_gather(pack_shards(shards_bf16), f"l{l}_gather_weights"))


def reduce_scatter_layer(gw, c_idx, l):
    g = pack_full_grads(gw)
    land1 = pair_exchange(g, f"l{l}_rs_pair_exchange")
    chip = pair_sum(g, land1, c_idx, f"l{l}_rs_pair_sum")
    land2 = chip_exchange(chip, f"l{l}_rs_chip_exchange")
    return unpack_shard_grads(sum_slots(land2, f"l{l}_rs_chip_sum", rows=4096))


def kernel(x, mem, attn_norm, w_in, gla_wg2_f, gla_bg_f, gla_wg2_b, gla_bg_b, gla_out_norm, na_q_norm, na_k_norm, na_rpb, na_out_norm, mem_norm, mem_wkv, mem_q_norm, mem_k_norm, mem_out_norm, w_out, ffn_norm, ffn_w13, ffn_w2, loss_target, m_attn_norm, m_w_in, m_gla_wg2_f, m_gla_bg_f, m_gla_wg2_b, m_gla_bg_b, m_gla_out_norm, m_na_q_norm, m_na_k_norm, m_na_rpb, m_na_out_norm, m_mem_norm, m_mem_wkv, m_mem_q_norm, m_mem_k_norm, m_mem_out_norm, m_w_out, m_ffn_norm, m_ffn_w13, m_ffn_w2, v_attn_norm, v_w_in, v_gla_wg2_f, v_gla_bg_f, v_gla_wg2_b, v_gla_bg_b, v_gla_out_norm, v_na_q_norm, v_na_k_norm, v_na_rpb, v_na_out_norm, v_mem_norm, v_mem_wkv, v_mem_q_norm, v_mem_k_norm, v_mem_out_norm, v_w_out, v_ffn_norm, v_ffn_w13, v_ffn_w2):
    w = dict(attn_norm=attn_norm, w_in=w_in, gla_wg2_f=gla_wg2_f, gla_bg_f=gla_bg_f, gla_wg2_b=gla_wg2_b, gla_bg_b=gla_bg_b,
             gla_out_norm=gla_out_norm, na_q_norm=na_q_norm, na_k_norm=na_k_norm, na_rpb=na_rpb, na_out_norm=na_out_norm,
             mem_norm=mem_norm, mem_wkv=mem_wkv, mem_q_norm=mem_q_norm, mem_k_norm=mem_k_norm, mem_out_norm=mem_out_norm,
             w_out=w_out, ffn_norm=ffn_norm, ffn_w13=ffn_w13, ffn_w2=ffn_w2)
    mom = dict(attn_norm=m_attn_norm, w_in=m_w_in, gla_wg2_f=m_gla_wg2_f, gla_bg_f=m_gla_bg_f, gla_wg2_b=m_gla_wg2_b,
               gla_bg_b=m_gla_bg_b, gla_out_norm=m_gla_out_norm, na_q_norm=m_na_q_norm, na_k_norm=m_na_k_norm, na_rpb=m_na_rpb,
               na_out_norm=m_na_out_norm, mem_norm=m_mem_norm, mem_wkv=m_mem_wkv, mem_q_norm=m_mem_q_norm,
               mem_k_norm=m_mem_k_norm, mem_out_norm=m_mem_out_norm, w_out=m_w_out, ffn_norm=m_ffn_norm, ffn_w13=m_ffn_w13,
               ffn_w2=m_ffn_w2)
    var = dict(attn_norm=v_attn_norm, w_in=v_w_in, gla_wg2_f=v_gla_wg2_f, gla_bg_f=v_gla_bg_f, gla_wg2_b=v_gla_wg2_b,
               gla_bg_b=v_gla_bg_b, gla_out_norm=v_gla_out_norm, na_q_norm=v_na_q_norm, na_k_norm=v_na_k_norm, na_rpb=v_na_rpb,
               na_out_norm=v_na_out_norm, mem_norm=v_mem_norm, mem_wkv=v_mem_wkv, mem_q_norm=v_mem_q_norm,
               mem_k_norm=v_mem_k_norm, mem_out_norm=v_mem_out_norm, w_out=v_w_out, ffn_norm=v_ffn_norm, ffn_w13=v_ffn_w13,
               ffn_w2=v_ffn_w2)
    depth = attn_norm.shape[0]
    T = x.shape[1]
    xs, mem0, tgt = x.reshape(T, D_MODEL), mem.reshape(MEM_TOK, D_MODEL), loss_target.reshape(T, D_MODEL)
    c_idx = lax.axis_index("c").astype(jnp.int32).reshape(1)

    W = [gather_layer_weights({n: w[n][l].astype(BF16) for n in SHARD_NAMES}, l) for l in range(depth)]
    P = [{n: w[n][l] for n in REPLICATED} for l in range(depth)]

    saved = []
    h = xs
    for l in range(depth):
        h, sv = layer_fwd(h, mem0, P[l], W[l], l)
        saved.append(sv)
    dy, dy_b, lsum = loss_bwd(h, tgt, "loss")
    loss = lax.psum(lsum[0, 0], ("x", "y", "c")) * (0.5 / D_MODEL)

    g_shard, g_small = [None] * depth, [None] * depth
    for l in range(depth - 1, -1, -1):
        dy, dy_b, gw, g_small[l] = layer_bwd(dy, dy_b, mem0, P[l], W[l], saved[l], l)
        g_shard[l] = reduce_scatter_layer(gw, c_idx, l)
        saved[l] = None
    grad_x = dy.reshape(x.shape)

    small = jnp.concatenate([g_small[l][n].reshape(-1) for l in range(depth) for n in REPLICATED])
    n_small = small.shape[0]
    rows = -(-n_small // 1024) * 8
    small = jnp.pad(small, (0, rows * 128 - n_small)).reshape(rows, 128)
    small = sum_slots(all_gather(small, "gather_small_grads"), "sum_small_grads").reshape(-1)
    grads, off = {}, 0
    per_layer = {n: [] for n in REPLICATED}
    for l in range(depth):
        for n in REPLICATED:
            sz = int(np.prod(w[n].shape[1:]))
            per_layer[n].append(small[off:off + sz].reshape(w[n].shape[1:]))
            off += sz
    for n in REPLICATED:
        grads[n] = jnp.stack(per_layer[n])
    for n in SHARD_NAMES:
        grads[n] = jnp.stack([g_shard[l][n] for l in range(depth)])

    delta, new_m, new_v = {}, {}, {}
    for n in WEIGHTS:
        shp = w[n].shape
        two_d = (shp[0], int(np.prod(shp[1:]))) if n in REPLICATED else (int(np.prod(shp[:-1])), shp[-1])
        d_, m_, v_ = adamw(w[n].reshape(two_d), grads[n].reshape(two_d), mom[n].reshape(two_d), var[n].reshape(two_d),
                           "adamw_" + n)
        delta[n], new_m[n], new_v[n] = d_.reshape(shp), m_.reshape(shp), v_.reshape(shp)

    return (loss, grad_x, *[grads[n] for n in WEIGHTS], *[delta[n] for n in WEIGHTS], *[new_m[n] for n in WEIGHTS],
            *[new_v[n] for n in WEIGHTS])
```

```python
import functools

import numpy as np
import jax
import jax.numpy as jnp
from jax import lax
from jax.experimental import pallas as pl
from jax.experimental.pallas import tpu as pltpu

F32, BF16 = jnp.float32, jnp.bfloat16
HI = lax.Precision.HIGHEST
SDS = jax.ShapeDtypeStruct
MESH = pl.DeviceIdType.MESH

D_MODEL = 2048
GRID_W = 64
GLA_H, GLA_HK, GLA_HV, GLA_RANK, GLA_TAU, GLA_C = 4, 128, 256, 16, 16.0, 64
GLA_DK, GLA_DV = GLA_H * GLA_HK, GLA_H * GLA_HV
NA_H, NA_HD, NA_ROWS, NA_COLS = 8, 64, 8, 16
NA_W = NA_H * NA_HD
MEM_H, MEM_HD, MEM_TOK = 4, 128, 256
MEM_W = MEM_H * MEM_HD
D_FF = 5632
IN_COLS = 5152
RMS_EPS = 1e-6
ADAM_LR, ADAM_B1, ADAM_B2, ADAM_EPS, ADAM_WD, ADAM_STEP = 0.001, 0.9, 0.999, 1e-08, 0.01, 10
N_DEV = 8

PC = 5376
COL_R, COL_NQ, COL_NK, COL_NV, COL_MQ, COL_TAIL = 2048, 3072, 3584, 4096, 4608, 5120
ORIG_GATE0 = 3072
ORIG_AFTER_GATE = 3104

GLA_G = 8
VMEM_LIMIT = 56 * 1024 * 1024
NEG = -1e30


def _cparams(sem):
    return pltpu.CompilerParams(dimension_semantics=sem, vmem_limit_bytes=VMEM_LIMIT)


def _fit(n, pref, unit=128):
    if n <= pref:
        return n
    t = (pref // unit) * unit
    while t >= unit:
        if n % t == 0:
            return t
        t -= unit
    return n


def _bdot(a, b, dims):
    return lax.dot_general(a.astype(BF16), b.astype(BF16), (dims, ((), ())), preferred_element_type=F32)


NN, NT, TN = ((1,), (0,)), ((1,), (1,)), ((0,), (0,))


@functools.partial(jax.custom_vjp, nondiff_argnums=(2,))
def _bdot_vjp(a, b, dims):
    return _bdot(a, b, dims)


def _bdot_vjp_fwd(a, b, dims):
    return _bdot(a, b, dims), (a, b)


def _bdot_vjp_bwd(dims, res, ct):
    a, b = res
    if dims == NN:
        return _bdot(ct, b, NT), _bdot(a, ct, TN)
    if dims == NT:
        return _bdot(ct, b, NN), _bdot(ct, a, TN)
    return _bdot(b, ct, NT), _bdot(a, ct, NN)


_bdot_vjp.defvjp(_bdot_vjp_fwd, _bdot_vjp_bwd)


def _split_dot(c, x, dims):
    hi = x.astype(BF16)
    lo = (x - hi.astype(F32)).astype(BF16)
    cb = c.astype(BF16)
    return (lax.dot_general(cb, hi, (dims, ((), ())), preferred_element_type=F32)
            + lax.dot_general(cb, lo, (dims, ((), ())), preferred_element_type=F32))


@jax.custom_vjp
def _tri_sum(cmat, x):
    return _split_dot(cmat, x, NN)


def _tri_sum_fwd(cmat, x):
    return _split_dot(cmat, x, NN), cmat


def _tri_sum_bwd(cmat, ct):
    return jnp.zeros_like(cmat), _split_dot(cmat, ct, TN)


_tri_sum.defvjp(_tri_sum_fwd, _tri_sum_bwd)


def matmul(a, b, mode, name, out_dtype=F32, res=None, tm=1024, tn=1024, tk=512, exact=False,
           a_halves=False, b_halves=False, b_blocked=False, out_blocked=False):
    if mode == "nn":
        M, K = a.shape
        N = b.shape[0] * b.shape[2] if b_blocked else b.shape[1]
        if b_blocked:
            tn = b.shape[2]
    elif mode == "nt":
        M, K = (a.shape[1], 2 * a.shape[2]) if a_halves else a.shape
        N = b.shape[1] if b_blocked else b.shape[0]
        if b_blocked:
            tk = b.shape[2]
    else:
        K, M = a.shape
        N = 2 * b.shape[2] if b_halves else b.shape[1]
    tm, tn, tk = _fit(M, tm, 8 if mode != "tn" else 128), _fit(N, tn), _fit(K, tk, 128 if mode != "tn" else 16)
    nk = K // tk
    dims = {"nn": NN, "nt": NT, "tn": TN}[mode]

    def body(*refs):
        if res is None:
            a_ref, b_ref, o_ref = refs[:3]
            r_ref = None
            acc = refs[3] if nk > 1 else None
        else:
            a_ref, b_ref, r_ref, o_ref = refs[:4]
            acc = refs[4] if nk > 1 else None
        if exact:
            part = lax.dot_general(a_ref[...], b_ref[...], (dims, ((), ())), preferred_element_type=F32, precision=HI)
        else:
            part = _bdot(a_ref[...], b_ref[...], dims)

        def finish(val):
            if r_ref is not None:
                val = val + r_ref[...]
            o_ref[...] = val.astype(out_dtype)

        if nk == 1:
            finish(part)
        else:
            kk = pl.program_id(2)

            @pl.when(kk == 0)
            def _():
                acc[...] = part

            @pl.when(kk > 0)
            def _():
                acc[...] += part

            @pl.when(kk == nk - 1)
            def _():
                finish(acc[...])

    if mode == "tn":
        a_spec = pl.BlockSpec((tk, tm), lambda i, j, k: (k, i))
    elif a_halves:
        nh = K // 2 // tk
        a_spec = pl.BlockSpec((None, tm, tk), lambda i, j, k: (k // nh, i, k % nh))
    else:
        a_spec = pl.BlockSpec((tm, tk), lambda i, j, k: (i, k))
    if b_blocked:
        b_spec = (pl.BlockSpec((None, tk, tn), lambda i, j, k: (j, k, 0)) if mode == "nn"
                  else pl.BlockSpec((None, tn, tk), lambda i, j, k: (k, j, 0)))
    elif b_halves:
        nh = N // 2 // tn
        b_spec = pl.BlockSpec((None, tk, tn), lambda i, j, k: (j // nh, k, j % nh))
    elif mode == "nt":
        b_spec = pl.BlockSpec((tn, tk), lambda i, j, k: (j, k))
    else:
        b_spec = pl.BlockSpec((tk, tn), lambda i, j, k: (k, j))
    if out_blocked:
        o_spec, o_shape = pl.BlockSpec((None, tm, tn), lambda i, j, k: (j, i, 0)), SDS((N // tn, M, tn), out_dtype)
    else:
        o_spec, o_shape = pl.BlockSpec((tm, tn), lambda i, j, k: (i, j)), SDS((M, N), out_dtype)
    in_specs, args = [a_spec, b_spec], [a, b]
    if res is not None:
        in_specs.append(o_spec)
        args.append(res)
    return pl.pallas_call(
        body, grid=(M // tm, N // tn, nk), in_specs=in_specs, out_specs=o_spec, out_shape=o_shape,
        scratch_shapes=[pltpu.VMEM((tm, tn), F32)] if nk > 1 else [],
        compiler_params=_cparams(("parallel", "parallel", "arbitrary")), name=name)(*args)


def rms_fwd(x, g, name):
    T, D = x.shape
    tm = _fit(T, 512, 8)

    def body(x_ref, g_ref, o_ref):
        xv = x_ref[...]
        r = lax.rsqrt(jnp.mean(xv * xv, axis=-1, keepdims=True) + RMS_EPS)
        o_ref[...] = (xv * r * g_ref[...]).astype(BF16)

    return pl.pallas_call(
        body, grid=(T // tm,), in_specs=[pl.BlockSpec((tm, D), lambda i: (i, 0)), pl.BlockSpec((1, D), lambda i: (0, 0))],
        out_specs=pl.BlockSpec((tm, D), lambda i: (i, 0)), out_shape=SDS((T, D), BF16),
        compiler_params=_cparams(("parallel",)), name=name)(x, g)


def rms_bwd(x, g, dy, dres, name):
    T, D = x.shape
    tm = _fit(T, 256, 8)
    has_res = dres is not None

    def body(*refs):
        if has_res:
            x_ref, g_ref, dy_ref, dres_ref, dx_ref, dxb_ref, dg_ref = refs
        else:
            x_ref, g_ref, dy_ref, dx_ref, dxb_ref, dg_ref = refs
        xv, dyv = x_ref[...], dy_ref[...]
        r = lax.rsqrt(jnp.mean(xv * xv, axis=-1, keepdims=True) + RMS_EPS)
        xh = xv * r
        dxh = dyv * g_ref[...]
        dx = r * (dxh - xh * jnp.mean(dxh * xh, axis=-1, keepdims=True))
        if has_res:
            dx = dx + dres_ref[...]
        dx_ref[...] = dx
        dxb_ref[...] = dx.astype(BF16)

        @pl.when(pl.program_id(0) == 0)
        def _():
            dg_ref[...] = jnp.zeros_like(dg_ref)

        dg_ref[...] += jnp.sum(dyv * xh, axis=0, keepdims=True)

    row = pl.BlockSpec((tm, D), lambda i: (i, 0))
    vec = pl.BlockSpec((1, D), lambda i: (0, 0))
    args = [x, g, dy] + ([dres] if has_res else [])
    return pl.pallas_call(
        body, grid=(T // tm,), in_specs=[row, vec, row] + ([row] if has_res else []),
        out_specs=[row, row, vec], out_shape=[SDS((T, D), F32), SDS((T, D), BF16), SDS((1, D), F32)],
        compiler_params=_cparams(("arbitrary",)), name=name)(*args)


def ffn_up_swiglu(h, w13b, name):
    T, D = h.shape
    nb, _, tb = w13b.shape
    nh = nb // 2
    tm = _fit(T, 512, 8)

    def body(h_ref, wg_ref, wu_ref, gu_ref, a_ref):
        hv = h_ref[...]
        gv = _bdot(hv, wg_ref[...], NN)
        uv = _bdot(hv, wu_ref[...], NN)
        gu_ref[0] = gv
        gu_ref[1] = uv
        a_ref[...] = (gv * jax.nn.sigmoid(gv) * uv).astype(BF16)

    return pl.pallas_call(
        body, grid=(nh, T // tm),
        in_specs=[pl.BlockSpec((tm, D), lambda j, i: (i, 0)), pl.BlockSpec((None, D, tb), lambda j, i: (j, 0, 0)),
                  pl.BlockSpec((None, D, tb), lambda j, i: (j + nh, 0, 0))],
        out_specs=[pl.BlockSpec((2, tm, tb), lambda j, i: (0, i, j)), pl.BlockSpec((tm, tb), lambda j, i: (i, j))],
        out_shape=[SDS((2, T, nh * tb), F32), SDS((T, nh * tb), BF16)],
        compiler_params=_cparams(("parallel", "parallel")), name=name)(h, w13b, w13b)


def ffn_down_bwd(dy_b, w2, gu, name):
    T, D = dy_b.shape
    Fh = w2.shape[0]
    tm, tf = _fit(T, 1024, 8), _fit(Fh, 512)
    nf = Fh // tf

    def body(dy_ref, w_ref, g_ref, u_ref, o_ref):
        dav = _bdot(dy_ref[...], w_ref[...], NT)
        gv, uv = g_ref[...], u_ref[...]
        sg = jax.nn.sigmoid(gv)
        o_ref[0] = (dav * uv * (sg * (1.0 + gv * (1.0 - sg)))).astype(BF16)
        o_ref[1] = (dav * gv * sg).astype(BF16)

    return pl.pallas_call(
        body, grid=(T // tm, nf),
        in_specs=[pl.BlockSpec((tm, D), lambda i, j: (i, 0)), pl.BlockSpec((tf, D), lambda i, j: (j, 0)),
                  pl.BlockSpec((None, tm, tf), lambda i, j: (0, i, j)), pl.BlockSpec((None, tm, tf), lambda i, j: (1, i, j))],
        out_specs=pl.BlockSpec((2, tm, tf), lambda i, j: (0, i, j)), out_shape=SDS((2, T, Fh), BF16),
        compiler_params=_cparams(("parallel", "parallel")), name=name)(dy_b, w2, gu, gu)


def loss_bwd(y, tgt, name):
    T, D = y.shape
    tm = _fit(T, 512, 8)

    def body(y_ref, t_ref, dy_ref, dyb_ref, l_ref):
        e = y_ref[...] - t_ref[...]
        dy = e * (1.0 / D)
        dy_ref[...] = dy
        dyb_ref[...] = dy.astype(BF16)

        @pl.when(pl.program_id(0) == 0)
        def _():
            l_ref[...] = jnp.zeros_like(l_ref)

        l_ref[...] += jnp.sum(jnp.sum(e * e, axis=1, keepdims=True), axis=0, keepdims=True)

    row = pl.BlockSpec((tm, D), lambda i: (i, 0))
    return pl.pallas_call(
        body, grid=(T // tm,), in_specs=[row, row], out_specs=[row, row, pl.BlockSpec((8, 128), lambda i: (0, 0))],
        out_shape=[SDS((T, D), F32), SDS((T, D), BF16), SDS((8, 128), F32)],
        compiler_params=_cparams(("arbitrary",)), name=name)(y, tgt)


def _log_sigmoid(z):
    return jnp.minimum(z, 0.0) - jnp.log(1.0 + jnp.exp(-jnp.abs(z)))


def _gla_chunk(q, k, v, tail, wg, bg, s_prev, cmat, mask):
    z = _bdot_vjp(tail, wg, NN) + bg
    la = _log_sigmoid(z) * (1.0 / GLA_TAU)
    cum = _tri_sum(cmat, la)
    last = jnp.sum(la, axis=0, keepdims=True)
    q_e = q * jnp.exp(cum) * (GLA_HK ** -0.5)
    k_e = k * jnp.exp(-cum)
    k_end = k * jnp.exp(last - cum)
    sc = jnp.where(mask > 0.5, _bdot_vjp(q_e, k_e, NT), 0.0)
    o = _bdot_vjp(sc, v, NN) + _bdot_vjp(q_e, s_prev, NN)
    kv = _bdot_vjp(k_end, v, TN)
    s_new = s_prev * jnp.transpose(jnp.exp(last)) + kv
    return o, s_new


def _gla_group(q, k, v, tail, wg, bg, s, cmat, mask, backward_dir):
    n = q.shape[0] // GLA_C
    outs = [None] * n
    for g in (range(n - 1, -1, -1) if backward_dir else range(n)):
        sl = slice(g * GLA_C, (g + 1) * GLA_C)
        outs[g], s = _gla_chunk(q[sl], k[sl], v[sl], tail[sl], wg, bg, s, cmat, mask)
    return jnp.concatenate(outs, axis=0), s


def _gla_consts(backward_dir):
    i = np.arange(GLA_C)
    if backward_dir:
        return (i[None, :] >= i[:, None]).astype(np.float32), (i[None, :] > i[:, None]).astype(np.float32)
    return (i[None, :] <= i[:, None]).astype(np.float32), (i[None, :] <= i[:, None]).astype(np.float32)


GLA_HB = 2


def _gla_in_specs(GC, nmap):
    return [
        pl.BlockSpec((GC, GLA_HB * GLA_HK), lambda h, n: (nmap(n), h)),
        pl.BlockSpec((GC, GLA_HB * GLA_HK), lambda h, n: (nmap(n), GLA_H // GLA_HB + h)),
        pl.BlockSpec((GC, GLA_HB * GLA_HV), lambda h, n: (nmap(n), 2 * GLA_DK // (GLA_HB * GLA_HV) + h)),
        pl.BlockSpec((GC, 128), lambda h, n: (nmap(n), COL_TAIL // 128)),
        pl.BlockSpec((GLA_HB, 128, 128), lambda h, n: (h, 0, 0)),
        pl.BlockSpec((GLA_HB, 1, 128), lambda h, n: (h, 0, 0)),
        pl.BlockSpec((GLA_C, GLA_C), lambda h, n: (0, 0)),
        pl.BlockSpec((GLA_C, GLA_C), lambda h, n: (0, 0)),
    ]


def _head_cols(ref, hh, width):
    return ref[:, hh * width:(hh + 1) * width]


def gla_fwd(proj, wgpad, bg, backward_dir, name):
    T = proj.shape[0]
    GC = min(GLA_G * GLA_C, T)
    NG = T // GC
    cmat, mask = _gla_consts(backward_dir)
    nmap = (lambda n: NG - 1 - n) if backward_dir else (lambda n: n)

    def body(q_ref, k_ref, v_ref, t_ref, wg_ref, bg_ref, c_ref, m_ref, o_ref, ss_ref, s_scr):
        @pl.when(pl.program_id(1) == 0)
        def _():
            s_scr[...] = jnp.zeros_like(s_scr)

        tail, cm, mk = t_ref[...], c_ref[...], m_ref[...]
        for hh in range(GLA_HB):
            s0 = s_scr[hh]
            ss_ref[hh] = s0
            o, s1 = _gla_group(_head_cols(q_ref, hh, GLA_HK), _head_cols(k_ref, hh, GLA_HK), _head_cols(v_ref, hh, GLA_HV),
                               tail, wg_ref[hh], bg_ref[hh], s0, cm, mk, backward_dir)
            o_ref[:, hh * GLA_HV:(hh + 1) * GLA_HV] = o
            s_scr[hh] = s1

    return pl.pallas_call(
        body, grid=(GLA_H // GLA_HB, NG), in_specs=_gla_in_specs(GC, nmap),
        out_specs=[pl.BlockSpec((GC, GLA_HB * GLA_HV), lambda h, n: (nmap(n), h)),
                   pl.BlockSpec((GLA_HB, None, GLA_HK, GLA_HV), lambda h, n: (h, nmap(n), 0, 0))],
        out_shape=[SDS((T, GLA_DV), F32), SDS((GLA_H, NG, GLA_HK, GLA_HV), F32)],
        scratch_shapes=[pltpu.VMEM((GLA_HB, GLA_HK, GLA_HV), F32)],
        compiler_params=_cparams(("parallel", "arbitrary")), name=name)(proj, proj, proj, proj, wgpad, bg, cmat, mask)


def gla_bwd(proj, wgpad, bg, ssave, do, backward_dir, name):
    T = proj.shape[0]
    GC = min(GLA_G * GLA_C, T)
    NG = T // GC
    cmat, mask = _gla_consts(backward_dir)
    nmap = (lambda n: n) if backward_dir else (lambda n: NG - 1 - n)

    def body(q_ref, k_ref, v_ref, t_ref, wg_ref, bg_ref, c_ref, m_ref, ss_ref, do_ref,
             dq_ref, dk_ref, dv_ref, dt_ref, dwg_ref, dbg_ref, ds_scr):
        @pl.when(pl.program_id(1) == 0)
        def _():
            ds_scr[...] = jnp.zeros_like(ds_scr)
            dwg_ref[...] = jnp.zeros_like(dwg_ref)
            dbg_ref[...] = jnp.zeros_like(dbg_ref)

        tail, cm, mk = t_ref[...], c_ref[...], m_ref[...]
        fn = lambda q, k, v, t, wg, b, s: _gla_group(q, k, v, t, wg, b, s, cm, mk, backward_dir)
        for hh in range(GLA_HB):
            _, vjp = jax.vjp(fn, _head_cols(q_ref, hh, GLA_HK), _head_cols(k_ref, hh, GLA_HK), _head_cols(v_ref, hh, GLA_HV),
                             tail, wg_ref[hh], bg_ref[hh], ss_ref[hh])
            dq, dk, dv, dt, dwg, dbg, ds = vjp((_head_cols(do_ref, hh, GLA_HV), ds_scr[hh]))
            dq_ref[:, hh * GLA_HK:(hh + 1) * GLA_HK] = dq
            dk_ref[:, hh * GLA_HK:(hh + 1) * GLA_HK] = dk
            dv_ref[:, hh * GLA_HV:(hh + 1) * GLA_HV] = dv
            dt_ref[hh] = dt
            dwg_ref[hh] += dwg
            dbg_ref[hh] += dbg
            ds_scr[hh] = ds

    in_specs = _gla_in_specs(GC, nmap) + [
        pl.BlockSpec((GLA_HB, None, GLA_HK, GLA_HV), lambda h, n: (h, nmap(n), 0, 0)),
        pl.BlockSpec((GC, GLA_HB * GLA_HV), lambda h, n: (nmap(n), h)),
    ]
    out_specs = [
        pl.BlockSpec((GC, GLA_HB * GLA_HK), lambda h, n: (nmap(n), h)),
        pl.BlockSpec((GC, GLA_HB * GLA_HK), lambda h, n: (nmap(n), h)),
        pl.BlockSpec((GC, GLA_HB * GLA_HV), lambda h, n: (nmap(n), h)),
        pl.BlockSpec((GLA_HB, GC, 128), lambda h, n: (h, nmap(n), 0)),
        pl.BlockSpec((GLA_HB, 128, 128), lambda h, n: (h, 0, 0)),
        pl.BlockSpec((GLA_HB, 1, 128), lambda h, n: (h, 0, 0)),
    ]
    out_shape = [SDS((T, GLA_DK), F32), SDS((T, GLA_DK), F32), SDS((T, GLA_DV), F32), SDS((GLA_H, T, 128), F32),
                 SDS((GLA_H, 128, 128), F32), SDS((GLA_H, 1, 128), F32)]
    return pl.pallas_call(
        body, grid=(GLA_H // GLA_HB, NG), in_specs=in_specs, out_specs=out_specs, out_shape=out_shape,
        scratch_shapes=[pltpu.VMEM((GLA_HB, GLA_HK, GLA_HV), F32)],
        compiler_params=_cparams(("parallel", "arbitrary")), name=name)(proj, proj, proj, proj, wgpad, bg, cmat, mask, ssave, do)


def _block_diag(width, hd):
    i = np.arange(width)
    return ((i[:, None] // hd) == (i[None, :] // hd)).astype(np.float32) / hd


def _norm_heads(t, hd):
    outs = []
    for h in range(t.shape[1] // hd):
        th = t[:, h * hd:(h + 1) * hd]
        outs.append(th * lax.rsqrt(jnp.mean(th * th, axis=-1, keepdims=True) + RMS_EPS))
    return jnp.concatenate(outs, axis=1)


def _norm_bd(t, bd):
    return t * lax.rsqrt(jnp.dot(t * t, bd, preferred_element_type=F32, precision=HI) + RMS_EPS)


def _pre_fn(nq, nk, mq, gq, gk, gm, bd):
    return _norm_bd(nq, bd) * gq, _norm_bd(nk, bd) * gk, _norm_heads(mq, MEM_HD) * gm


def pre_fwd(proj, gq, gk, gm, name):
    T = proj.shape[0]
    tm = _fit(T, 512, 8)
    bd = _block_diag(NA_W, NA_HD)

    def body(nq_ref, nk_ref, nv_ref, mq_ref, gq_ref, gk_ref, gm_ref, bd_ref, q_ref, k_ref, v_ref, m_ref):
        qn, kn, mn = _pre_fn(nq_ref[...], nk_ref[...], mq_ref[...], gq_ref[...], gk_ref[...], gm_ref[...], bd_ref[...])
        q_ref[...] = qn.astype(BF16)
        k_ref[...] = kn.astype(BF16)
        v_ref[...] = nv_ref[...].astype(BF16)
        m_ref[...] = mn.astype(BF16)

    col = lambda c0: pl.BlockSpec((tm, 512), lambda i: (i, c0 // 512))
    vec = pl.BlockSpec((1, 512), lambda i: (0, 0))
    row = pl.BlockSpec((tm, 512), lambda i: (i, 0))
    return pl.pallas_call(
        body, grid=(T // tm,),
        in_specs=[col(COL_NQ), col(COL_NK), col(COL_NV), col(COL_MQ), vec, vec, vec, pl.BlockSpec((NA_W, NA_W), lambda i: (0, 0))],
        out_specs=[row] * 4, out_shape=[SDS((T, 512), BF16)] * 4,
        compiler_params=_cparams(("parallel",)), name=name)(proj, proj, proj, proj, gq, gk, gm, bd)


def pre_bwd(proj, gq, gk, gm, d_qn, d_kn, d_nv, d_mn, dq_f, dq_b, dk_f, dk_b, dv_f, dv_b, d_r, dt_f, dt_b, name):
    T = proj.shape[0]
    tm = _fit(T, 256, 8)
    bd = _block_diag(NA_W, NA_HD)

    def body(nq_ref, nk_ref, mq_ref, gq_ref, gk_ref, gm_ref, bd_ref, dqn_ref, dkn_ref, dnv_ref, dmn_ref,
             dqf_ref, dqb_ref, dkf_ref, dkb_ref, dvf_ref, dvb_ref, dr_ref, dtf_ref, dtb_ref,
             o_ref, dgq_ref, dgk_ref, dgm_ref):
        bdv = bd_ref[...]
        fn = lambda a, b, c, x, y, z: _pre_fn(a, b, c, x, y, z, bdv)
        _, vjp = jax.vjp(fn, nq_ref[...], nk_ref[...], mq_ref[...], gq_ref[...], gk_ref[...], gm_ref[...])
        d_nq, d_nk, d_mq, dgq, dgk, dgm = vjp((dqn_ref[...], dkn_ref[...], dmn_ref[...]))
        o_ref[:, 0:512] = (dqf_ref[...] + dqb_ref[...]).astype(BF16)
        o_ref[:, 512:1024] = (dkf_ref[...] + dkb_ref[...]).astype(BF16)
        o_ref[:, 1024:2048] = (dvf_ref[...] + dvb_ref[...]).astype(BF16)
        o_ref[:, COL_R:COL_R + 1024] = dr_ref[...].astype(BF16)
        o_ref[:, COL_NQ:COL_NQ + 512] = d_nq.astype(BF16)
        o_ref[:, COL_NK:COL_NK + 512] = d_nk.astype(BF16)
        o_ref[:, COL_NV:COL_NV + 512] = dnv_ref[...].astype(BF16)
        o_ref[:, COL_MQ:COL_MQ + 512] = d_mq.astype(BF16)
        dt = dtf_ref[0] + dtb_ref[0]
        for h in range(1, GLA_H):
            dt = dt + dtf_ref[h] + dtb_ref[h]
        o_ref[:, COL_TAIL:COL_TAIL + 128] = dt.astype(BF16)
        o_ref[:, COL_TAIL + 128:PC] = jnp.zeros((tm, PC - COL_TAIL - 128), BF16)

        @pl.when(pl.program_id(0) == 0)
        def _():
            dgq_ref[...] = jnp.zeros_like(dgq_ref)
            dgk_ref[...] = jnp.zeros_like(dgk_ref)
            dgm_ref[...] = jnp.zeros_like(dgm_ref)

        dgq_ref[...] += dgq
        dgk_ref[...] += dgk
        dgm_ref[...] += dgm

    col = lambda c0: pl.BlockSpec((tm, 512), lambda i: (i, c0 // 512))
    vec = pl.BlockSpec((1, 512), lambda i: (0, 0))
    r512 = pl.BlockSpec((tm, 512), lambda i: (i, 0))
    r1024 = pl.BlockSpec((tm, 1024), lambda i: (i, 0))
    tl = pl.BlockSpec((GLA_H, tm, 128), lambda i: (0, i, 0))
    in_specs = [col(COL_NQ), col(COL_NK), col(COL_MQ), vec, vec, vec, pl.BlockSpec((NA_W, NA_W), lambda i: (0, 0)),
                r512, r512, r512, r512, r512, r512, r512, r512, r1024, r1024, r1024, tl, tl]
    return pl.pallas_call(
        body, grid=(T // tm,), in_specs=in_specs,
        out_specs=[pl.BlockSpec((tm, PC), lambda i: (i, 0)), vec, vec, vec],
        out_shape=[SDS((T, PC), BF16), SDS((1, 512), F32), SDS((1, 512), F32), SDS((1, 512), F32)],
        compiler_params=_cparams(("arbitrary",)), name=name)(
            proj, proj, proj, gq, gk, gm, bd, d_qn, d_kn, d_nv, d_mn, dq_f, dq_b, dk_f, dk_b, dv_f, dv_b, d_r, dt_f, dt_b)


def _post_fn(o_f, o_b, r, o_na, o_mem, g_gla, g_na, g_mem, bd):
    y_gla = _norm_heads(o_f + o_b, GLA_HV) * g_gla * (r * jax.nn.sigmoid(r))
    y_na = _norm_bd(o_na, bd) * g_na
    y_mem = _norm_heads(o_mem, MEM_HD) * g_mem
    return jnp.concatenate([y_gla, y_na, y_mem], axis=1)


def post_fwd(o_f, o_b, proj, o_na, o_mem, g_gla, g_na, g_mem, name):
    T = proj.shape[0]
    tm = _fit(T, 256, 8)
    bd = _block_diag(NA_W, NA_HD)

    def body(of_ref, ob_ref, r_ref, ona_ref, omem_ref, gg_ref, gn_ref, gm_ref, bd_ref, y_ref):
        y_ref[...] = _post_fn(of_ref[...], ob_ref[...], r_ref[...], ona_ref[...], omem_ref[...],
                              gg_ref[...], gn_ref[...], gm_ref[...], bd_ref[...]).astype(BF16)

    r1024 = pl.BlockSpec((tm, 1024), lambda i: (i, 0))
    r512 = pl.BlockSpec((tm, 512), lambda i: (i, 0))
    in_specs = [r1024, r1024, pl.BlockSpec((tm, 1024), lambda i: (i, COL_R // 1024)), r512, r512,
                pl.BlockSpec((1, 1024), lambda i: (0, 0)), pl.BlockSpec((1, 512), lambda i: (0, 0)),
                pl.BlockSpec((1, 512), lambda i: (0, 0)), pl.BlockSpec((NA_W, NA_W), lambda i: (0, 0))]
    return pl.pallas_call(
        body, grid=(T // tm,), in_specs=in_specs, out_specs=pl.BlockSpec((tm, D_MODEL), lambda i: (i, 0)),
        out_shape=SDS((T, D_MODEL), BF16), compiler_params=_cparams(("parallel",)), name=name)(
            o_f, o_b, proj, o_na, o_mem, g_gla, g_na, g_mem, bd)


def post_bwd(o_f, o_b, proj, o_na, o_mem, g_gla, g_na, g_mem, dy, name):
    T = proj.shape[0]
    tm = _fit(T, 256, 8)
    bd = _block_diag(NA_W, NA_HD)

    def body(of_ref, ob_ref, r_ref, ona_ref, omem_ref, gg_ref, gn_ref, gm_ref, bd_ref, dy_ref,
             do_ref, dr_ref, dna_ref, dmem_ref, dgg_ref, dgn_ref, dgm_ref):
        bdv = bd_ref[...]
        fn = lambda o, r, a, m, x, y, z: _post_fn(o, 0.0, r, a, m, x, y, z, bdv)
        _, vjp = jax.vjp(fn, of_ref[...] + ob_ref[...], r_ref[...], ona_ref[...], omem_ref[...],
                         gg_ref[...], gn_ref[...], gm_ref[...])
        d_o, d_r, d_na, d_mem, dgg, dgn, dgm = vjp(dy_ref[...])
        do_ref[...] = d_o
        dr_ref[...] = d_r
        dna_ref[...] = d_na
        dmem_ref[...] = d_mem

        @pl.when(pl.program_id(0) == 0)
        def _():
            dgg_ref[...] = jnp.zeros_like(dgg_ref)
            dgn_ref[...] = jnp.zeros_like(dgn_ref)
            dgm_ref[...] = jnp.zeros_like(dgm_ref)

        dgg_ref[...] += dgg
        dgn_ref[...] += dgn
        dgm_ref[...] += dgm

    r1024 = pl.BlockSpec((tm, 1024), lambda i: (i, 0))
    r512 = pl.BlockSpec((tm, 512), lambda i: (i, 0))
    v1024 = pl.BlockSpec((1, 1024), lambda i: (0, 0))
    v512 = pl.BlockSpec((1, 512), lambda i: (0, 0))
    in_specs = [r1024, r1024, pl.BlockSpec((tm, 1024), lambda i: (i, COL_R // 1024)), r512, r512, v1024, v512, v512,
                pl.BlockSpec((NA_W, NA_W), lambda i: (0, 0)), pl.BlockSpec((tm, D_MODEL), lambda i: (i, 0))]
    return pl.pallas_call(
        body, grid=(T // tm,), in_specs=in_specs, out_specs=[r1024, r1024, r512, r512, v1024, v512, v512],
        out_shape=[SDS((T, 1024), F32), SDS((T, 1024), F32), SDS((T, 512), F32), SDS((T, 512), F32),
                   SDS((1, 1024), F32), SDS((1, 512), F32), SDS((1, 512), F32)],
        compiler_params=_cparams(("arbitrary",)), name=name)(o_f, o_b, proj, o_na, o_mem, g_gla, g_na, g_mem, bd, dy)


NA_RB = 8


def _na_row_scores(q_ref, k_ref, v_ref, tb_ref, rb, j, n_rows):
    r = rb * NA_RB + j
    rs = jnp.clip(r - NA_ROWS // 2, 0, n_rows - NA_ROWS)
    dr0 = rs - r + (NA_ROWS - 1)
    tok = pl.ds(pl.multiple_of(rs * GRID_W, GRID_W), NA_ROWS * GRID_W)
    q = q_ref[j * GRID_W:(j + 1) * GRID_W, :]
    kk, vv = k_ref[tok, :], v_ref[tok, :]
    bias = jnp.concatenate([tb_ref[dr0 + 2 * i] for i in range(NA_ROWS // 2)], axis=1)
    s = _bdot(q, kk, NT) * (NA_HD ** -0.5) + bias
    m = jnp.max(s, axis=1, keepdims=True)
    p = jnp.exp(s - m)
    l = jnp.sum(p, axis=1, keepdims=True)
    return q, kk, vv, p, l, tok, dr0


def natten_fwd(q, k, v, tb2, name):
    H, T, hd = q.shape
    n_rows = T // GRID_W
    rbt = NA_RB * GRID_W

    def body(q_ref, k_ref, v_ref, tb_ref, o_ref):
        rb = pl.program_id(1)
        for j in range(NA_RB):
            _, _, vv, p, l, _, _ = _na_row_scores(q_ref, k_ref, v_ref, tb_ref, rb, j, n_rows)
            o_ref[j * GRID_W:(j + 1) * GRID_W, :] = _bdot(p, vv, NN) / l

    whole = pl.BlockSpec((None, T, hd), lambda h, r: (h, 0, 0))
    blk = pl.BlockSpec((None, rbt, hd), lambda h, r: (h, r, 0))
    return pl.pallas_call(
        body, grid=(H, n_rows // NA_RB),
        in_specs=[blk, whole, whole, pl.BlockSpec((None, 2 * NA_ROWS - 2, GRID_W, 2 * GRID_W), lambda h, r: (h, 0, 0, 0))],
        out_specs=blk, out_shape=SDS((H, T, hd), F32),
        compiler_params=_cparams(("parallel", "arbitrary")), name=name)(q, k, v, tb2)


def natten_bwd(q, k, v, tb2, do, name):
    H, T, hd = q.shape
    n_rows = T // GRID_W
    rbt = NA_RB * GRID_W
    scale = NA_HD ** -0.5

    def body(q_ref, k_ref, v_ref, tb_ref, do_ref, dq_ref, dk_ref, dv_ref, dtb_ref):
        rb = pl.program_id(1)

        @pl.when(rb == 0)
        def _():
            dk_ref[...] = jnp.zeros_like(dk_ref)
            dv_ref[...] = jnp.zeros_like(dv_ref)
            dtb_ref[...] = jnp.zeros_like(dtb_ref)

        for j in range(NA_RB):
            qv, kk, vv, p, l, tok, dr0 = _na_row_scores(q_ref, k_ref, v_ref, tb_ref, rb, j, n_rows)
            p = p / l
            dov = do_ref[j * GRID_W:(j + 1) * GRID_W, :]
            dp = _bdot(dov, vv, NT)
            ds = p * (dp - jnp.sum(dp * p, axis=1, keepdims=True))
            dq_ref[j * GRID_W:(j + 1) * GRID_W, :] = _bdot(ds, kk, NN) * scale
            dk_ref[tok, :] += _bdot(ds, qv, TN) * scale
            dv_ref[tok, :] += _bdot(p, dov, TN)
            for i in range(NA_ROWS // 2):
                dtb_ref[dr0 + 2 * i] += ds[:, 2 * GRID_W * i:2 * GRID_W * (i + 1)]

    whole = pl.BlockSpec((None, T, hd), lambda h, r: (h, 0, 0))
    blk = pl.BlockSpec((None, rbt, hd), lambda h, r: (h, r, 0))
    tbs = pl.BlockSpec((None, 2 * NA_ROWS - 2, GRID_W, 2 * GRID_W), lambda h, r: (h, 0, 0, 0))
    return pl.pallas_call(
        body, grid=(H, n_rows // NA_RB), in_specs=[blk, whole, whole, tbs, blk],
        out_specs=[blk, whole, whole, tbs],
        out_shape=[SDS((H, T, hd), F32), SDS((H, T, hd), F32), SDS((H, T, hd), F32), SDS(tb2.shape, F32)],
        compiler_params=_cparams(("parallel", "arbitrary")), name=name)(q, k, v, tb2, do)


def _rpb_expand_consts():
    qc = np.arange(GRID_W)[:, None]
    kc = np.arange(GRID_W)[None, :]
    cs = np.clip(qc - NA_COLS // 2, 0, GRID_W - NA_COLS)
    inside = (kc >= cs) & (kc < cs + NA_COLS)
    dc = np.clip(kc - qc, -(NA_COLS - 1), NA_COLS - 1) + (NA_COLS - 1)
    e = np.zeros((128, GRID_W * GRID_W), np.float32)
    flat = (qc * GRID_W + kc)
    e[dc[inside], flat[inside]] = 1.0
    neg = np.where(inside, 0.0, NEG).astype(np.float32).reshape(1, -1)
    return e, neg


def _rpb_fold_consts():
    sa = np.zeros((NA_H * 15, NA_H * 14), np.float32)
    sb = np.zeros((NA_H * 15, NA_H * 14), np.float32)
    for h in range(NA_H):
        for d in range(14):
            sa[h * 15 + d, h * 14 + d] = 1.0
            sb[h * 15 + d + 1, h * 14 + d] = 1.0
    return sa, sb


def rpb_table(rpb, name):
    e, neg = _rpb_expand_consts()
    rp = jnp.pad(rpb.reshape(NA_H * 15, 31), ((0, 0), (0, 128 - 31)))

    def body(r_ref, e_ref, n_ref, o_ref):
        o_ref[...] = jnp.dot(r_ref[...], e_ref[...], preferred_element_type=F32, precision=HI) + n_ref[...]

    t = pl.pallas_call(body, out_shape=SDS((NA_H * 15, GRID_W * GRID_W), F32), name=name)(rp, e, neg)
    t = t.reshape(NA_H, 15, GRID_W, GRID_W)
    return jnp.concatenate([t[:, :14], t[:, 1:]], axis=-1)


def rpb_table_bwd(dtb2, name):
    e, _ = _rpb_expand_consts()
    sa, sb = _rpb_fold_consts()
    a = dtb2[..., :GRID_W].reshape(NA_H * 14, GRID_W * GRID_W)
    b = dtb2[..., GRID_W:].reshape(NA_H * 14, GRID_W * GRID_W)

    def body(a_ref, b_ref, e_ref, sa_ref, sb_ref, o_ref):
        ev = e_ref[...]
        pa = lax.dot_general(a_ref[...], ev, (NT, ((), ())), preferred_element_type=F32, precision=HI)
        pb = lax.dot_general(b_ref[...], ev, (NT, ((), ())), preferred_element_type=F32, precision=HI)
        o_ref[...] = (jnp.dot(sa_ref[...], pa, preferred_element_type=F32, precision=HI)
                      + jnp.dot(sb_ref[...], pb, preferred_element_type=F32, precision=HI))

    d = pl.pallas_call(body, out_shape=SDS((NA_H * 15, 128), F32), name=name)(a, b, e, sa, sb)
    return d[:, :31].reshape(NA_H, 15, 31)


def _kprep_fn(kv, gk):
    return _norm_heads(kv[:, :MEM_W], MEM_HD) * gk, kv[:, MEM_W:]


def mem_kprep(kv, gk, name):
    def body(kv_ref, g_ref, k_ref, v_ref):
        kn, vv = _kprep_fn(kv_ref[...], g_ref[...])
        k_ref[...] = kn.astype(BF16)
        v_ref[...] = vv.astype(BF16)

    return pl.pallas_call(body, out_shape=[SDS((MEM_TOK, MEM_W), BF16)] * 2, name=name)(kv, gk)


def mem_kprep_bwd(kv, gk, dk, dv, name):
    def body(kv_ref, g_ref, dk_ref, dv_ref, dkv_ref, dg_ref):
        _, vjp = jax.vjp(_kprep_fn, kv_ref[...], g_ref[...])
        dkv, dg = vjp((dk_ref[...], dv_ref[...]))
        dkv_ref[...] = dkv.astype(BF16)
        dg_ref[...] = dg

    return pl.pallas_call(body, out_shape=[SDS((MEM_TOK, 2 * MEM_W), BF16), SDS((1, MEM_W), F32)], name=name)(kv, gk, dk, dv)


def _mem_probs(q_ref, k_ref, h):
    hs = slice(h * MEM_HD, (h + 1) * MEM_HD)
    qh, kh = q_ref[:, hs], k_ref[:, hs]
    s = _bdot(qh, kh, NT) * (MEM_HD ** -0.5)
    p = jnp.exp(s - jnp.max(s, axis=1, keepdims=True))
    return hs, qh, kh, p, jnp.sum(p, axis=1, keepdims=True)


def mem_attn_fwd(q, km, vm, name):
    T = q.shape[0]
    tm = _fit(T, 512, 8)

    def body(q_ref, k_ref, v_ref, o_ref):
        for h in range(MEM_H):
            hs, _, _, p, l = _mem_probs(q_ref, k_ref, h)
            o_ref[:, hs] = _bdot(p, v_ref[:, hs], NN) / l

    row = pl.BlockSpec((tm, MEM_W), lambda i: (i, 0))
    full = pl.BlockSpec((MEM_TOK, MEM_W), lambda i: (0, 0))
    return pl.pallas_call(body, grid=(T // tm,), in_specs=[row, full, full], out_specs=row, out_shape=SDS((T, MEM_W), F32),
                          compiler_params=_cparams(("parallel",)), name=name)(q, km, vm)


def mem_attn_bwd(q, km, vm, do, name):
    T = q.shape[0]
    tm = _fit(T, 512, 8)
    scale = MEM_HD ** -0.5

    def body(q_ref, k_ref, v_ref, do_ref, dq_ref, dk_ref, dv_ref):
        @pl.when(pl.program_id(0) == 0)
        def _():
            dk_ref[...] = jnp.zeros_like(dk_ref)
            dv_ref[...] = jnp.zeros_like(dv_ref)

        for h in range(MEM_H):
            hs, qh, kh, p, l = _mem_probs(q_ref, k_ref, h)
            p = p / l
            dov = do_ref[:, hs]
            dp = _bdot(dov, v_ref[:, hs], NT)
            ds = p * (dp - jnp.sum(dp * p, axis=1, keepdims=True))
            dq_ref[:, hs] = _bdot(ds, kh, NN) * scale
            dk_ref[:, hs] += _bdot(ds, qh, TN) * scale
            dv_ref[:, hs] += _bdot(p, dov, TN)

    row = pl.BlockSpec((tm, MEM_W), lambda i: (i, 0))
    full = pl.BlockSpec((MEM_TOK, MEM_W), lambda i: (0, 0))
    return pl.pallas_call(
        body, grid=(T // tm,), in_specs=[row, full, full, row], out_specs=[row, full, full],
        out_shape=[SDS((T, MEM_W), F32), SDS((MEM_TOK, MEM_W), F32), SDS((MEM_TOK, MEM_W), F32)],
        compiler_params=_cparams(("arbitrary",)), name=name)(q, km, vm, do)


SLOT_BLOCK_ELEMS = 512 * 1024


def _slot_rows(R, Cc):
    return _fit(R, max(16, SLOT_BLOCK_ELEMS // Cc // 16 * 16), 16)


def sum_slots(a, name):
    n, R, Cc = a.shape
    tr = _slot_rows(R, Cc)

    def body(a_ref, o_ref):
        s = a_ref[0].astype(F32)
        for i in range(1, n):
            s = s + a_ref[i].astype(F32)
        o_ref[...] = s

    return pl.pallas_call(body, grid=(R // tr,), in_specs=[pl.BlockSpec((n, tr, Cc), lambda i: (0, i, 0))],
                          out_specs=pl.BlockSpec((tr, Cc), lambda i: (i, 0)), out_shape=SDS((R, Cc), F32),
                          compiler_params=_cparams(("parallel",)), name=name)(a)


def pair_sum(g, land, c_idx, name):
    _, R, Cc = g.shape
    tr = _slot_rows(R, Cc)

    def body(c_ref, g_ref, l_ref, o_ref):
        o_ref[...] = (g_ref[...].astype(F32) + l_ref[...].astype(F32)).astype(o_ref.dtype)

    blk = pl.BlockSpec((None, tr, Cc), lambda k, i, c: (k, i, 0))
    gs = pltpu.PrefetchScalarGridSpec(
        num_scalar_prefetch=1, grid=(4, R // tr),
        in_specs=[pl.BlockSpec((None, tr, Cc), lambda k, i, c: (2 * k + c[0], i, 0)), blk], out_specs=blk)
    return pl.pallas_call(body, grid_spec=gs, out_shape=SDS((4, R, Cc), g.dtype),
                          compiler_params=_cparams(("parallel", "parallel")), name=name)(c_idx, g, land)


def adamw(w, g, m, v, name):
    R, Cc = w.shape
    tr = _fit(R, max(8, (262144 // max(Cc, 128)) // 8 * 8), 8)
    c1 = 1.0 - ADAM_B1 ** ADAM_STEP
    c2 = 1.0 - ADAM_B2 ** ADAM_STEP

    def body(w_ref, g_ref, m_ref, v_ref, d_ref, mo_ref, vo_ref):
        gv = g_ref[...]
        m2 = ADAM_B1 * m_ref[...] + (1.0 - ADAM_B1) * gv
        v2 = ADAM_B2 * v_ref[...] + (1.0 - ADAM_B2) * (gv * gv)
        d_ref[...] = -ADAM_LR * ((m2 / c1) / (jnp.sqrt(v2 / c2) + ADAM_EPS) + ADAM_WD * w_ref[...])
        mo_ref[...] = m2
        vo_ref[...] = v2

    blk = pl.BlockSpec((tr, Cc), lambda i: (i, 0))
    return pl.pallas_call(body, grid=(R // tr,), in_specs=[blk] * 4, out_specs=[blk] * 3, out_shape=[SDS((R, Cc), F32)] * 3,
                          compiler_params=_cparams(("parallel",)), name=name)(w, g, m, v)


ANY = pl.BlockSpec(memory_space=pl.ANY)


def _my_pos():
    return lax.axis_index("x"), lax.axis_index("y"), lax.axis_index("c")


def all_gather(shards, name):
    n = len(shards)

    def body(*refs):
        x_refs, out_refs = refs[:n], refs[n:2 * n]
        send_sems, recv_sems, local_sems = refs[2 * n:]
        x, y, c = _my_pos()
        me, sibling = (x, y, c), (x, y, 1 - c)
        chips = [(1 - x, y), (x, 1 - y), (1 - x, 1 - y)]

        def slot(a, px, py, pc):
            return out_refs[a].at[4 * px + 2 * py + pc]

        def copy(a, k, block, to, src=None):
            return pltpu.make_async_remote_copy(
                src_ref=slot(a, *block) if src is None else src, dst_ref=slot(a, *block),
                send_sem=send_sems.at[7 * a + k], recv_sem=recv_sems.at[7 * a + k], device_id=to, device_id_type=MESH)

        mine = [pltpu.make_async_copy(x_refs[a], slot(a, *me), local_sems.at[a]) for a in range(n)]
        for cp in mine:
            cp.start()
        first = []
        for a in range(n):
            first.append(copy(a, 0, me, sibling, src=x_refs[a]))
            first += [copy(a, 1 + j, me, (*chip, c), src=x_refs[a]) for j, chip in enumerate(chips)]
        for cp in first:
            cp.start()
        passed = []
        for j, chip in enumerate(chips):
            for a in range(n):
                copy(a, 1 + j, (*chip, c), me).wait_recv()
                passed.append(copy(a, 4 + j, (*chip, c), sibling))
                passed[-1].start()
        for a in range(n):
            copy(a, 0, sibling, me).wait_recv()
        for j, chip in enumerate(chips):
            for a in range(n):
                copy(a, 4 + j, (*chip, 1 - c), me).wait_recv()
        for cp in first + passed:
            cp.wait_send()
        for cp in mine:
            cp.wait()

    return pl.pallas_call(
        body, out_shape=[SDS((N_DEV,) + s.shape, s.dtype) for s in shards], in_specs=[ANY] * n, out_specs=[ANY] * n,
        scratch_shapes=[pltpu.SemaphoreType.DMA((7 * n,)), pltpu.SemaphoreType.DMA((7 * n,)), pltpu.SemaphoreType.DMA((n,))],
        name=name)(*shards)


def pair_exchange(gs, name):
    n = len(gs)

    def body(*refs):
        g_refs, land_refs = refs[:n], refs[n:2 * n]
        send_sems, recv_sems = refs[2 * n:]
        x, y, c = _my_pos()
        sibling = (x, y, 1 - c)
        copies = [pltpu.make_async_remote_copy(
            src_ref=g_refs[a].at[2 * k + (1 - c)], dst_ref=land_refs[a].at[k], send_sem=send_sems.at[4 * a + k],
            recv_sem=recv_sems.at[4 * a + k], device_id=sibling, device_id_type=MESH) for a in range(n) for k in range(4)]
        for cp in copies:
            cp.start()
        for cp in copies:
            cp.wait_recv()
        for cp in copies:
            cp.wait_send()

    return pl.pallas_call(
        body, out_shape=[SDS((4,) + g.shape[1:], g.dtype) for g in gs], in_specs=[ANY] * n, out_specs=[ANY] * n,
        scratch_shapes=[pltpu.SemaphoreType.DMA((4 * n,)), pltpu.SemaphoreType.DMA((4 * n,))], name=name)(*gs)


def chip_exchange(ss, name):
    n = len(ss)

    def body(*refs):
        s_refs, land_refs = refs[:n], refs[n:2 * n]
        send_sems, recv_sems, local_sems = refs[2 * n:]
        x, y, c = _my_pos()
        my_chip = 2 * x + y
        chips = [(1 - x, y), (x, 1 - y), (1 - x, 1 - y)]
        own = [pltpu.make_async_copy(s_refs[a].at[my_chip], land_refs[a].at[my_chip], local_sems.at[a]) for a in range(n)]
        for cp in own:
            cp.start()

        def copy(a, j, src_chip, dst_chip):
            px, py = chips[j]
            return pltpu.make_async_remote_copy(
                src_ref=s_refs[a].at[src_chip], dst_ref=land_refs[a].at[dst_chip], send_sem=send_sems.at[3 * a + j],
                recv_sem=recv_sems.at[3 * a + j], device_id=(px, py, c), device_id_type=MESH)

        sends = [copy(a, j, 2 * px + py, my_chip) for a in range(n) for j, (px, py) in enumerate(chips)]
        for cp in sends:
            cp.start()
        for a in range(n):
            for j, (px, py) in enumerate(chips):
                copy(a, j, my_chip, 2 * px + py).wait_recv()
        for cp in sends:
            cp.wait_send()
        for cp in own:
            cp.wait()

    return pl.pallas_call(
        body, out_shape=[SDS(s.shape, s.dtype) for s in ss], in_specs=[ANY] * n, out_specs=[ANY] * n,
        scratch_shapes=[pltpu.SemaphoreType.DMA((3 * n,)), pltpu.SemaphoreType.DMA((3 * n,)), pltpu.SemaphoreType.DMA((n,))],
        name=name)(*ss)


W_IN_SHARD, W_IN_PACKED = 644, 768
BIG = ("ffn_w13", "ffn_w2", "w_out", "mem_wkv", "w_in")
GATE_W = ("gla_wg2_f", "gla_wg2_b")
SHARD_NAMES = BIG + GATE_W


def _permuted_ranges(c0, c1):
    res = []
    for o0, o1, p0 in ((0, ORIG_GATE0, 0), (ORIG_GATE0, ORIG_AFTER_GATE, COL_TAIL), (ORIG_AFTER_GATE, IN_COLS, ORIG_GATE0)):
        lo, hi = max(c0, o0), min(c1, o1)
        if lo < hi:
            res.append((p0 + lo - o0, p0 + hi - o0))
    return res


def assemble_w_in(blocks):
    placed = []
    for d in range(N_DEV):
        c = d * W_IN_SHARD
        for p0, p1 in _permuted_ranges(c, c + W_IN_SHARD):
            placed.append((p0, blocks[d][:, c - d * W_IN_SHARD:c - d * W_IN_SHARD + (p1 - p0)]))
            c += p1 - p0
    placed.sort(key=lambda t: t[0])
    return jnp.concatenate([t[1] for t in placed] + [jnp.zeros((blocks.shape[1], PC - IN_COLS), blocks.dtype)], axis=1)


def split_w_in_grad(g):
    pad = jnp.zeros((g.shape[0], W_IN_PACKED - W_IN_SHARD), g.dtype)
    return jnp.stack([jnp.concatenate([g[:, p0:p1] for p0, p1 in _permuted_ranges(d * W_IN_SHARD, (d + 1) * W_IN_SHARD)]
                                      + [pad], axis=1) for d in range(N_DEV)])


def gather_layer_weights(shards, l):
    send = [jnp.pad(shards[n], ((0, 0), (0, W_IN_PACKED - W_IN_SHARD))) if n == "w_in" else shards[n] for n in BIG]
    got = dict(zip(BIG, all_gather(send, f"l{l}_gather_weights")))
    full = {n: got[n].reshape(-1, got[n].shape[-1]) for n in ("ffn_w2", "w_out", "mem_wkv")}
    full["ffn_w13"] = got["ffn_w13"]
    full["w_in"] = assemble_w_in(got["w_in"])
    return full


def reduce_scatter_layer(gw, c_idx, l):
    g = [split_w_in_grad(gw[n]) if n == "w_in" else gw[n].reshape((N_DEV, -1, gw[n].shape[-1])) for n in BIG]
    land1 = pair_exchange(g, f"l{l}_rs_pair_exchange")
    chip = [pair_sum(g[i], land1[i], c_idx, f"l{l}_rs_pair_sum_{n}") for i, n in enumerate(BIG)]
    land2 = chip_exchange(chip, f"l{l}_rs_chip_exchange")
    out = {n: sum_slots(land2[i], f"l{l}_rs_chip_sum_{n}") for i, n in enumerate(BIG)}
    out["w_in"] = out["w_in"][:, :W_IN_SHARD]
    return out


def gather_gate_weights(w, depth):
    mine = jnp.concatenate([w[n].reshape(-1) for n in GATE_W]).reshape(-1, 128)
    got = all_gather([mine], "gather_gate_weights")[0].reshape(N_DEV, len(GATE_W), depth, GLA_RANK, GLA_DK // N_DEV)
    return [{n: got[:, i, l].transpose(1, 0, 2).reshape(GLA_RANK, GLA_DK) for i, n in enumerate(GATE_W)} for l in range(depth)]


def pad_gate_weight(wg2, backward_dir):
    r0 = GLA_RANK if backward_dir else 0
    w = wg2.astype(F32).reshape(GLA_RANK, GLA_H, GLA_HK).transpose(1, 0, 2)
    return jnp.pad(w, ((0, 0), (r0, 128 - GLA_RANK - r0), (0, 0)))


def unpad_gate_grad(dw, backward_dir):
    r0 = GLA_RANK if backward_dir else 0
    return dw[:, r0:r0 + GLA_RANK, :].transpose(1, 0, 2).reshape(GLA_RANK, GLA_DK)


def to_heads(t):
    T = t.shape[0]
    return t.reshape(T, NA_H, NA_HD).transpose(1, 0, 2)


def from_heads(t):
    return t.transpose(1, 0, 2).reshape(t.shape[1], NA_W)


REPLICATED = ("attn_norm", "gla_bg_f", "gla_bg_b", "gla_out_norm", "na_q_norm", "na_k_norm", "na_rpb", "na_out_norm",
              "mem_norm", "mem_q_norm", "mem_k_norm", "mem_out_norm", "ffn_norm")
WEIGHTS = ("attn_norm", "w_in", "gla_wg2_f", "gla_bg_f", "gla_wg2_b", "gla_bg_b", "gla_out_norm", "na_q_norm", "na_k_norm",
           "na_rpb", "na_out_norm", "mem_norm", "mem_wkv", "mem_q_norm", "mem_k_norm", "mem_out_norm", "w_out", "ffn_norm",
           "ffn_w13", "ffn_w2")


def fold_heads(dg, n_heads, name):
    hd = dg.shape[1] // n_heads
    fold = (np.arange(dg.shape[1])[:, None] % hd == np.arange(128)[None, :]).astype(np.float32)
    out = matmul(jnp.pad(dg, ((0, 7), (0, 0))), fold, "nn", name, exact=True)
    return out[0, :hd]


def layer_fwd(x, mem_n_in, p, W, l):
    tag = f"l{l}_"
    row = lambda v: v.reshape(1, -1)
    sv = {"x": x}
    sv["xn"] = rms_fwd(x, row(p["attn_norm"]), tag + "attn_rms")
    proj = matmul(sv["xn"], W["w_in"], "nn", tag + "proj", tn=768, tk=2048)
    sv["proj"] = proj
    sv["wg_f"], sv["wg_b"] = pad_gate_weight(W["gla_wg2_f"], False), pad_gate_weight(W["gla_wg2_b"], True)
    sv["bg_f"], sv["bg_b"] = p["gla_bg_f"].reshape(GLA_H, 1, GLA_HK), p["gla_bg_b"].reshape(GLA_H, 1, GLA_HK)
    sv["o_f"], sv["s_f"] = gla_fwd(proj, sv["wg_f"], sv["bg_f"], False, tag + "gla_f")
    sv["o_b"], sv["s_b"] = gla_fwd(proj, sv["wg_b"], sv["bg_b"], True, tag + "gla_b")
    sv["gq"], sv["gk"] = jnp.tile(row(p["na_q_norm"]), (1, NA_H)), jnp.tile(row(p["na_k_norm"]), (1, NA_H))
    sv["gmq"], sv["gmk"] = jnp.tile(row(p["mem_q_norm"]), (1, MEM_H)), jnp.tile(row(p["mem_k_norm"]), (1, MEM_H))
    qn, kn, vn, mqn = pre_fwd(proj, sv["gq"], sv["gk"], sv["gmq"], tag + "pre")
    sv["q_hm"], sv["k_hm"], sv["v_hm"], sv["mqn"] = to_heads(qn), to_heads(kn), to_heads(vn), mqn
    sv["tb2"] = rpb_table(p["na_rpb"], tag + "rpb_table")
    sv["o_na"] = from_heads(natten_fwd(sv["q_hm"], sv["k_hm"], sv["v_hm"], sv["tb2"], tag + "natten"))
    sv["mem_n"] = rms_fwd(mem_n_in, row(p["mem_norm"]), tag + "mem_rms")
    sv["kv"] = matmul(sv["mem_n"], W["mem_wkv"], "nn", tag + "mem_kv", tn=512, tk=2048)
    sv["km"], sv["vm"] = mem_kprep(sv["kv"], sv["gmk"], tag + "mem_kprep")
    sv["o_mem"] = mem_attn_fwd(mqn, sv["km"], sv["vm"], tag + "mem_attn")
    sv["ycat"] = post_fwd(sv["o_f"], sv["o_b"], proj, sv["o_na"], sv["o_mem"], row(p["gla_out_norm"]),
                          row(p["na_out_norm"]), row(p["mem_out_norm"]), tag + "post")
    x1 = matmul(sv["ycat"], W["w_out"], "nn", tag + "out_proj", res=x, tk=2048)
    sv["x1"] = x1
    sv["h"] = rms_fwd(x1, row(p["ffn_norm"]), tag + "ffn_rms")
    sv["gu"], sv["a"] = ffn_up_swiglu(sv["h"], W["ffn_w13"], tag + "ffn_up")
    x2 = matmul(sv["a"], W["ffn_w2"], "nn", tag + "ffn_down", res=x1, tk=1408)
    return x2, sv


def layer_bwd(dx2, dx2_b, mem_n_in, p, W, sv, l):
    tag = f"l{l}_b_"
    row = lambda v: v.reshape(1, -1)
    gw, gs = {}, {}
    gw["ffn_w2"] = matmul(sv["a"], dx2_b, "tn", tag + "dw2", out_dtype=BF16, tm=1408, tn=1024, tk=1024)
    dgu = ffn_down_bwd(dx2_b, W["ffn_w2"], sv["gu"], tag + "d_swiglu")
    dh = matmul(dgu, W["ffn_w13"], "nt", tag + "d_h", a_halves=True, b_blocked=True)
    gw["ffn_w13"] = matmul(sv["h"], dgu, "tn", tag + "dw13", out_dtype=BF16, tm=2048, tn=1408, tk=1024,
                           b_halves=True, out_blocked=True)
    dx1, dx1_b, dg = rms_bwd(sv["x1"], row(p["ffn_norm"]), dh, dx2, tag + "ffn_rms")
    gs["ffn_norm"] = dg[0]
    dycat = matmul(dx1_b, W["w_out"], "nt", tag + "d_ycat", tk=2048)
    gw["w_out"] = matmul(sv["ycat"], dx1_b, "tn", tag + "dw_out", out_dtype=BF16, tm=2048, tn=1024, tk=1024)
    d_o, d_r, d_ona, d_omem, dgg, dgn, dgm = post_bwd(
        sv["o_f"], sv["o_b"], sv["proj"], sv["o_na"], sv["o_mem"], row(p["gla_out_norm"]), row(p["na_out_norm"]),
        row(p["mem_out_norm"]), dycat, tag + "post")
    gs["gla_out_norm"], gs["na_out_norm"], gs["mem_out_norm"] = dgg[0], dgn[0], dgm[0]
    dq_f, dk_f, dv_f, dt_f, dwg_f, dbg_f = gla_bwd(sv["proj"], sv["wg_f"], sv["bg_f"], sv["s_f"], d_o, False, tag + "gla_f")
    dq_b, dk_b, dv_b, dt_b, dwg_b, dbg_b = gla_bwd(sv["proj"], sv["wg_b"], sv["bg_b"], sv["s_b"], d_o, True, tag + "gla_b")
    gs["gla_wg2_f"], gs["gla_wg2_b"] = unpad_gate_grad(dwg_f, False), unpad_gate_grad(dwg_b, True)
    gs["gla_bg_f"], gs["gla_bg_b"] = dbg_f.reshape(-1), dbg_b.reshape(-1)
    dq_hm, dk_hm, dv_hm, dtb2 = natten_bwd(sv["q_hm"], sv["k_hm"], sv["v_hm"], sv["tb2"], to_heads(d_ona), tag + "natten")
    gs["na_rpb"] = rpb_table_bwd(dtb2, tag + "rpb_table")
    d_mqn, dkm, dvm = mem_attn_bwd(sv["mqn"], sv["km"], sv["vm"], d_omem, tag + "mem_attn")
    dkv, dgmk = mem_kprep_bwd(sv["kv"], sv["gmk"], dkm, dvm, tag + "mem_kprep")
    gs["mem_k_norm"] = fold_heads(dgmk, MEM_H, tag + "fold_mk")
    gw["mem_wkv"] = matmul(sv["mem_n"], dkv, "tn", tag + "dw_kv", out_dtype=BF16, tm=2048, tn=1024, tk=256)
    d_memn = matmul(dkv, W["mem_wkv"], "nt", tag + "d_memn", tk=1024)
    _, _, dg = rms_bwd(mem_n_in, row(p["mem_norm"]), d_memn, None, tag + "mem_rms")
    gs["mem_norm"] = dg[0]
    dproj, dgq, dgk, dgmq = pre_bwd(sv["proj"], sv["gq"], sv["gk"], sv["gmq"], from_heads(dq_hm), from_heads(dk_hm),
                                    from_heads(dv_hm), d_mqn, dq_f, dq_b, dk_f, dk_b, dv_f, dv_b, d_r, dt_f, dt_b, tag + "pre")
    gs["na_q_norm"] = fold_heads(dgq, NA_H, tag + "fold_q")
    gs["na_k_norm"] = fold_heads(dgk, NA_H, tag + "fold_k")
    gs["mem_q_norm"] = fold_heads(dgmq, MEM_H, tag + "fold_mq")
    dxn = matmul(dproj, W["w_in"], "nt", tag + "d_xn", tk=768)
    gw["w_in"] = matmul(sv["xn"], dproj, "tn", tag + "dw_in", out_dtype=BF16, tm=2048, tn=768, tk=1024)
    dx, dx_b, dg = rms_bwd(sv["x"], row(p["attn_norm"]), dxn, dx1, tag + "attn_rms")
    gs["attn_norm"] = dg[0]
    return dx, dx_b, gw, gs


def kernel(x, mem, attn_norm, w_in, gla_wg2_f, gla_bg_f, gla_wg2_b, gla_bg_b, gla_out_norm, na_q_norm, na_k_norm, na_rpb, na_out_norm, mem_norm, mem_wkv, mem_q_norm, mem_k_norm, mem_out_norm, w_out, ffn_norm, ffn_w13, ffn_w2, loss_target, m_attn_norm, m_w_in, m_gla_wg2_f, m_gla_bg_f, m_gla_wg2_b, m_gla_bg_b, m_gla_out_norm, m_na_q_norm, m_na_k_norm, m_na_rpb, m_na_out_norm, m_mem_norm, m_mem_wkv, m_mem_q_norm, m_mem_k_norm, m_mem_out_norm, m_w_out, m_ffn_norm, m_ffn_w13, m_ffn_w2, v_attn_norm, v_w_in, v_gla_wg2_f, v_gla_bg_f, v_gla_wg2_b, v_gla_bg_b, v_gla_out_norm, v_na_q_norm, v_na_k_norm, v_na_rpb, v_na_out_norm, v_mem_norm, v_mem_wkv, v_mem_q_norm, v_mem_k_norm, v_mem_out_norm, v_w_out, v_ffn_norm, v_ffn_w13, v_ffn_w2):
    w = dict(attn_norm=attn_norm, w_in=w_in, gla_wg2_f=gla_wg2_f, gla_bg_f=gla_bg_f, gla_wg2_b=gla_wg2_b, gla_bg_b=gla_bg_b,
             gla_out_norm=gla_out_norm, na_q_norm=na_q_norm, na_k_norm=na_k_norm, na_rpb=na_rpb, na_out_norm=na_out_norm,
             mem_norm=mem_norm, mem_wkv=mem_wkv, mem_q_norm=mem_q_norm, mem_k_norm=mem_k_norm, mem_out_norm=mem_out_norm,
             w_out=w_out, ffn_norm=ffn_norm, ffn_w13=ffn_w13, ffn_w2=ffn_w2)
    mom = dict(attn_norm=m_attn_norm, w_in=m_w_in, gla_wg2_f=m_gla_wg2_f, gla_bg_f=m_gla_bg_f, gla_wg2_b=m_gla_wg2_b,
               gla_bg_b=m_gla_bg_b, gla_out_norm=m_gla_out_norm, na_q_norm=m_na_q_norm, na_k_norm=m_na_k_norm, na_rpb=m_na_rpb,
               na_out_norm=m_na_out_norm, mem_norm=m_mem_norm, mem_wkv=m_mem_wkv, mem_q_norm=m_mem_q_norm,
               mem_k_norm=m_mem_k_norm, mem_out_norm=m_mem_out_norm, w_out=m_w_out, ffn_norm=m_ffn_norm, ffn_w13=m_ffn_w13,
               ffn_w2=m_ffn_w2)
    var = dict(attn_norm=v_attn_norm, w_in=v_w_in, gla_wg2_f=v_gla_wg2_f, gla_bg_f=v_gla_bg_f, gla_wg2_b=v_gla_wg2_b,
               gla_bg_b=v_gla_bg_b, gla_out_norm=v_gla_out_norm, na_q_norm=v_na_q_norm, na_k_norm=v_na_k_norm, na_rpb=v_na_rpb,
               na_out_norm=v_na_out_norm, mem_norm=v_mem_norm, mem_wkv=v_mem_wkv, mem_q_norm=v_mem_q_norm,
               mem_k_norm=v_mem_k_norm, mem_out_norm=v_mem_out_norm, w_out=v_w_out, ffn_norm=v_ffn_norm, ffn_w13=v_ffn_w13,
               ffn_w2=v_ffn_w2)
    depth = attn_norm.shape[0]
    T = x.shape[1]
    xs, mem0, tgt = x.reshape(T, D_MODEL), mem.reshape(MEM_TOK, D_MODEL), loss_target.reshape(T, D_MODEL)
    c_idx = lax.axis_index("c").astype(jnp.int32).reshape(1)

    gates = gather_gate_weights(w, depth)
    W = [{**gather_layer_weights({n: w[n][l].astype(BF16) for n in BIG}, l), **gates[l]} for l in range(depth)]
    P = [{n: w[n][l] for n in REPLICATED} for l in range(depth)]

    saved = []
    h = xs
    for l in range(depth):
        h, sv = layer_fwd(h, mem0, P[l], W[l], l)
        saved.append(sv)
    dy, dy_b, lsum = loss_bwd(h, tgt, "loss")
    loss = lax.psum(lsum[0, 0], ("x", "y", "c")) * (0.5 / D_MODEL)

    g_shard, g_small = [None] * depth, [None] * depth
    for l in range(depth - 1, -1, -1):
        dy, dy_b, gw, g_small[l] = layer_bwd(dy, dy_b, mem0, P[l], W[l], saved[l], l)
        g_shard[l] = reduce_scatter_layer(gw, c_idx, l)
        saved[l] = None
    grad_x = dy.reshape(x.shape)

    small_names = REPLICATED + GATE_W
    small = jnp.concatenate([g_small[l][n].reshape(-1) for l in range(depth) for n in small_names])
    n_small = small.shape[0]
    rows = -(-n_small // 1024) * 8
    small = jnp.pad(small, (0, rows * 128 - n_small)).reshape(rows, 128)
    small = sum_slots(all_gather([small], "gather_small_grads")[0], "sum_small_grads").reshape(-1)
    grads, off = {}, 0
    per_layer = {n: [] for n in small_names}
    my_cols = (4 * lax.axis_index("x") + 2 * lax.axis_index("y") + lax.axis_index("c")) * (GLA_DK // N_DEV)
    for l in range(depth):
        for n in small_names:
            shp = (GLA_RANK, GLA_DK) if n in GATE_W else w[n].shape[1:]
            g = small[off:off + int(np.prod(shp))].reshape(shp)
            off += int(np.prod(shp))
            per_layer[n].append(lax.dynamic_slice_in_dim(g, my_cols, GLA_DK // N_DEV, axis=1) if n in GATE_W else g)
    for n in small_names:
        grads[n] = jnp.stack(per_layer[n])
    for n in BIG:
        grads[n] = jnp.stack([g_shard[l][n] for l in range(depth)])

    delta, new_m, new_v = {}, {}, {}
    for n in WEIGHTS:
        shp = w[n].shape
        two_d = (shp[0], int(np.prod(shp[1:]))) if n in REPLICATED else (int(np.prod(shp[:-1])), shp[-1])
        d_, m_, v_ = adamw(w[n].reshape(two_d), grads[n].reshape(two_d), mom[n].reshape(two_d), var[n].reshape(two_d),
                           "adamw_" + n)
        delta[n], new_m[n], new_v[n] = d_.reshape(shp), m_.reshape(shp), v_.reshape(shp)

    return (loss, grad_x, *[grads[n] for n in WEIGHTS], *[delta[n] for n in WEIGHTS], *[new_m[n] for n in WEIGHTS],
            *[new_v[n] for n in WEIGHTS])
```

```python
import functools

import numpy as np
import jax
import jax.numpy as jnp
from jax import lax
from jax.experimental import pallas as pl
from jax.experimental.pallas import tpu as pltpu

F32, BF16 = jnp.float32, jnp.bfloat16
HI = lax.Precision.HIGHEST
SDS = jax.ShapeDtypeStruct
MESH = pl.DeviceIdType.MESH

D_MODEL = 2048
GRID_W = 64
GLA_H, GLA_HK, GLA_HV, GLA_RANK, GLA_TAU, GLA_C = 4, 128, 256, 16, 16.0, 64
GLA_DK, GLA_DV = GLA_H * GLA_HK, GLA_H * GLA_HV
NA_H, NA_HD, NA_ROWS, NA_COLS = 8, 64, 8, 16
NA_W = NA_H * NA_HD
MEM_H, MEM_HD, MEM_TOK = 4, 128, 256
MEM_W = MEM_H * MEM_HD
D_FF = 5632
IN_COLS = 5152
RMS_EPS = 1e-6
ADAM_LR, ADAM_B1, ADAM_B2, ADAM_EPS, ADAM_WD, ADAM_STEP = 0.001, 0.9, 0.999, 1e-08, 0.01, 10
N_DEV = 8

PC = 5376
COL_R, COL_NQ, COL_NK, COL_NV, COL_MQ, COL_TAIL = 2048, 3072, 3584, 4096, 4608, 5120
ORIG_GATE0 = 3072
ORIG_AFTER_GATE = 3104

GLA_G = 8
VMEM_LIMIT = 56 * 1024 * 1024
NEG = -1e30


def _cparams(sem):
    return pltpu.CompilerParams(dimension_semantics=sem, vmem_limit_bytes=VMEM_LIMIT)


def _fit(n, pref, unit=128):
    if n <= pref:
        return n
    t = (pref // unit) * unit
    while t >= unit:
        if n % t == 0:
            return t
        t -= unit
    return n


def _bdot(a, b, dims):
    return lax.dot_general(a.astype(BF16), b.astype(BF16), (dims, ((), ())), preferred_element_type=F32)


NN, NT, TN = ((1,), (0,)), ((1,), (1,)), ((0,), (0,))


@functools.partial(jax.custom_vjp, nondiff_argnums=(2,))
def _bdot_vjp(a, b, dims):
    return _bdot(a, b, dims)


def _bdot_vjp_fwd(a, b, dims):
    return _bdot(a, b, dims), (a, b)


def _bdot_vjp_bwd(dims, res, ct):
    a, b = res
    if dims == NN:
        return _bdot(ct, b, NT), _bdot(a, ct, TN)
    if dims == NT:
        return _bdot(ct, b, NN), _bdot(ct, a, TN)
    return _bdot(b, ct, NT), _bdot(a, ct, NN)


_bdot_vjp.defvjp(_bdot_vjp_fwd, _bdot_vjp_bwd)


def _split_dot(c, x, dims):
    hi = x.astype(BF16)
    lo = (x - hi.astype(F32)).astype(BF16)
    cb = c.astype(BF16)
    return (lax.dot_general(cb, hi, (dims, ((), ())), preferred_element_type=F32)
            + lax.dot_general(cb, lo, (dims, ((), ())), preferred_element_type=F32))


@jax.custom_vjp
def _tri_sum(cmat, x):
    return _split_dot(cmat, x, NN)


def _tri_sum_fwd(cmat, x):
    return _split_dot(cmat, x, NN), cmat


def _tri_sum_bwd(cmat, ct):
    return jnp.zeros_like(cmat), _split_dot(cmat, ct, TN)


_tri_sum.defvjp(_tri_sum_fwd, _tri_sum_bwd)


def matmul(a, b, mode, name, out_dtype=F32, res=None, tm=1024, tn=1024, tk=512, exact=False,
           a_halves=False, b_halves=False, b_blocked=False, out_blocked=False):
    if mode == "nn":
        M, K = a.shape
        N = b.shape[0] * b.shape[2] if b_blocked else b.shape[1]
        if b_blocked:
            tn = b.shape[2]
    elif mode == "nt":
        M, K = (a.shape[1], 2 * a.shape[2]) if a_halves else a.shape
        N = b.shape[1] if b_blocked else b.shape[0]
        if b_blocked:
            tk = b.shape[2]
    else:
        K, M = a.shape
        N = 2 * b.shape[2] if b_halves else b.shape[1]
    tm, tn, tk = _fit(M, tm, 8 if mode != "tn" else 128), _fit(N, tn), _fit(K, tk, 128 if mode != "tn" else 16)
    nk = K // tk
    dims = {"nn": NN, "nt": NT, "tn": TN}[mode]

    def body(*refs):
        if res is None:
            a_ref, b_ref, o_ref = refs[:3]
            r_ref = None
            acc = refs[3] if nk > 1 else None
        else:
            a_ref, b_ref, r_ref, o_ref = refs[:4]
            acc = refs[4] if nk > 1 else None
        if exact:
            part = lax.dot_general(a_ref[...], b_ref[...], (dims, ((), ())), preferred_element_type=F32, precision=HI)
        else:
            part = _bdot(a_ref[...], b_ref[...], dims)

        def finish(val):
            if r_ref is not None:
                val = val + r_ref[...]
            o_ref[...] = val.astype(out_dtype)

        if nk == 1:
            finish(part)
        else:
            kk = pl.program_id(2)

            @pl.when(kk == 0)
            def _():
                acc[...] = part

            @pl.when(kk > 0)
            def _():
                acc[...] += part

            @pl.when(kk == nk - 1)
            def _():
                finish(acc[...])

    if mode == "tn":
        a_spec = pl.BlockSpec((tk, tm), lambda i, j, k: (k, i))
    elif a_halves:
        nh = K // 2 // tk
        a_spec = pl.BlockSpec((None, tm, tk), lambda i, j, k: (k // nh, i, k % nh))
    else:
        a_spec = pl.BlockSpec((tm, tk), lambda i, j, k: (i, k))
    if b_blocked:
        b_spec = (pl.BlockSpec((None, tk, tn), lambda i, j, k: (j, k, 0)) if mode == "nn"
                  else pl.BlockSpec((None, tn, tk), lambda i, j, k: (k, j, 0)))
    elif b_halves:
        nh = N // 2 // tn
        b_spec = pl.BlockSpec((None, tk, tn), lambda i, j, k: (j // nh, k, j % nh))
    elif mode == "nt":
        b_spec = pl.BlockSpec((tn, tk), lambda i, j, k: (j, k))
    else:
        b_spec = pl.BlockSpec((tk, tn), lambda i, j, k: (k, j))
    if out_blocked:
        o_spec, o_shape = pl.BlockSpec((None, tm, tn), lambda i, j, k: (j, i, 0)), SDS((N // tn, M, tn), out_dtype)
    else:
        o_spec, o_shape = pl.BlockSpec((tm, tn), lambda i, j, k: (i, j)), SDS((M, N), out_dtype)
    in_specs, args = [a_spec, b_spec], [a, b]
    if res is not None:
        in_specs.append(o_spec)
        args.append(res)
    return pl.pallas_call(
        body, grid=(M // tm, N // tn, nk), in_specs=in_specs, out_specs=o_spec, out_shape=o_shape,
        scratch_shapes=[pltpu.VMEM((tm, tn), F32)] if nk > 1 else [],
        compiler_params=_cparams(("parallel", "parallel", "arbitrary")), name=name)(*args)


def rms_fwd(x, g, name):
    T, D = x.shape
    tm = _fit(T, 512, 8)

    def body(x_ref, g_ref, o_ref):
        xv = x_ref[...]
        r = lax.rsqrt(jnp.mean(xv * xv, axis=-1, keepdims=True) + RMS_EPS)
        o_ref[...] = (xv * r * g_ref[...]).astype(BF16)

    return pl.pallas_call(
        body, grid=(T // tm,), in_specs=[pl.BlockSpec((tm, D), lambda i: (i, 0)), pl.BlockSpec((1, D), lambda i: (0, 0))],
        out_specs=pl.BlockSpec((tm, D), lambda i: (i, 0)), out_shape=SDS((T, D), BF16),
        compiler_params=_cparams(("parallel",)), name=name)(x, g)


def rms_bwd(x, g, dy, dres, name):
    T, D = x.shape
    tm = _fit(T, 256, 8)
    has_res = dres is not None

    def body(*refs):
        if has_res:
            x_ref, g_ref, dy_ref, dres_ref, dx_ref, dxb_ref, dg_ref = refs
        else:
            x_ref, g_ref, dy_ref, dx_ref, dxb_ref, dg_ref = refs
        xv, dyv = x_ref[...], dy_ref[...]
        r = lax.rsqrt(jnp.mean(xv * xv, axis=-1, keepdims=True) + RMS_EPS)
        xh = xv * r
        dxh = dyv * g_ref[...]
        dx = r * (dxh - xh * jnp.mean(dxh * xh, axis=-1, keepdims=True))
        if has_res:
            dx = dx + dres_ref[...]
        dx_ref[...] = dx
        dxb_ref[...] = dx.astype(BF16)

        @pl.when(pl.program_id(0) == 0)
        def _():
            dg_ref[...] = jnp.zeros_like(dg_ref)

        dg_ref[...] += jnp.sum(dyv * xh, axis=0, keepdims=True)

    row = pl.BlockSpec((tm, D), lambda i: (i, 0))
    vec = pl.BlockSpec((1, D), lambda i: (0, 0))
    args = [x, g, dy] + ([dres] if has_res else [])
    return pl.pallas_call(
        body, grid=(T // tm,), in_specs=[row, vec, row] + ([row] if has_res else []),
        out_specs=[row, row, vec], out_shape=[SDS((T, D), F32), SDS((T, D), BF16), SDS((1, D), F32)],
        compiler_params=_cparams(("arbitrary",)), name=name)(*args)


def ffn_up_swiglu(h, w13b, name):
    T, D = h.shape
    nb, _, tb = w13b.shape
    nh = nb // 2
    tm = _fit(T, 512, 8)

    def body(h_ref, wg_ref, wu_ref, gu_ref, a_ref):
        hv = h_ref[...]
        gv = _bdot(hv, wg_ref[...], NN)
        uv = _bdot(hv, wu_ref[...], NN)
        gu_ref[0] = gv
        gu_ref[1] = uv
        a_ref[...] = (gv * jax.nn.sigmoid(gv) * uv).astype(BF16)

    return pl.pallas_call(
        body, grid=(nh, T // tm),
        in_specs=[pl.BlockSpec((tm, D), lambda j, i: (i, 0)), pl.BlockSpec((None, D, tb), lambda j, i: (j, 0, 0)),
                  pl.BlockSpec((None, D, tb), lambda j, i: (j + nh, 0, 0))],
        out_specs=[pl.BlockSpec((2, tm, tb), lambda j, i: (0, i, j)), pl.BlockSpec((tm, tb), lambda j, i: (i, j))],
        out_shape=[SDS((2, T, nh * tb), F32), SDS((T, nh * tb), BF16)],
        compiler_params=_cparams(("parallel", "parallel")), name=name)(h, w13b, w13b)


def ffn_down_bwd(dy_b, w2, gu, name):
    T, D = dy_b.shape
    Fh = w2.shape[0]
    tm, tf = _fit(T, 1024, 8), _fit(Fh, 512)
    nf = Fh // tf

    def body(dy_ref, w_ref, g_ref, u_ref, o_ref):
        dav = _bdot(dy_ref[...], w_ref[...], NT)
        gv, uv = g_ref[...], u_ref[...]
        sg = jax.nn.sigmoid(gv)
        o_ref[0] = (dav * uv * (sg * (1.0 + gv * (1.0 - sg)))).astype(BF16)
        o_ref[1] = (dav * gv * sg).astype(BF16)

    return pl.pallas_call(
        body, grid=(T // tm, nf),
        in_specs=[pl.BlockSpec((tm, D), lambda i, j: (i, 0)), pl.BlockSpec((tf, D), lambda i, j: (j, 0)),
                  pl.BlockSpec((None, tm, tf), lambda i, j: (0, i, j)), pl.BlockSpec((None, tm, tf), lambda i, j: (1, i, j))],
        out_specs=pl.BlockSpec((2, tm, tf), lambda i, j: (0, i, j)), out_shape=SDS((2, T, Fh), BF16),
        compiler_params=_cparams(("parallel", "parallel")), name=name)(dy_b, w2, gu, gu)


def loss_bwd(y, tgt, name):
    T, D = y.shape
    tm = _fit(T, 512, 8)

    def body(y_ref, t_ref, dy_ref, dyb_ref, l_ref):
        e = y_ref[...] - t_ref[...]
        dy = e * (1.0 / D)
        dy_ref[...] = dy
        dyb_ref[...] = dy.astype(BF16)

        @pl.when(pl.program_id(0) == 0)
        def _():
            l_ref[...] = jnp.zeros_like(l_ref)

        l_ref[...] += jnp.sum(jnp.sum(e * e, axis=1, keepdims=True), axis=0, keepdims=True)

    row = pl.BlockSpec((tm, D), lambda i: (i, 0))
    return pl.pallas_call(
        body, grid=(T // tm,), in_specs=[row, row], out_specs=[row, row, pl.BlockSpec((8, 128), lambda i: (0, 0))],
        out_shape=[SDS((T, D), F32), SDS((T, D), BF16), SDS((8, 128), F32)],
        compiler_params=_cparams(("arbitrary",)), name=name)(y, tgt)


def _log_sigmoid(z):
    return jnp.minimum(z, 0.0) - jnp.log(1.0 + jnp.exp(-jnp.abs(z)))


def _gla_group(q, k, v, tail, wg, bg, s, cmat, mask, backward_dir):
    n = q.shape[0] // GLA_C
    z = _bdot_vjp(tail, wg, NN) + bg
    la = _log_sigmoid(z) * (1.0 / GLA_TAU)
    cum = _tri_sum(cmat, la)
    chunks = [slice(g * GLA_C, (g + 1) * GLA_C) for g in range(n)]
    last = [jnp.sum(la[sl], axis=0, keepdims=True) for sl in chunks]
    last_rows = jnp.concatenate([jnp.broadcast_to(t, (GLA_C, t.shape[1])) for t in last], axis=0)
    q_e = q * jnp.exp(cum) * (GLA_HK ** -0.5)
    k_e = k * jnp.exp(-cum)
    k_end = k * jnp.exp(last_rows - cum)
    sc = jnp.where(mask > 0.5, _bdot_vjp(q_e, k_e, NT), 0.0)
    o = _bdot_vjp(sc, v, NN)
    o_inter = [None] * n
    for g in (range(n - 1, -1, -1) if backward_dir else range(n)):
        o_inter[g] = _bdot_vjp(q_e[chunks[g]], s, NN)
        s = s * jnp.transpose(jnp.exp(last[g])) + _bdot_vjp(k_end[chunks[g]], v[chunks[g]], TN)
    return o + jnp.concatenate(o_inter, axis=0), s


def _gla_consts(backward_dir, GC):
    i = np.arange(GC)
    same = (i[:, None] // GLA_C) == (i[None, :] // GLA_C)
    if backward_dir:
        return (same & (i[None, :] >= i[:, None])).astype(np.float32), (same & (i[None, :] > i[:, None])).astype(np.float32)
    tri = (same & (i[None, :] <= i[:, None])).astype(np.float32)
    return tri, tri


GLA_HB = 2


def _gla_in_specs(GC, nmap):
    return [
        pl.BlockSpec((GC, GLA_HB * GLA_HK), lambda h, n: (nmap(n), h)),
        pl.BlockSpec((GC, GLA_HB * GLA_HK), lambda h, n: (nmap(n), GLA_H // GLA_HB + h)),
        pl.BlockSpec((GC, GLA_HB * GLA_HV), lambda h, n: (nmap(n), 2 * GLA_DK // (GLA_HB * GLA_HV) + h)),
        pl.BlockSpec((GC, 128), lambda h, n: (nmap(n), COL_TAIL // 128)),
        pl.BlockSpec((GLA_HB, 128, 128), lambda h, n: (h, 0, 0)),
        pl.BlockSpec((GLA_HB, 1, 128), lambda h, n: (h, 0, 0)),
        pl.BlockSpec((GC, GC), lambda h, n: (0, 0)),
        pl.BlockSpec((GC, GC), lambda h, n: (0, 0)),
    ]


def _head_cols(ref, hh, width):
    return ref[:, hh * width:(hh + 1) * width]


def gla_fwd(proj, wgpad, bg, backward_dir, name):
    T = proj.shape[0]
    GC = min(GLA_G * GLA_C, T)
    NG = T // GC
    cmat, mask = _gla_consts(backward_dir, GC)
    nmap = (lambda n: NG - 1 - n) if backward_dir else (lambda n: n)

    def body(q_ref, k_ref, v_ref, t_ref, wg_ref, bg_ref, c_ref, m_ref, o_ref, ss_ref, s_scr):
        @pl.when(pl.program_id(1) == 0)
        def _():
            s_scr[...] = jnp.zeros_like(s_scr)

        tail, cm, mk = t_ref[...], c_ref[...], m_ref[...]
        for hh in range(GLA_HB):
            s0 = s_scr[hh]
            ss_ref[hh] = s0
            o, s1 = _gla_group(_head_cols(q_ref, hh, GLA_HK), _head_cols(k_ref, hh, GLA_HK), _head_cols(v_ref, hh, GLA_HV),
                               tail, wg_ref[hh], bg_ref[hh], s0, cm, mk, backward_dir)
            o_ref[:, hh * GLA_HV:(hh + 1) * GLA_HV] = o
            s_scr[hh] = s1

    return pl.pallas_call(
        body, grid=(GLA_H // GLA_HB, NG), in_specs=_gla_in_specs(GC, nmap),
        out_specs=[pl.BlockSpec((GC, GLA_HB * GLA_HV), lambda h, n: (nmap(n), h)),
                   pl.BlockSpec((GLA_HB, None, GLA_HK, GLA_HV), lambda h, n: (h, nmap(n), 0, 0))],
        out_shape=[SDS((T, GLA_DV), F32), SDS((GLA_H, NG, GLA_HK, GLA_HV), F32)],
        scratch_shapes=[pltpu.VMEM((GLA_HB, GLA_HK, GLA_HV), F32)],
        compiler_params=_cparams(("parallel", "arbitrary")), name=name)(proj, proj, proj, proj, wgpad, bg, cmat, mask)


def gla_bwd(proj, wgpad, bg, ssave, do, backward_dir, name):
    T = proj.shape[0]
    GC = min(GLA_G * GLA_C, T)
    NG = T // GC
    cmat, mask = _gla_consts(backward_dir, GC)
    nmap = (lambda n: n) if backward_dir else (lambda n: NG - 1 - n)

    def body(q_ref, k_ref, v_ref, t_ref, wg_ref, bg_ref, c_ref, m_ref, ss_ref, do_ref,
             dq_ref, dk_ref, dv_ref, dt_ref, dwg_ref, dbg_ref, ds_scr):
        @pl.when(pl.program_id(1) == 0)
        def _():
            ds_scr[...] = jnp.zeros_like(ds_scr)
            dwg_ref[...] = jnp.zeros_like(dwg_ref)
            dbg_ref[...] = jnp.zeros_like(dbg_ref)

        tail, cm, mk = t_ref[...], c_ref[...], m_ref[...]
        fn = lambda q, k, v, t, wg, b, s: _gla_group(q, k, v, t, wg, b, s, cm, mk, backward_dir)
        for hh in range(GLA_HB):
            _, vjp = jax.vjp(fn, _head_cols(q_ref, hh, GLA_HK), _head_cols(k_ref, hh, GLA_HK), _head_cols(v_ref, hh, GLA_HV),
                             tail, wg_ref[hh], bg_ref[hh], ss_ref[hh])
            dq, dk, dv, dt, dwg, dbg, ds = vjp((_head_cols(do_ref, hh, GLA_HV), ds_scr[hh]))
            dq_ref[:, hh * GLA_HK:(hh + 1) * GLA_HK] = dq
            dk_ref[:, hh * GLA_HK:(hh + 1) * GLA_HK] = dk
            dv_ref[:, hh * GLA_HV:(hh + 1) * GLA_HV] = dv
            dt_ref[hh] = dt
            dwg_ref[hh] += dwg
            dbg_ref[hh] += dbg
            ds_scr[hh] = ds

    in_specs = _gla_in_specs(GC, nmap) + [
        pl.BlockSpec((GLA_HB, None, GLA_HK, GLA_HV), lambda h, n: (h, nmap(n), 0, 0)),
        pl.BlockSpec((GC, GLA_HB * GLA_HV), lambda h, n: (nmap(n), h)),
    ]
    out_specs = [
        pl.BlockSpec((GC, GLA_HB * GLA_HK), lambda h, n: (nmap(n), h)),
        pl.BlockSpec((GC, GLA_HB * GLA_HK), lambda h, n: (nmap(n), h)),
        pl.BlockSpec((GC, GLA_HB * GLA_HV), lambda h, n: (nmap(n), h)),
        pl.BlockSpec((GLA_HB, GC, 128), lambda h, n: (h, nmap(n), 0)),
        pl.BlockSpec((GLA_HB, 128, 128), lambda h, n: (h, 0, 0)),
        pl.BlockSpec((GLA_HB, 1, 128), lambda h, n: (h, 0, 0)),
    ]
    out_shape = [SDS((T, GLA_DK), F32), SDS((T, GLA_DK), F32), SDS((T, GLA_DV), F32), SDS((GLA_H, T, 128), F32),
                 SDS((GLA_H, 128, 128), F32), SDS((GLA_H, 1, 128), F32)]
    return pl.pallas_call(
        body, grid=(GLA_H // GLA_HB, NG), in_specs=in_specs, out_specs=out_specs, out_shape=out_shape,
        scratch_shapes=[pltpu.VMEM((GLA_HB, GLA_HK, GLA_HV), F32)],
        compiler_params=_cparams(("parallel", "arbitrary")), name=name)(proj, proj, proj, proj, wgpad, bg, cmat, mask, ssave, do)


def _block_diag(width, hd):
    i = np.arange(width)
    return ((i[:, None] // hd) == (i[None, :] // hd)).astype(np.float32) / hd


def _norm_heads(t, hd):
    outs = []
    for h in range(t.shape[1] // hd):
        th = t[:, h * hd:(h + 1) * hd]
        outs.append(th * lax.rsqrt(jnp.mean(th * th, axis=-1, keepdims=True) + RMS_EPS))
    return jnp.concatenate(outs, axis=1)


def _norm_bd(t, bd):
    return t * lax.rsqrt(jnp.dot(t * t, bd, preferred_element_type=F32, precision=HI) + RMS_EPS)


def _pre_fn(nq, nk, mq, gq, gk, gm, bd):
    return _norm_bd(nq, bd) * gq, _norm_bd(nk, bd) * gk, _norm_heads(mq, MEM_HD) * gm


def pre_fwd(proj, gq, gk, gm, name):
    T = proj.shape[0]
    tm = _fit(T, 512, 8)
    bd = _block_diag(NA_W, NA_HD)

    def body(nq_ref, nk_ref, nv_ref, mq_ref, gq_ref, gk_ref, gm_ref, bd_ref, q_ref, k_ref, v_ref, m_ref):
        qn, kn, mn = _pre_fn(nq_ref[...], nk_ref[...], mq_ref[...], gq_ref[...], gk_ref[...], gm_ref[...], bd_ref[...])
        q_ref[...] = qn.astype(BF16)
        k_ref[...] = kn.astype(BF16)
        v_ref[...] = nv_ref[...].astype(BF16)
        m_ref[...] = mn.astype(BF16)

    col = lambda c0: pl.BlockSpec((tm, 512), lambda i: (i, c0 // 512))
    vec = pl.BlockSpec((1, 512), lambda i: (0, 0))
    row = pl.BlockSpec((tm, 512), lambda i: (i, 0))
    return pl.pallas_call(
        body, grid=(T // tm,),
        in_specs=[col(COL_NQ), col(COL_NK), col(COL_NV), col(COL_MQ), vec, vec, vec, pl.BlockSpec((NA_W, NA_W), lambda i: (0, 0))],
        out_specs=[row] * 4, out_shape=[SDS((T, 512), BF16)] * 4,
        compiler_params=_cparams(("parallel",)), name=name)(proj, proj, proj, proj, gq, gk, gm, bd)


def pre_bwd(proj, gq, gk, gm, d_qn, d_kn, d_nv, d_mn, dq_f, dq_b, dk_f, dk_b, dv_f, dv_b, d_r, dt_f, dt_b, name):
    T = proj.shape[0]
    tm = _fit(T, 256, 8)
    bd = _block_diag(NA_W, NA_HD)

    def body(nq_ref, nk_ref, mq_ref, gq_ref, gk_ref, gm_ref, bd_ref, dqn_ref, dkn_ref, dnv_ref, dmn_ref,
             dqf_ref, dqb_ref, dkf_ref, dkb_ref, dvf_ref, dvb_ref, dr_ref, dtf_ref, dtb_ref,
             o_ref, dgq_ref, dgk_ref, dgm_ref):
        bdv = bd_ref[...]
        fn = lambda a, b, c, x, y, z: _pre_fn(a, b, c, x, y, z, bdv)
        _, vjp = jax.vjp(fn, nq_ref[...], nk_ref[...], mq_ref[...], gq_ref[...], gk_ref[...], gm_ref[...])
        d_nq, d_nk, d_mq, dgq, dgk, dgm = vjp((dqn_ref[...], dkn_ref[...], dmn_ref[...]))
        o_ref[:, 0:512] = (dqf_ref[...] + dqb_ref[...]).astype(BF16)
        o_ref[:, 512:1024] = (dkf_ref[...] + dkb_ref[...]).astype(BF16)
        o_ref[:, 1024:2048] = (dvf_ref[...] + dvb_ref[...]).astype(BF16)
        o_ref[:, COL_R:COL_R + 1024] = dr_ref[...].astype(BF16)
        o_ref[:, COL_NQ:COL_NQ + 512] = d_nq.astype(BF16)
        o_ref[:, COL_NK:COL_NK + 512] = d_nk.astype(BF16)
        o_ref[:, COL_NV:COL_NV + 512] = dnv_ref[...].astype(BF16)
        o_ref[:, COL_MQ:COL_MQ + 512] = d_mq.astype(BF16)
        dt = dtf_ref[0] + dtb_ref[0]
        for h in range(1, GLA_H):
            dt = dt + dtf_ref[h] + dtb_ref[h]
        o_ref[:, COL_TAIL:COL_TAIL + 128] = dt.astype(BF16)
        o_ref[:, COL_TAIL + 128:PC] = jnp.zeros((tm, PC - COL_TAIL - 128), BF16)

        @pl.when(pl.program_id(0) == 0)
        def _():
            dgq_ref[...] = jnp.zeros_like(dgq_ref)
            dgk_ref[...] = jnp.zeros_like(dgk_ref)
            dgm_ref[...] = jnp.zeros_like(dgm_ref)

        dgq_ref[...] += dgq
        dgk_ref[...] += dgk
        dgm_ref[...] += dgm

    col = lambda c0: pl.BlockSpec((tm, 512), lambda i: (i, c0 // 512))
    vec = pl.BlockSpec((1, 512), lambda i: (0, 0))
    r512 = pl.BlockSpec((tm, 512), lambda i: (i, 0))
    r1024 = pl.BlockSpec((tm, 1024), lambda i: (i, 0))
    tl = pl.BlockSpec((GLA_H, tm, 128), lambda i: (0, i, 0))
    in_specs = [col(COL_NQ), col(COL_NK), col(COL_MQ), vec, vec, vec, pl.BlockSpec((NA_W, NA_W), lambda i: (0, 0)),
                r512, r512, r512, r512, r512, r512, r512, r512, r1024, r1024, r1024, tl, tl]
    return pl.pallas_call(
        body, grid=(T // tm,), in_specs=in_specs,
        out_specs=[pl.BlockSpec((tm, PC), lambda i: (i, 0)), vec, vec, vec],
        out_shape=[SDS((T, PC), BF16), SDS((1, 512), F32), SDS((1, 512), F32), SDS((1, 512), F32)],
        compiler_params=_cparams(("arbitrary",)), name=name)(
            proj, proj, proj, gq, gk, gm, bd, d_qn, d_kn, d_nv, d_mn, dq_f, dq_b, dk_f, dk_b, dv_f, dv_b, d_r, dt_f, dt_b)


def _post_fn(o_f, o_b, r, o_na, o_mem, g_gla, g_na, g_mem, bd):
    y_gla = _norm_heads(o_f + o_b, GLA_HV) * g_gla * (r * jax.nn.sigmoid(r))
    y_na = _norm_bd(o_na, bd) * g_na
    y_mem = _norm_heads(o_mem, MEM_HD) * g_mem
    return jnp.concatenate([y_gla, y_na, y_mem], axis=1)


def post_fwd(o_f, o_b, proj, o_na, o_mem, g_gla, g_na, g_mem, name):
    T = proj.shape[0]
    tm = _fit(T, 256, 8)
    bd = _block_diag(NA_W, NA_HD)

    def body(of_ref, ob_ref, r_ref, ona_ref, omem_ref, gg_ref, gn_ref, gm_ref, bd_ref, y_ref):
        y_ref[...] = _post_fn(of_ref[...], ob_ref[...], r_ref[...], ona_ref[...], omem_ref[...],
                              gg_ref[...], gn_ref[...], gm_ref[...], bd_ref[...]).astype(BF16)

    r1024 = pl.BlockSpec((tm, 1024), lambda i: (i, 0))
    r512 = pl.BlockSpec((tm, 512), lambda i: (i, 0))
    in_specs = [r1024, r1024, pl.BlockSpec((tm, 1024), lambda i: (i, COL_R // 1024)), r512, r512,
                pl.BlockSpec((1, 1024), lambda i: (0, 0)), pl.BlockSpec((1, 512), lambda i: (0, 0)),
                pl.BlockSpec((1, 512), lambda i: (0, 0)), pl.BlockSpec((NA_W, NA_W), lambda i: (0, 0))]
    return pl.pallas_call(
        body, grid=(T // tm,), in_specs=in_specs, out_specs=pl.BlockSpec((tm, D_MODEL), lambda i: (i, 0)),
        out_shape=SDS((T, D_MODEL), BF16), compiler_params=_cparams(("parallel",)), name=name)(
            o_f, o_b, proj, o_na, o_mem, g_gla, g_na, g_mem, bd)


def post_bwd(o_f, o_b, proj, o_na, o_mem, g_gla, g_na, g_mem, dy, name):
    T = proj.shape[0]
    tm = _fit(T, 256, 8)
    bd = _block_diag(NA_W, NA_HD)

    def body(of_ref, ob_ref, r_ref, ona_ref, omem_ref, gg_ref, gn_ref, gm_ref, bd_ref, dy_ref,
             do_ref, dr_ref, dna_ref, dmem_ref, dgg_ref, dgn_ref, dgm_ref):
        bdv = bd_ref[...]
        fn = lambda o, r, a, m, x, y, z: _post_fn(o, 0.0, r, a, m, x, y, z, bdv)
        _, vjp = jax.vjp(fn, of_ref[...] + ob_ref[...], r_ref[...], ona_ref[...], omem_ref[...],
                         gg_ref[...], gn_ref[...], gm_ref[...])
        d_o, d_r, d_na, d_mem, dgg, dgn, dgm = vjp(dy_ref[...])
        do_ref[...] = d_o
        dr_ref[...] = d_r
        dna_ref[...] = d_na
        dmem_ref[...] = d_mem

        @pl.when(pl.program_id(0) == 0)
        def _():
            dgg_ref[...] = jnp.zeros_like(dgg_ref)
            dgn_ref[...] = jnp.zeros_like(dgn_ref)
            dgm_ref[...] = jnp.zeros_like(dgm_ref)

        dgg_ref[...] += dgg
        dgn_ref[...] += dgn
        dgm_ref[...] += dgm

    r1024 = pl.BlockSpec((tm, 1024), lambda i: (i, 0))
    r512 = pl.BlockSpec((tm, 512), lambda i: (i, 0))
    v1024 = pl.BlockSpec((1, 1024), lambda i: (0, 0))
    v512 = pl.BlockSpec((1, 512), lambda i: (0, 0))
    in_specs = [r1024, r1024, pl.BlockSpec((tm, 1024), lambda i: (i, COL_R // 1024)), r512, r512, v1024, v512, v512,
                pl.BlockSpec((NA_W, NA_W), lambda i: (0, 0)), pl.BlockSpec((tm, D_MODEL), lambda i: (i, 0))]
    return pl.pallas_call(
        body, grid=(T // tm,), in_specs=in_specs, out_specs=[r1024, r1024, r512, r512, v1024, v512, v512],
        out_shape=[SDS((T, 1024), F32), SDS((T, 1024), F32), SDS((T, 512), F32), SDS((T, 512), F32),
                   SDS((1, 1024), F32), SDS((1, 512), F32), SDS((1, 512), F32)],
        compiler_params=_cparams(("arbitrary",)), name=name)(o_f, o_b, proj, o_na, o_mem, g_gla, g_na, g_mem, bd, dy)


NA_RB = 8


def _na_row_scores(q_ref, k_ref, v_ref, tb_ref, rb, j, n_rows):
    r = rb * NA_RB + j
    rs = jnp.clip(r - NA_ROWS // 2, 0, n_rows - NA_ROWS)
    dr0 = rs - r + (NA_ROWS - 1)
    tok = pl.ds(pl.multiple_of(rs * GRID_W, GRID_W), NA_ROWS * GRID_W)
    q = q_ref[j * GRID_W:(j + 1) * GRID_W, :]
    kk, vv = k_ref[tok, :], v_ref[tok, :]
    bias = jnp.concatenate([tb_ref[dr0 + 2 * i] for i in range(NA_ROWS // 2)], axis=1)
    s = _bdot(q, kk, NT) * (NA_HD ** -0.5) + bias
    m = jnp.max(s, axis=1, keepdims=True)
    p = jnp.exp(s - m)
    l = jnp.sum(p, axis=1, keepdims=True)
    return q, kk, vv, p, l, tok, dr0


def natten_fwd(q, k, v, tb2, name):
    H, T, hd = q.shape
    n_rows = T // GRID_W
    rbt = NA_RB * GRID_W

    def body(q_ref, k_ref, v_ref, tb_ref, o_ref):
        rb = pl.program_id(1)
        for j in range(NA_RB):
            _, _, vv, p, l, _, _ = _na_row_scores(q_ref, k_ref, v_ref, tb_ref, rb, j, n_rows)
            o_ref[j * GRID_W:(j + 1) * GRID_W, :] = _bdot(p, vv, NN) / l

    whole = pl.BlockSpec((None, T, hd), lambda h, r: (h, 0, 0))
    blk = pl.BlockSpec((None, rbt, hd), lambda h, r: (h, r, 0))
    return pl.pallas_call(
        body, grid=(H, n_rows // NA_RB),
        in_specs=[blk, whole, whole, pl.BlockSpec((None, 2 * NA_ROWS - 2, GRID_W, 2 * GRID_W), lambda h, r: (h, 0, 0, 0))],
        out_specs=blk, out_shape=SDS((H, T, hd), F32),
        compiler_params=_cparams(("parallel", "arbitrary")), name=name)(q, k, v, tb2)


def natten_bwd(q, k, v, tb2, do, name):
    H, T, hd = q.shape
    n_rows = T // GRID_W
    rbt = NA_RB * GRID_W
    scale = NA_HD ** -0.5

    def body(q_ref, k_ref, v_ref, tb_ref, do_ref, dq_ref, dk_ref, dv_ref, dtb_ref):
        rb = pl.program_id(1)

        @pl.when(rb == 0)
        def _():
            dk_ref[...] = jnp.zeros_like(dk_ref)
            dv_ref[...] = jnp.zeros_like(dv_ref)
            dtb_ref[...] = jnp.zeros_like(dtb_ref)

        for j in range(NA_RB):
            qv, kk, vv, p, l, tok, dr0 = _na_row_scores(q_ref, k_ref, v_ref, tb_ref, rb, j, n_rows)
            p = p / l
            dov = do_ref[j * GRID_W:(j + 1) * GRID_W, :]
            dp = _bdot(dov, vv, NT)
            ds = p * (dp - jnp.sum(dp * p, axis=1, keepdims=True))
            dq_ref[j * GRID_W:(j + 1) * GRID_W, :] = _bdot(ds, kk, NN) * scale
            dk_ref[tok, :] += _bdot(ds, qv, TN) * scale
            dv_ref[tok, :] += _bdot(p, dov, TN)
            for i in range(NA_ROWS // 2):
                dtb_ref[dr0 + 2 * i] += ds[:, 2 * GRID_W * i:2 * GRID_W * (i + 1)]

    whole = pl.BlockSpec((None, T, hd), lambda h, r: (h, 0, 0))
    blk = pl.BlockSpec((None, rbt, hd), lambda h, r: (h, r, 0))
    tbs = pl.BlockSpec((None, 2 * NA_ROWS - 2, GRID_W, 2 * GRID_W), lambda h, r: (h, 0, 0, 0))
    return pl.pallas_call(
        body, grid=(H, n_rows // NA_RB), in_specs=[blk, whole, whole, tbs, blk],
        out_specs=[blk, whole, whole, tbs],
        out_shape=[SDS((H, T, hd), F32), SDS((H, T, hd), F32), SDS((H, T, hd), F32), SDS(tb2.shape, F32)],
        compiler_params=_cparams(("parallel", "arbitrary")), name=name)(q, k, v, tb2, do)


def _rpb_expand_consts():
    qc = np.arange(GRID_W)[:, None]
    kc = np.arange(GRID_W)[None, :]
    cs = np.clip(qc - NA_COLS // 2, 0, GRID_W - NA_COLS)
    inside = (kc >= cs) & (kc < cs + NA_COLS)
    dc = np.clip(kc - qc, -(NA_COLS - 1), NA_COLS - 1) + (NA_COLS - 1)
    e = np.zeros((128, GRID_W * GRID_W), np.float32)
    flat = (qc * GRID_W + kc)
    e[dc[inside], flat[inside]] = 1.0
    neg = np.where(inside, 0.0, NEG).astype(np.float32).reshape(1, -1)
    return e, neg


def _rpb_fold_consts():
    sa = np.zeros((NA_H * 15, NA_H * 14), np.float32)
    sb = np.zeros((NA_H * 15, NA_H * 14), np.float32)
    for h in range(NA_H):
        for d in range(14):
            sa[h * 15 + d, h * 14 + d] = 1.0
            sb[h * 15 + d + 1, h * 14 + d] = 1.0
    return sa, sb


def rpb_table(rpb, name):
    e, neg = _rpb_expand_consts()
    rp = jnp.pad(rpb.reshape(NA_H * 15, 31), ((0, 0), (0, 128 - 31)))

    def body(r_ref, e_ref, n_ref, o_ref):
        o_ref[...] = jnp.dot(r_ref[...], e_ref[...], preferred_element_type=F32, precision=HI) + n_ref[...]

    t = pl.pallas_call(body, out_shape=SDS((NA_H * 15, GRID_W * GRID_W), F32), name=name)(rp, e, neg)
    t = t.reshape(NA_H, 15, GRID_W, GRID_W)
    return jnp.concatenate([t[:, :14], t[:, 1:]], axis=-1)


def rpb_table_bwd(dtb2, name):
    e, _ = _rpb_expand_consts()
    sa, sb = _rpb_fold_consts()
    a = dtb2[..., :GRID_W].reshape(NA_H * 14, GRID_W * GRID_W)
    b = dtb2[..., GRID_W:].reshape(NA_H * 14, GRID_W * GRID_W)

    def body(a_ref, b_ref, e_ref, sa_ref, sb_ref, o_ref):
        ev = e_ref[...]
        pa = lax.dot_general(a_ref[...], ev, (NT, ((), ())), preferred_element_type=F32, precision=HI)
        pb = lax.dot_general(b_ref[...], ev, (NT, ((), ())), preferred_element_type=F32, precision=HI)
        o_ref[...] = (jnp.dot(sa_ref[...], pa, preferred_element_type=F32, precision=HI)
                      + jnp.dot(sb_ref[...], pb, preferred_element_type=F32, precision=HI))

    d = pl.pallas_call(body, out_shape=SDS((NA_H * 15, 128), F32), name=name)(a, b, e, sa, sb)
    return d[:, :31].reshape(NA_H, 15, 31)


def _kprep_fn(kv, gk):
    return _norm_heads(kv[:, :MEM_W], MEM_HD) * gk, kv[:, MEM_W:]


def mem_kprep(kv, gk, name):
    def body(kv_ref, g_ref, k_ref, v_ref):
        kn, vv = _kprep_fn(kv_ref[...], g_ref[...])
        k_ref[...] = kn.astype(BF16)
        v_ref[...] = vv.astype(BF16)

    return pl.pallas_call(body, out_shape=[SDS((MEM_TOK, MEM_W), BF16)] * 2, name=name)(kv, gk)


def mem_kprep_bwd(kv, gk, dk, dv, name):
    def body(kv_ref, g_ref, dk_ref, dv_ref, dkv_ref, dg_ref):
        _, vjp = jax.vjp(_kprep_fn, kv_ref[...], g_ref[...])
        dkv, dg = vjp((dk_ref[...], dv_ref[...]))
        dkv_ref[...] = dkv.astype(BF16)
        dg_ref[...] = dg

    return pl.pallas_call(body, out_shape=[SDS((MEM_TOK, 2 * MEM_W), BF16), SDS((1, MEM_W), F32)], name=name)(kv, gk, dk, dv)


def _mem_probs(q_ref, k_ref, h):
    hs = slice(h * MEM_HD, (h + 1) * MEM_HD)
    qh, kh = q_ref[:, hs], k_ref[:, hs]
    s = _bdot(qh, kh, NT) * (MEM_HD ** -0.5)
    p = jnp.exp(s - jnp.max(s, axis=1, keepdims=True))
    return hs, qh, kh, p, jnp.sum(p, axis=1, keepdims=True)


def mem_attn_fwd(q, km, vm, name):
    T = q.shape[0]
    tm = _fit(T, 512, 8)

    def body(q_ref, k_ref, v_ref, o_ref):
        for h in range(MEM_H):
            hs, _, _, p, l = _mem_probs(q_ref, k_ref, h)
            o_ref[:, hs] = _bdot(p, v_ref[:, hs], NN) / l

    row = pl.BlockSpec((tm, MEM_W), lambda i: (i, 0))
    full = pl.BlockSpec((MEM_TOK, MEM_W), lambda i: (0, 0))
    return pl.pallas_call(body, grid=(T // tm,), in_specs=[row, full, full], out_specs=row, out_shape=SDS((T, MEM_W), F32),
                          compiler_params=_cparams(("parallel",)), name=name)(q, km, vm)


def mem_attn_bwd(q, km, vm, do, name):
    T = q.shape[0]
    tm = _fit(T, 512, 8)
    scale = MEM_HD ** -0.5

    def body(q_ref, k_ref, v_ref, do_ref, dq_ref, dk_ref, dv_ref):
        @pl.when(pl.program_id(0) == 0)
        def _():
            dk_ref[...] = jnp.zeros_like(dk_ref)
            dv_ref[...] = jnp.zeros_like(dv_ref)

        for h in range(MEM_H):
            hs, qh, kh, p, l = _mem_probs(q_ref, k_ref, h)
            p = p / l
            dov = do_ref[:, hs]
            dp = _bdot(dov, v_ref[:, hs], NT)
            ds = p * (dp - jnp.sum(dp * p, axis=1, keepdims=True))
            dq_ref[:, hs] = _bdot(ds, kh, NN) * scale
            dk_ref[:, hs] += _bdot(ds, qh, TN) * scale
            dv_ref[:, hs] += _bdot(p, dov, TN)

    row = pl.BlockSpec((tm, MEM_W), lambda i: (i, 0))
    full = pl.BlockSpec((MEM_TOK, MEM_W), lambda i: (0, 0))
    return pl.pallas_call(
        body, grid=(T // tm,), in_specs=[row, full, full, row], out_specs=[row, full, full],
        out_shape=[SDS((T, MEM_W), F32), SDS((MEM_TOK, MEM_W), F32), SDS((MEM_TOK, MEM_W), F32)],
        compiler_params=_cparams(("arbitrary",)), name=name)(q, km, vm, do)


SLOT_BLOCK_ELEMS = 512 * 1024


def _slot_rows(R, Cc):
    return _fit(R, max(16, SLOT_BLOCK_ELEMS // Cc // 16 * 16), 16)


def sum_slots(a, name):
    n, R, Cc = a.shape
    tr = _slot_rows(R, Cc)

    def body(a_ref, o_ref):
        s = a_ref[0].astype(F32)
        for i in range(1, n):
            s = s + a_ref[i].astype(F32)
        o_ref[...] = s

    return pl.pallas_call(body, grid=(R // tr,), in_specs=[pl.BlockSpec((n, tr, Cc), lambda i: (0, i, 0))],
                          out_specs=pl.BlockSpec((tr, Cc), lambda i: (i, 0)), out_shape=SDS((R, Cc), F32),
                          compiler_params=_cparams(("parallel",)), name=name)(a)


def pair_sum(g, land, c_idx, name):
    _, R, Cc = g.shape
    tr = _slot_rows(R, Cc)

    def body(c_ref, g_ref, l_ref, o_ref):
        o_ref[...] = (g_ref[...].astype(F32) + l_ref[...].astype(F32)).astype(o_ref.dtype)

    blk = pl.BlockSpec((None, tr, Cc), lambda k, i, c: (k, i, 0))
    gs = pltpu.PrefetchScalarGridSpec(
        num_scalar_prefetch=1, grid=(4, R // tr),
        in_specs=[pl.BlockSpec((None, tr, Cc), lambda k, i, c: (2 * k + c[0], i, 0)), blk], out_specs=blk)
    return pl.pallas_call(body, grid_spec=gs, out_shape=SDS((4, R, Cc), g.dtype),
                          compiler_params=_cparams(("parallel", "parallel")), name=name)(c_idx, g, land)


def adamw(w, g, m, v, name):
    R, Cc = w.shape
    tr = _fit(R, max(8, (262144 // max(Cc, 128)) // 8 * 8), 8)
    c1 = 1.0 - ADAM_B1 ** ADAM_STEP
    c2 = 1.0 - ADAM_B2 ** ADAM_STEP

    def body(w_ref, g_ref, m_ref, v_ref, d_ref, mo_ref, vo_ref):
        gv = g_ref[...]
        m2 = ADAM_B1 * m_ref[...] + (1.0 - ADAM_B1) * gv
        v2 = ADAM_B2 * v_ref[...] + (1.0 - ADAM_B2) * (gv * gv)
        d_ref[...] = -ADAM_LR * ((m2 / c1) / (jnp.sqrt(v2 / c2) + ADAM_EPS) + ADAM_WD * w_ref[...])
        mo_ref[...] = m2
        vo_ref[...] = v2

    blk = pl.BlockSpec((tr, Cc), lambda i: (i, 0))
    return pl.pallas_call(body, grid=(R // tr,), in_specs=[blk] * 4, out_specs=[blk] * 3, out_shape=[SDS((R, Cc), F32)] * 3,
                          compiler_params=_cparams(("parallel",)), name=name)(w, g, m, v)


ANY = pl.BlockSpec(memory_space=pl.ANY)


def _my_pos():
    return lax.axis_index("x"), lax.axis_index("y"), lax.axis_index("c")


def all_gather(shards, name):
    n = len(shards)

    def body(*refs):
        x_refs, out_refs = refs[:n], refs[n:2 * n]
        send_sems, recv_sems, local_sems = refs[2 * n:]
        x, y, c = _my_pos()
        me, sibling = (x, y, c), (x, y, 1 - c)
        chips = [(1 - x, y), (x, 1 - y), (1 - x, 1 - y)]

        def slot(a, px, py, pc):
            return out_refs[a].at[4 * px + 2 * py + pc]

        def copy(a, k, block, to, src=None):
            return pltpu.make_async_remote_copy(
                src_ref=slot(a, *block) if src is None else src, dst_ref=slot(a, *block),
                send_sem=send_sems.at[7 * a + k], recv_sem=recv_sems.at[7 * a + k], device_id=to, device_id_type=MESH)

        mine = [pltpu.make_async_copy(x_refs[a], slot(a, *me), local_sems.at[a]) for a in range(n)]
        for cp in mine:
            cp.start()
        first = []
        for a in range(n):
            first.append(copy(a, 0, me, sibling, src=x_refs[a]))
            first += [copy(a, 1 + j, me, (*chip, c), src=x_refs[a]) for j, chip in enumerate(chips)]
        for cp in first:
            cp.start()
        passed = []
        for j, chip in enumerate(chips):
            for a in range(n):
                copy(a, 1 + j, (*chip, c), me).wait_recv()
                passed.append(copy(a, 4 + j, (*chip, c), sibling))
                passed[-1].start()
        for a in range(n):
            copy(a, 0, sibling, me).wait_recv()
        for j, chip in enumerate(chips):
            for a in range(n):
                copy(a, 4 + j, (*chip, 1 - c), me).wait_recv()
        for cp in first + passed:
            cp.wait_send()
        for cp in mine:
            cp.wait()

    return pl.pallas_call(
        body, out_shape=[SDS((N_DEV,) + s.shape, s.dtype) for s in shards], in_specs=[ANY] * n, out_specs=[ANY] * n,
        scratch_shapes=[pltpu.SemaphoreType.DMA((7 * n,)), pltpu.SemaphoreType.DMA((7 * n,)), pltpu.SemaphoreType.DMA((n,))],
        name=name)(*shards)


def pair_exchange(gs, name):
    n = len(gs)

    def body(*refs):
        g_refs, land_refs = refs[:n], refs[n:2 * n]
        send_sems, recv_sems = refs[2 * n:]
        x, y, c = _my_pos()
        sibling = (x, y, 1 - c)
        copies = [pltpu.make_async_remote_copy(
            src_ref=g_refs[a].at[2 * k + (1 - c)], dst_ref=land_refs[a].at[k], send_sem=send_sems.at[4 * a + k],
            recv_sem=recv_sems.at[4 * a + k], device_id=sibling, device_id_type=MESH) for a in range(n) for k in range(4)]
        for cp in copies:
            cp.start()
        for cp in copies:
            cp.wait_recv()
        for cp in copies:
            cp.wait_send()

    return pl.pallas_call(
        body, out_shape=[SDS((4,) + g.shape[1:], g.dtype) for g in gs], in_specs=[ANY] * n, out_specs=[ANY] * n,
        scratch_shapes=[pltpu.SemaphoreType.DMA((4 * n,)), pltpu.SemaphoreType.DMA((4 * n,))], name=name)(*gs)


def chip_exchange(ss, name):
    n = len(ss)

    def body(*refs):
        s_refs, land_refs = refs[:n], refs[n:2 * n]
        send_sems, recv_sems, local_sems = refs[2 * n:]
        x, y, c = _my_pos()
        my_chip = 2 * x + y
        chips = [(1 - x, y), (x, 1 - y), (1 - x, 1 - y)]
        own = [pltpu.make_async_copy(s_refs[a].at[my_chip], land_refs[a].at[my_chip], local_sems.at[a]) for a in range(n)]
        for cp in own:
            cp.start()

        def copy(a, j, src_chip, dst_chip):
            px, py = chips[j]
            return pltpu.make_async_remote_copy(
                src_ref=s_refs[a].at[src_chip], dst_ref=land_refs[a].at[dst_chip], send_sem=send_sems.at[3 * a + j],
                recv_sem=recv_sems.at[3 * a + j], device_id=(px, py, c), device_id_type=MESH)

        sends = [copy(a, j, 2 * px + py, my_chip) for a in range(n) for j, (px, py) in enumerate(chips)]
        for cp in sends:
            cp.start()
        for a in range(n):
            for j, (px, py) in enumerate(chips):
                copy(a, j, my_chip, 2 * px + py).wait_recv()
        for cp in sends:
            cp.wait_send()
        for cp in own:
            cp.wait()

    return pl.pallas_call(
        body, out_shape=[SDS(s.shape, s.dtype) for s in ss], in_specs=[ANY] * n, out_specs=[ANY] * n,
        scratch_shapes=[pltpu.SemaphoreType.DMA((3 * n,)), pltpu.SemaphoreType.DMA((3 * n,)), pltpu.SemaphoreType.DMA((n,))],
        name=name)(*ss)


W_IN_SHARD, W_IN_PACKED = 644, 768
BIG = ("ffn_w13", "ffn_w2", "w_out", "mem_wkv", "w_in")
GATE_W = ("gla_wg2_f", "gla_wg2_b")
SHARD_NAMES = BIG + GATE_W


def _permuted_ranges(c0, c1):
    res = []
    for o0, o1, p0 in ((0, ORIG_GATE0, 0), (ORIG_GATE0, ORIG_AFTER_GATE, COL_TAIL), (ORIG_AFTER_GATE, IN_COLS, ORIG_GATE0)):
        lo, hi = max(c0, o0), min(c1, o1)
        if lo < hi:
            res.append((p0 + lo - o0, p0 + hi - o0))
    return res


def assemble_w_in(blocks):
    placed = []
    for d in range(N_DEV):
        c = d * W_IN_SHARD
        for p0, p1 in _permuted_ranges(c, c + W_IN_SHARD):
            placed.append((p0, blocks[d][:, c - d * W_IN_SHARD:c - d * W_IN_SHARD + (p1 - p0)]))
            c += p1 - p0
    placed.sort(key=lambda t: t[0])
    return jnp.concatenate([t[1] for t in placed] + [jnp.zeros((blocks.shape[1], PC - IN_COLS), blocks.dtype)], axis=1)


def split_w_in_grad(g):
    pad = jnp.zeros((g.shape[0], W_IN_PACKED - W_IN_SHARD), g.dtype)
    return jnp.stack([jnp.concatenate([g[:, p0:p1] for p0, p1 in _permuted_ranges(d * W_IN_SHARD, (d + 1) * W_IN_SHARD)]
                                      + [pad], axis=1) for d in range(N_DEV)])


def gather_layer_weights(shards, l):
    send = [jnp.pad(shards[n], ((0, 0), (0, W_IN_PACKED - W_IN_SHARD))) if n == "w_in" else shards[n] for n in BIG]
    got = dict(zip(BIG, all_gather(send, f"l{l}_gather_weights")))
    full = {n: got[n].reshape(-1, got[n].shape[-1]) for n in ("ffn_w2", "w_out", "mem_wkv")}
    full["ffn_w13"] = got["ffn_w13"]
    full["w_in"] = assemble_w_in(got["w_in"])
    return full


def reduce_scatter_layer(gw, c_idx, l):
    g = [split_w_in_grad(gw[n]) if n == "w_in" else gw[n].reshape((N_DEV, -1, gw[n].shape[-1])) for n in BIG]
    land1 = pair_exchange(g, f"l{l}_rs_pair_exchange")
    chip = [pair_sum(g[i], land1[i], c_idx, f"l{l}_rs_pair_sum_{n}") for i, n in enumerate(BIG)]
    land2 = chip_exchange(chip, f"l{l}_rs_chip_exchange")
    out = {n: sum_slots(land2[i], f"l{l}_rs_chip_sum_{n}") for i, n in enumerate(BIG)}
    out["w_in"] = out["w_in"][:, :W_IN_SHARD]
    return out


def gather_gate_weights(w, depth):
    mine = jnp.concatenate([w[n].reshape(-1) for n in GATE_W]).reshape(-1, 128)
    got = all_gather([mine], "gather_gate_weights")[0].reshape(N_DEV, len(GATE_W), depth, GLA_RANK, GLA_DK // N_DEV)
    return [{n: got[:, i, l].transpose(1, 0, 2).reshape(GLA_RANK, GLA_DK) for i, n in enumerate(GATE_W)} for l in range(depth)]


def pad_gate_weight(wg2, backward_dir):
    r0 = GLA_RANK if backward_dir else 0
    w = wg2.astype(F32).reshape(GLA_RANK, GLA_H, GLA_HK).transpose(1, 0, 2)
    return jnp.pad(w, ((0, 0), (r0, 128 - GLA_RANK - r0), (0, 0)))


def unpad_gate_grad(dw, backward_dir):
    r0 = GLA_RANK if backward_dir else 0
    return dw[:, r0:r0 + GLA_RANK, :].transpose(1, 0, 2).reshape(GLA_RANK, GLA_DK)


def to_heads(t):
    T = t.shape[0]
    return t.reshape(T, NA_H, NA_HD).transpose(1, 0, 2)


def from_heads(t):
    return t.transpose(1, 0, 2).reshape(t.shape[1], NA_W)


REPLICATED = ("attn_norm", "gla_bg_f", "gla_bg_b", "gla_out_norm", "na_q_norm", "na_k_norm", "na_rpb", "na_out_norm",
              "mem_norm", "mem_q_norm", "mem_k_norm", "mem_out_norm", "ffn_norm")
WEIGHTS = ("attn_norm", "w_in", "gla_wg2_f", "gla_bg_f", "gla_wg2_b", "gla_bg_b", "gla_out_norm", "na_q_norm", "na_k_norm",
           "na_rpb", "na_out_norm", "mem_norm", "mem_wkv", "mem_q_norm", "mem_k_norm", "mem_out_norm", "w_out", "ffn_norm",
           "ffn_w13", "ffn_w2")


def fold_heads(dg, n_heads, name):
    hd = dg.shape[1] // n_heads
    fold = (np.arange(dg.shape[1])[:, None] % hd == np.arange(128)[None, :]).astype(np.float32)
    out = matmul(jnp.pad(dg, ((0, 7), (0, 0))), fold, "nn", name, exact=True)
    return out[0, :hd]


def layer_fwd(x, mem_n_in, p, W, l):
    tag = f"l{l}_"
    row = lambda v: v.reshape(1, -1)
    sv = {"x": x}
    sv["xn"] = rms_fwd(x, row(p["attn_norm"]), tag + "attn_rms")
    proj = matmul(sv["xn"], W["w_in"], "nn", tag + "proj", tn=768, tk=2048)
    sv["proj"] = proj
    sv["wg_f"], sv["wg_b"] = pad_gate_weight(W["gla_wg2_f"], False), pad_gate_weight(W["gla_wg2_b"], True)
    sv["bg_f"], sv["bg_b"] = p["gla_bg_f"].reshape(GLA_H, 1, GLA_HK), p["gla_bg_b"].reshape(GLA_H, 1, GLA_HK)
    sv["o_f"], sv["s_f"] = gla_fwd(proj, sv["wg_f"], sv["bg_f"], False, tag + "gla_f")
    sv["o_b"], sv["s_b"] = gla_fwd(proj, sv["wg_b"], sv["bg_b"], True, tag + "gla_b")
    sv["gq"], sv["gk"] = jnp.tile(row(p["na_q_norm"]), (1, NA_H)), jnp.tile(row(p["na_k_norm"]), (1, NA_H))
    sv["gmq"], sv["gmk"] = jnp.tile(row(p["mem_q_norm"]), (1, MEM_H)), jnp.tile(row(p["mem_k_norm"]), (1, MEM_H))
    qn, kn, vn, mqn = pre_fwd(proj, sv["gq"], sv["gk"], sv["gmq"], tag + "pre")
    sv["q_hm"], sv["k_hm"], sv["v_hm"], sv["mqn"] = to_heads(qn), to_heads(kn), to_heads(vn), mqn
    sv["tb2"] = rpb_table(p["na_rpb"], tag + "rpb_table")
    sv["o_na"] = from_heads(natten_fwd(sv["q_hm"], sv["k_hm"], sv["v_hm"], sv["tb2"], tag + "natten"))
    sv["mem_n"] = rms_fwd(mem_n_in, row(p["mem_norm"]), tag + "mem_rms")
    sv["kv"] = matmul(sv["mem_n"], W["mem_wkv"], "nn", tag + "mem_kv", tn=512, tk=2048)
    sv["km"], sv["vm"] = mem_kprep(sv["kv"], sv["gmk"], tag + "mem_kprep")
    sv["o_mem"] = mem_attn_fwd(mqn, sv["km"], sv["vm"], tag + "mem_attn")
    sv["ycat"] = post_fwd(sv["o_f"], sv["o_b"], proj, sv["o_na"], sv["o_mem"], row(p["gla_out_norm"]),
                          row(p["na_out_norm"]), row(p["mem_out_norm"]), tag + "post")
    x1 = matmul(sv["ycat"], W["w_out"], "nn", tag + "out_proj", res=x, tk=2048)
    sv["x1"] = x1
    sv["h"] = rms_fwd(x1, row(p["ffn_norm"]), tag + "ffn_rms")
    sv["gu"], sv["a"] = ffn_up_swiglu(sv["h"], W["ffn_w13"], tag + "ffn_up")
    x2 = matmul(sv["a"], W["ffn_w2"], "nn", tag + "ffn_down", res=x1, tk=1408)
    return x2, sv


def layer_bwd(dx2, dx2_b, mem_n_in, p, W, sv, l):
    tag = f"l{l}_b_"
    row = lambda v: v.reshape(1, -1)
    gw, gs = {}, {}
    gw["ffn_w2"] = matmul(sv["a"], dx2_b, "tn", tag + "dw2", out_dtype=BF16, tm=1408, tn=1024, tk=1024)
    dgu = ffn_down_bwd(dx2_b, W["ffn_w2"], sv["gu"], tag + "d_swiglu")
    dh = matmul(dgu, W["ffn_w13"], "nt", tag + "d_h", a_halves=True, b_blocked=True)
    gw["ffn_w13"] = matmul(sv["h"], dgu, "tn", tag + "dw13", out_dtype=BF16, tm=2048, tn=1408, tk=1024,
                           b_halves=True, out_blocked=True)
    dx1, dx1_b, dg = rms_bwd(sv["x1"], row(p["ffn_norm"]), dh, dx2, tag + "ffn_rms")
    gs["ffn_norm"] = dg[0]
    dycat = matmul(dx1_b, W["w_out"], "nt", tag + "d_ycat", tk=2048)
    gw["w_out"] = matmul(sv["ycat"], dx1_b, "tn", tag + "dw_out", out_dtype=BF16, tm=2048, tn=1024, tk=1024)
    d_o, d_r, d_ona, d_omem, dgg, dgn, dgm = post_bwd(
        sv["o_f"], sv["o_b"], sv["proj"], sv["o_na"], sv["o_mem"], row(p["gla_out_norm"]), row(p["na_out_norm"]),
        row(p["mem_out_norm"]), dycat, tag + "post")
    gs["gla_out_norm"], gs["na_out_norm"], gs["mem_out_norm"] = dgg[0], dgn[0], dgm[0]
    dq_f, dk_f, dv_f, dt_f, dwg_f, dbg_f = gla_bwd(sv["proj"], sv["wg_f"], sv["bg_f"], sv["s_f"], d_o, False, tag + "gla_f")
    dq_b, dk_b, dv_b, dt_b, dwg_b, dbg_b = gla_bwd(sv["proj"], sv["wg_b"], sv["bg_b"], sv["s_b"], d_o, True, tag + "gla_b")
    gs["gla_wg2_f"], gs["gla_wg2_b"] = unpad_gate_grad(dwg_f, False), unpad_gate_grad(dwg_b, True)
    gs["gla_bg_f"], gs["gla_bg_b"] = dbg_f.reshape(-1), dbg_b.reshape(-1)
    dq_hm, dk_hm, dv_hm, dtb2 = natten_bwd(sv["q_hm"], sv["k_hm"], sv["v_hm"], sv["tb2"], to_heads(d_ona), tag + "natten")
    gs["na_rpb"] = rpb_table_bwd(dtb2, tag + "rpb_table")
    d_mqn, dkm, dvm = mem_attn_bwd(sv["mqn"], sv["km"], sv["vm"], d_omem, tag + "mem_attn")
    dkv, dgmk = mem_kprep_bwd(sv["kv"], sv["gmk"], dkm, dvm, tag + "mem_kprep")
    gs["mem_k_norm"] = fold_heads(dgmk, MEM_H, tag + "fold_mk")
    gw["mem_wkv"] = matmul(sv["mem_n"], dkv, "tn", tag + "dw_kv", out_dtype=BF16, tm=2048, tn=1024, tk=256)
    d_memn = matmul(dkv, W["mem_wkv"], "nt", tag + "d_memn", tk=1024)
    _, _, dg = rms_bwd(mem_n_in, row(p["mem_norm"]), d_memn, None, tag + "mem_rms")
    gs["mem_norm"] = dg[0]
    dproj, dgq, dgk, dgmq = pre_bwd(sv["proj"], sv["gq"], sv["gk"], sv["gmq"], from_heads(dq_hm), from_heads(dk_hm),
                                    from_heads(dv_hm), d_mqn, dq_f, dq_b, dk_f, dk_b, dv_f, dv_b, d_r, dt_f, dt_b, tag + "pre")
    gs["na_q_norm"] = fold_heads(dgq, NA_H, tag + "fold_q")
    gs["na_k_norm"] = fold_heads(dgk, NA_H, tag + "fold_k")
    gs["mem_q_norm"] = fold_heads(dgmq, MEM_H, tag + "fold_mq")
    dxn = matmul(dproj, W["w_in"], "nt", tag + "d_xn", tk=768)
    gw["w_in"] = matmul(sv["xn"], dproj, "tn", tag + "dw_in", out_dtype=BF16, tm=2048, tn=768, tk=1024)
    dx, dx_b, dg = rms_bwd(sv["x"], row(p["attn_norm"]), dxn, dx1, tag + "attn_rms")
    gs["attn_norm"] = dg[0]
    return dx, dx_b, gw, gs


def kernel(x, mem, attn_norm, w_in, gla_wg2_f, gla_bg_f, gla_wg2_b, gla_bg_b, gla_out_norm, na_q_norm, na_k_norm, na_rpb, na_out_norm, mem_norm, mem_wkv, mem_q_norm, mem_k_norm, mem_out_norm, w_out, ffn_norm, ffn_w13, ffn_w2, loss_target, m_attn_norm, m_w_in, m_gla_wg2_f, m_gla_bg_f, m_gla_wg2_b, m_gla_bg_b, m_gla_out_norm, m_na_q_norm, m_na_k_norm, m_na_rpb, m_na_out_norm, m_mem_norm, m_mem_wkv, m_mem_q_norm, m_mem_k_norm, m_mem_out_norm, m_w_out, m_ffn_norm, m_ffn_w13, m_ffn_w2, v_attn_norm, v_w_in, v_gla_wg2_f, v_gla_bg_f, v_gla_wg2_b, v_gla_bg_b, v_gla_out_norm, v_na_q_norm, v_na_k_norm, v_na_rpb, v_na_out_norm, v_mem_norm, v_mem_wkv, v_mem_q_norm, v_mem_k_norm, v_mem_out_norm, v_w_out, v_ffn_norm, v_ffn_w13, v_ffn_w2):
    w = dict(attn_norm=attn_norm, w_in=w_in, gla_wg2_f=gla_wg2_f, gla_bg_f=gla_bg_f, gla_wg2_b=gla_wg2_b, gla_bg_b=gla_bg_b,
             gla_out_norm=gla_out_norm, na_q_norm=na_q_norm, na_k_norm=na_k_norm, na_rpb=na_rpb, na_out_norm=na_out_norm,
             mem_norm=mem_norm, mem_wkv=mem_wkv, mem_q_norm=mem_q_norm, mem_k_norm=mem_k_norm, mem_out_norm=mem_out_norm,
             w_out=w_out, ffn_norm=ffn_norm, ffn_w13=ffn_w13, ffn_w2=ffn_w2)
    mom = dict(attn_norm=m_attn_norm, w_in=m_w_in, gla_wg2_f=m_gla_wg2_f, gla_bg_f=m_gla_bg_f, gla_wg2_b=m_gla_wg2_b,
               gla_bg_b=m_gla_bg_b, gla_out_norm=m_gla_out_norm, na_q_norm=m_na_q_norm, na_k_norm=m_na_k_norm, na_rpb=m_na_rpb,
               na_out_norm=m_na_out_norm, mem_norm=m_mem_norm, mem_wkv=m_mem_wkv, mem_q_norm=m_mem_q_norm,
               mem_k_norm=m_mem_k_norm, mem_out_norm=m_mem_out_norm, w_out=m_w_out, ffn_norm=m_ffn_norm, ffn_w13=m_ffn_w13,
               ffn_w2=m_ffn_w2)
    var = dict(attn_norm=v_attn_norm, w_in=v_w_in, gla_wg2_f=v_gla_wg2_f, gla_bg_f=v_gla_bg_f, gla_wg2_b=v_gla_wg2_b,
               gla_bg_b=v_gla_bg_b, gla_out_norm=v_gla_out_norm, na_q_norm=v_na_q_norm, na_k_norm=v_na_k_norm, na_rpb=v_na_rpb,
               na_out_norm=v_na_out_norm, mem_norm=v_mem_norm, mem_wkv=v_mem_wkv, mem_q_norm=v_mem_q_norm,
               mem_k_norm=v_mem_k_norm, mem_out_norm=v_mem_out_norm, w_out=v_w_out, ffn_norm=v_ffn_norm, ffn_w13=v_ffn_w13,
               ffn_w2=v_ffn_w2)
    depth = attn_norm.shape[0]
    T = x.shape[1]
    xs, mem0, tgt = x.reshape(T, D_MODEL), mem.reshape(MEM_TOK, D_MODEL), loss_target.reshape(T, D_MODEL)
    c_idx = lax.axis_index("c").astype(jnp.int32).reshape(1)

    gates = gather_gate_weights(w, depth)
    W = [{**gather_layer_weights({n: w[n][l].astype(BF16) for n in BIG}, l), **gates[l]} for l in range(depth)]
    P = [{n: w[n][l] for n in REPLICATED} for l in range(depth)]

    saved = []
    h = xs
    for l in range(depth):
        h, sv = layer_fwd(h, mem0, P[l], W[l], l)
        saved.append(sv)
    dy, dy_b, lsum = loss_bwd(h, tgt, "loss")
    loss = lax.psum(lsum[0, 0], ("x", "y", "c")) * (0.5 / D_MODEL)

    g_shard, g_small = [None] * depth, [None] * depth
    for l in range(depth - 1, -1, -1):
        dy, dy_b, gw, g_small[l] = layer_bwd(dy, dy_b, mem0, P[l], W[l], saved[l], l)
        g_shard[l] = reduce_scatter_layer(gw, c_idx, l)
        saved[l] = None
    grad_x = dy.reshape(x.shape)

    small_names = REPLICATED + GATE_W
    small = jnp.concatenate([g_small[l][n].reshape(-1) for l in range(depth) for n in small_names])
    n_small = small.shape[0]
    rows = -(-n_small // 1024) * 8
    small = jnp.pad(small, (0, rows * 128 - n_small)).reshape(rows, 128)
    small = sum_slots(all_gather([small], "gather_small_grads")[0], "sum_small_grads").reshape(-1)
    grads, off = {}, 0
    per_layer = {n: [] for n in small_names}
    my_cols = (4 * lax.axis_index("x") + 2 * lax.axis_index("y") + lax.axis_index("c")) * (GLA_DK // N_DEV)
    for l in range(depth):
        for n in small_names:
            shp = (GLA_RANK, GLA_DK) if n in GATE_W else w[n].shape[1:]
            g = small[off:off + int(np.prod(shp))].reshape(shp)
            off += int(np.prod(shp))
            per_layer[n].append(lax.dynamic_slice_in_dim(g, my_cols, GLA_DK // N_DEV, axis=1) if n in GATE_W else g)
    for n in small_names:
        grads[n] = jnp.stack(per_layer[n])
    for n in BIG:
        grads[n] = jnp.stack([g_shard[l][n] for l in range(depth)])

    delta, new_m, new_v = {}, {}, {}
    for n in WEIGHTS:
        shp = w[n].shape
        two_d = (shp[0], int(np.prod(shp[1:]))) if n in REPLICATED else (int(np.prod(shp[:-1])), shp[-1])
        d_, m_, v_ = adamw(w[n].reshape(two_d), grads[n].reshape(two_d), mom[n].reshape(two_d), var[n].reshape(two_d),
                           "adamw_" + n)
        delta[n], new_m[n], new_v[n] = d_.reshape(shp), m_.reshape(shp), v_.reshape(shp)

    return (loss, grad_x, *[grads[n] for n in WEIGHTS], *[delta[n] for n in WEIGHTS], *[new_m[n] for n in WEIGHTS],
            *[new_v[n] for n in WEIGHTS])
```

```python
import functools

import numpy as np
import jax
import jax.numpy as jnp
from jax import lax
from jax.experimental import pallas as pl
from jax.experimental.pallas import tpu as pltpu

F32, BF16 = jnp.float32, jnp.bfloat16
HI = lax.Precision.HIGHEST
SDS = jax.ShapeDtypeStruct
MESH = pl.DeviceIdType.MESH

D_MODEL = 2048
GRID_W = 64
GLA_H, GLA_HK, GLA_HV, GLA_RANK, GLA_TAU, GLA_C = 4, 128, 256, 16, 16.0, 64
GLA_DK, GLA_DV = GLA_H * GLA_HK, GLA_H * GLA_HV
NA_H, NA_HD, NA_ROWS, NA_COLS = 8, 64, 8, 16
NA_W = NA_H * NA_HD
MEM_H, MEM_HD, MEM_TOK = 4, 128, 256
MEM_W = MEM_H * MEM_HD
D_FF = 5632
IN_COLS = 5152
RMS_EPS = 1e-6
ADAM_LR, ADAM_B1, ADAM_B2, ADAM_EPS, ADAM_WD, ADAM_STEP = 0.001, 0.9, 0.999, 1e-08, 0.01, 10
N_DEV = 8

PC = 5376
COL_R, COL_NQ, COL_NK, COL_NV, COL_MQ, COL_TAIL = 2048, 3072, 3584, 4096, 4608, 5120
ORIG_GATE0 = 3072
ORIG_AFTER_GATE = 3104

GLA_G = 8
VMEM_LIMIT = 56 * 1024 * 1024
NEG = -1e30


def _cparams(sem):
    return pltpu.CompilerParams(dimension_semantics=sem, vmem_limit_bytes=VMEM_LIMIT)


def _fit(n, pref, unit=128):
    if n <= pref:
        return n
    t = (pref // unit) * unit
    while t >= unit:
        if n % t == 0:
            return t
        t -= unit
    return n


def _bdot(a, b, dims):
    return lax.dot_general(a.astype(BF16), b.astype(BF16), (dims, ((), ())), preferred_element_type=F32)


NN, NT, TN = ((1,), (0,)), ((1,), (1,)), ((0,), (0,))


@functools.partial(jax.custom_vjp, nondiff_argnums=(2,))
def _bdot_vjp(a, b, dims):
    return _bdot(a, b, dims)


def _bdot_vjp_fwd(a, b, dims):
    return _bdot(a, b, dims), (a, b)


def _bdot_vjp_bwd(dims, res, ct):
    a, b = res
    if dims == NN:
        return _bdot(ct, b, NT), _bdot(a, ct, TN)
    if dims == NT:
        return _bdot(ct, b, NN), _bdot(ct, a, TN)
    return _bdot(b, ct, NT), _bdot(a, ct, NN)


_bdot_vjp.defvjp(_bdot_vjp_fwd, _bdot_vjp_bwd)


def _split_dot(c, x, dims):
    hi = x.astype(BF16)
    lo = (x - hi.astype(F32)).astype(BF16)
    cb = c.astype(BF16)
    return (lax.dot_general(cb, hi, (dims, ((), ())), preferred_element_type=F32)
            + lax.dot_general(cb, lo, (dims, ((), ())), preferred_element_type=F32))


@jax.custom_vjp
def _tri_sum(cmat, x):
    return _split_dot(cmat, x, NN)


def _tri_sum_fwd(cmat, x):
    return _split_dot(cmat, x, NN), cmat


def _tri_sum_bwd(cmat, ct):
    return jnp.zeros_like(cmat), _split_dot(cmat, ct, TN)


_tri_sum.defvjp(_tri_sum_fwd, _tri_sum_bwd)


def _call_with_rider(body, grid, in_specs, out_specs, out_shape, scratch, sem, name, args, rider):
    if rider is None:
        return pl.pallas_call(body, grid=grid, in_specs=in_specs, out_specs=out_specs, out_shape=out_shape,
                              scratch_shapes=scratch, compiler_params=_cparams(sem), name=name)(*args), None
    outs_l = list(out_shape) if isinstance(out_shape, (list, tuple)) else [out_shape]
    specs_l = list(out_specs) if isinstance(out_specs, (list, tuple)) else [out_specs]
    ni, no, ns = len(in_specs), len(outs_l), len(scratch)
    nri, nro = len(rider.inputs), len(rider.out_shapes)
    any_spec = pl.BlockSpec(memory_space=pl.ANY)

    def wrapped(*refs):
        ins, rin = refs[:ni], refs[ni:ni + nri]
        o0 = ni + nri
        outs, rout = refs[o0:o0 + no], refs[o0 + no:o0 + no + nro]
        s0 = o0 + no + nro
        scr, rsem = refs[s0:s0 + ns], refs[s0 + ns:]
        ids = [pl.program_id(d) for d in range(len(grid))]
        first = functools.reduce(jnp.logical_and, [i == 0 for i in ids])
        last = functools.reduce(jnp.logical_and, [i == g - 1 for i, g in zip(ids, grid)])

        @pl.when(first)
        def _():
            rider.start(rin, rout, rsem)

        body(*ins, *outs, *scr)

        @pl.when(last)
        def _():
            rider.finish(rin, rout, rsem)

    res = pl.pallas_call(
        wrapped, grid=grid, in_specs=list(in_specs) + [any_spec] * nri, out_specs=specs_l + [any_spec] * nro,
        out_shape=outs_l + rider.out_shapes, scratch_shapes=list(scratch) + rider.scratch,
        input_output_aliases={ni + i: no + j for i, j in rider.aliases.items()},
        compiler_params=_cparams(("arbitrary",) * len(grid)), name=name)(*args, *rider.inputs)
    main = res[:no]
    return (main if isinstance(out_shape, (list, tuple)) else main[0]), list(res[no:])


def matmul(a, b, mode, name, out_dtype=F32, res=None, tm=1024, tn=1024, tk=512, exact=False,
           a_halves=False, b_halves=False, b_blocked=False, out_blocked=False, rider=None):
    if mode == "nn":
        M, K = a.shape
        N = b.shape[0] * b.shape[2] if b_blocked else b.shape[1]
        if b_blocked:
            tn = b.shape[2]
    elif mode == "nt":
        M, K = (a.shape[1], 2 * a.shape[2]) if a_halves else a.shape
        N = b.shape[1] if b_blocked else b.shape[0]
        if b_blocked:
            tk = b.shape[2]
    else:
        K, M = a.shape
        N = 2 * b.shape[2] if b_halves else b.shape[1]
    tm, tn, tk = _fit(M, tm, 8 if mode != "tn" else 128), _fit(N, tn), _fit(K, tk, 128 if mode != "tn" else 16)
    nk = K // tk
    dims = {"nn": NN, "nt": NT, "tn": TN}[mode]

    def body(*refs):
        if res is None:
            a_ref, b_ref, o_ref = refs[:3]
            r_ref = None
            acc = refs[3] if nk > 1 else None
        else:
            a_ref, b_ref, r_ref, o_ref = refs[:4]
            acc = refs[4] if nk > 1 else None
        if exact:
            part = lax.dot_general(a_ref[...], b_ref[...], (dims, ((), ())), preferred_element_type=F32, precision=HI)
        else:
            part = _bdot(a_ref[...], b_ref[...], dims)

        def finish(val):
            if r_ref is not None:
                val = val + r_ref[...]
            o_ref[...] = val.astype(out_dtype)

        if nk == 1:
            finish(part)
        else:
            kk = pl.program_id(2)

            @pl.when(kk == 0)
            def _():
                acc[...] = part

            @pl.when(kk > 0)
            def _():
                acc[...] += part

            @pl.when(kk == nk - 1)
            def _():
                finish(acc[...])

    if mode == "tn":
        a_spec = pl.BlockSpec((tk, tm), lambda i, j, k: (k, i))
    elif a_halves:
        nh = K // 2 // tk
        a_spec = pl.BlockSpec((None, tm, tk), lambda i, j, k: (k // nh, i, k % nh))
    else:
        a_spec = pl.BlockSpec((tm, tk), lambda i, j, k: (i, k))
    if b_blocked:
        b_spec = (pl.BlockSpec((None, tk, tn), lambda i, j, k: (j, k, 0)) if mode == "nn"
                  else pl.BlockSpec((None, tn, tk), lambda i, j, k: (k, j, 0)))
    elif b_halves:
        nh = N // 2 // tn
        b_spec = pl.BlockSpec((None, tk, tn), lambda i, j, k: (j // nh, k, j % nh))
    elif mode == "nt":
        b_spec = pl.BlockSpec((tn, tk), lambda i, j, k: (j, k))
    else:
        b_spec = pl.BlockSpec((tk, tn), lambda i, j, k: (k, j))
    if out_blocked:
        o_spec, o_shape = pl.BlockSpec((None, tm, tn), lambda i, j, k: (j, i, 0)), SDS((N // tn, M, tn), out_dtype)
    else:
        o_spec, o_shape = pl.BlockSpec((tm, tn), lambda i, j, k: (i, j)), SDS((M, N), out_dtype)
    in_specs, args = [a_spec, b_spec], [a, b]
    if res is not None:
        in_specs.append(o_spec)
        args.append(res)
    out, carried = _call_with_rider(
        body, (M // tm, N // tn, nk), in_specs, o_spec, o_shape, [pltpu.VMEM((tm, tn), F32)] if nk > 1 else [],
        ("parallel", "parallel", "arbitrary"), name, args, rider)
    return out if rider is None else (out, carried)


def rms_fwd(x, g, name):
    T, D = x.shape
    tm = _fit(T, 512, 8)

    def body(x_ref, g_ref, o_ref):
        xv = x_ref[...]
        r = lax.rsqrt(jnp.mean(xv * xv, axis=-1, keepdims=True) + RMS_EPS)
        o_ref[...] = (xv * r * g_ref[...]).astype(BF16)

    return pl.pallas_call(
        body, grid=(T // tm,), in_specs=[pl.BlockSpec((tm, D), lambda i: (i, 0)), pl.BlockSpec((1, D), lambda i: (0, 0))],
        out_specs=pl.BlockSpec((tm, D), lambda i: (i, 0)), out_shape=SDS((T, D), BF16),
        compiler_params=_cparams(("parallel",)), name=name)(x, g)


def rms_bwd(x, g, dy, dres, name):
    T, D = x.shape
    tm = _fit(T, 256, 8)
    has_res = dres is not None

    def body(*refs):
        if has_res:
            x_ref, g_ref, dy_ref, dres_ref, dx_ref, dxb_ref, dg_ref = refs
        else:
            x_ref, g_ref, dy_ref, dx_ref, dxb_ref, dg_ref = refs
        xv, dyv = x_ref[...], dy_ref[...]
        r = lax.rsqrt(jnp.mean(xv * xv, axis=-1, keepdims=True) + RMS_EPS)
        xh = xv * r
        dxh = dyv * g_ref[...]
        dx = r * (dxh - xh * jnp.mean(dxh * xh, axis=-1, keepdims=True))
        if has_res:
            dx = dx + dres_ref[...]
        dx_ref[...] = dx
        dxb_ref[...] = dx.astype(BF16)

        @pl.when(pl.program_id(0) == 0)
        def _():
            dg_ref[...] = jnp.zeros_like(dg_ref)

        dg_ref[...] += jnp.sum(dyv * xh, axis=0, keepdims=True)

    row = pl.BlockSpec((tm, D), lambda i: (i, 0))
    vec = pl.BlockSpec((1, D), lambda i: (0, 0))
    args = [x, g, dy] + ([dres] if has_res else [])
    return pl.pallas_call(
        body, grid=(T // tm,), in_specs=[row, vec, row] + ([row] if has_res else []),
        out_specs=[row, row, vec], out_shape=[SDS((T, D), F32), SDS((T, D), BF16), SDS((1, D), F32)],
        compiler_params=_cparams(("arbitrary",)), name=name)(*args)


def ffn_up_swiglu(h, w13b, name, rider=None):
    T, D = h.shape
    nb, _, tb = w13b.shape
    nh = nb // 2
    tm = _fit(T, 512, 8)

    def body(h_ref, wg_ref, wu_ref, gu_ref, a_ref):
        hv = h_ref[...]
        gv = _bdot(hv, wg_ref[...], NN)
        uv = _bdot(hv, wu_ref[...], NN)
        gu_ref[0] = gv
        gu_ref[1] = uv
        a_ref[...] = (gv * jax.nn.sigmoid(gv) * uv).astype(BF16)

    (gu, act), carried = _call_with_rider(
        body, (nh, T // tm),
        [pl.BlockSpec((tm, D), lambda j, i: (i, 0)), pl.BlockSpec((None, D, tb), lambda j, i: (j, 0, 0)),
         pl.BlockSpec((None, D, tb), lambda j, i: (j + nh, 0, 0))],
        [pl.BlockSpec((2, tm, tb), lambda j, i: (0, i, j)), pl.BlockSpec((tm, tb), lambda j, i: (i, j))],
        [SDS((2, T, nh * tb), F32), SDS((T, nh * tb), BF16)], [], ("parallel", "parallel"), name, [h, w13b, w13b], rider)
    return gu, act, carried


def ffn_down_bwd(dy_b, w2, gu, name):
    T, D = dy_b.shape
    Fh = w2.shape[0]
    tm, tf = _fit(T, 1024, 8), _fit(Fh, 512)
    nf = Fh // tf

    def body(dy_ref, w_ref, g_ref, u_ref, o_ref):
        dav = _bdot(dy_ref[...], w_ref[...], NT)
        gv, uv = g_ref[...], u_ref[...]
        sg = jax.nn.sigmoid(gv)
        o_ref[0] = (dav * uv * (sg * (1.0 + gv * (1.0 - sg)))).astype(BF16)
        o_ref[1] = (dav * gv * sg).astype(BF16)

    return pl.pallas_call(
        body, grid=(T // tm, nf),
        in_specs=[pl.BlockSpec((tm, D), lambda i, j: (i, 0)), pl.BlockSpec((tf, D), lambda i, j: (j, 0)),
                  pl.BlockSpec((None, tm, tf), lambda i, j: (0, i, j)), pl.BlockSpec((None, tm, tf), lambda i, j: (1, i, j))],
        out_specs=pl.BlockSpec((2, tm, tf), lambda i, j: (0, i, j)), out_shape=SDS((2, T, Fh), BF16),
        compiler_params=_cparams(("parallel", "parallel")), name=name)(dy_b, w2, gu, gu)


def loss_bwd(y, tgt, name):
    T, D = y.shape
    tm = _fit(T, 512, 8)

    def body(y_ref, t_ref, dy_ref, dyb_ref, l_ref):
        e = y_ref[...] - t_ref[...]
        dy = e * (1.0 / D)
        dy_ref[...] = dy
        dyb_ref[...] = dy.astype(BF16)

        @pl.when(pl.program_id(0) == 0)
        def _():
            l_ref[...] = jnp.zeros_like(l_ref)

        l_ref[...] += jnp.sum(jnp.sum(e * e, axis=1, keepdims=True), axis=0, keepdims=True)

    row = pl.BlockSpec((tm, D), lambda i: (i, 0))
    return pl.pallas_call(
        body, grid=(T // tm,), in_specs=[row, row], out_specs=[row, row, pl.BlockSpec((8, 128), lambda i: (0, 0))],
        out_shape=[SDS((T, D), F32), SDS((T, D), BF16), SDS((8, 128), F32)],
        compiler_params=_cparams(("arbitrary",)), name=name)(y, tgt)


def _log_sigmoid(z):
    return jnp.minimum(z, 0.0) - jnp.log(1.0 + jnp.exp(-jnp.abs(z)))


def _gla_group(q, k, v, tail, wg, bg, s, cmat, mask, backward_dir):
    n = q.shape[0] // GLA_C
    z = _bdot_vjp(tail, wg, NN) + bg
    la = _log_sigmoid(z) * (1.0 / GLA_TAU)
    cum = _tri_sum(cmat, la)
    chunks = [slice(g * GLA_C, (g + 1) * GLA_C) for g in range(n)]
    last = [jnp.sum(la[sl], axis=0, keepdims=True) for sl in chunks]
    last_rows = jnp.concatenate([jnp.broadcast_to(t, (GLA_C, t.shape[1])) for t in last], axis=0)
    q_e = q * jnp.exp(cum) * (GLA_HK ** -0.5)
    k_e = k * jnp.exp(-cum)
    k_end = k * jnp.exp(last_rows - cum)
    sc = jnp.where(mask > 0.5, _bdot_vjp(q_e, k_e, NT), 0.0)
    o = _bdot_vjp(sc, v, NN)
    o_inter = [None] * n
    for g in (range(n - 1, -1, -1) if backward_dir else range(n)):
        o_inter[g] = _bdot_vjp(q_e[chunks[g]], s, NN)
        s = s * jnp.transpose(jnp.exp(last[g])) + _bdot_vjp(k_end[chunks[g]], v[chunks[g]], TN)
    return o + jnp.concatenate(o_inter, axis=0), s


def _gla_consts(backward_dir, GC):
    i = np.arange(GC)
    same = (i[:, None] // GLA_C) == (i[None, :] // GLA_C)
    if backward_dir:
        return (same & (i[None, :] >= i[:, None])).astype(np.float32), (same & (i[None, :] > i[:, None])).astype(np.float32)
    tri = (same & (i[None, :] <= i[:, None])).astype(np.float32)
    return tri, tri


GLA_HB = 2


def _gla_in_specs(GC, nmap):
    return [
        pl.BlockSpec((GC, GLA_HB * GLA_HK), lambda h, n: (nmap(n), h)),
        pl.BlockSpec((GC, GLA_HB * GLA_HK), lambda h, n: (nmap(n), GLA_H // GLA_HB + h)),
        pl.BlockSpec((GC, GLA_HB * GLA_HV), lambda h, n: (nmap(n), 2 * GLA_DK // (GLA_HB * GLA_HV) + h)),
        pl.BlockSpec((GC, 128), lambda h, n: (nmap(n), COL_TAIL // 128)),
        pl.BlockSpec((GLA_HB, 128, 128), lambda h, n: (h, 0, 0)),
        pl.BlockSpec((GLA_HB, 1, 128), lambda h, n: (h, 0, 0)),
        pl.BlockSpec((GC, GC), lambda h, n: (0, 0)),
        pl.BlockSpec((GC, GC), lambda h, n: (0, 0)),
    ]


def _head_cols(ref, hh, width):
    return ref[:, hh * width:(hh + 1) * width]


def gla_fwd(proj, wgpad, bg, backward_dir, name):
    T = proj.shape[0]
    GC = min(GLA_G * GLA_C, T)
    NG = T // GC
    cmat, mask = _gla_consts(backward_dir, GC)
    nmap = (lambda n: NG - 1 - n) if backward_dir else (lambda n: n)

    def body(q_ref, k_ref, v_ref, t_ref, wg_ref, bg_ref, c_ref, m_ref, o_ref, ss_ref, s_scr):
        @pl.when(pl.program_id(1) == 0)
        def _():
            s_scr[...] = jnp.zeros_like(s_scr)

        tail, cm, mk = t_ref[...], c_ref[...], m_ref[...]
        for hh in range(GLA_HB):
            s0 = s_scr[hh]
            ss_ref[hh] = s0
            o, s1 = _gla_group(_head_cols(q_ref, hh, GLA_HK), _head_cols(k_ref, hh, GLA_HK), _head_cols(v_ref, hh, GLA_HV),
                               tail, wg_ref[hh], bg_ref[hh], s0, cm, mk, backward_dir)
            o_ref[:, hh * GLA_HV:(hh + 1) * GLA_HV] = o
            s_scr[hh] = s1

    return pl.pallas_call(
        body, grid=(GLA_H // GLA_HB, NG), in_specs=_gla_in_specs(GC, nmap),
        out_specs=[pl.BlockSpec((GC, GLA_HB * GLA_HV), lambda h, n: (nmap(n), h)),
                   pl.BlockSpec((GLA_HB, None, GLA_HK, GLA_HV), lambda h, n: (h, nmap(n), 0, 0))],
        out_shape=[SDS((T, GLA_DV), F32), SDS((GLA_H, NG, GLA_HK, GLA_HV), F32)],
        scratch_shapes=[pltpu.VMEM((GLA_HB, GLA_HK, GLA_HV), F32)],
        compiler_params=_cparams(("parallel", "arbitrary")), name=name)(proj, proj, proj, proj, wgpad, bg, cmat, mask)


def gla_bwd(proj, wgpad, bg, ssave, do, backward_dir, name):
    T = proj.shape[0]
    GC = min(GLA_G * GLA_C, T)
    NG = T // GC
    cmat, mask = _gla_consts(backward_dir, GC)
    nmap = (lambda n: n) if backward_dir else (lambda n: NG - 1 - n)

    def body(q_ref, k_ref, v_ref, t_ref, wg_ref, bg_ref, c_ref, m_ref, ss_ref, do_ref,
             dq_ref, dk_ref, dv_ref, dt_ref, dwg_ref, dbg_ref, ds_scr):
        @pl.when(pl.program_id(1) == 0)
        def _():
            ds_scr[...] = jnp.zeros_like(ds_scr)
            dwg_ref[...] = jnp.zeros_like(dwg_ref)
            dbg_ref[...] = jnp.zeros_like(dbg_ref)

        tail, cm, mk = t_ref[...], c_ref[...], m_ref[...]
        fn = lambda q, k, v, t, wg, b, s: _gla_group(q, k, v, t, wg, b, s, cm, mk, backward_dir)
        for hh in range(GLA_HB):
            _, vjp = jax.vjp(fn, _head_cols(q_ref, hh, GLA_HK), _head_cols(k_ref, hh, GLA_HK), _head_cols(v_ref, hh, GLA_HV),
                             tail, wg_ref[hh], bg_ref[hh], ss_ref[hh])
            dq, dk, dv, dt, dwg, dbg, ds = vjp((_head_cols(do_ref, hh, GLA_HV), ds_scr[hh]))
            dq_ref[:, hh * GLA_HK:(hh + 1) * GLA_HK] = dq
            dk_ref[:, hh * GLA_HK:(hh + 1) * GLA_HK] = dk
            dv_ref[:, hh * GLA_HV:(hh + 1) * GLA_HV] = dv
            dt_ref[hh] = dt
            dwg_ref[hh] += dwg
            dbg_ref[hh] += dbg
            ds_scr[hh] = ds

    in_specs = _gla_in_specs(GC, nmap) + [
        pl.BlockSpec((GLA_HB, None, GLA_HK, GLA_HV), lambda h, n: (h, nmap(n), 0, 0)),
        pl.BlockSpec((GC, GLA_HB * GLA_HV), lambda h, n: (nmap(n), h)),
    ]
    out_specs = [
        pl.BlockSpec((GC, GLA_HB * GLA_HK), lambda h, n: (nmap(n), h)),
        pl.BlockSpec((GC, GLA_HB * GLA_HK), lambda h, n: (nmap(n), h)),
        pl.BlockSpec((GC, GLA_HB * GLA_HV), lambda h, n: (nmap(n), h)),
        pl.BlockSpec((GLA_HB, GC, 128), lambda h, n: (h, nmap(n), 0)),
        pl.BlockSpec((GLA_HB, 128, 128), lambda h, n: (h, 0, 0)),
        pl.BlockSpec((GLA_HB, 1, 128), lambda h, n: (h, 0, 0)),
    ]
    out_shape = [SDS((T, GLA_DK), F32), SDS((T, GLA_DK), F32), SDS((T, GLA_DV), F32), SDS((GLA_H, T, 128), F32),
                 SDS((GLA_H, 128, 128), F32), SDS((GLA_H, 1, 128), F32)]
    return pl.pallas_call(
        body, grid=(GLA_H // GLA_HB, NG), in_specs=in_specs, out_specs=out_specs, out_shape=out_shape,
        scratch_shapes=[pltpu.VMEM((GLA_HB, GLA_HK, GLA_HV), F32)],
        compiler_params=_cparams(("parallel", "arbitrary")), name=name)(proj, proj, proj, proj, wgpad, bg, cmat, mask, ssave, do)


def _block_diag(width, hd):
    i = np.arange(width)
    return ((i[:, None] // hd) == (i[None, :] // hd)).astype(np.float32) / hd


def _norm_heads(t, hd):
    outs = []
    for h in range(t.shape[1] // hd):
        th = t[:, h * hd:(h + 1) * hd]
        outs.append(th * lax.rsqrt(jnp.mean(th * th, axis=-1, keepdims=True) + RMS_EPS))
    return jnp.concatenate(outs, axis=1)


def _norm_bd(t, bd):
    return t * lax.rsqrt(jnp.dot(t * t, bd, preferred_element_type=F32, precision=HI) + RMS_EPS)


def _pre_fn(nq, nk, mq, gq, gk, gm, bd):
    return _norm_bd(nq, bd) * gq, _norm_bd(nk, bd) * gk, _norm_heads(mq, MEM_HD) * gm


def pre_fwd(proj, gq, gk, gm, name):
    T = proj.shape[0]
    tm = _fit(T, 512, 8)
    bd = _block_diag(NA_W, NA_HD)

    def body(nq_ref, nk_ref, nv_ref, mq_ref, gq_ref, gk_ref, gm_ref, bd_ref, q_ref, k_ref, v_ref, m_ref):
        qn, kn, mn = _pre_fn(nq_ref[...], nk_ref[...], mq_ref[...], gq_ref[...], gk_ref[...], gm_ref[...], bd_ref[...])
        q_ref[...] = qn.astype(BF16)
        k_ref[...] = kn.astype(BF16)
        v_ref[...] = nv_ref[...].astype(BF16)
        m_ref[...] = mn.astype(BF16)

    col = lambda c0: pl.BlockSpec((tm, 512), lambda i: (i, c0 // 512))
    vec = pl.BlockSpec((1, 512), lambda i: (0, 0))
    row = pl.BlockSpec((tm, 512), lambda i: (i, 0))
    return pl.pallas_call(
        body, grid=(T // tm,),
        in_specs=[col(COL_NQ), col(COL_NK), col(COL_NV), col(COL_MQ), vec, vec, vec, pl.BlockSpec((NA_W, NA_W), lambda i: (0, 0))],
        out_specs=[row] * 4, out_shape=[SDS((T, 512), BF16)] * 4,
        compiler_params=_cparams(("parallel",)), name=name)(proj, proj, proj, proj, gq, gk, gm, bd)


def pre_bwd(proj, gq, gk, gm, d_qn, d_kn, d_nv, d_mn, dq_f, dq_b, dk_f, dk_b, dv_f, dv_b, d_r, dt_f, dt_b, name):
    T = proj.shape[0]
    tm = _fit(T, 256, 8)
    bd = _block_diag(NA_W, NA_HD)

    def body(nq_ref, nk_ref, mq_ref, gq_ref, gk_ref, gm_ref, bd_ref, dqn_ref, dkn_ref, dnv_ref, dmn_ref,
             dqf_ref, dqb_ref, dkf_ref, dkb_ref, dvf_ref, dvb_ref, dr_ref, dtf_ref, dtb_ref,
             o_ref, dgq_ref, dgk_ref, dgm_ref):
        bdv = bd_ref[...]
        fn = lambda a, b, c, x, y, z: _pre_fn(a, b, c, x, y, z, bdv)
        _, vjp = jax.vjp(fn, nq_ref[...], nk_ref[...], mq_ref[...], gq_ref[...], gk_ref[...], gm_ref[...])
        d_nq, d_nk, d_mq, dgq, dgk, dgm = vjp((dqn_ref[...], dkn_ref[...], dmn_ref[...]))
        o_ref[:, 0:512] = (dqf_ref[...] + dqb_ref[...]).astype(BF16)
        o_ref[:, 512:1024] = (dkf_ref[...] + dkb_ref[...]).astype(BF16)
        o_ref[:, 1024:2048] = (dvf_ref[...] + dvb_ref[...]).astype(BF16)
        o_ref[:, COL_R:COL_R + 1024] = dr_ref[...].astype(BF16)
        o_ref[:, COL_NQ:COL_NQ + 512] = d_nq.astype(BF16)
        o_ref[:, COL_NK:COL_NK + 512] = d_nk.astype(BF16)
        o_ref[:, COL_NV:COL_NV + 512] = dnv_ref[...].astype(BF16)
        o_ref[:, COL_MQ:COL_MQ + 512] = d_mq.astype(BF16)
        dt = dtf_ref[0] + dtb_ref[0]
        for h in range(1, GLA_H):
            dt = dt + dtf_ref[h] + dtb_ref[h]
        o_ref[:, COL_TAIL:COL_TAIL + 128] = dt.astype(BF16)
        o_ref[:, COL_TAIL + 128:PC] = jnp.zeros((tm, PC - COL_TAIL - 128), BF16)

        @pl.when(pl.program_id(0) == 0)
        def _():
            dgq_ref[...] = jnp.zeros_like(dgq_ref)
            dgk_ref[...] = jnp.zeros_like(dgk_ref)
            dgm_ref[...] = jnp.zeros_like(dgm_ref)

        dgq_ref[...] += dgq
        dgk_ref[...] += dgk
        dgm_ref[...] += dgm

    col = lambda c0: pl.BlockSpec((tm, 512), lambda i: (i, c0 // 512))
    vec = pl.BlockSpec((1, 512), lambda i: (0, 0))
    r512 = pl.BlockSpec((tm, 512), lambda i: (i, 0))
    r1024 = pl.BlockSpec((tm, 1024), lambda i: (i, 0))
    tl = pl.BlockSpec((GLA_H, tm, 128), lambda i: (0, i, 0))
    in_specs = [col(COL_NQ), col(COL_NK), col(COL_MQ), vec, vec, vec, pl.BlockSpec((NA_W, NA_W), lambda i: (0, 0)),
                r512, r512, r512, r512, r512, r512, r512, r512, r1024, r1024, r1024, tl, tl]
    return pl.pallas_call(
        body, grid=(T // tm,), in_specs=in_specs,
        out_specs=[pl.BlockSpec((tm, PC), lambda i: (i, 0)), vec, vec, vec],
        out_shape=[SDS((T, PC), BF16), SDS((1, 512), F32), SDS((1, 512), F32), SDS((1, 512), F32)],
        compiler_params=_cparams(("arbitrary",)), name=name)(
            proj, proj, proj, gq, gk, gm, bd, d_qn, d_kn, d_nv, d_mn, dq_f, dq_b, dk_f, dk_b, dv_f, dv_b, d_r, dt_f, dt_b)


def _post_fn(o_f, o_b, r, o_na, o_mem, g_gla, g_na, g_mem, bd):
    y_gla = _norm_heads(o_f + o_b, GLA_HV) * g_gla * (r * jax.nn.sigmoid(r))
    y_na = _norm_bd(o_na, bd) * g_na
    y_mem = _norm_heads(o_mem, MEM_HD) * g_mem
    return jnp.concatenate([y_gla, y_na, y_mem], axis=1)


def post_fwd(o_f, o_b, proj, o_na, o_mem, g_gla, g_na, g_mem, name):
    T = proj.shape[0]
    tm = _fit(T, 256, 8)
    bd = _block_diag(NA_W, NA_HD)

    def body(of_ref, ob_ref, r_ref, ona_ref, omem_ref, gg_ref, gn_ref, gm_ref, bd_ref, y_ref):
        y_ref[...] = _post_fn(of_ref[...], ob_ref[...], r_ref[...], ona_ref[...], omem_ref[...],
                              gg_ref[...], gn_ref[...], gm_ref[...], bd_ref[...]).astype(BF16)

    r1024 = pl.BlockSpec((tm, 1024), lambda i: (i, 0))
    r512 = pl.BlockSpec((tm, 512), lambda i: (i, 0))
    in_specs = [r1024, r1024, pl.BlockSpec((tm, 1024), lambda i: (i, COL_R // 1024)), r512, r512,
                pl.BlockSpec((1, 1024), lambda i: (0, 0)), pl.BlockSpec((1, 512), lambda i: (0, 0)),
                pl.BlockSpec((1, 512), lambda i: (0, 0)), pl.BlockSpec((NA_W, NA_W), lambda i: (0, 0))]
    return pl.pallas_call(
        body, grid=(T // tm,), in_specs=in_specs, out_specs=pl.BlockSpec((tm, D_MODEL), lambda i: (i, 0)),
        out_shape=SDS((T, D_MODEL), BF16), compiler_params=_cparams(("parallel",)), name=name)(
            o_f, o_b, proj, o_na, o_mem, g_gla, g_na, g_mem, bd)


def post_bwd(o_f, o_b, proj, o_na, o_mem, g_gla, g_na, g_mem, dy, name):
    T = proj.shape[0]
    tm = _fit(T, 256, 8)
    bd = _block_diag(NA_W, NA_HD)

    def body(of_ref, ob_ref, r_ref, ona_ref, omem_ref, gg_ref, gn_ref, gm_ref, bd_ref, dy_ref,
             do_ref, dr_ref, dna_ref, dmem_ref, dgg_ref, dgn_ref, dgm_ref):
        bdv = bd_ref[...]
        fn = lambda o, r, a, m, x, y, z: _post_fn(o, 0.0, r, a, m, x, y, z, bdv)
        _, vjp = jax.vjp(fn, of_ref[...] + ob_ref[...], r_ref[...], ona_ref[...], omem_ref[...],
                         gg_ref[...], gn_ref[...], gm_ref[...])
        d_o, d_r, d_na, d_mem, dgg, dgn, dgm = vjp(dy_ref[...])
        do_ref[...] = d_o
        dr_ref[...] = d_r
        dna_ref[...] = d_na
        dmem_ref[...] = d_mem

        @pl.when(pl.program_id(0) == 0)
        def _():
            dgg_ref[...] = jnp.zeros_like(dgg_ref)
            dgn_ref[...] = jnp.zeros_like(dgn_ref)
            dgm_ref[...] = jnp.zeros_like(dgm_ref)

        dgg_ref[...] += dgg
        dgn_ref[...] += dgn
        dgm_ref[...] += dgm

    r1024 = pl.BlockSpec((tm, 1024), lambda i: (i, 0))
    r512 = pl.BlockSpec((tm, 512), lambda i: (i, 0))
    v1024 = pl.BlockSpec((1, 1024), lambda i: (0, 0))
    v512 = pl.BlockSpec((1, 512), lambda i: (0, 0))
    in_specs = [r1024, r1024, pl.BlockSpec((tm, 1024), lambda i: (i, COL_R // 1024)), r512, r512, v1024, v512, v512,
                pl.BlockSpec((NA_W, NA_W), lambda i: (0, 0)), pl.BlockSpec((tm, D_MODEL), lambda i: (i, 0))]
    return pl.pallas_call(
        body, grid=(T // tm,), in_specs=in_specs, out_specs=[r1024, r1024, r512, r512, v1024, v512, v512],
        out_shape=[SDS((T, 1024), F32), SDS((T, 1024), F32), SDS((T, 512), F32), SDS((T, 512), F32),
                   SDS((1, 1024), F32), SDS((1, 512), F32), SDS((1, 512), F32)],
        compiler_params=_cparams(("arbitrary",)), name=name)(o_f, o_b, proj, o_na, o_mem, g_gla, g_na, g_mem, bd, dy)


NA_RB = 8


def _na_row_scores(q_ref, k_ref, v_ref, tb_ref, rb, j, n_rows):
    r = rb * NA_RB + j
    rs = jnp.clip(r - NA_ROWS // 2, 0, n_rows - NA_ROWS)
    dr0 = rs - r + (NA_ROWS - 1)
    tok = pl.ds(pl.multiple_of(rs * GRID_W, GRID_W), NA_ROWS * GRID_W)
    q = q_ref[j * GRID_W:(j + 1) * GRID_W, :]
    kk, vv = k_ref[tok, :], v_ref[tok, :]
    bias = jnp.concatenate([tb_ref[dr0 + 2 * i] for i in range(NA_ROWS // 2)], axis=1)
    s = _bdot(q, kk, NT) * (NA_HD ** -0.5) + bias
    m = jnp.max(s, axis=1, keepdims=True)
    p = jnp.exp(s - m)
    l = jnp.sum(p, axis=1, keepdims=True)
    return q, kk, vv, p, l, tok, dr0


def natten_fwd(q, k, v, tb2, name):
    H, T, hd = q.shape
    n_rows = T // GRID_W
    rbt = NA_RB * GRID_W

    def body(q_ref, k_ref, v_ref, tb_ref, o_ref):
        rb = pl.program_id(1)
        for j in range(NA_RB):
            _, _, vv, p, l, _, _ = _na_row_scores(q_ref, k_ref, v_ref, tb_ref, rb, j, n_rows)
            o_ref[j * GRID_W:(j + 1) * GRID_W, :] = _bdot(p, vv, NN) / l

    whole = pl.BlockSpec((None, T, hd), lambda h, r: (h, 0, 0))
    blk = pl.BlockSpec((None, rbt, hd), lambda h, r: (h, r, 0))
    return pl.pallas_call(
        body, grid=(H, n_rows // NA_RB),
        in_specs=[blk, whole, whole, pl.BlockSpec((None, 2 * NA_ROWS - 2, GRID_W, 2 * GRID_W), lambda h, r: (h, 0, 0, 0))],
        out_specs=blk, out_shape=SDS((H, T, hd), F32),
        compiler_params=_cparams(("parallel", "arbitrary")), name=name)(q, k, v, tb2)


def natten_bwd(q, k, v, tb2, do, name):
    H, T, hd = q.shape
    n_rows = T // GRID_W
    rbt = NA_RB * GRID_W
    scale = NA_HD ** -0.5

    def body(q_ref, k_ref, v_ref, tb_ref, do_ref, dq_ref, dk_ref, dv_ref, dtb_ref):
        rb = pl.program_id(1)

        @pl.when(rb == 0)
        def _():
            dk_ref[...] = jnp.zeros_like(dk_ref)
            dv_ref[...] = jnp.zeros_like(dv_ref)
            dtb_ref[...] = jnp.zeros_like(dtb_ref)

        for j in range(NA_RB):
            qv, kk, vv, p, l, tok, dr0 = _na_row_scores(q_ref, k_ref, v_ref, tb_ref, rb, j, n_rows)
            p = p / l
            dov = do_ref[j * GRID_W:(j + 1) * GRID_W, :]
            dp = _bdot(dov, vv, NT)
            ds = p * (dp - jnp.sum(dp * p, axis=1, keepdims=True))
            dq_ref[j * GRID_W:(j + 1) * GRID_W, :] = _bdot(ds, kk, NN) * scale
            dk_ref[tok, :] += _bdot(ds, qv, TN) * scale
            dv_ref[tok, :] += _bdot(p, dov, TN)
            for i in range(NA_ROWS // 2):
                dtb_ref[dr0 + 2 * i] += ds[:, 2 * GRID_W * i:2 * GRID_W * (i + 1)]

    whole = pl.BlockSpec((None, T, hd), lambda h, r: (h, 0, 0))
    blk = pl.BlockSpec((None, rbt, hd), lambda h, r: (h, r, 0))
    tbs = pl.BlockSpec((None, 2 * NA_ROWS - 2, GRID_W, 2 * GRID_W), lambda h, r: (h, 0, 0, 0))
    return pl.pallas_call(
        body, grid=(H, n_rows // NA_RB), in_specs=[blk, whole, whole, tbs, blk],
        out_specs=[blk, whole, whole, tbs],
        out_shape=[SDS((H, T, hd), F32), SDS((H, T, hd), F32), SDS((H, T, hd), F32), SDS(tb2.shape, F32)],
        compiler_params=_cparams(("parallel", "arbitrary")), name=name)(q, k, v, tb2, do)


def _rpb_expand_consts():
    qc = np.arange(GRID_W)[:, None]
    kc = np.arange(GRID_W)[None, :]
    cs = np.clip(qc - NA_COLS // 2, 0, GRID_W - NA_COLS)
    inside = (kc >= cs) & (kc < cs + NA_COLS)
    dc = np.clip(kc - qc, -(NA_COLS - 1), NA_COLS - 1) + (NA_COLS - 1)
    e = np.zeros((128, GRID_W * GRID_W), np.float32)
    flat = (qc * GRID_W + kc)
    e[dc[inside], flat[inside]] = 1.0
    neg = np.where(inside, 0.0, NEG).astype(np.float32).reshape(1, -1)
    return e, neg


def _rpb_fold_consts():
    sa = np.zeros((NA_H * 15, NA_H * 14), np.float32)
    sb = np.zeros((NA_H * 15, NA_H * 14), np.float32)
    for h in range(NA_H):
        for d in range(14):
            sa[h * 15 + d, h * 14 + d] = 1.0
            sb[h * 15 + d + 1, h * 14 + d] = 1.0
    return sa, sb


def rpb_table(rpb, name):
    e, neg = _rpb_expand_consts()
    rp = jnp.pad(rpb.reshape(NA_H * 15, 31), ((0, 0), (0, 128 - 31)))

    def body(r_ref, e_ref, n_ref, o_ref):
        o_ref[...] = jnp.dot(r_ref[...], e_ref[...], preferred_element_type=F32, precision=HI) + n_ref[...]

    t = pl.pallas_call(body, out_shape=SDS((NA_H * 15, GRID_W * GRID_W), F32), name=name)(rp, e, neg)
    t = t.reshape(NA_H, 15, GRID_W, GRID_W)
    return jnp.concatenate([t[:, :14], t[:, 1:]], axis=-1)


def rpb_table_bwd(dtb2, name):
    e, _ = _rpb_expand_consts()
    sa, sb = _rpb_fold_consts()
    a = dtb2[..., :GRID_W].reshape(NA_H * 14, GRID_W * GRID_W)
    b = dtb2[..., GRID_W:].reshape(NA_H * 14, GRID_W * GRID_W)

    def body(a_ref, b_ref, e_ref, sa_ref, sb_ref, o_ref):
        ev = e_ref[...]
        pa = lax.dot_general(a_ref[...], ev, (NT, ((), ())), preferred_element_type=F32, precision=HI)
        pb = lax.dot_general(b_ref[...], ev, (NT, ((), ())), preferred_element_type=F32, precision=HI)
        o_ref[...] = (jnp.dot(sa_ref[...], pa, preferred_element_type=F32, precision=HI)
                      + jnp.dot(sb_ref[...], pb, preferred_element_type=F32, precision=HI))

    d = pl.pallas_call(body, out_shape=SDS((NA_H * 15, 128), F32), name=name)(a, b, e, sa, sb)
    return d[:, :31].reshape(NA_H, 15, 31)


def _kprep_fn(kv, gk):
    return _norm_heads(kv[:, :MEM_W], MEM_HD) * gk, kv[:, MEM_W:]


def mem_kprep(kv, gk, name):
    def body(kv_ref, g_ref, k_ref, v_ref):
        kn, vv = _kprep_fn(kv_ref[...], g_ref[...])
        k_ref[...] = kn.astype(BF16)
        v_ref[...] = vv.astype(BF16)

    return pl.pallas_call(body, out_shape=[SDS((MEM_TOK, MEM_W), BF16)] * 2, name=name)(kv, gk)


def mem_kprep_bwd(kv, gk, dk, dv, name):
    def body(kv_ref, g_ref, dk_ref, dv_ref, dkv_ref, dg_ref):
        _, vjp = jax.vjp(_kprep_fn, kv_ref[...], g_ref[...])
        dkv, dg = vjp((dk_ref[...], dv_ref[...]))
        dkv_ref[...] = dkv.astype(BF16)
        dg_ref[...] = dg

    return pl.pallas_call(body, out_shape=[SDS((MEM_TOK, 2 * MEM_W), BF16), SDS((1, MEM_W), F32)], name=name)(kv, gk, dk, dv)


def _mem_probs(q_ref, k_ref, h):
    hs = slice(h * MEM_HD, (h + 1) * MEM_HD)
    qh, kh = q_ref[:, hs], k_ref[:, hs]
    s = _bdot(qh, kh, NT) * (MEM_HD ** -0.5)
    p = jnp.exp(s - jnp.max(s, axis=1, keepdims=True))
    return hs, qh, kh, p, jnp.sum(p, axis=1, keepdims=True)


def mem_attn_fwd(q, km, vm, name):
    T = q.shape[0]
    tm = _fit(T, 512, 8)

    def body(q_ref, k_ref, v_ref, o_ref):
        for h in range(MEM_H):
            hs, _, _, p, l = _mem_probs(q_ref, k_ref, h)
            o_ref[:, hs] = _bdot(p, v_ref[:, hs], NN) / l

    row = pl.BlockSpec((tm, MEM_W), lambda i: (i, 0))
    full = pl.BlockSpec((MEM_TOK, MEM_W), lambda i: (0, 0))
    return pl.pallas_call(body, grid=(T // tm,), in_specs=[row, full, full], out_specs=row, out_shape=SDS((T, MEM_W), F32),
                          compiler_params=_cparams(("parallel",)), name=name)(q, km, vm)


def mem_attn_bwd(q, km, vm, do, name):
    T = q.shape[0]
    tm = _fit(T, 512, 8)
    scale = MEM_HD ** -0.5

    def body(q_ref, k_ref, v_ref, do_ref, dq_ref, dk_ref, dv_ref):
        @pl.when(pl.program_id(0) == 0)
        def _():
            dk_ref[...] = jnp.zeros_like(dk_ref)
            dv_ref[...] = jnp.zeros_like(dv_ref)

        for h in range(MEM_H):
            hs, qh, kh, p, l = _mem_probs(q_ref, k_ref, h)
            p = p / l
            dov = do_ref[:, hs]
            dp = _bdot(dov, v_ref[:, hs], NT)
            ds = p * (dp - jnp.sum(dp * p, axis=1, keepdims=True))
            dq_ref[:, hs] = _bdot(ds, kh, NN) * scale
            dk_ref[:, hs] += _bdot(ds, qh, TN) * scale
            dv_ref[:, hs] += _bdot(p, dov, TN)

    row = pl.BlockSpec((tm, MEM_W), lambda i: (i, 0))
    full = pl.BlockSpec((MEM_TOK, MEM_W), lambda i: (0, 0))
    return pl.pallas_call(
        body, grid=(T // tm,), in_specs=[row, full, full, row], out_specs=[row, full, full],
        out_shape=[SDS((T, MEM_W), F32), SDS((MEM_TOK, MEM_W), F32), SDS((MEM_TOK, MEM_W), F32)],
        compiler_params=_cparams(("arbitrary",)), name=name)(q, km, vm, do)


SLOT_BLOCK_ELEMS = 512 * 1024


def _slot_rows(R, Cc):
    return _fit(R, max(16, SLOT_BLOCK_ELEMS // Cc // 16 * 16), 16)


def sum_slots(a, name):
    n, R, Cc = a.shape
    tr = _slot_rows(R, Cc)

    def body(a_ref, o_ref):
        s = a_ref[0].astype(F32)
        for i in range(1, n):
            s = s + a_ref[i].astype(F32)
        o_ref[...] = s

    return pl.pallas_call(body, grid=(R // tr,), in_specs=[pl.BlockSpec((n, tr, Cc), lambda i: (0, i, 0))],
                          out_specs=pl.BlockSpec((tr, Cc), lambda i: (i, 0)), out_shape=SDS((R, Cc), F32),
                          compiler_params=_cparams(("parallel",)), name=name)(a)


def pair_sum(g, land, c_idx, name):
    _, R, Cc = g.shape
    tr = _slot_rows(R, Cc)

    def body(c_ref, g_ref, l_ref, o_ref):
        o_ref[...] = (g_ref[...].astype(F32) + l_ref[...].astype(F32)).astype(o_ref.dtype)

    blk = pl.BlockSpec((None, tr, Cc), lambda k, i, c: (k, i, 0))
    gs = pltpu.PrefetchScalarGridSpec(
        num_scalar_prefetch=1, grid=(4, R // tr),
        in_specs=[pl.BlockSpec((None, tr, Cc), lambda k, i, c: (2 * k + c[0], i, 0)), blk], out_specs=blk)
    return pl.pallas_call(body, grid_spec=gs, out_shape=SDS((4, R, Cc), g.dtype),
                          compiler_params=_cparams(("parallel", "parallel")), name=name)(c_idx, g, land)


def adamw(w, g, m, v, name):
    R, Cc = w.shape
    tr = _fit(R, max(8, (262144 // max(Cc, 128)) // 8 * 8), 8)
    c1 = 1.0 - ADAM_B1 ** ADAM_STEP
    c2 = 1.0 - ADAM_B2 ** ADAM_STEP

    def body(w_ref, g_ref, m_ref, v_ref, d_ref, mo_ref, vo_ref):
        gv = g_ref[...]
        m2 = ADAM_B1 * m_ref[...] + (1.0 - ADAM_B1) * gv
        v2 = ADAM_B2 * v_ref[...] + (1.0 - ADAM_B2) * (gv * gv)
        d_ref[...] = -ADAM_LR * ((m2 / c1) / (jnp.sqrt(v2 / c2) + ADAM_EPS) + ADAM_WD * w_ref[...])
        mo_ref[...] = m2
        vo_ref[...] = v2

    blk = pl.BlockSpec((tr, Cc), lambda i: (i, 0))
    return pl.pallas_call(body, grid=(R // tr,), in_specs=[blk] * 4, out_specs=[blk] * 3, out_shape=[SDS((R, Cc), F32)] * 3,
                          compiler_params=_cparams(("parallel",)), name=name)(w, g, m, v)


ANY = pl.BlockSpec(memory_space=pl.ANY)


def _my_pos():
    return lax.axis_index("x"), lax.axis_index("y"), lax.axis_index("c")


class Rider:
    def __init__(self, inputs, out_shapes, scratch, start, finish, aliases=None):
        self.inputs, self.out_shapes, self.scratch = list(inputs), list(out_shapes), list(scratch)
        self.start, self.finish, self.aliases = start, finish, dict(aliases or {})


def _gather_phases(x_refs, out_refs, sems):
    send_sems, recv_sems, local_sems = sems
    n = len(out_refs)
    x, y, c = _my_pos()
    me, sibling = (x, y, c), (x, y, 1 - c)
    chips = [(1 - x, y), (x, 1 - y), (1 - x, 1 - y)]

    def slot(a, px, py, pc):
        return out_refs[a].at[4 * px + 2 * py + pc]

    def copy(a, k, block, to, src=None):
        return pltpu.make_async_remote_copy(
            src_ref=slot(a, *block) if src is None else src, dst_ref=slot(a, *block),
            send_sem=send_sems.at[7 * a + k], recv_sem=recv_sems.at[7 * a + k], device_id=to, device_id_type=MESH)

    def mine(a):
        return pltpu.make_async_copy(x_refs[a], slot(a, *me), local_sems.at[a])

    def spread(a):
        return [copy(a, 0, me, sibling, src=x_refs[a])] + [copy(a, 1 + j, me, (*chip, c), src=x_refs[a])
                                                           for j, chip in enumerate(chips)]

    def spread_start():
        for a in range(n):
            mine(a).start()
        for a in range(n):
            for cp in spread(a):
                cp.start()

    def spread_finish():
        for j, chip in enumerate(chips):
            for a in range(n):
                copy(a, 1 + j, (*chip, c), me).wait_recv()
        for a in range(n):
            copy(a, 0, sibling, me).wait_recv()
        for a in range(n):
            for cp in spread(a):
                cp.wait_send()
            mine(a).wait()

    def forward_start():
        for j, chip in enumerate(chips):
            for a in range(n):
                copy(a, 4 + j, (*chip, c), sibling).start()

    def forward_finish():
        for j, chip in enumerate(chips):
            for a in range(n):
                copy(a, 4 + j, (*chip, 1 - c), me).wait_recv()
        for j, chip in enumerate(chips):
            for a in range(n):
                copy(a, 4 + j, (*chip, c), sibling).wait_send()

    return spread_start, spread_finish, forward_start, forward_finish


def _gather_scratch(n):
    return [pltpu.SemaphoreType.DMA((7 * n,)), pltpu.SemaphoreType.DMA((7 * n,)), pltpu.SemaphoreType.DMA((n,))]


def all_gather(shards, name):
    n = len(shards)

    def body(*refs):
        phases = _gather_phases(refs[:n], refs[n:2 * n], refs[2 * n:])
        for phase in phases:
            phase()

    return pl.pallas_call(
        body, out_shape=[SDS((N_DEV,) + s.shape, s.dtype) for s in shards], in_specs=[ANY] * n, out_specs=[ANY] * n,
        scratch_shapes=_gather_scratch(n), name=name)(*shards)


def gather_spread_rider(shards):
    return Rider(shards, [SDS((N_DEV,) + s.shape, s.dtype) for s in shards], _gather_scratch(len(shards)),
                 lambda i, o, s: _gather_phases(i, o, s)[0](), lambda i, o, s: _gather_phases(i, o, s)[1]())


def gather_forward_rider(bufs):
    n = len(bufs)
    return Rider(bufs, [SDS(b.shape, b.dtype) for b in bufs], _gather_scratch(n),
                 lambda i, o, s: _gather_phases(None, o, s)[2](), lambda i, o, s: _gather_phases(None, o, s)[3](),
                 aliases={a: a for a in range(n)})


def pair_exchange(gs, name):
    n = len(gs)

    def body(*refs):
        g_refs, land_refs = refs[:n], refs[n:2 * n]
        send_sems, recv_sems = refs[2 * n:]
        x, y, c = _my_pos()
        sibling = (x, y, 1 - c)
        copies = [pltpu.make_async_remote_copy(
            src_ref=g_refs[a].at[2 * k + (1 - c)], dst_ref=land_refs[a].at[k], send_sem=send_sems.at[4 * a + k],
            recv_sem=recv_sems.at[4 * a + k], device_id=sibling, device_id_type=MESH) for a in range(n) for k in range(4)]
        for cp in copies:
            cp.start()
        for cp in copies:
            cp.wait_recv()
        for cp in copies:
            cp.wait_send()

    return pl.pallas_call(
        body, out_shape=[SDS((4,) + g.shape[1:], g.dtype) for g in gs], in_specs=[ANY] * n, out_specs=[ANY] * n,
        scratch_shapes=[pltpu.SemaphoreType.DMA((4 * n,)), pltpu.SemaphoreType.DMA((4 * n,))], name=name)(*gs)


def chip_exchange(ss, name):
    n = len(ss)

    def body(*refs):
        start, finish = _chip_exchange_phases(refs[:n], refs[n:2 * n], refs[2 * n:])
        start()
        finish()

    return pl.pallas_call(
        body, out_shape=[SDS(s.shape, s.dtype) for s in ss], in_specs=[ANY] * n, out_specs=[ANY] * n,
        scratch_shapes=_chip_exchange_scratch(n), name=name)(*ss)


def _chip_exchange_scratch(n):
    return [pltpu.SemaphoreType.DMA((3 * n,)), pltpu.SemaphoreType.DMA((3 * n,)), pltpu.SemaphoreType.DMA((n,))]


def _chip_exchange_phases(s_refs, land_refs, sems):
    send_sems, recv_sems, local_sems = sems
    n = len(s_refs)
    x, y, c = _my_pos()
    my_chip = 2 * x + y
    chips = [(1 - x, y), (x, 1 - y), (1 - x, 1 - y)]

    def own(a):
        return pltpu.make_async_copy(s_refs[a].at[my_chip], land_refs[a].at[my_chip], local_sems.at[a])

    def copy(a, j, src_chip, dst_chip):
        px, py = chips[j]
        return pltpu.make_async_remote_copy(
            src_ref=s_refs[a].at[src_chip], dst_ref=land_refs[a].at[dst_chip], send_sem=send_sems.at[3 * a + j],
            recv_sem=recv_sems.at[3 * a + j], device_id=(px, py, c), device_id_type=MESH)

    def start():
        for a in range(n):
            own(a).start()
            for j, (px, py) in enumerate(chips):
                copy(a, j, 2 * px + py, my_chip).start()

    def finish():
        for a in range(n):
            for j, (px, py) in enumerate(chips):
                copy(a, j, my_chip, 2 * px + py).wait_recv()
        for a in range(n):
            for j, (px, py) in enumerate(chips):
                copy(a, j, 2 * px + py, my_chip).wait_send()
            own(a).wait()

    return start, finish


def chip_exchange_rider(ss):
    return Rider(ss, [SDS(s.shape, s.dtype) for s in ss], _chip_exchange_scratch(len(ss)),
                 lambda i, o, s: _chip_exchange_phases(i, o, s)[0](), lambda i, o, s: _chip_exchange_phases(i, o, s)[1]())


W_IN_SHARD, W_IN_PACKED = 644, 768
BIG = ("ffn_w13", "ffn_w2", "w_out", "mem_wkv", "w_in")
GATE_W = ("gla_wg2_f", "gla_wg2_b")
SHARD_NAMES = BIG + GATE_W


def _permuted_ranges(c0, c1):
    res = []
    for o0, o1, p0 in ((0, ORIG_GATE0, 0), (ORIG_GATE0, ORIG_AFTER_GATE, COL_TAIL), (ORIG_AFTER_GATE, IN_COLS, ORIG_GATE0)):
        lo, hi = max(c0, o0), min(c1, o1)
        if lo < hi:
            res.append((p0 + lo - o0, p0 + hi - o0))
    return res


def assemble_w_in(blocks):
    placed = []
    for d in range(N_DEV):
        c = d * W_IN_SHARD
        for p0, p1 in _permuted_ranges(c, c + W_IN_SHARD):
            placed.append((p0, blocks[d][:, c - d * W_IN_SHARD:c - d * W_IN_SHARD + (p1 - p0)]))
            c += p1 - p0
    placed.sort(key=lambda t: t[0])
    return jnp.concatenate([t[1] for t in placed] + [jnp.zeros((blocks.shape[1], PC - IN_COLS), blocks.dtype)], axis=1)


def split_w_in_grad(g):
    pad = jnp.zeros((g.shape[0], W_IN_PACKED - W_IN_SHARD), g.dtype)
    return jnp.stack([jnp.concatenate([g[:, p0:p1] for p0, p1 in _permuted_ranges(d * W_IN_SHARD, (d + 1) * W_IN_SHARD)]
                                      + [pad], axis=1) for d in range(N_DEV)])


def wire_shards(shards):
    return [jnp.pad(shards[n], ((0, 0), (0, W_IN_PACKED - W_IN_SHARD))) if n == "w_in" else shards[n] for n in BIG]


def weights_from_gathered(gathered):
    got = dict(zip(BIG, gathered))
    full = {n: got[n].reshape(-1, got[n].shape[-1]) for n in ("ffn_w2", "w_out", "mem_wkv")}
    full["ffn_w13"] = got["ffn_w13"]
    full["w_in"] = assemble_w_in(got["w_in"])
    return full


def reduce_scatter_begin(gw, c_idx, l):
    g = [split_w_in_grad(gw[n]) if n == "w_in" else gw[n].reshape((N_DEV, -1, gw[n].shape[-1])) for n in BIG]
    land = pair_exchange(g, f"l{l}_rs_pair_exchange")
    return [pair_sum(g[i], land[i], c_idx, f"l{l}_rs_pair_sum_{n}") for i, n in enumerate(BIG)]


def reduce_scatter_end(landed, l):
    out = {n: sum_slots(landed[i], f"l{l}_rs_chip_sum_{n}") for i, n in enumerate(BIG)}
    out["w_in"] = out["w_in"][:, :W_IN_SHARD]
    return out


def gather_gate_weights(w, depth):
    mine = jnp.concatenate([w[n].reshape(-1) for n in GATE_W]).reshape(-1, 128)
    got = all_gather([mine], "gather_gate_weights")[0].reshape(N_DEV, len(GATE_W), depth, GLA_RANK, GLA_DK // N_DEV)
    return [{n: got[:, i, l].transpose(1, 0, 2).reshape(GLA_RANK, GLA_DK) for i, n in enumerate(GATE_W)} for l in range(depth)]


def pad_gate_weight(wg2, backward_dir):
    r0 = GLA_RANK if backward_dir else 0
    w = wg2.astype(F32).reshape(GLA_RANK, GLA_H, GLA_HK).transpose(1, 0, 2)
    return jnp.pad(w, ((0, 0), (r0, 128 - GLA_RANK - r0), (0, 0)))


def unpad_gate_grad(dw, backward_dir):
    r0 = GLA_RANK if backward_dir else 0
    return dw[:, r0:r0 + GLA_RANK, :].transpose(1, 0, 2).reshape(GLA_RANK, GLA_DK)


def to_heads(t):
    T = t.shape[0]
    return t.reshape(T, NA_H, NA_HD).transpose(1, 0, 2)


def from_heads(t):
    return t.transpose(1, 0, 2).reshape(t.shape[1], NA_W)


REPLICATED = ("attn_norm", "gla_bg_f", "gla_bg_b", "gla_out_norm", "na_q_norm", "na_k_norm", "na_rpb", "na_out_norm",
              "mem_norm", "mem_q_norm", "mem_k_norm", "mem_out_norm", "ffn_norm")
WEIGHTS = ("attn_norm", "w_in", "gla_wg2_f", "gla_bg_f", "gla_wg2_b", "gla_bg_b", "gla_out_norm", "na_q_norm", "na_k_norm",
           "na_rpb", "na_out_norm", "mem_norm", "mem_wkv", "mem_q_norm", "mem_k_norm", "mem_out_norm", "w_out", "ffn_norm",
           "ffn_w13", "ffn_w2")


def fold_heads(dg, n_heads, name):
    hd = dg.shape[1] // n_heads
    fold = (np.arange(dg.shape[1])[:, None] % hd == np.arange(128)[None, :]).astype(np.float32)
    out = matmul(jnp.pad(dg, ((0, 7), (0, 0))), fold, "nn", name, exact=True)
    return out[0, :hd]


def layer_fwd(x, mem_n_in, p, W, l, next_shards=None):
    tag = f"l{l}_"
    row = lambda v: v.reshape(1, -1)
    sv = {"x": x}
    sv["xn"] = rms_fwd(x, row(p["attn_norm"]), tag + "attn_rms")
    proj = matmul(sv["xn"], W["w_in"], "nn", tag + "proj", tn=768, tk=2048)
    sv["proj"] = proj
    sv["wg_f"], sv["wg_b"] = pad_gate_weight(W["gla_wg2_f"], False), pad_gate_weight(W["gla_wg2_b"], True)
    sv["bg_f"], sv["bg_b"] = p["gla_bg_f"].reshape(GLA_H, 1, GLA_HK), p["gla_bg_b"].reshape(GLA_H, 1, GLA_HK)
    sv["o_f"], sv["s_f"] = gla_fwd(proj, sv["wg_f"], sv["bg_f"], False, tag + "gla_f")
    sv["o_b"], sv["s_b"] = gla_fwd(proj, sv["wg_b"], sv["bg_b"], True, tag + "gla_b")
    sv["gq"], sv["gk"] = jnp.tile(row(p["na_q_norm"]), (1, NA_H)), jnp.tile(row(p["na_k_norm"]), (1, NA_H))
    sv["gmq"], sv["gmk"] = jnp.tile(row(p["mem_q_norm"]), (1, MEM_H)), jnp.tile(row(p["mem_k_norm"]), (1, MEM_H))
    qn, kn, vn, mqn = pre_fwd(proj, sv["gq"], sv["gk"], sv["gmq"], tag + "pre")
    sv["q_hm"], sv["k_hm"], sv["v_hm"], sv["mqn"] = to_heads(qn), to_heads(kn), to_heads(vn), mqn
    sv["tb2"] = rpb_table(p["na_rpb"], tag + "rpb_table")
    sv["o_na"] = from_heads(natten_fwd(sv["q_hm"], sv["k_hm"], sv["v_hm"], sv["tb2"], tag + "natten"))
    sv["mem_n"] = rms_fwd(mem_n_in, row(p["mem_norm"]), tag + "mem_rms")
    sv["kv"] = matmul(sv["mem_n"], W["mem_wkv"], "nn", tag + "mem_kv", tn=512, tk=2048)
    sv["km"], sv["vm"] = mem_kprep(sv["kv"], sv["gmk"], tag + "mem_kprep")
    sv["o_mem"] = mem_attn_fwd(mqn, sv["km"], sv["vm"], tag + "mem_attn")
    sv["ycat"] = post_fwd(sv["o_f"], sv["o_b"], proj, sv["o_na"], sv["o_mem"], row(p["gla_out_norm"]),
                          row(p["na_out_norm"]), row(p["mem_out_norm"]), tag + "post")
    x1 = matmul(sv["ycat"], W["w_out"], "nn", tag + "out_proj", res=x, tk=2048)
    sv["x1"] = x1
    sv["h"] = rms_fwd(x1, row(p["ffn_norm"]), tag + "ffn_rms")
    if next_shards is None:
        sv["gu"], sv["a"], _ = ffn_up_swiglu(sv["h"], W["ffn_w13"], tag + "ffn_up")
        return matmul(sv["a"], W["ffn_w2"], "nn", tag + "ffn_down", res=x1, tk=1408), sv, None
    sv["gu"], sv["a"], spread = ffn_up_swiglu(sv["h"], W["ffn_w13"], tag + "ffn_up", rider=gather_spread_rider(next_shards))
    x2, gathered = matmul(sv["a"], W["ffn_w2"], "nn", tag + "ffn_down", res=x1, tk=1408, rider=gather_forward_rider(spread))
    return x2, sv, gathered


def layer_bwd(dx2, dx2_b, mem_n_in, p, W, sv, l, rider=None):
    tag = f"l{l}_b_"
    row = lambda v: v.reshape(1, -1)
    gw, gs = {}, {}
    gw["ffn_w2"] = matmul(sv["a"], dx2_b, "tn", tag + "dw2", out_dtype=BF16, tm=1408, tn=1024, tk=1024)
    dgu = ffn_down_bwd(dx2_b, W["ffn_w2"], sv["gu"], tag + "d_swiglu")
    dh = matmul(dgu, W["ffn_w13"], "nt", tag + "d_h", a_halves=True, b_blocked=True, rider=rider)
    dh, carried = dh if rider is not None else (dh, None)
    gw["ffn_w13"] = matmul(sv["h"], dgu, "tn", tag + "dw13", out_dtype=BF16, tm=2048, tn=1408, tk=1024,
                           b_halves=True, out_blocked=True)
    dx1, dx1_b, dg = rms_bwd(sv["x1"], row(p["ffn_norm"]), dh, dx2, tag + "ffn_rms")
    gs["ffn_norm"] = dg[0]
    dycat = matmul(dx1_b, W["w_out"], "nt", tag + "d_ycat", tk=2048)
    gw["w_out"] = matmul(sv["ycat"], dx1_b, "tn", tag + "dw_out", out_dtype=BF16, tm=2048, tn=1024, tk=1024)
    d_o, d_r, d_ona, d_omem, dgg, dgn, dgm = post_bwd(
        sv["o_f"], sv["o_b"], sv["proj"], sv["o_na"], sv["o_mem"], row(p["gla_out_norm"]), row(p["na_out_norm"]),
        row(p["mem_out_norm"]), dycat, tag + "post")
    gs["gla_out_norm"], gs["na_out_norm"], gs["mem_out_norm"] = dgg[0], dgn[0], dgm[0]
    dq_f, dk_f, dv_f, dt_f, dwg_f, dbg_f = gla_bwd(sv["proj"], sv["wg_f"], sv["bg_f"], sv["s_f"], d_o, False, tag + "gla_f")
    dq_b, dk_b, dv_b, dt_b, dwg_b, dbg_b = gla_bwd(sv["proj"], sv["wg_b"], sv["bg_b"], sv["s_b"], d_o, True, tag + "gla_b")
    gs["gla_wg2_f"], gs["gla_wg2_b"] = unpad_gate_grad(dwg_f, False), unpad_gate_grad(dwg_b, True)
    gs["gla_bg_f"], gs["gla_bg_b"] = dbg_f.reshape(-1), dbg_b.reshape(-1)
    dq_hm, dk_hm, dv_hm, dtb2 = natten_bwd(sv["q_hm"], sv["k_hm"], sv["v_hm"], sv["tb2"], to_heads(d_ona), tag + "natten")
    gs["na_rpb"] = rpb_table_bwd(dtb2, tag + "rpb_table")
    d_mqn, dkm, dvm = mem_attn_bwd(sv["mqn"], sv["km"], sv["vm"], d_omem, tag + "mem_attn")
    dkv, dgmk = mem_kprep_bwd(sv["kv"], sv["gmk"], dkm, dvm, tag + "mem_kprep")
    gs["mem_k_norm"] = fold_heads(dgmk, MEM_H, tag + "fold_mk")
    gw["mem_wkv"] = matmul(sv["mem_n"], dkv, "tn", tag + "dw_kv", out_dtype=BF16, tm=2048, tn=1024, tk=256)
    d_memn = matmul(dkv, W["mem_wkv"], "nt", tag + "d_memn", tk=1024)
    _, _, dg = rms_bwd(mem_n_in, row(p["mem_norm"]), d_memn, None, tag + "mem_rms")
    gs["mem_norm"] = dg[0]
    dproj, dgq, dgk, dgmq = pre_bwd(sv["proj"], sv["gq"], sv["gk"], sv["gmq"], from_heads(dq_hm), from_heads(dk_hm),
                                    from_heads(dv_hm), d_mqn, dq_f, dq_b, dk_f, dk_b, dv_f, dv_b, d_r, dt_f, dt_b, tag + "pre")
    gs["na_q_norm"] = fold_heads(dgq, NA_H, tag + "fold_q")
    gs["na_k_norm"] = fold_heads(dgk, NA_H, tag + "fold_k")
    gs["mem_q_norm"] = fold_heads(dgmq, MEM_H, tag + "fold_mq")
    dxn = matmul(dproj, W["w_in"], "nt", tag + "d_xn", tk=768)
    gw["w_in"] = matmul(sv["xn"], dproj, "tn", tag + "dw_in", out_dtype=BF16, tm=2048, tn=768, tk=1024)
    dx, dx_b, dg = rms_bwd(sv["x"], row(p["attn_norm"]), dxn, dx1, tag + "attn_rms")
    gs["attn_norm"] = dg[0]
    return dx, dx_b, gw, gs, carried


def kernel(x, mem, attn_norm, w_in, gla_wg2_f, gla_bg_f, gla_wg2_b, gla_bg_b, gla_out_norm, na_q_norm, na_k_norm, na_rpb, na_out_norm, mem_norm, mem_wkv, mem_q_norm, mem_k_norm, mem_out_norm, w_out, ffn_norm, ffn_w13, ffn_w2, loss_target, m_attn_norm, m_w_in, m_gla_wg2_f, m_gla_bg_f, m_gla_wg2_b, m_gla_bg_b, m_gla_out_norm, m_na_q_norm, m_na_k_norm, m_na_rpb, m_na_out_norm, m_mem_norm, m_mem_wkv, m_mem_q_norm, m_mem_k_norm, m_mem_out_norm, m_w_out, m_ffn_norm, m_ffn_w13, m_ffn_w2, v_attn_norm, v_w_in, v_gla_wg2_f, v_gla_bg_f, v_gla_wg2_b, v_gla_bg_b, v_gla_out_norm, v_na_q_norm, v_na_k_norm, v_na_rpb, v_na_out_norm, v_mem_norm, v_mem_wkv, v_mem_q_norm, v_mem_k_norm, v_mem_out_norm, v_w_out, v_ffn_norm, v_ffn_w13, v_ffn_w2):
    w = dict(attn_norm=attn_norm, w_in=w_in, gla_wg2_f=gla_wg2_f, gla_bg_f=gla_bg_f, gla_wg2_b=gla_wg2_b, gla_bg_b=gla_bg_b,
             gla_out_norm=gla_out_norm, na_q_norm=na_q_norm, na_k_norm=na_k_norm, na_rpb=na_rpb, na_out_norm=na_out_norm,
             mem_norm=mem_norm, mem_wkv=mem_wkv, mem_q_norm=mem_q_norm, mem_k_norm=mem_k_norm, mem_out_norm=mem_out_norm,
             w_out=w_out, ffn_norm=ffn_norm, ffn_w13=ffn_w13, ffn_w2=ffn_w2)
    mom = dict(attn_norm=m_attn_norm, w_in=m_w_in, gla_wg2_f=m_gla_wg2_f, gla_bg_f=m_gla_bg_f, gla_wg2_b=m_gla_wg2_b,
               gla_bg_b=m_gla_bg_b, gla_out_norm=m_gla_out_norm, na_q_norm=m_na_q_norm, na_k_norm=m_na_k_norm, na_rpb=m_na_rpb,
               na_out_norm=m_na_out_norm, mem_norm=m_mem_norm, mem_wkv=m_mem_wkv, mem_q_norm=m_mem_q_norm,
               mem_k_norm=m_mem_k_norm, mem_out_norm=m_mem_out_norm, w_out=m_w_out, ffn_norm=m_ffn_norm, ffn_w13=m_ffn_w13,
               ffn_w2=m_ffn_w2)
    var = dict(attn_norm=v_attn_norm, w_in=v_w_in, gla_wg2_f=v_gla_wg2_f, gla_bg_f=v_gla_bg_f, gla_wg2_b=v_gla_wg2_b,
               gla_bg_b=v_gla_bg_b, gla_out_norm=v_gla_out_norm, na_q_norm=v_na_q_norm, na_k_norm=v_na_k_norm, na_rpb=v_na_rpb,
               na_out_norm=v_na_out_norm, mem_norm=v_mem_norm, mem_wkv=v_mem_wkv, mem_q_norm=v_mem_q_norm,
               mem_k_norm=v_mem_k_norm, mem_out_norm=v_mem_out_norm, w_out=v_w_out, ffn_norm=v_ffn_norm, ffn_w13=v_ffn_w13,
               ffn_w2=v_ffn_w2)
    depth = attn_norm.shape[0]
    T = x.shape[1]
    xs, mem0, tgt = x.reshape(T, D_MODEL), mem.reshape(MEM_TOK, D_MODEL), loss_target.reshape(T, D_MODEL)
    c_idx = lax.axis_index("c").astype(jnp.int32).reshape(1)

    gates = gather_gate_weights(w, depth)
    send = [wire_shards({n: w[n][l].astype(BF16) for n in BIG}) for l in range(depth)]
    P = [{n: w[n][l] for n in REPLICATED} for l in range(depth)]

    W = [{**weights_from_gathered(all_gather(send[0], "l0_gather_weights")), **gates[0]}]
    saved = []
    h = xs
    for l in range(depth):
        h, sv, gathered = layer_fwd(h, mem0, P[l], W[l], l, send[l + 1] if l + 1 < depth else None)
        saved.append(sv)
        if gathered is not None:
            W.append({**weights_from_gathered(gathered), **gates[l + 1]})
    dy, dy_b, lsum = loss_bwd(h, tgt, "loss")
    loss = lax.psum(lsum[0, 0], ("x", "y", "c")) * (0.5 / D_MODEL)

    g_shard, g_small = [None] * depth, [None] * depth
    chip_sums = None
    for l in range(depth - 1, -1, -1):
        rider = chip_exchange_rider(chip_sums) if chip_sums is not None else None
        dy, dy_b, gw, g_small[l], landed = layer_bwd(dy, dy_b, mem0, P[l], W[l], saved[l], l, rider)
        if landed is not None:
            g_shard[l + 1] = reduce_scatter_end(landed, l + 1)
        chip_sums = reduce_scatter_begin(gw, c_idx, l)
        saved[l] = None
    g_shard[0] = reduce_scatter_end(chip_exchange(chip_sums, "l0_rs_chip_exchange"), 0)
    grad_x = dy.reshape(x.shape)

    small_names = REPLICATED + GATE_W
    small = jnp.concatenate([g_small[l][n].reshape(-1) for l in range(depth) for n in small_names])
    n_small = small.shape[0]
    rows = -(-n_small // 1024) * 8
    small = jnp.pad(small, (0, rows * 128 - n_small)).reshape(rows, 128)
    small = sum_slots(all_gather([small], "gather_small_grads")[0], "sum_small_grads").reshape(-1)
    grads, off = {}, 0
    per_layer = {n: [] for n in small_names}
    my_cols = (4 * lax.axis_index("x") + 2 * lax.axis_index("y") + lax.axis_index("c")) * (GLA_DK // N_DEV)
    for l in range(depth):
        for n in small_names:
            shp = (GLA_RANK, GLA_DK) if n in GATE_W else w[n].shape[1:]
            g = small[off:off + int(np.prod(shp))].reshape(shp)
            off += int(np.prod(shp))
            per_layer[n].append(lax.dynamic_slice_in_dim(g, my_cols, GLA_DK // N_DEV, axis=1) if n in GATE_W else g)
    for n in small_names:
        grads[n] = jnp.stack(per_layer[n])
    for n in BIG:
        grads[n] = jnp.stack([g_shard[l][n] for l in range(depth)])

    delta, new_m, new_v = {}, {}, {}
    for n in WEIGHTS:
        shp = w[n].shape
        two_d = (shp[0], int(np.prod(shp[1:]))) if n in REPLICATED else (int(np.prod(shp[:-1])), shp[-1])
        d_, m_, v_ = adamw(w[n].reshape(two_d), grads[n].reshape(two_d), mom[n].reshape(two_d), var[n].reshape(two_d),
                           "adamw_" + n)
        delta[n], new_m[n], new_v[n] = d_.reshape(shp), m_.reshape(shp), v_.reshape(shp)

    return (loss, grad_x, *[grads[n] for n in WEIGHTS], *[delta[n] for n in WEIGHTS], *[new_m[n] for n in WEIGHTS],
            *[new_v[n] for n in WEIGHTS])
```

```python
import functools

import numpy as np
import jax
import jax.numpy as jnp
from jax import lax
from jax.experimental import pallas as pl
from jax.experimental.pallas import tpu as pltpu

F32, BF16 = jnp.float32, jnp.bfloat16
HI = lax.Precision.HIGHEST
SDS = jax.ShapeDtypeStruct
MESH = pl.DeviceIdType.MESH

D_MODEL = 2048
GRID_W = 64
GLA_H, GLA_HK, GLA_HV, GLA_RANK, GLA_TAU, GLA_C = 4, 128, 256, 16, 16.0, 64
GLA_DK, GLA_DV = GLA_H * GLA_HK, GLA_H * GLA_HV
NA_H, NA_HD, NA_ROWS, NA_COLS = 8, 64, 8, 16
NA_W = NA_H * NA_HD
MEM_H, MEM_HD, MEM_TOK = 4, 128, 256
MEM_W = MEM_H * MEM_HD
D_FF = 5632
IN_COLS = 5152
RMS_EPS = 1e-6
ADAM_LR, ADAM_B1, ADAM_B2, ADAM_EPS, ADAM_WD, ADAM_STEP = 0.001, 0.9, 0.999, 1e-08, 0.01, 10
N_DEV = 8

PC = 5376
COL_R, COL_NQ, COL_NK, COL_NV, COL_MQ, COL_TAIL = 2048, 3072, 3584, 4096, 4608, 5120
ORIG_GATE0 = 3072
ORIG_AFTER_GATE = 3104

GLA_G = 8
VMEM_LIMIT = 56 * 1024 * 1024
NEG = -1e30


def _cparams(sem):
    return pltpu.CompilerParams(dimension_semantics=sem, vmem_limit_bytes=VMEM_LIMIT)


def _fit(n, pref, unit=128):
    if n <= pref:
        return n
    t = (pref // unit) * unit
    while t >= unit:
        if n % t == 0:
            return t
        t -= unit
    return n


def _bdot(a, b, dims):
    return lax.dot_general(a.astype(BF16), b.astype(BF16), (dims, ((), ())), preferred_element_type=F32)


NN, NT, TN = ((1,), (0,)), ((1,), (1,)), ((0,), (0,))


@functools.partial(jax.custom_vjp, nondiff_argnums=(2,))
def _bdot_vjp(a, b, dims):
    return _bdot(a, b, dims)


def _bdot_vjp_fwd(a, b, dims):
    return _bdot(a, b, dims), (a, b)


def _bdot_vjp_bwd(dims, res, ct):
    a, b = res
    if dims == NN:
        return _bdot(ct, b, NT), _bdot(a, ct, TN)
    if dims == NT:
        return _bdot(ct, b, NN), _bdot(ct, a, TN)
    return _bdot(b, ct, NT), _bdot(a, ct, NN)


_bdot_vjp.defvjp(_bdot_vjp_fwd, _bdot_vjp_bwd)


def _split_dot(c, x, dims):
    hi = x.astype(BF16)
    lo = (x - hi.astype(F32)).astype(BF16)
    cb = c.astype(BF16)
    return (lax.dot_general(cb, hi, (dims, ((), ())), preferred_element_type=F32)
            + lax.dot_general(cb, lo, (dims, ((), ())), preferred_element_type=F32))


@jax.custom_vjp
def _tri_sum(cmat, x):
    return _split_dot(cmat, x, NN)


def _tri_sum_fwd(cmat, x):
    return _split_dot(cmat, x, NN), cmat


def _tri_sum_bwd(cmat, ct):
    return jnp.zeros_like(cmat), _split_dot(cmat, ct, TN)


_tri_sum.defvjp(_tri_sum_fwd, _tri_sum_bwd)


def _call_with_rider(body, grid, in_specs, out_specs, out_shape, scratch, sem, name, args, rider):
    if rider is None:
        return pl.pallas_call(body, grid=grid, in_specs=in_specs, out_specs=out_specs, out_shape=out_shape,
                              scratch_shapes=scratch, compiler_params=_cparams(sem), name=name)(*args), None
    outs_l = list(out_shape) if isinstance(out_shape, (list, tuple)) else [out_shape]
    specs_l = list(out_specs) if isinstance(out_specs, (list, tuple)) else [out_specs]
    ni, no, ns = len(in_specs), len(outs_l), len(scratch)
    nri, nro = len(rider.inputs), len(rider.out_shapes)
    any_spec = pl.BlockSpec(memory_space=pl.ANY)

    def wrapped(*refs):
        ins, rin = refs[:ni], refs[ni:ni + nri]
        o0 = ni + nri
        outs, rout = refs[o0:o0 + no], refs[o0 + no:o0 + no + nro]
        s0 = o0 + no + nro
        scr, rsem = refs[s0:s0 + ns], refs[s0 + ns:]
        ids = [pl.program_id(d) for d in range(len(grid))]
        first = functools.reduce(jnp.logical_and, [i == 0 for i in ids])
        last = functools.reduce(jnp.logical_and, [i == g - 1 for i, g in zip(ids, grid)])

        @pl.when(first)
        def _():
            rider.start(rin, rout, rsem)

        body(*ins, *outs, *scr)

        @pl.when(last)
        def _():
            rider.finish(rin, rout, rsem)

    res = pl.pallas_call(
        wrapped, grid=grid, in_specs=list(in_specs) + [any_spec] * nri, out_specs=specs_l + [any_spec] * nro,
        out_shape=outs_l + rider.out_shapes, scratch_shapes=list(scratch) + rider.scratch,
        input_output_aliases={ni + i: no + j for i, j in rider.aliases.items()},
        compiler_params=_cparams(("arbitrary",) * len(grid)), name=name)(*args, *rider.inputs)
    main = res[:no]
    return (main if isinstance(out_shape, (list, tuple)) else main[0]), list(res[no:])


def matmul(a, b, mode, name, out_dtype=F32, res=None, tm=1024, tn=1024, tk=512, exact=False,
           a_halves=False, b_halves=False, b_blocked=False, out_blocked=False, rider=None):
    if mode == "nn":
        M, K = a.shape
        N = b.shape[0] * b.shape[2] if b_blocked else b.shape[1]
        if b_blocked:
            tn = b.shape[2]
    elif mode == "nt":
        M, K = (a.shape[1], 2 * a.shape[2]) if a_halves else a.shape
        N = b.shape[1] if b_blocked else b.shape[0]
        if b_blocked:
            tk = b.shape[2]
    else:
        K, M = a.shape
        N = 2 * b.shape[2] if b_halves else b.shape[1]
    tm, tn, tk = _fit(M, tm, 8 if mode != "tn" else 128), _fit(N, tn), _fit(K, tk, 128 if mode != "tn" else 16)
    nk = K // tk
    dims = {"nn": NN, "nt": NT, "tn": TN}[mode]

    def body(*refs):
        if res is None:
            a_ref, b_ref, o_ref = refs[:3]
            r_ref = None
            acc = refs[3] if nk > 1 else None
        else:
            a_ref, b_ref, r_ref, o_ref = refs[:4]
            acc = refs[4] if nk > 1 else None
        def product():
            if exact:
                return lax.dot_general(a_ref[...], b_ref[...], (dims, ((), ())), preferred_element_type=F32, precision=HI)
            return _bdot(a_ref[...], b_ref[...], dims)

        def finish(val):
            if r_ref is not None:
                val = val + r_ref[...]
            o_ref[...] = val.astype(out_dtype)

        if nk == 1:
            finish(product())
        else:
            kk = pl.program_id(2)

            @pl.when(kk == 0)
            def _():
                acc[...] = jnp.zeros_like(acc)

            acc[...] += product()

            @pl.when(kk == nk - 1)
            def _():
                finish(acc[...])

    if mode == "tn":
        a_spec = pl.BlockSpec((tk, tm), lambda i, j, k: (k, i))
    elif a_halves:
        nh = K // 2 // tk
        a_spec = pl.BlockSpec((None, tm, tk), lambda i, j, k: (k // nh, i, k % nh))
    else:
        a_spec = pl.BlockSpec((tm, tk), lambda i, j, k: (i, k))
    if b_blocked:
        b_spec = (pl.BlockSpec((None, tk, tn), lambda i, j, k: (j, k, 0)) if mode == "nn"
                  else pl.BlockSpec((None, tn, tk), lambda i, j, k: (k, j, 0)))
    elif b_halves:
        nh = N // 2 // tn
        b_spec = pl.BlockSpec((None, tk, tn), lambda i, j, k: (j // nh, k, j % nh))
    elif mode == "nt":
        b_spec = pl.BlockSpec((tn, tk), lambda i, j, k: (j, k))
    else:
        b_spec = pl.BlockSpec((tk, tn), lambda i, j, k: (k, j))
    if out_blocked:
        o_spec, o_shape = pl.BlockSpec((None, tm, tn), lambda i, j, k: (j, i, 0)), SDS((N // tn, M, tn), out_dtype)
    else:
        o_spec, o_shape = pl.BlockSpec((tm, tn), lambda i, j, k: (i, j)), SDS((M, N), out_dtype)
    in_specs, args = [a_spec, b_spec], [a, b]
    if res is not None:
        in_specs.append(o_spec)
        args.append(res)
    out, carried = _call_with_rider(
        body, (M // tm, N // tn, nk), in_specs, o_spec, o_shape, [pltpu.VMEM((tm, tn), F32)] if nk > 1 else [],
        ("parallel", "parallel", "arbitrary"), name, args, rider)
    return out if rider is None else (out, carried)


def rms_fwd(x, g, name, rider=None):
    T, D = x.shape
    tm = _fit(T, 512, 8)

    def body(x_ref, g_ref, o_ref):
        xv = x_ref[...]
        r = lax.rsqrt(jnp.mean(xv * xv, axis=-1, keepdims=True) + RMS_EPS)
        o_ref[...] = (xv * r * g_ref[...]).astype(BF16)

    out, carried = _call_with_rider(
        body, (T // tm,), [pl.BlockSpec((tm, D), lambda i: (i, 0)), pl.BlockSpec((1, D), lambda i: (0, 0))],
        pl.BlockSpec((tm, D), lambda i: (i, 0)), SDS((T, D), BF16), [], ("parallel",), name, [x, g], rider)
    return out if rider is None else (out, carried)


def rms_bwd(x, g, dy, dres, name):
    T, D = x.shape
    tm = _fit(T, 256, 8)
    has_res = dres is not None

    def body(*refs):
        if has_res:
            x_ref, g_ref, dy_ref, dres_ref, dx_ref, dxb_ref, dg_ref = refs
        else:
            x_ref, g_ref, dy_ref, dx_ref, dxb_ref, dg_ref = refs
        xv, dyv = x_ref[...], dy_ref[...]
        r = lax.rsqrt(jnp.mean(xv * xv, axis=-1, keepdims=True) + RMS_EPS)
        xh = xv * r
        dxh = dyv * g_ref[...]
        dx = r * (dxh - xh * jnp.mean(dxh * xh, axis=-1, keepdims=True))
        if has_res:
            dx = dx + dres_ref[...]
        dx_ref[...] = dx
        dxb_ref[...] = dx.astype(BF16)

        @pl.when(pl.program_id(0) == 0)
        def _():
            dg_ref[...] = jnp.zeros_like(dg_ref)

        dg_ref[...] += jnp.sum(dyv * xh, axis=0, keepdims=True)

    row = pl.BlockSpec((tm, D), lambda i: (i, 0))
    vec = pl.BlockSpec((1, D), lambda i: (0, 0))
    args = [x, g, dy] + ([dres] if has_res else [])
    return pl.pallas_call(
        body, grid=(T // tm,), in_specs=[row, vec, row] + ([row] if has_res else []),
        out_specs=[row, row, vec], out_shape=[SDS((T, D), F32), SDS((T, D), BF16), SDS((1, D), F32)],
        compiler_params=_cparams(("arbitrary",)), name=name)(*args)


def ffn_up_swiglu(h, w13b, name, rider=None):
    T, D = h.shape
    nb, _, tb = w13b.shape
    nh = nb // 2
    tm = _fit(T, 512, 8)

    def body(h_ref, wg_ref, wu_ref, gu_ref, a_ref):
        hv = h_ref[...]
        gv = _bdot(hv, wg_ref[...], NN)
        uv = _bdot(hv, wu_ref[...], NN)
        gu_ref[0] = gv
        gu_ref[1] = uv
        a_ref[...] = (gv * jax.nn.sigmoid(gv) * uv).astype(BF16)

    (gu, act), carried = _call_with_rider(
        body, (nh, T // tm),
        [pl.BlockSpec((tm, D), lambda j, i: (i, 0)), pl.BlockSpec((None, D, tb), lambda j, i: (j, 0, 0)),
         pl.BlockSpec((None, D, tb), lambda j, i: (j + nh, 0, 0))],
        [pl.BlockSpec((2, tm, tb), lambda j, i: (0, i, j)), pl.BlockSpec((tm, tb), lambda j, i: (i, j))],
        [SDS((2, T, nh * tb), F32), SDS((T, nh * tb), BF16)], [], ("parallel", "parallel"), name, [h, w13b, w13b], rider)
    return gu, act, carried


def ffn_down_bwd(dy_b, w2, gu, name):
    T, D = dy_b.shape
    Fh = w2.shape[0]
    tm, tf = _fit(T, 1024, 8), _fit(Fh, 512)
    nf = Fh // tf

    def body(dy_ref, w_ref, g_ref, u_ref, o_ref):
        dav = _bdot(dy_ref[...], w_ref[...], NT)
        gv, uv = g_ref[...], u_ref[...]
        sg = jax.nn.sigmoid(gv)
        o_ref[0] = (dav * uv * (sg * (1.0 + gv * (1.0 - sg)))).astype(BF16)
        o_ref[1] = (dav * gv * sg).astype(BF16)

    return pl.pallas_call(
        body, grid=(T // tm, nf),
        in_specs=[pl.BlockSpec((tm, D), lambda i, j: (i, 0)), pl.BlockSpec((tf, D), lambda i, j: (j, 0)),
                  pl.BlockSpec((None, tm, tf), lambda i, j: (0, i, j)), pl.BlockSpec((None, tm, tf), lambda i, j: (1, i, j))],
        out_specs=pl.BlockSpec((2, tm, tf), lambda i, j: (0, i, j)), out_shape=SDS((2, T, Fh), BF16),
        compiler_params=_cparams(("parallel", "parallel")), name=name)(dy_b, w2, gu, gu)


def loss_bwd(y, tgt, name):
    T, D = y.shape
    tm = _fit(T, 512, 8)

    def body(y_ref, t_ref, dy_ref, dyb_ref, l_ref):
        e = y_ref[...] - t_ref[...]
        dy = e * (1.0 / D)
        dy_ref[...] = dy
        dyb_ref[...] = dy.astype(BF16)

        @pl.when(pl.program_id(0) == 0)
        def _():
            l_ref[...] = jnp.zeros_like(l_ref)

        l_ref[...] += jnp.sum(jnp.sum(e * e, axis=1, keepdims=True), axis=0, keepdims=True)

    row = pl.BlockSpec((tm, D), lambda i: (i, 0))
    return pl.pallas_call(
        body, grid=(T // tm,), in_specs=[row, row], out_specs=[row, row, pl.BlockSpec((8, 128), lambda i: (0, 0))],
        out_shape=[SDS((T, D), F32), SDS((T, D), BF16), SDS((8, 128), F32)],
        compiler_params=_cparams(("arbitrary",)), name=name)(y, tgt)


def _log_sigmoid(z):
    return jnp.minimum(z, 0.0) - jnp.log(1.0 + jnp.exp(-jnp.abs(z)))


def _gla_group(q, k, v, tail, wg, bg, s, cmat, mask, backward_dir):
    n = q.shape[0] // GLA_C
    z = _bdot_vjp(tail, wg, NN) + bg
    la = _log_sigmoid(z) * (1.0 / GLA_TAU)
    cum = _tri_sum(cmat, la)
    chunks = [slice(g * GLA_C, (g + 1) * GLA_C) for g in range(n)]
    last = [jnp.sum(la[sl], axis=0, keepdims=True) for sl in chunks]
    last_rows = jnp.concatenate([jnp.broadcast_to(t, (GLA_C, t.shape[1])) for t in last], axis=0)
    q_e = q * jnp.exp(cum) * (GLA_HK ** -0.5)
    k_e = k * jnp.exp(-cum)
    k_end = k * jnp.exp(last_rows - cum)
    sc = jnp.where(mask > 0.5, _bdot_vjp(q_e, k_e, NT), 0.0)
    o = _bdot_vjp(sc, v, NN)
    o_inter = [None] * n
    for g in (range(n - 1, -1, -1) if backward_dir else range(n)):
        o_inter[g] = _bdot_vjp(q_e[chunks[g]], s, NN)
        s = s * jnp.transpose(jnp.exp(last[g])) + _bdot_vjp(k_end[chunks[g]], v[chunks[g]], TN)
    return o + jnp.concatenate(o_inter, axis=0), s


def _gla_consts(backward_dir, GC):
    i = np.arange(GC)
    same = (i[:, None] // GLA_C) == (i[None, :] // GLA_C)
    if backward_dir:
        return (same & (i[None, :] >= i[:, None])).astype(np.float32), (same & (i[None, :] > i[:, None])).astype(np.float32)
    tri = (same & (i[None, :] <= i[:, None])).astype(np.float32)
    return tri, tri


GLA_HB = 2


def _gla_in_specs(GC, nmap):
    return [
        pl.BlockSpec((GC, GLA_HB * GLA_HK), lambda h, n: (nmap(n), h)),
        pl.BlockSpec((GC, GLA_HB * GLA_HK), lambda h, n: (nmap(n), GLA_H // GLA_HB + h)),
        pl.BlockSpec((GC, GLA_HB * GLA_HV), lambda h, n: (nmap(n), 2 * GLA_DK // (GLA_HB * GLA_HV) + h)),
        pl.BlockSpec((GC, 128), lambda h, n: (nmap(n), COL_TAIL // 128)),
        pl.BlockSpec((GLA_HB, 128, 128), lambda h, n: (h, 0, 0)),
        pl.BlockSpec((GLA_HB, 1, 128), lambda h, n: (h, 0, 0)),
        pl.BlockSpec((GC, GC), lambda h, n: (0, 0)),
        pl.BlockSpec((GC, GC), lambda h, n: (0, 0)),
    ]


def _head_cols(ref, hh, width):
    return ref[:, hh * width:(hh + 1) * width]


def gla_fwd(proj, wgpad, bg, backward_dir, name):
    T = proj.shape[0]
    GC = min(GLA_G * GLA_C, T)
    NG = T // GC
    cmat, mask = _gla_consts(backward_dir, GC)
    nmap = (lambda n: NG - 1 - n) if backward_dir else (lambda n: n)

    def body(q_ref, k_ref, v_ref, t_ref, wg_ref, bg_ref, c_ref, m_ref, o_ref, ss_ref, s_scr):
        @pl.when(pl.program_id(1) == 0)
        def _():
            s_scr[...] = jnp.zeros_like(s_scr)

        tail, cm, mk = t_ref[...], c_ref[...], m_ref[...]
        for hh in range(GLA_HB):
            s0 = s_scr[hh]
            ss_ref[hh] = s0
            o, s1 = _gla_group(_head_cols(q_ref, hh, GLA_HK), _head_cols(k_ref, hh, GLA_HK), _head_cols(v_ref, hh, GLA_HV),
                               tail, wg_ref[hh], bg_ref[hh], s0, cm, mk, backward_dir)
            o_ref[:, hh * GLA_HV:(hh + 1) * GLA_HV] = o
            s_scr[hh] = s1

    return pl.pallas_call(
        body, grid=(GLA_H // GLA_HB, NG), in_specs=_gla_in_specs(GC, nmap),
        out_specs=[pl.BlockSpec((GC, GLA_HB * GLA_HV), lambda h, n: (nmap(n), h)),
                   pl.BlockSpec((GLA_HB, None, GLA_HK, GLA_HV), lambda h, n: (h, nmap(n), 0, 0))],
        out_shape=[SDS((T, GLA_DV), F32), SDS((GLA_H, NG, GLA_HK, GLA_HV), F32)],
        scratch_shapes=[pltpu.VMEM((GLA_HB, GLA_HK, GLA_HV), F32)],
        compiler_params=_cparams(("parallel", "arbitrary")), name=name)(proj, proj, proj, proj, wgpad, bg, cmat, mask)


def gla_bwd(proj, wgpad, bg, ssave, do, backward_dir, name):
    T = proj.shape[0]
    GC = min(GLA_G * GLA_C, T)
    NG = T // GC
    cmat, mask = _gla_consts(backward_dir, GC)
    nmap = (lambda n: n) if backward_dir else (lambda n: NG - 1 - n)

    def body(q_ref, k_ref, v_ref, t_ref, wg_ref, bg_ref, c_ref, m_ref, ss_ref, do_ref,
             dq_ref, dk_ref, dv_ref, dt_ref, dwg_ref, dbg_ref, ds_scr):
        @pl.when(pl.program_id(1) == 0)
        def _():
            ds_scr[...] = jnp.zeros_like(ds_scr)
            dwg_ref[...] = jnp.zeros_like(dwg_ref)
            dbg_ref[...] = jnp.zeros_like(dbg_ref)

        tail, cm, mk = t_ref[...], c_ref[...], m_ref[...]
        fn = lambda q, k, v, t, wg, b, s: _gla_group(q, k, v, t, wg, b, s, cm, mk, backward_dir)
        for hh in range(GLA_HB):
            _, vjp = jax.vjp(fn, _head_cols(q_ref, hh, GLA_HK), _head_cols(k_ref, hh, GLA_HK), _head_cols(v_ref, hh, GLA_HV),
                             tail, wg_ref[hh], bg_ref[hh], ss_ref[hh])
            dq, dk, dv, dt, dwg, dbg, ds = vjp((_head_cols(do_ref, hh, GLA_HV), ds_scr[hh]))
            dq_ref[:, hh * GLA_HK:(hh + 1) * GLA_HK] = dq
            dk_ref[:, hh * GLA_HK:(hh + 1) * GLA_HK] = dk
            dv_ref[:, hh * GLA_HV:(hh + 1) * GLA_HV] = dv
            dt_ref[hh] = dt
            dwg_ref[hh] += dwg
            dbg_ref[hh] += dbg
            ds_scr[hh] = ds

    in_specs = _gla_in_specs(GC, nmap) + [
        pl.BlockSpec((GLA_HB, None, GLA_HK, GLA_HV), lambda h, n: (h, nmap(n), 0, 0)),
        pl.BlockSpec((GC, GLA_HB * GLA_HV), lambda h, n: (nmap(n), h)),
    ]
    out_specs = [
        pl.BlockSpec((GC, GLA_HB * GLA_HK), lambda h, n: (nmap(n), h)),
        pl.BlockSpec((GC, GLA_HB * GLA_HK), lambda h, n: (nmap(n), h)),
        pl.BlockSpec((GC, GLA_HB * GLA_HV), lambda h, n: (nmap(n), h)),
        pl.BlockSpec((GLA_HB, GC, 128), lambda h, n: (h, nmap(n), 0)),
        pl.BlockSpec((GLA_HB, 128, 128), lambda h, n: (h, 0, 0)),
        pl.BlockSpec((GLA_HB, 1, 128), lambda h, n: (h, 0, 0)),
    ]
    out_shape = [SDS((T, GLA_DK), F32), SDS((T, GLA_DK), F32), SDS((T, GLA_DV), F32), SDS((GLA_H, T, 128), F32),
                 SDS((GLA_H, 128, 128), F32), SDS((GLA_H, 1, 128), F32)]
    return pl.pallas_call(
        body, grid=(GLA_H // GLA_HB, NG), in_specs=in_specs, out_specs=out_specs, out_shape=out_shape,
        scratch_shapes=[pltpu.VMEM((GLA_HB, GLA_HK, GLA_HV), F32)],
        compiler_params=_cparams(("parallel", "arbitrary")), name=name)(proj, proj, proj, proj, wgpad, bg, cmat, mask, ssave, do)


def _block_diag(width, hd):
    i = np.arange(width)
    return ((i[:, None] // hd) == (i[None, :] // hd)).astype(np.float32) / hd


def _norm_heads(t, hd):
    outs = []
    for h in range(t.shape[1] // hd):
        th = t[:, h * hd:(h + 1) * hd]
        outs.append(th * lax.rsqrt(jnp.mean(th * th, axis=-1, keepdims=True) + RMS_EPS))
    return jnp.concatenate(outs, axis=1)


def _norm_bd(t, bd):
    return t * lax.rsqrt(jnp.dot(t * t, bd, preferred_element_type=F32, precision=HI) + RMS_EPS)


def _pre_fn(nq, nk, mq, gq, gk, gm, bd):
    return _norm_bd(nq, bd) * gq, _norm_bd(nk, bd) * gk, _norm_heads(mq, MEM_HD) * gm


def pre_fwd(proj, gq, gk, gm, name):
    T = proj.shape[0]
    tm = _fit(T, 512, 8)
    bd = _block_diag(NA_W, NA_HD)

    def body(nq_ref, nk_ref, nv_ref, mq_ref, gq_ref, gk_ref, gm_ref, bd_ref, q_ref, k_ref, v_ref, m_ref):
        qn, kn, mn = _pre_fn(nq_ref[...], nk_ref[...], mq_ref[...], gq_ref[...], gk_ref[...], gm_ref[...], bd_ref[...])
        q_ref[...] = qn.astype(BF16)
        k_ref[...] = kn.astype(BF16)
        v_ref[...] = nv_ref[...].astype(BF16)
        m_ref[...] = mn.astype(BF16)

    col = lambda c0: pl.BlockSpec((tm, 512), lambda i: (i, c0 // 512))
    vec = pl.BlockSpec((1, 512), lambda i: (0, 0))
    row = pl.BlockSpec((tm, 512), lambda i: (i, 0))
    return pl.pallas_call(
        body, grid=(T // tm,),
        in_specs=[col(COL_NQ), col(COL_NK), col(COL_NV), col(COL_MQ), vec, vec, vec, pl.BlockSpec((NA_W, NA_W), lambda i: (0, 0))],
        out_specs=[row] * 4, out_shape=[SDS((T, 512), BF16)] * 4,
        compiler_params=_cparams(("parallel",)), name=name)(proj, proj, proj, proj, gq, gk, gm, bd)


def pre_bwd(proj, gq, gk, gm, d_qn, d_kn, d_nv, d_mn, dq_f, dq_b, dk_f, dk_b, dv_f, dv_b, d_r, dt_f, dt_b, name):
    T = proj.shape[0]
    tm = _fit(T, 256, 8)
    bd = _block_diag(NA_W, NA_HD)

    def body(nq_ref, nk_ref, mq_ref, gq_ref, gk_ref, gm_ref, bd_ref, dqn_ref, dkn_ref, dnv_ref, dmn_ref,
             dqf_ref, dqb_ref, dkf_ref, dkb_ref, dvf_ref, dvb_ref, dr_ref, dtf_ref, dtb_ref,
             o_ref, dgq_ref, dgk_ref, dgm_ref):
        bdv = bd_ref[...]
        fn = lambda a, b, c, x, y, z: _pre_fn(a, b, c, x, y, z, bdv)
        _, vjp = jax.vjp(fn, nq_ref[...], nk_ref[...], mq_ref[...], gq_ref[...], gk_ref[...], gm_ref[...])
        d_nq, d_nk, d_mq, dgq, dgk, dgm = vjp((dqn_ref[...], dkn_ref[...], dmn_ref[...]))
        o_ref[:, 0:512] = (dqf_ref[...] + dqb_ref[...]).astype(BF16)
        o_ref[:, 512:1024] = (dkf_ref[...] + dkb_ref[...]).astype(BF16)
        o_ref[:, 1024:2048] = (dvf_ref[...] + dvb_ref[...]).astype(BF16)
        o_ref[:, COL_R:COL_R + 1024] = dr_ref[...].astype(BF16)
        o_ref[:, COL_NQ:COL_NQ + 512] = d_nq.astype(BF16)
        o_ref[:, COL_NK:COL_NK + 512] = d_nk.astype(BF16)
        o_ref[:, COL_NV:COL_NV + 512] = dnv_ref[...].astype(BF16)
        o_ref[:, COL_MQ:COL_MQ + 512] = d_mq.astype(BF16)
        dt = dtf_ref[0] + dtb_ref[0]
        for h in range(1, GLA_H):
            dt = dt + dtf_ref[h] + dtb_ref[h]
        o_ref[:, COL_TAIL:COL_TAIL + 128] = dt.astype(BF16)
        o_ref[:, COL_TAIL + 128:PC] = jnp.zeros((tm, PC - COL_TAIL - 128), BF16)

        @pl.when(pl.program_id(0) == 0)
        def _():
            dgq_ref[...] = jnp.zeros_like(dgq_ref)
            dgk_ref[...] = jnp.zeros_like(dgk_ref)
            dgm_ref[...] = jnp.zeros_like(dgm_ref)

        dgq_ref[...] += dgq
        dgk_ref[...] += dgk
        dgm_ref[...] += dgm

    col = lambda c0: pl.BlockSpec((tm, 512), lambda i: (i, c0 // 512))
    vec = pl.BlockSpec((1, 512), lambda i: (0, 0))
    r512 = pl.BlockSpec((tm, 512), lambda i: (i, 0))
    r1024 = pl.BlockSpec((tm, 1024), lambda i: (i, 0))
    tl = pl.BlockSpec((GLA_H, tm, 128), lambda i: (0, i, 0))
    in_specs = [col(COL_NQ), col(COL_NK), col(COL_MQ), vec, vec, vec, pl.BlockSpec((NA_W, NA_W), lambda i: (0, 0)),
                r512, r512, r512, r512, r512, r512, r512, r512, r1024, r1024, r1024, tl, tl]
    return pl.pallas_call(
        body, grid=(T // tm,), in_specs=in_specs,
        out_specs=[pl.BlockSpec((tm, PC), lambda i: (i, 0)), vec, vec, vec],
        out_shape=[SDS((T, PC), BF16), SDS((1, 512), F32), SDS((1, 512), F32), SDS((1, 512), F32)],
        compiler_params=_cparams(("arbitrary",)), name=name)(
            proj, proj, proj, gq, gk, gm, bd, d_qn, d_kn, d_nv, d_mn, dq_f, dq_b, dk_f, dk_b, dv_f, dv_b, d_r, dt_f, dt_b)


def _post_fn(o_f, o_b, r, o_na, o_mem, g_gla, g_na, g_mem, bd):
    y_gla = _norm_heads(o_f + o_b, GLA_HV) * g_gla * (r * jax.nn.sigmoid(r))
    y_na = _norm_bd(o_na, bd) * g_na
    y_mem = _norm_heads(o_mem, MEM_HD) * g_mem
    return jnp.concatenate([y_gla, y_na, y_mem], axis=1)


def post_fwd(o_f, o_b, proj, o_na, o_mem, g_gla, g_na, g_mem, name):
    T = proj.shape[0]
    tm = _fit(T, 256, 8)
    bd = _block_diag(NA_W, NA_HD)

    def body(of_ref, ob_ref, r_ref, ona_ref, omem_ref, gg_ref, gn_ref, gm_ref, bd_ref, y_ref):
        y_ref[...] = _post_fn(of_ref[...], ob_ref[...], r_ref[...], ona_ref[...], omem_ref[...],
                              gg_ref[...], gn_ref[...], gm_ref[...], bd_ref[...]).astype(BF16)

    r1024 = pl.BlockSpec((tm, 1024), lambda i: (i, 0))
    r512 = pl.BlockSpec((tm, 512), lambda i: (i, 0))
    in_specs = [r1024, r1024, pl.BlockSpec((tm, 1024), lambda i: (i, COL_R // 1024)), r512, r512,
                pl.BlockSpec((1, 1024), lambda i: (0, 0)), pl.BlockSpec((1, 512), lambda i: (0, 0)),
                pl.BlockSpec((1, 512), lambda i: (0, 0)), pl.BlockSpec((NA_W, NA_W), lambda i: (0, 0))]
    return pl.pallas_call(
        body, grid=(T // tm,), in_specs=in_specs, out_specs=pl.BlockSpec((tm, D_MODEL), lambda i: (i, 0)),
        out_shape=SDS((T, D_MODEL), BF16), compiler_params=_cparams(("parallel",)), name=name)(
            o_f, o_b, proj, o_na, o_mem, g_gla, g_na, g_mem, bd)


def post_bwd(o_f, o_b, proj, o_na, o_mem, g_gla, g_na, g_mem, dy, name):
    T = proj.shape[0]
    tm = _fit(T, 256, 8)
    bd = _block_diag(NA_W, NA_HD)

    def body(of_ref, ob_ref, r_ref, ona_ref, omem_ref, gg_ref, gn_ref, gm_ref, bd_ref, dy_ref,
             do_ref, dr_ref, dna_ref, dmem_ref, dgg_ref, dgn_ref, dgm_ref):
        bdv = bd_ref[...]
        fn = lambda o, r, a, m, x, y, z: _post_fn(o, 0.0, r, a, m, x, y, z, bdv)
        _, vjp = jax.vjp(fn, of_ref[...] + ob_ref[...], r_ref[...], ona_ref[...], omem_ref[...],
                         gg_ref[...], gn_ref[...], gm_ref[...])
        d_o, d_r, d_na, d_mem, dgg, dgn, dgm = vjp(dy_ref[...])
        do_ref[...] = d_o
        dr_ref[...] = d_r
        dna_ref[...] = d_na
        dmem_ref[...] = d_mem

        @pl.when(pl.program_id(0) == 0)
        def _():
            dgg_ref[...] = jnp.zeros_like(dgg_ref)
            dgn_ref[...] = jnp.zeros_like(dgn_ref)
            dgm_ref[...] = jnp.zeros_like(dgm_ref)

        dgg_ref[...] += dgg
        dgn_ref[...] += dgn
        dgm_ref[...] += dgm

    r1024 = pl.BlockSpec((tm, 1024), lambda i: (i, 0))
    r512 = pl.BlockSpec((tm, 512), lambda i: (i, 0))
    v1024 = pl.BlockSpec((1, 1024), lambda i: (0, 0))
    v512 = pl.BlockSpec((1, 512), lambda i: (0, 0))
    in_specs = [r1024, r1024, pl.BlockSpec((tm, 1024), lambda i: (i, COL_R // 1024)), r512, r512, v1024, v512, v512,
                pl.BlockSpec((NA_W, NA_W), lambda i: (0, 0)), pl.BlockSpec((tm, D_MODEL), lambda i: (i, 0))]
    return pl.pallas_call(
        body, grid=(T // tm,), in_specs=in_specs, out_specs=[r1024, r1024, r512, r512, v1024, v512, v512],
        out_shape=[SDS((T, 1024), F32), SDS((T, 1024), F32), SDS((T, 512), F32), SDS((T, 512), F32),
                   SDS((1, 1024), F32), SDS((1, 512), F32), SDS((1, 512), F32)],
        compiler_params=_cparams(("arbitrary",)), name=name)(o_f, o_b, proj, o_na, o_mem, g_gla, g_na, g_mem, bd, dy)


NA_RB = 8


def _na_row_scores(q_ref, k_ref, v_ref, tb_ref, rb, j, n_rows):
    r = rb * NA_RB + j
    rs = jnp.clip(r - NA_ROWS // 2, 0, n_rows - NA_ROWS)
    dr0 = rs - r + (NA_ROWS - 1)
    tok = pl.ds(pl.multiple_of(rs * GRID_W, GRID_W), NA_ROWS * GRID_W)
    q = q_ref[j * GRID_W:(j + 1) * GRID_W, :]
    kk, vv = k_ref[tok, :], v_ref[tok, :]
    bias = jnp.concatenate([tb_ref[dr0 + 2 * i] for i in range(NA_ROWS // 2)], axis=1)
    s = _bdot(q, kk, NT) * (NA_HD ** -0.5) + bias
    m = jnp.max(s, axis=1, keepdims=True)
    p = jnp.exp(s - m)
    l = jnp.sum(p, axis=1, keepdims=True)
    return q, kk, vv, p, l, tok, dr0


def natten_fwd(q, k, v, tb2, name):
    H, T, hd = q.shape
    n_rows = T // GRID_W
    rbt = NA_RB * GRID_W

    def body(q_ref, k_ref, v_ref, tb_ref, o_ref):
        rb = pl.program_id(1)
        for j in range(NA_RB):
            _, _, vv, p, l, _, _ = _na_row_scores(q_ref, k_ref, v_ref, tb_ref, rb, j, n_rows)
            o_ref[j * GRID_W:(j + 1) * GRID_W, :] = _bdot(p, vv, NN) / l

    whole = pl.BlockSpec((None, T, hd), lambda h, r: (h, 0, 0))
    blk = pl.BlockSpec((None, rbt, hd), lambda h, r: (h, r, 0))
    return pl.pallas_call(
        body, grid=(H, n_rows // NA_RB),
        in_specs=[blk, whole, whole, pl.BlockSpec((None, 2 * NA_ROWS - 2, GRID_W, 2 * GRID_W), lambda h, r: (h, 0, 0, 0))],
        out_specs=blk, out_shape=SDS((H, T, hd), F32),
        compiler_params=_cparams(("parallel", "arbitrary")), name=name)(q, k, v, tb2)


def natten_bwd(q, k, v, tb2, do, name):
    H, T, hd = q.shape
    n_rows = T // GRID_W
    rbt = NA_RB * GRID_W
    scale = NA_HD ** -0.5

    def body(q_ref, k_ref, v_ref, tb_ref, do_ref, dq_ref, dk_ref, dv_ref, dtb_ref):
        rb = pl.program_id(1)

        @pl.when(rb == 0)
        def _():
            dk_ref[...] = jnp.zeros_like(dk_ref)
            dv_ref[...] = jnp.zeros_like(dv_ref)
            dtb_ref[...] = jnp.zeros_like(dtb_ref)

        for j in range(NA_RB):
            qv, kk, vv, p, l, tok, dr0 = _na_row_scores(q_ref, k_ref, v_ref, tb_ref, rb, j, n_rows)
            p = p / l
            dov = do_ref[j * GRID_W:(j + 1) * GRID_W, :]
            dp = _bdot(dov, vv, NT)
            ds = p * (dp - jnp.sum(dp * p, axis=1, keepdims=True))
            dq_ref[j * GRID_W:(j + 1) * GRID_W, :] = _bdot(ds, kk, NN) * scale
            dk_ref[tok, :] += _bdot(ds, qv, TN) * scale
            dv_ref[tok, :] += _bdot(p, dov, TN)
            for i in range(NA_ROWS // 2):
                dtb_ref[dr0 + 2 * i] += ds[:, 2 * GRID_W * i:2 * GRID_W * (i + 1)]

    whole = pl.BlockSpec((None, T, hd), lambda h, r: (h, 0, 0))
    blk = pl.BlockSpec((None, rbt, hd), lambda h, r: (h, r, 0))
    tbs = pl.BlockSpec((None, 2 * NA_ROWS - 2, GRID_W, 2 * GRID_W), lambda h, r: (h, 0, 0, 0))
    return pl.pallas_call(
        body, grid=(H, n_rows // NA_RB), in_specs=[blk, whole, whole, tbs, blk],
        out_specs=[blk, whole, whole, tbs],
        out_shape=[SDS((H, T, hd), F32), SDS((H, T, hd), F32), SDS((H, T, hd), F32), SDS(tb2.shape, F32)],
        compiler_params=_cparams(("parallel", "arbitrary")), name=name)(q, k, v, tb2, do)


def _rpb_expand_consts():
    qc = np.arange(GRID_W)[:, None]
    kc = np.arange(GRID_W)[None, :]
    cs = np.clip(qc - NA_COLS // 2, 0, GRID_W - NA_COLS)
    inside = (kc >= cs) & (kc < cs + NA_COLS)
    dc = np.clip(kc - qc, -(NA_COLS - 1), NA_COLS - 1) + (NA_COLS - 1)
    e = np.zeros((128, GRID_W * GRID_W), np.float32)
    flat = (qc * GRID_W + kc)
    e[dc[inside], flat[inside]] = 1.0
    neg = np.where(inside, 0.0, NEG).astype(np.float32).reshape(1, -1)
    return e, neg


def _rpb_fold_consts():
    sa = np.zeros((NA_H * 15, NA_H * 14), np.float32)
    sb = np.zeros((NA_H * 15, NA_H * 14), np.float32)
    for h in range(NA_H):
        for d in range(14):
            sa[h * 15 + d, h * 14 + d] = 1.0
            sb[h * 15 + d + 1, h * 14 + d] = 1.0
    return sa, sb


def rpb_table(rpb, name):
    e, neg = _rpb_expand_consts()
    rp = jnp.pad(rpb.reshape(NA_H * 15, 31), ((0, 0), (0, 128 - 31)))

    def body(r_ref, e_ref, n_ref, o_ref):
        o_ref[...] = jnp.dot(r_ref[...], e_ref[...], preferred_element_type=F32, precision=HI) + n_ref[...]

    t = pl.pallas_call(body, out_shape=SDS((NA_H * 15, GRID_W * GRID_W), F32), name=name)(rp, e, neg)
    t = t.reshape(NA_H, 15, GRID_W, GRID_W)
    return jnp.concatenate([t[:, :14], t[:, 1:]], axis=-1)


def rpb_table_bwd(dtb2, name):
    e, _ = _rpb_expand_consts()
    sa, sb = _rpb_fold_consts()
    a = dtb2[..., :GRID_W].reshape(NA_H * 14, GRID_W * GRID_W)
    b = dtb2[..., GRID_W:].reshape(NA_H * 14, GRID_W * GRID_W)

    def body(a_ref, b_ref, e_ref, sa_ref, sb_ref, o_ref):
        ev = e_ref[...]
        pa = lax.dot_general(a_ref[...], ev, (NT, ((), ())), preferred_element_type=F32, precision=HI)
        pb = lax.dot_general(b_ref[...], ev, (NT, ((), ())), preferred_element_type=F32, precision=HI)
        o_ref[...] = (jnp.dot(sa_ref[...], pa, preferred_element_type=F32, precision=HI)
                      + jnp.dot(sb_ref[...], pb, preferred_element_type=F32, precision=HI))

    d = pl.pallas_call(body, out_shape=SDS((NA_H * 15, 128), F32), name=name)(a, b, e, sa, sb)
    return d[:, :31].reshape(NA_H, 15, 31)


def _kprep_fn(kv, gk):
    return _norm_heads(kv[:, :MEM_W], MEM_HD) * gk, kv[:, MEM_W:]


def mem_kprep(kv, gk, name):
    def body(kv_ref, g_ref, k_ref, v_ref):
        kn, vv = _kprep_fn(kv_ref[...], g_ref[...])
        k_ref[...] = kn.astype(BF16)
        v_ref[...] = vv.astype(BF16)

    return pl.pallas_call(body, out_shape=[SDS((MEM_TOK, MEM_W), BF16)] * 2, name=name)(kv, gk)


def mem_kprep_bwd(kv, gk, dk, dv, name):
    def body(kv_ref, g_ref, dk_ref, dv_ref, dkv_ref, dg_ref):
        _, vjp = jax.vjp(_kprep_fn, kv_ref[...], g_ref[...])
        dkv, dg = vjp((dk_ref[...], dv_ref[...]))
        dkv_ref[...] = dkv.astype(BF16)
        dg_ref[...] = dg

    return pl.pallas_call(body, out_shape=[SDS((MEM_TOK, 2 * MEM_W), BF16), SDS((1, MEM_W), F32)], name=name)(kv, gk, dk, dv)


def _mem_probs(q_ref, k_ref, h):
    hs = slice(h * MEM_HD, (h + 1) * MEM_HD)
    qh, kh = q_ref[:, hs], k_ref[:, hs]
    s = _bdot(qh, kh, NT) * (MEM_HD ** -0.5)
    p = jnp.exp(s - jnp.max(s, axis=1, keepdims=True))
    return hs, qh, kh, p, jnp.sum(p, axis=1, keepdims=True)


def mem_attn_fwd(q, km, vm, name):
    T = q.shape[0]
    tm = _fit(T, 512, 8)

    def body(q_ref, k_ref, v_ref, o_ref):
        for h in range(MEM_H):
            hs, _, _, p, l = _mem_probs(q_ref, k_ref, h)
            o_ref[:, hs] = _bdot(p, v_ref[:, hs], NN) / l

    row = pl.BlockSpec((tm, MEM_W), lambda i: (i, 0))
    full = pl.BlockSpec((MEM_TOK, MEM_W), lambda i: (0, 0))
    return pl.pallas_call(body, grid=(T // tm,), in_specs=[row, full, full], out_specs=row, out_shape=SDS((T, MEM_W), F32),
                          compiler_params=_cparams(("parallel",)), name=name)(q, km, vm)


def mem_attn_bwd(q, km, vm, do, name):
    T = q.shape[0]
    tm = _fit(T, 512, 8)
    scale = MEM_HD ** -0.5

    def body(q_ref, k_ref, v_ref, do_ref, dq_ref, dk_ref, dv_ref):
        @pl.when(pl.program_id(0) == 0)
        def _():
            dk_ref[...] = jnp.zeros_like(dk_ref)
            dv_ref[...] = jnp.zeros_like(dv_ref)

        for h in range(MEM_H):
            hs, qh, kh, p, l = _mem_probs(q_ref, k_ref, h)
            p = p / l
            dov = do_ref[:, hs]
            dp = _bdot(dov, v_ref[:, hs], NT)
            ds = p * (dp - jnp.sum(dp * p, axis=1, keepdims=True))
            dq_ref[:, hs] = _bdot(ds, kh, NN) * scale
            dk_ref[:, hs] += _bdot(ds, qh, TN) * scale
            dv_ref[:, hs] += _bdot(p, dov, TN)

    row = pl.BlockSpec((tm, MEM_W), lambda i: (i, 0))
    full = pl.BlockSpec((MEM_TOK, MEM_W), lambda i: (0, 0))
    return pl.pallas_call(
        body, grid=(T // tm,), in_specs=[row, full, full, row], out_specs=[row, full, full],
        out_shape=[SDS((T, MEM_W), F32), SDS((MEM_TOK, MEM_W), F32), SDS((MEM_TOK, MEM_W), F32)],
        compiler_params=_cparams(("arbitrary",)), name=name)(q, km, vm, do)


SLOT_BLOCK_ELEMS = 512 * 1024


def _slot_rows(R, Cc):
    return _fit(R, max(16, SLOT_BLOCK_ELEMS // Cc // 16 * 16), 16)


def sum_slots(a, name):
    n, R, Cc = a.shape
    tr = _slot_rows(R, Cc)

    def body(a_ref, o_ref):
        s = a_ref[0].astype(F32)
        for i in range(1, n):
            s = s + a_ref[i].astype(F32)
        o_ref[...] = s

    return pl.pallas_call(body, grid=(R // tr,), in_specs=[pl.BlockSpec((n, tr, Cc), lambda i: (0, i, 0))],
                          out_specs=pl.BlockSpec((tr, Cc), lambda i: (i, 0)), out_shape=SDS((R, Cc), F32),
                          compiler_params=_cparams(("parallel",)), name=name)(a)


def pair_sum(g, land, c_idx, name):
    _, R, Cc = g.shape
    tr = _slot_rows(R, Cc)

    def body(c_ref, g_ref, l_ref, o_ref):
        o_ref[...] = (g_ref[...].astype(F32) + l_ref[...].astype(F32)).astype(o_ref.dtype)

    blk = pl.BlockSpec((None, tr, Cc), lambda k, i, c: (k, i, 0))
    gs = pltpu.PrefetchScalarGridSpec(
        num_scalar_prefetch=1, grid=(4, R // tr),
        in_specs=[pl.BlockSpec((None, tr, Cc), lambda k, i, c: (2 * k + c[0], i, 0)), blk], out_specs=blk)
    return pl.pallas_call(body, grid_spec=gs, out_shape=SDS((4, R, Cc), g.dtype),
                          compiler_params=_cparams(("parallel", "parallel")), name=name)(c_idx, g, land)


def adamw(w, g, m, v, name):
    R, Cc = w.shape
    tr = _fit(R, max(8, (262144 // max(Cc, 128)) // 8 * 8), 8)
    c1 = 1.0 - ADAM_B1 ** ADAM_STEP
    c2 = 1.0 - ADAM_B2 ** ADAM_STEP

    def body(w_ref, g_ref, m_ref, v_ref, d_ref, mo_ref, vo_ref):
        gv = g_ref[...]
        m2 = ADAM_B1 * m_ref[...] + (1.0 - ADAM_B1) * gv
        v2 = ADAM_B2 * v_ref[...] + (1.0 - ADAM_B2) * (gv * gv)
        d_ref[...] = -ADAM_LR * ((m2 / c1) / (jnp.sqrt(v2 / c2) + ADAM_EPS) + ADAM_WD * w_ref[...])
        mo_ref[...] = m2
        vo_ref[...] = v2

    blk = pl.BlockSpec((tr, Cc), lambda i: (i, 0))
    return pl.pallas_call(body, grid=(R // tr,), in_specs=[blk] * 4, out_specs=[blk] * 3, out_shape=[SDS((R, Cc), F32)] * 3,
                          compiler_params=_cparams(("parallel",)), name=name)(w, g, m, v)


ANY = pl.BlockSpec(memory_space=pl.ANY)


def _my_pos():
    return lax.axis_index("x"), lax.axis_index("y"), lax.axis_index("c")


class Rider:
    def __init__(self, inputs, out_shapes, scratch, start, finish, aliases=None):
        self.inputs, self.out_shapes, self.scratch = list(inputs), list(out_shapes), list(scratch)
        self.start, self.finish, self.aliases = start, finish, dict(aliases or {})


def combine_riders(r1, r2):
    ni, no, ns = len(r1.inputs), len(r1.out_shapes), len(r1.scratch)

    def both(f1, f2):
        def run(i, o, s):
            f1(i[:ni], o[:no], s[:ns])
            f2(i[ni:], o[no:], s[ns:])
        return run

    aliases = {**r1.aliases, **{ni + i: no + j for i, j in r2.aliases.items()}}
    return Rider(r1.inputs + r2.inputs, r1.out_shapes + r2.out_shapes, r1.scratch + r2.scratch,
                 both(r1.start, r2.start), both(r1.finish, r2.finish), aliases)


def _gather_phases(x_refs, out_refs, sems):
    send_sems, recv_sems, local_sems = sems
    n = len(out_refs)
    x, y, c = _my_pos()
    me, sibling = (x, y, c), (x, y, 1 - c)
    chips = [(1 - x, y), (x, 1 - y), (1 - x, 1 - y)]

    def slot(a, px, py, pc):
        return out_refs[a].at[4 * px + 2 * py + pc]

    def copy(a, k, block, to, src=None):
        return pltpu.make_async_remote_copy(
            src_ref=slot(a, *block) if src is None else src, dst_ref=slot(a, *block),
            send_sem=send_sems.at[7 * a + k], recv_sem=recv_sems.at[7 * a + k], device_id=to, device_id_type=MESH)

    def mine(a):
        return pltpu.make_async_copy(x_refs[a], slot(a, *me), local_sems.at[a])

    def spread(a):
        return [copy(a, 0, me, sibling, src=x_refs[a])] + [copy(a, 1 + j, me, (*chip, c), src=x_refs[a])
                                                           for j, chip in enumerate(chips)]

    def spread_start():
        for a in range(n):
            mine(a).start()
        for a in range(n):
            for cp in spread(a):
                cp.start()

    def spread_finish():
        for j, chip in enumerate(chips):
            for a in range(n):
                copy(a, 1 + j, (*chip, c), me).wait_recv()
        for a in range(n):
            copy(a, 0, sibling, me).wait_recv()
        for a in range(n):
            for cp in spread(a):
                cp.wait_send()
            mine(a).wait()

    def forward_start():
        for j, chip in enumerate(chips):
            for a in range(n):
                copy(a, 4 + j, (*chip, c), sibling).start()

    def forward_finish():
        for j, chip in enumerate(chips):
            for a in range(n):
                copy(a, 4 + j, (*chip, 1 - c), me).wait_recv()
        for j, chip in enumerate(chips):
            for a in range(n):
                copy(a, 4 + j, (*chip, c), sibling).wait_send()

    return spread_start, spread_finish, forward_start, forward_finish


def _gather_scratch(n):
    return [pltpu.SemaphoreType.DMA((7 * n,)), pltpu.SemaphoreType.DMA((7 * n,)), pltpu.SemaphoreType.DMA((n,))]


def all_gather(shards, name):
    n = len(shards)

    def body(*refs):
        phases = _gather_phases(refs[:n], refs[n:2 * n], refs[2 * n:])
        for phase in phases:
            phase()

    return pl.pallas_call(
        body, out_shape=[SDS((N_DEV,) + s.shape, s.dtype) for s in shards], in_specs=[ANY] * n, out_specs=[ANY] * n,
        scratch_shapes=_gather_scratch(n), name=name)(*shards)


def gather_spread_rider(shards):
    return Rider(shards, [SDS((N_DEV,) + s.shape, s.dtype) for s in shards], _gather_scratch(len(shards)),
                 lambda i, o, s: _gather_phases(i, o, s)[0](), lambda i, o, s: _gather_phases(i, o, s)[1]())


def gather_forward_rider(bufs):
    n = len(bufs)
    return Rider(bufs, [SDS(b.shape, b.dtype) for b in bufs], _gather_scratch(n),
                 lambda i, o, s: _gather_phases(None, o, s)[2](), lambda i, o, s: _gather_phases(None, o, s)[3](),
                 aliases={a: a for a in range(n)})


def pair_exchange(gs, name):
    n = len(gs)

    def body(*refs):
        g_refs, land_refs = refs[:n], refs[n:2 * n]
        send_sems, recv_sems = refs[2 * n:]
        x, y, c = _my_pos()
        sibling = (x, y, 1 - c)
        copies = [pltpu.make_async_remote_copy(
            src_ref=g_refs[a].at[2 * k + (1 - c)], dst_ref=land_refs[a].at[k], send_sem=send_sems.at[4 * a + k],
            recv_sem=recv_sems.at[4 * a + k], device_id=sibling, device_id_type=MESH) for a in range(n) for k in range(4)]
        for cp in copies:
            cp.start()
        for cp in copies:
            cp.wait_recv()
        for cp in copies:
            cp.wait_send()

    return pl.pallas_call(
        body, out_shape=[SDS((4,) + g.shape[1:], g.dtype) for g in gs], in_specs=[ANY] * n, out_specs=[ANY] * n,
        scratch_shapes=[pltpu.SemaphoreType.DMA((4 * n,)), pltpu.SemaphoreType.DMA((4 * n,))], name=name)(*gs)


def chip_exchange(ss, name):
    n = len(ss)

    def body(*refs):
        start, finish = _chip_exchange_phases(refs[:n], refs[n:2 * n], refs[2 * n:])
        start()
        finish()

    return pl.pallas_call(
        body, out_shape=[SDS(s.shape, s.dtype) for s in ss], in_specs=[ANY] * n, out_specs=[ANY] * n,
        scratch_shapes=_chip_exchange_scratch(n), name=name)(*ss)


def _chip_exchange_scratch(n):
    return [pltpu.SemaphoreType.DMA((3 * n,)), pltpu.SemaphoreType.DMA((3 * n,)), pltpu.SemaphoreType.DMA((n,))]


def _chip_exchange_phases(s_refs, land_refs, sems):
    send_sems, recv_sems, local_sems = sems
    n = len(s_refs)
    x, y, c = _my_pos()
    my_chip = 2 * x + y
    chips = [(1 - x, y), (x, 1 - y), (1 - x, 1 - y)]

    def own(a):
        return pltpu.make_async_copy(s_refs[a].at[my_chip], land_refs[a].at[my_chip], local_sems.at[a])

    def copy(a, j, src_chip, dst_chip):
        px, py = chips[j]
        return pltpu.make_async_remote_copy(
            src_ref=s_refs[a].at[src_chip], dst_ref=land_refs[a].at[dst_chip], send_sem=send_sems.at[3 * a + j],
            recv_sem=recv_sems.at[3 * a + j], device_id=(px, py, c), device_id_type=MESH)

    def start():
        for a in range(n):
            own(a).start()
            for j, (px, py) in enumerate(chips):
                copy(a, j, 2 * px + py, my_chip).start()

    def finish():
        for a in range(n):
            for j, (px, py) in enumerate(chips):
                copy(a, j, my_chip, 2 * px + py).wait_recv()
        for a in range(n):
            for j, (px, py) in enumerate(chips):
                copy(a, j, 2 * px + py, my_chip).wait_send()
            own(a).wait()

    return start, finish


def chip_exchange_rider(ss):
    return Rider(ss, [SDS(s.shape, s.dtype) for s in ss], _chip_exchange_scratch(len(ss)),
                 lambda i, o, s: _chip_exchange_phases(i, o, s)[0](), lambda i, o, s: _chip_exchange_phases(i, o, s)[1]())


W_IN_SHARD, W_IN_PACKED = 644, 768
BIG = ("ffn_w13", "ffn_w2", "w_out", "mem_wkv", "w_in")
GATE_W = ("gla_wg2_f", "gla_wg2_b")
SHARD_NAMES = BIG + GATE_W


def _permuted_ranges(c0, c1):
    res = []
    for o0, o1, p0 in ((0, ORIG_GATE0, 0), (ORIG_GATE0, ORIG_AFTER_GATE, COL_TAIL), (ORIG_AFTER_GATE, IN_COLS, ORIG_GATE0)):
        lo, hi = max(c0, o0), min(c1, o1)
        if lo < hi:
            res.append((p0 + lo - o0, p0 + hi - o0))
    return res


def assemble_w_in(blocks):
    placed = []
    for d in range(N_DEV):
        c = d * W_IN_SHARD
        for p0, p1 in _permuted_ranges(c, c + W_IN_SHARD):
            placed.append((p0, blocks[d][:, c - d * W_IN_SHARD:c - d * W_IN_SHARD + (p1 - p0)]))
            c += p1 - p0
    placed.sort(key=lambda t: t[0])
    return jnp.concatenate([t[1] for t in placed] + [jnp.zeros((blocks.shape[1], PC - IN_COLS), blocks.dtype)], axis=1)


def split_w_in_grad(g):
    pad = jnp.zeros((g.shape[0], W_IN_PACKED - W_IN_SHARD), g.dtype)
    return jnp.stack([jnp.concatenate([g[:, p0:p1] for p0, p1 in _permuted_ranges(d * W_IN_SHARD, (d + 1) * W_IN_SHARD)]
                                      + [pad], axis=1) for d in range(N_DEV)])


def wire_shards(shards):
    return [jnp.pad(shards[n], ((0, 0), (0, W_IN_PACKED - W_IN_SHARD))) if n == "w_in" else shards[n] for n in BIG]


def weights_from_gathered(names, gathered):
    full = {}
    for n, g in zip(names, gathered):
        full[n] = g if n == "ffn_w13" else assemble_w_in(g) if n == "w_in" else g.reshape(-1, g.shape[-1])
    return full


def reduce_scatter_begin(names, gw, c_idx, tag):
    g = [split_w_in_grad(gw[n]) if n == "w_in" else gw[n].reshape((N_DEV, -1, gw[n].shape[-1])) for n in names]
    land = pair_exchange(g, tag + "rs_pair_exchange_" + names[0])
    return [pair_sum(g[i], land[i], c_idx, tag + "rs_pair_sum_" + n) for i, n in enumerate(names)]


def reduce_scatter_end(names, landed, tag):
    out = {n: sum_slots(landed[i], tag + "rs_chip_sum_" + n) for i, n in enumerate(names)}
    if "w_in" in out:
        out["w_in"] = out["w_in"][:, :W_IN_SHARD]
    return out


def gather_gate_weights(w, depth):
    mine = jnp.concatenate([w[n].reshape(-1) for n in GATE_W]).reshape(-1, 128)
    got = all_gather([mine], "gather_gate_weights")[0].reshape(N_DEV, len(GATE_W), depth, GLA_RANK, GLA_DK // N_DEV)
    return [{n: got[:, i, l].transpose(1, 0, 2).reshape(GLA_RANK, GLA_DK) for i, n in enumerate(GATE_W)} for l in range(depth)]


def pad_gate_weight(wg2, backward_dir):
    r0 = GLA_RANK if backward_dir else 0
    w = wg2.astype(F32).reshape(GLA_RANK, GLA_H, GLA_HK).transpose(1, 0, 2)
    return jnp.pad(w, ((0, 0), (r0, 128 - GLA_RANK - r0), (0, 0)))


def unpad_gate_grad(dw, backward_dir):
    r0 = GLA_RANK if backward_dir else 0
    return dw[:, r0:r0 + GLA_RANK, :].transpose(1, 0, 2).reshape(GLA_RANK, GLA_DK)


def to_heads(t):
    T = t.shape[0]
    return t.reshape(T, NA_H, NA_HD).transpose(1, 0, 2)


def from_heads(t):
    return t.transpose(1, 0, 2).reshape(t.shape[1], NA_W)


REPLICATED = ("attn_norm", "gla_bg_f", "gla_bg_b", "gla_out_norm", "na_q_norm", "na_k_norm", "na_rpb", "na_out_norm",
              "mem_norm", "mem_q_norm", "mem_k_norm", "mem_out_norm", "ffn_norm")
WEIGHTS = ("attn_norm", "w_in", "gla_wg2_f", "gla_bg_f", "gla_wg2_b", "gla_bg_b", "gla_out_norm", "na_q_norm", "na_k_norm",
           "na_rpb", "na_out_norm", "mem_norm", "mem_wkv", "mem_q_norm", "mem_k_norm", "mem_out_norm", "w_out", "ffn_norm",
           "ffn_w13", "ffn_w2")


def fold_heads(dg, n_heads, name):
    hd = dg.shape[1] // n_heads
    fold = (np.arange(dg.shape[1])[:, None] % hd == np.arange(128)[None, :]).astype(np.float32)
    out = matmul(jnp.pad(dg, ((0, 7), (0, 0))), fold, "nn", name, exact=True)
    return out[0, :hd]


def layer_fwd(x, mem_n_in, p, W, l, arriving=None, next_shards=None):
    tag = f"l{l}_"
    row = lambda v: v.reshape(1, -1)
    sv = {"x": x}
    if arriving is None:
        sv["xn"] = rms_fwd(x, row(p["attn_norm"]), tag + "attn_rms")
    else:
        sv["xn"], rest = rms_fwd(x, row(p["attn_norm"]), tag + "attn_rms", rider=gather_forward_rider(arriving))
        W = {**W, **weights_from_gathered(BIG[1:], rest)}
    proj = matmul(sv["xn"], W["w_in"], "nn", tag + "proj", tn=768, tk=2048)
    sv["proj"] = proj
    sv["wg_f"], sv["wg_b"] = pad_gate_weight(W["gla_wg2_f"], False), pad_gate_weight(W["gla_wg2_b"], True)
    sv["bg_f"], sv["bg_b"] = p["gla_bg_f"].reshape(GLA_H, 1, GLA_HK), p["gla_bg_b"].reshape(GLA_H, 1, GLA_HK)
    sv["o_f"], sv["s_f"] = gla_fwd(proj, sv["wg_f"], sv["bg_f"], False, tag + "gla_f")
    sv["o_b"], sv["s_b"] = gla_fwd(proj, sv["wg_b"], sv["bg_b"], True, tag + "gla_b")
    sv["gq"], sv["gk"] = jnp.tile(row(p["na_q_norm"]), (1, NA_H)), jnp.tile(row(p["na_k_norm"]), (1, NA_H))
    sv["gmq"], sv["gmk"] = jnp.tile(row(p["mem_q_norm"]), (1, MEM_H)), jnp.tile(row(p["mem_k_norm"]), (1, MEM_H))
    qn, kn, vn, mqn = pre_fwd(proj, sv["gq"], sv["gk"], sv["gmq"], tag + "pre")
    sv["q_hm"], sv["k_hm"], sv["v_hm"], sv["mqn"] = to_heads(qn), to_heads(kn), to_heads(vn), mqn
    sv["tb2"] = rpb_table(p["na_rpb"], tag + "rpb_table")
    sv["o_na"] = from_heads(natten_fwd(sv["q_hm"], sv["k_hm"], sv["v_hm"], sv["tb2"], tag + "natten"))
    sv["mem_n"] = rms_fwd(mem_n_in, row(p["mem_norm"]), tag + "mem_rms")
    sv["kv"] = matmul(sv["mem_n"], W["mem_wkv"], "nn", tag + "mem_kv", tn=512, tk=2048)
    sv["km"], sv["vm"] = mem_kprep(sv["kv"], sv["gmk"], tag + "mem_kprep")
    sv["o_mem"] = mem_attn_fwd(mqn, sv["km"], sv["vm"], tag + "mem_attn")
    sv["ycat"] = post_fwd(sv["o_f"], sv["o_b"], proj, sv["o_na"], sv["o_mem"], row(p["gla_out_norm"]),
                          row(p["na_out_norm"]), row(p["mem_out_norm"]), tag + "post")
    x1 = matmul(sv["ycat"], W["w_out"], "nn", tag + "out_proj", res=x, tk=2048)
    sv["x1"] = x1
    sv["h"] = rms_fwd(x1, row(p["ffn_norm"]), tag + "ffn_rms")
    if next_shards is None:
        sv["gu"], sv["a"], _ = ffn_up_swiglu(sv["h"], W["ffn_w13"], tag + "ffn_up")
        return matmul(sv["a"], W["ffn_w2"], "nn", tag + "ffn_down", res=x1, tk=2816), sv, W, None, None
    sv["gu"], sv["a"], spread13 = ffn_up_swiglu(sv["h"], W["ffn_w13"], tag + "ffn_up",
                                                rider=gather_spread_rider(next_shards[:1]))
    x2, carried = matmul(sv["a"], W["ffn_w2"], "nn", tag + "ffn_down", res=x1, tk=2816,
                         rider=combine_riders(gather_forward_rider(spread13), gather_spread_rider(next_shards[1:])))
    return x2, sv, W, weights_from_gathered(BIG[:1], carried[:1]), carried[1:]


RS_EARLY, RS_LATE = ("ffn_w13", "ffn_w2", "w_out"), ("mem_wkv", "w_in")


def layer_bwd(dx2, dx2_b, mem_n_in, p, W, sv, l, c_idx, rider=None):
    tag = f"l{l}_b_"
    row = lambda v: v.reshape(1, -1)
    gw, gs = {}, {}
    gw["ffn_w2"] = matmul(sv["a"], dx2_b, "tn", tag + "dw2", out_dtype=BF16, tm=1408, tn=1024, tk=2048)
    dgu = ffn_down_bwd(dx2_b, W["ffn_w2"], sv["gu"], tag + "d_swiglu")
    dh = matmul(dgu, W["ffn_w13"], "nt", tag + "d_h", a_halves=True, b_blocked=True, rider=rider)
    dh, carried = dh if rider is not None else (dh, None)
    gw["ffn_w13"] = matmul(sv["h"], dgu, "tn", tag + "dw13", out_dtype=BF16, tm=2048, tn=1408, tk=1024,
                           b_halves=True, out_blocked=True)
    dx1, dx1_b, dg = rms_bwd(sv["x1"], row(p["ffn_norm"]), dh, dx2, tag + "ffn_rms")
    gs["ffn_norm"] = dg[0]
    dycat = matmul(dx1_b, W["w_out"], "nt", tag + "d_ycat", tk=2048)
    gw["w_out"] = matmul(sv["ycat"], dx1_b, "tn", tag + "dw_out", out_dtype=BF16, tm=2048, tn=1024, tk=2048)
    early = reduce_scatter_begin(RS_EARLY, gw, c_idx, tag)
    d_o, d_r, d_ona, d_omem, dgg, dgn, dgm = post_bwd(
        sv["o_f"], sv["o_b"], sv["proj"], sv["o_na"], sv["o_mem"], row(p["gla_out_norm"]), row(p["na_out_norm"]),
        row(p["mem_out_norm"]), dycat, tag + "post")
    gs["gla_out_norm"], gs["na_out_norm"], gs["mem_out_norm"] = dgg[0], dgn[0], dgm[0]
    dq_f, dk_f, dv_f, dt_f, dwg_f, dbg_f = gla_bwd(sv["proj"], sv["wg_f"], sv["bg_f"], sv["s_f"], d_o, False, tag + "gla_f")
    dq_b, dk_b, dv_b, dt_b, dwg_b, dbg_b = gla_bwd(sv["proj"], sv["wg_b"], sv["bg_b"], sv["s_b"], d_o, True, tag + "gla_b")
    gs["gla_wg2_f"], gs["gla_wg2_b"] = unpad_gate_grad(dwg_f, False), unpad_gate_grad(dwg_b, True)
    gs["gla_bg_f"], gs["gla_bg_b"] = dbg_f.reshape(-1), dbg_b.reshape(-1)
    dq_hm, dk_hm, dv_hm, dtb2 = natten_bwd(sv["q_hm"], sv["k_hm"], sv["v_hm"], sv["tb2"], to_heads(d_ona), tag + "natten")
    gs["na_rpb"] = rpb_table_bwd(dtb2, tag + "rpb_table")
    d_mqn, dkm, dvm = mem_attn_bwd(sv["mqn"], sv["km"], sv["vm"], d_omem, tag + "mem_attn")
    dkv, dgmk = mem_kprep_bwd(sv["kv"], sv["gmk"], dkm, dvm, tag + "mem_kprep")
    gs["mem_k_norm"] = fold_heads(dgmk, MEM_H, tag + "fold_mk")
    gw["mem_wkv"] = matmul(sv["mem_n"], dkv, "tn", tag + "dw_kv", out_dtype=BF16, tm=2048, tn=1024, tk=256)
    d_memn = matmul(dkv, W["mem_wkv"], "nt", tag + "d_memn", tk=1024)
    _, _, dg = rms_bwd(mem_n_in, row(p["mem_norm"]), d_memn, None, tag + "mem_rms")
    gs["mem_norm"] = dg[0]
    dproj, dgq, dgk, dgmq = pre_bwd(sv["proj"], sv["gq"], sv["gk"], sv["gmq"], from_heads(dq_hm), from_heads(dk_hm),
                                    from_heads(dv_hm), d_mqn, dq_f, dq_b, dk_f, dk_b, dv_f, dv_b, d_r, dt_f, dt_b, tag + "pre")
    gs["na_q_norm"] = fold_heads(dgq, NA_H, tag + "fold_q")
    gs["na_k_norm"] = fold_heads(dgk, NA_H, tag + "fold_k")
    gs["mem_q_norm"] = fold_heads(dgmq, MEM_H, tag + "fold_mq")
    dxn, landed13 = matmul(dproj, W["w_in"], "nt", tag + "d_xn", tk=1792, rider=chip_exchange_rider(early[:1]))
    gw["w_in"], landed_rest = matmul(sv["xn"], dproj, "tn", tag + "dw_in", out_dtype=BF16, tm=2048, tn=768, tk=2048,
                                     rider=chip_exchange_rider(early[1:]))
    shard_grads = reduce_scatter_end(RS_EARLY, landed13 + landed_rest, tag)
    late = reduce_scatter_begin(RS_LATE, gw, c_idx, tag)
    dx, dx_b, dg = rms_bwd(sv["x"], row(p["attn_norm"]), dxn, dx1, tag + "attn_rms")
    gs["attn_norm"] = dg[0]
    return dx, dx_b, shard_grads, gs, carried, late


def kernel(x, mem, attn_norm, w_in, gla_wg2_f, gla_bg_f, gla_wg2_b, gla_bg_b, gla_out_norm, na_q_norm, na_k_norm, na_rpb, na_out_norm, mem_norm, mem_wkv, mem_q_norm, mem_k_norm, mem_out_norm, w_out, ffn_norm, ffn_w13, ffn_w2, loss_target, m_attn_norm, m_w_in, m_gla_wg2_f, m_gla_bg_f, m_gla_wg2_b, m_gla_bg_b, m_gla_out_norm, m_na_q_norm, m_na_k_norm, m_na_rpb, m_na_out_norm, m_mem_norm, m_mem_wkv, m_mem_q_norm, m_mem_k_norm, m_mem_out_norm, m_w_out, m_ffn_norm, m_ffn_w13, m_ffn_w2, v_attn_norm, v_w_in, v_gla_wg2_f, v_gla_bg_f, v_gla_wg2_b, v_gla_bg_b, v_gla_out_norm, v_na_q_norm, v_na_k_norm, v_na_rpb, v_na_out_norm, v_mem_norm, v_mem_wkv, v_mem_q_norm, v_mem_k_norm, v_mem_out_norm, v_w_out, v_ffn_norm, v_ffn_w13, v_ffn_w2):
    w = dict(attn_norm=attn_norm, w_in=w_in, gla_wg2_f=gla_wg2_f, gla_bg_f=gla_bg_f, gla_wg2_b=gla_wg2_b, gla_bg_b=gla_bg_b,
             gla_out_norm=gla_out_norm, na_q_norm=na_q_norm, na_k_norm=na_k_norm, na_rpb=na_rpb, na_out_norm=na_out_norm,
             mem_norm=mem_norm, mem_wkv=mem_wkv, mem_q_norm=mem_q_norm, mem_k_norm=mem_k_norm, mem_out_norm=mem_out_norm,
             w_out=w_out, ffn_norm=ffn_norm, ffn_w13=ffn_w13, ffn_w2=ffn_w2)
    mom = dict(attn_norm=m_attn_norm, w_in=m_w_in, gla_wg2_f=m_gla_wg2_f, gla_bg_f=m_gla_bg_f, gla_wg2_b=m_gla_wg2_b,
               gla_bg_b=m_gla_bg_b, gla_out_norm=m_gla_out_norm, na_q_norm=m_na_q_norm, na_k_norm=m_na_k_norm, na_rpb=m_na_rpb,
               na_out_norm=m_na_out_norm, mem_norm=m_mem_norm, mem_wkv=m_mem_wkv, mem_q_norm=m_mem_q_norm,
               mem_k_norm=m_mem_k_norm, mem_out_norm=m_mem_out_norm, w_out=m_w_out, ffn_norm=m_ffn_norm, ffn_w13=m_ffn_w13,
               ffn_w2=m_ffn_w2)
    var = dict(attn_norm=v_attn_norm, w_in=v_w_in, gla_wg2_f=v_gla_wg2_f, gla_bg_f=v_gla_bg_f, gla_wg2_b=v_gla_wg2_b,
               gla_bg_b=v_gla_bg_b, gla_out_norm=v_gla_out_norm, na_q_norm=v_na_q_norm, na_k_norm=v_na_k_norm, na_rpb=v_na_rpb,
               na_out_norm=v_na_out_norm, mem_norm=v_mem_norm, mem_wkv=v_mem_wkv, mem_q_norm=v_mem_q_norm,
               mem_k_norm=v_mem_k_norm, mem_out_norm=v_mem_out_norm, w_out=v_w_out, ffn_norm=v_ffn_norm, ffn_w13=v_ffn_w13,
               ffn_w2=v_ffn_w2)
    depth = attn_norm.shape[0]
    T = x.shape[1]
    xs, mem0, tgt = x.reshape(T, D_MODEL), mem.reshape(MEM_TOK, D_MODEL), loss_target.reshape(T, D_MODEL)
    c_idx = lax.axis_index("c").astype(jnp.int32).reshape(1)

    gates = gather_gate_weights(w, depth)
    send = [wire_shards({n: w[n][l].astype(BF16) for n in BIG}) for l in range(depth)]
    P = [{n: w[n][l] for n in REPLICATED} for l in range(depth)]

    ready = {**weights_from_gathered(BIG, all_gather(send[0], "l0_gather_weights")), **gates[0]}
    arriving = None
    W, saved = [], []
    h = xs
    for l in range(depth):
        h, sv, w_l, ready, arriving = layer_fwd(h, mem0, P[l], ready, l, arriving, send[l + 1] if l + 1 < depth else None)
        W.append(w_l)
        saved.append(sv)
        if ready is not None:
            ready = {**ready, **gates[l + 1]}
    dy, dy_b, lsum = loss_bwd(h, tgt, "loss")
    loss = lax.psum(lsum[0, 0], ("x", "y", "c")) * (0.5 / D_MODEL)

    g_shard, g_small = [None] * depth, [None] * depth
    late = None
    for l in range(depth - 1, -1, -1):
        rider = chip_exchange_rider(late) if late is not None else None
        dy, dy_b, g_shard[l], g_small[l], landed, late = layer_bwd(dy, dy_b, mem0, P[l], W[l], saved[l], l, c_idx, rider)
        if landed is not None:
            g_shard[l + 1].update(reduce_scatter_end(RS_LATE, landed, f"l{l + 1}_b_"))
        saved[l] = None
    g_shard[0].update(reduce_scatter_end(RS_LATE, chip_exchange(late, "l0_rs_chip_exchange"), "l0_b_"))
    grad_x = dy.reshape(x.shape)

    small_names = REPLICATED + GATE_W
    small = jnp.concatenate([g_small[l][n].reshape(-1) for l in range(depth) for n in small_names])
    n_small = small.shape[0]
    rows = -(-n_small // 1024) * 8
    small = jnp.pad(small, (0, rows * 128 - n_small)).reshape(rows, 128)
    small = sum_slots(all_gather([small], "gather_small_grads")[0], "sum_small_grads").reshape(-1)
    grads, off = {}, 0
    per_layer = {n: [] for n in small_names}
    my_cols = (4 * lax.axis_index("x") + 2 * lax.axis_index("y") + lax.axis_index("c")) * (GLA_DK // N_DEV)
    for l in range(depth):
        for n in small_names:
            shp = (GLA_RANK, GLA_DK) if n in GATE_W else w[n].shape[1:]
            g = small[off:off + int(np.prod(shp))].reshape(shp)
            off += int(np.prod(shp))
            per_layer[n].append(lax.dynamic_slice_in_dim(g, my_cols, GLA_DK // N_DEV, axis=1) if n in GATE_W else g)
    for n in small_names:
        grads[n] = jnp.stack(per_layer[n])
    for n in BIG:
        grads[n] = jnp.stack([g_shard[l][n] for l in range(depth)])

    delta, new_m, new_v = {}, {}, {}
    for n in WEIGHTS:
        shp = w[n].shape
        two_d = (shp[0], int(np.prod(shp[1:]))) if n in REPLICATED else (int(np.prod(shp[:-1])), shp[-1])
        d_, m_, v_ = adamw(w[n].reshape(two_d), grads[n].reshape(two_d), mom[n].reshape(two_d), var[n].reshape(two_d),
                           "adamw_" + n)
        delta[n], new_m[n], new_v[n] = d_.reshape(shp), m_.reshape(shp), v_.reshape(shp)

    return (loss, grad_x, *[grads[n] for n in WEIGHTS], *[delta[n] for n in WEIGHTS], *[new_m[n] for n in WEIGHTS],
            *[new_v[n] for n in WEIGHTS])
```

```python
import functools

import numpy as np
import jax
import jax.numpy as jnp
from jax import lax
from jax.experimental import pallas as pl
from jax.experimental.pallas import tpu as pltpu

F32, BF16 = jnp.float32, jnp.bfloat16
HI = lax.Precision.HIGHEST
SDS = jax.ShapeDtypeStruct
MESH = pl.DeviceIdType.MESH

D_MODEL = 2048
GRID_W = 64
GLA_H, GLA_HK, GLA_HV, GLA_RANK, GLA_TAU, GLA_C = 4, 128, 256, 16, 16.0, 64
GLA_DK, GLA_DV = GLA_H * GLA_HK, GLA_H * GLA_HV
NA_H, NA_HD, NA_ROWS, NA_COLS = 8, 64, 8, 16
NA_W = NA_H * NA_HD
MEM_H, MEM_HD, MEM_TOK = 4, 128, 256
MEM_W = MEM_H * MEM_HD
D_FF = 5632
IN_COLS = 5152
RMS_EPS = 1e-6
ADAM_LR, ADAM_B1, ADAM_B2, ADAM_EPS, ADAM_WD, ADAM_STEP = 0.001, 0.9, 0.999, 1e-08, 0.01, 10
N_DEV = 8

PC = 5376
COL_R, COL_NQ, COL_NK, COL_NV, COL_MQ, COL_TAIL = 2048, 3072, 3584, 4096, 4608, 5120
ORIG_GATE0 = 3072
ORIG_AFTER_GATE = 3104

GLA_G = 8
VMEM_LIMIT = 56 * 1024 * 1024
NEG = -1e30


def _cparams(sem):
    return pltpu.CompilerParams(dimension_semantics=sem, vmem_limit_bytes=VMEM_LIMIT)


def _fit(n, pref, unit=128):
    if n <= pref:
        return n
    t = (pref // unit) * unit
    while t >= unit:
        if n % t == 0:
            return t
        t -= unit
    return n


def _bdot(a, b, dims):
    return lax.dot_general(a.astype(BF16), b.astype(BF16), (dims, ((), ())), preferred_element_type=F32)


NN, NT, TN = ((1,), (0,)), ((1,), (1,)), ((0,), (0,))


@functools.partial(jax.custom_vjp, nondiff_argnums=(2,))
def _bdot_vjp(a, b, dims):
    return _bdot(a, b, dims)


def _bdot_vjp_fwd(a, b, dims):
    return _bdot(a, b, dims), (a, b)


def _bdot_vjp_bwd(dims, res, ct):
    a, b = res
    if dims == NN:
        return _bdot(ct, b, NT), _bdot(a, ct, TN)
    if dims == NT:
        return _bdot(ct, b, NN), _bdot(ct, a, TN)
    return _bdot(b, ct, NT), _bdot(a, ct, NN)


_bdot_vjp.defvjp(_bdot_vjp_fwd, _bdot_vjp_bwd)


def _split_dot(c, x, dims):
    hi = x.astype(BF16)
    lo = (x - hi.astype(F32)).astype(BF16)
    cb = c.astype(BF16)
    return (lax.dot_general(cb, hi, (dims, ((), ())), preferred_element_type=F32)
            + lax.dot_general(cb, lo, (dims, ((), ())), preferred_element_type=F32))


@jax.custom_vjp
def _tri_sum(cmat, x):
    return _split_dot(cmat, x, NN)


def _tri_sum_fwd(cmat, x):
    return _split_dot(cmat, x, NN), cmat


def _tri_sum_bwd(cmat, ct):
    return jnp.zeros_like(cmat), _split_dot(cmat, ct, TN)


_tri_sum.defvjp(_tri_sum_fwd, _tri_sum_bwd)


def _call_with_rider(body, grid, in_specs, out_specs, out_shape, scratch, sem, name, args, rider):
    if rider is None:
        return pl.pallas_call(body, grid=grid, in_specs=in_specs, out_specs=out_specs, out_shape=out_shape,
                              scratch_shapes=scratch, compiler_params=_cparams(sem), name=name)(*args), None
    outs_l = list(out_shape) if isinstance(out_shape, (list, tuple)) else [out_shape]
    specs_l = list(out_specs) if isinstance(out_specs, (list, tuple)) else [out_specs]
    ni, no, ns = len(in_specs), len(outs_l), len(scratch)
    nri, nro = len(rider.inputs), len(rider.out_shapes)
    any_spec = pl.BlockSpec(memory_space=pl.ANY)

    def wrapped(*refs):
        ins, rin = refs[:ni], refs[ni:ni + nri]
        o0 = ni + nri
        outs, rout = refs[o0:o0 + no], refs[o0 + no:o0 + no + nro]
        s0 = o0 + no + nro
        scr, rsem = refs[s0:s0 + ns], refs[s0 + ns:]
        ids = [pl.program_id(d) for d in range(len(grid))]
        first = functools.reduce(jnp.logical_and, [i == 0 for i in ids])
        last = functools.reduce(jnp.logical_and, [i == g - 1 for i, g in zip(ids, grid)])

        @pl.when(first)
        def _():
            rider.start(rin, rout, rsem)

        body(*ins, *outs, *scr)

        @pl.when(last)
        def _():
            rider.finish(rin, rout, rsem)

    res = pl.pallas_call(
        wrapped, grid=grid, in_specs=list(in_specs) + [any_spec] * nri, out_specs=specs_l + [any_spec] * nro,
        out_shape=outs_l + rider.out_shapes, scratch_shapes=list(scratch) + rider.scratch,
        input_output_aliases={ni + i: no + j for i, j in rider.aliases.items()},
        compiler_params=_cparams(("arbitrary",) * len(grid)), name=name)(*args, *rider.inputs)
    main = res[:no]
    return (main if isinstance(out_shape, (list, tuple)) else main[0]), list(res[no:])


def matmul(a, b, mode, name, out_dtype=F32, res=None, tm=1024, tn=1024, tk=512, exact=False,
           a_halves=False, b_halves=False, b_blocked=False, out_blocked=False, rider=None):
    if mode == "nn":
        M, K = a.shape
        N = b.shape[0] * b.shape[2] if b_blocked else b.shape[1]
        if b_blocked:
            tn = b.shape[2]
    elif mode == "nt":
        M, K = (a.shape[1], 2 * a.shape[2]) if a_halves else a.shape
        N = b.shape[1] if b_blocked else b.shape[0]
        if b_blocked:
            tk = b.shape[2]
    else:
        K, M = a.shape
        N = 2 * b.shape[2] if b_halves else b.shape[1]
    tm, tn, tk = _fit(M, tm, 8 if mode != "tn" else 128), _fit(N, tn), _fit(K, tk, 128 if mode != "tn" else 16)
    nk = K // tk
    dims = {"nn": NN, "nt": NT, "tn": TN}[mode]

    def body(*refs):
        if res is None:
            a_ref, b_ref, o_ref = refs[:3]
            r_ref = None
            acc = refs[3] if nk > 1 else None
        else:
            a_ref, b_ref, r_ref, o_ref = refs[:4]
            acc = refs[4] if nk > 1 else None
        def product():
            if exact:
                return lax.dot_general(a_ref[...], b_ref[...], (dims, ((), ())), preferred_element_type=F32, precision=HI)
            return _bdot(a_ref[...], b_ref[...], dims)

        def finish(val):
            if r_ref is not None:
                val = val + r_ref[...]
            o_ref[...] = val.astype(out_dtype)

        if nk == 1:
            finish(product())
        else:
            kk = pl.program_id(2)

            @pl.when(kk == 0)
            def _():
                acc[...] = jnp.zeros_like(acc)

            acc[...] += product()

            @pl.when(kk == nk - 1)
            def _():
                finish(acc[...])

    if mode == "tn":
        a_spec = pl.BlockSpec((tk, tm), lambda i, j, k: (k, i))
    elif a_halves:
        nh = K // 2 // tk
        a_spec = pl.BlockSpec((None, tm, tk), lambda i, j, k: (k // nh, i, k % nh))
    else:
        a_spec = pl.BlockSpec((tm, tk), lambda i, j, k: (i, k))
    if b_blocked:
        b_spec = (pl.BlockSpec((None, tk, tn), lambda i, j, k: (j, k, 0)) if mode == "nn"
                  else pl.BlockSpec((None, tn, tk), lambda i, j, k: (k, j, 0)))
    elif b_halves:
        nh = N // 2 // tn
        b_spec = pl.BlockSpec((None, tk, tn), lambda i, j, k: (j // nh, k, j % nh))
    elif mode == "nt":
        b_spec = pl.BlockSpec((tn, tk), lambda i, j, k: (j, k))
    else:
        b_spec = pl.BlockSpec((tk, tn), lambda i, j, k: (k, j))
    if out_blocked:
        o_spec, o_shape = pl.BlockSpec((None, tm, tn), lambda i, j, k: (j, i, 0)), SDS((N // tn, M, tn), out_dtype)
    else:
        o_spec, o_shape = pl.BlockSpec((tm, tn), lambda i, j, k: (i, j)), SDS((M, N), out_dtype)
    in_specs, args = [a_spec, b_spec], [a, b]
    if res is not None:
        in_specs.append(o_spec)
        args.append(res)
    out, carried = _call_with_rider(
        body, (M // tm, N // tn, nk), in_specs, o_spec, o_shape, [pltpu.VMEM((tm, tn), F32)] if nk > 1 else [],
        ("parallel", "parallel", "arbitrary"), name, args, rider)
    return out if rider is None else (out, carried)


def rms_fwd(x, g, name, rider=None):
    T, D = x.shape
    tm = _fit(T, 512, 8)

    def body(x_ref, g_ref, o_ref):
        xv = x_ref[...]
        r = lax.rsqrt(jnp.mean(xv * xv, axis=-1, keepdims=True) + RMS_EPS)
        o_ref[...] = (xv * r * g_ref[...]).astype(BF16)

    out, carried = _call_with_rider(
        body, (T // tm,), [pl.BlockSpec((tm, D), lambda i: (i, 0)), pl.BlockSpec((1, D), lambda i: (0, 0))],
        pl.BlockSpec((tm, D), lambda i: (i, 0)), SDS((T, D), BF16), [], ("parallel",), name, [x, g], rider)
    return out if rider is None else (out, carried)


def rms_bwd(x, g, dy, dres, name):
    T, D = x.shape
    tm = _fit(T, 256, 8)
    has_res = dres is not None

    def body(*refs):
        if has_res:
            x_ref, g_ref, dy_ref, dres_ref, dx_ref, dxb_ref, dg_ref = refs
        else:
            x_ref, g_ref, dy_ref, dx_ref, dxb_ref, dg_ref = refs
        xv, dyv = x_ref[...], dy_ref[...]
        r = lax.rsqrt(jnp.mean(xv * xv, axis=-1, keepdims=True) + RMS_EPS)
        xh = xv * r
        dxh = dyv * g_ref[...]
        dx = r * (dxh - xh * jnp.mean(dxh * xh, axis=-1, keepdims=True))
        if has_res:
            dx = dx + dres_ref[...]
        dx_ref[...] = dx
        dxb_ref[...] = dx.astype(BF16)

        @pl.when(pl.program_id(0) == 0)
        def _():
            dg_ref[...] = jnp.zeros_like(dg_ref)

        dg_ref[...] += jnp.sum(dyv * xh, axis=0, keepdims=True)

    row = pl.BlockSpec((tm, D), lambda i: (i, 0))
    vec = pl.BlockSpec((1, D), lambda i: (0, 0))
    args = [x, g, dy] + ([dres] if has_res else [])
    return pl.pallas_call(
        body, grid=(T // tm,), in_specs=[row, vec, row] + ([row] if has_res else []),
        out_specs=[row, row, vec], out_shape=[SDS((T, D), F32), SDS((T, D), BF16), SDS((1, D), F32)],
        compiler_params=_cparams(("arbitrary",)), name=name)(*args)


def ffn_up_swiglu(h, w13b, name, rider=None):
    T, D = h.shape
    nb, _, tb = w13b.shape
    nh = nb // 2
    tm = _fit(T, 512, 8)

    def body(h_ref, wg_ref, wu_ref, gu_ref, a_ref):
        hv = h_ref[...]
        gv = _bdot(hv, wg_ref[...], NN)
        uv = _bdot(hv, wu_ref[...], NN)
        gu_ref[0] = gv.astype(BF16)
        gu_ref[1] = uv.astype(BF16)
        a_ref[...] = (gv * jax.nn.sigmoid(gv) * uv).astype(BF16)

    (gu, act), carried = _call_with_rider(
        body, (nh, T // tm),
        [pl.BlockSpec((tm, D), lambda j, i: (i, 0)), pl.BlockSpec((None, D, tb), lambda j, i: (j, 0, 0)),
         pl.BlockSpec((None, D, tb), lambda j, i: (j + nh, 0, 0))],
        [pl.BlockSpec((2, tm, tb), lambda j, i: (0, i, j)), pl.BlockSpec((tm, tb), lambda j, i: (i, j))],
        [SDS((2, T, nh * tb), BF16), SDS((T, nh * tb), BF16)], [], ("parallel", "parallel"), name, [h, w13b, w13b], rider)
    return gu, act, carried


def ffn_down_bwd(dy_b, w2, gu, name):
    T, D = dy_b.shape
    Fh = w2.shape[0]
    tm, tf = _fit(T, 1024, 8), _fit(Fh, 512)
    nf = Fh // tf

    def body(dy_ref, w_ref, g_ref, u_ref, o_ref):
        dav = _bdot(dy_ref[...], w_ref[...], NT)
        gv, uv = g_ref[...].astype(F32), u_ref[...].astype(F32)
        sg = jax.nn.sigmoid(gv)
        o_ref[0] = (dav * uv * (sg * (1.0 + gv * (1.0 - sg)))).astype(BF16)
        o_ref[1] = (dav * gv * sg).astype(BF16)

    return pl.pallas_call(
        body, grid=(T // tm, nf),
        in_specs=[pl.BlockSpec((tm, D), lambda i, j: (i, 0)), pl.BlockSpec((tf, D), lambda i, j: (j, 0)),
                  pl.BlockSpec((None, tm, tf), lambda i, j: (0, i, j)), pl.BlockSpec((None, tm, tf), lambda i, j: (1, i, j))],
        out_specs=pl.BlockSpec((2, tm, tf), lambda i, j: (0, i, j)), out_shape=SDS((2, T, Fh), BF16),
        compiler_params=_cparams(("parallel", "parallel")), name=name)(dy_b, w2, gu, gu)


def loss_bwd(y, tgt, name):
    T, D = y.shape
    tm = _fit(T, 512, 8)

    def body(y_ref, t_ref, dy_ref, dyb_ref, l_ref):
        e = y_ref[...] - t_ref[...]
        dy = e * (1.0 / D)
        dy_ref[...] = dy
        dyb_ref[...] = dy.astype(BF16)

        @pl.when(pl.program_id(0) == 0)
        def _():
            l_ref[...] = jnp.zeros_like(l_ref)

        l_ref[...] += jnp.sum(jnp.sum(e * e, axis=1, keepdims=True), axis=0, keepdims=True)

    row = pl.BlockSpec((tm, D), lambda i: (i, 0))
    return pl.pallas_call(
        body, grid=(T // tm,), in_specs=[row, row], out_specs=[row, row, pl.BlockSpec((8, 128), lambda i: (0, 0))],
        out_shape=[SDS((T, D), F32), SDS((T, D), BF16), SDS((8, 128), F32)],
        compiler_params=_cparams(("arbitrary",)), name=name)(y, tgt)


def _log_sigmoid(z):
    return jnp.minimum(z, 0.0) - jnp.log(1.0 + jnp.exp(-jnp.abs(z)))


def _gla_group(q, k, v, tail, wg, bg, s, cmat, mask, backward_dir):
    n = q.shape[0] // GLA_C
    z = _bdot_vjp(tail, wg, NN) + bg
    la = _log_sigmoid(z) * (1.0 / GLA_TAU)
    cum = _tri_sum(cmat, la)
    chunks = [slice(g * GLA_C, (g + 1) * GLA_C) for g in range(n)]
    last = [jnp.sum(la[sl], axis=0, keepdims=True) for sl in chunks]
    last_rows = jnp.concatenate([jnp.broadcast_to(t, (GLA_C, t.shape[1])) for t in last], axis=0)
    q_e = q * jnp.exp(cum) * (GLA_HK ** -0.5)
    k_e = k * jnp.exp(-cum)
    k_end = k * jnp.exp(last_rows - cum)
    sc = jnp.where(mask > 0.5, _bdot_vjp(q_e, k_e, NT), 0.0)
    o = _bdot_vjp(sc, v, NN)
    o_inter = [None] * n
    for g in (range(n - 1, -1, -1) if backward_dir else range(n)):
        o_inter[g] = _bdot_vjp(q_e[chunks[g]], s, NN)
        s = s * jnp.transpose(jnp.exp(last[g])) + _bdot_vjp(k_end[chunks[g]], v[chunks[g]], TN)
    return o + jnp.concatenate(o_inter, axis=0), s


def _gla_consts(backward_dir, GC):
    i = np.arange(GC)
    same = (i[:, None] // GLA_C) == (i[None, :] // GLA_C)
    if backward_dir:
        return (same & (i[None, :] >= i[:, None])).astype(np.float32), (same & (i[None, :] > i[:, None])).astype(np.float32)
    tri = (same & (i[None, :] <= i[:, None])).astype(np.float32)
    return tri, tri


GLA_HB = 2


def _gla_in_specs(GC, nmap):
    return [
        pl.BlockSpec((GC, GLA_HB * GLA_HK), lambda h, n: (nmap(n), h)),
        pl.BlockSpec((GC, GLA_HB * GLA_HK), lambda h, n: (nmap(n), GLA_H // GLA_HB + h)),
        pl.BlockSpec((GC, GLA_HB * GLA_HV), lambda h, n: (nmap(n), 2 * GLA_DK // (GLA_HB * GLA_HV) + h)),
        pl.BlockSpec((GC, 128), lambda h, n: (nmap(n), COL_TAIL // 128)),
        pl.BlockSpec((GLA_HB, 128, 128), lambda h, n: (h, 0, 0)),
        pl.BlockSpec((GLA_HB, 1, 128), lambda h, n: (h, 0, 0)),
        pl.BlockSpec((GC, GC), lambda h, n: (0, 0)),
        pl.BlockSpec((GC, GC), lambda h, n: (0, 0)),
    ]


def _head_cols(ref, hh, width):
    return ref[:, hh * width:(hh + 1) * width].astype(F32)


def gla_fwd(proj, wgpad, bg, backward_dir, name):
    T = proj.shape[0]
    GC = min(GLA_G * GLA_C, T)
    NG = T // GC
    cmat, mask = _gla_consts(backward_dir, GC)
    nmap = (lambda n: NG - 1 - n) if backward_dir else (lambda n: n)

    def body(q_ref, k_ref, v_ref, t_ref, wg_ref, bg_ref, c_ref, m_ref, o_ref, ss_ref, s_scr):
        @pl.when(pl.program_id(1) == 0)
        def _():
            s_scr[...] = jnp.zeros_like(s_scr)

        tail, cm, mk = t_ref[...].astype(F32), c_ref[...], m_ref[...]
        for hh in range(GLA_HB):
            s0 = s_scr[hh]
            ss_ref[hh] = s0
            o, s1 = _gla_group(_head_cols(q_ref, hh, GLA_HK), _head_cols(k_ref, hh, GLA_HK), _head_cols(v_ref, hh, GLA_HV),
                               tail, wg_ref[hh], bg_ref[hh], s0, cm, mk, backward_dir)
            o_ref[:, hh * GLA_HV:(hh + 1) * GLA_HV] = o
            s_scr[hh] = s1

    return pl.pallas_call(
        body, grid=(GLA_H // GLA_HB, NG), in_specs=_gla_in_specs(GC, nmap),
        out_specs=[pl.BlockSpec((GC, GLA_HB * GLA_HV), lambda h, n: (nmap(n), h)),
                   pl.BlockSpec((GLA_HB, None, GLA_HK, GLA_HV), lambda h, n: (h, nmap(n), 0, 0))],
        out_shape=[SDS((T, GLA_DV), F32), SDS((GLA_H, NG, GLA_HK, GLA_HV), F32)],
        scratch_shapes=[pltpu.VMEM((GLA_HB, GLA_HK, GLA_HV), F32)],
        compiler_params=_cparams(("parallel", "arbitrary")), name=name)(proj, proj, proj, proj, wgpad, bg, cmat, mask)


def gla_bwd(proj, wgpad, bg, ssave, do, backward_dir, name):
    T = proj.shape[0]
    GC = min(GLA_G * GLA_C, T)
    NG = T // GC
    cmat, mask = _gla_consts(backward_dir, GC)
    nmap = (lambda n: n) if backward_dir else (lambda n: NG - 1 - n)

    def body(q_ref, k_ref, v_ref, t_ref, wg_ref, bg_ref, c_ref, m_ref, ss_ref, do_ref,
             dq_ref, dk_ref, dv_ref, dt_ref, dwg_ref, dbg_ref, ds_scr):
        @pl.when(pl.program_id(1) == 0)
        def _():
            ds_scr[...] = jnp.zeros_like(ds_scr)
            dwg_ref[...] = jnp.zeros_like(dwg_ref)
            dbg_ref[...] = jnp.zeros_like(dbg_ref)

        tail, cm, mk = t_ref[...].astype(F32), c_ref[...], m_ref[...]
        fn = lambda q, k, v, t, wg, b, s: _gla_group(q, k, v, t, wg, b, s, cm, mk, backward_dir)
        for hh in range(GLA_HB):
            _, vjp = jax.vjp(fn, _head_cols(q_ref, hh, GLA_HK), _head_cols(k_ref, hh, GLA_HK), _head_cols(v_ref, hh, GLA_HV),
                             tail, wg_ref[hh], bg_ref[hh], ss_ref[hh])
            dq, dk, dv, dt, dwg, dbg, ds = vjp((_head_cols(do_ref, hh, GLA_HV), ds_scr[hh]))
            dq_ref[:, hh * GLA_HK:(hh + 1) * GLA_HK] = dq.astype(BF16)
            dk_ref[:, hh * GLA_HK:(hh + 1) * GLA_HK] = dk.astype(BF16)
            dv_ref[:, hh * GLA_HV:(hh + 1) * GLA_HV] = dv.astype(BF16)
            dt_ref[hh] = dt.astype(BF16)
            dwg_ref[hh] += dwg
            dbg_ref[hh] += dbg
            ds_scr[hh] = ds

    in_specs = _gla_in_specs(GC, nmap) + [
        pl.BlockSpec((GLA_HB, None, GLA_HK, GLA_HV), lambda h, n: (h, nmap(n), 0, 0)),
        pl.BlockSpec((GC, GLA_HB * GLA_HV), lambda h, n: (nmap(n), h)),
    ]
    out_specs = [
        pl.BlockSpec((GC, GLA_HB * GLA_HK), lambda h, n: (nmap(n), h)),
        pl.BlockSpec((GC, GLA_HB * GLA_HK), lambda h, n: (nmap(n), h)),
        pl.BlockSpec((GC, GLA_HB * GLA_HV), lambda h, n: (nmap(n), h)),
        pl.BlockSpec((GLA_HB, GC, 128), lambda h, n: (h, nmap(n), 0)),
        pl.BlockSpec((GLA_HB, 128, 128), lambda h, n: (h, 0, 0)),
        pl.BlockSpec((GLA_HB, 1, 128), lambda h, n: (h, 0, 0)),
    ]
    out_shape = [SDS((T, GLA_DK), BF16), SDS((T, GLA_DK), BF16), SDS((T, GLA_DV), BF16), SDS((GLA_H, T, 128), BF16),
                 SDS((GLA_H, 128, 128), F32), SDS((GLA_H, 1, 128), F32)]
    return pl.pallas_call(
        body, grid=(GLA_H // GLA_HB, NG), in_specs=in_specs, out_specs=out_specs, out_shape=out_shape,
        scratch_shapes=[pltpu.VMEM((GLA_HB, GLA_HK, GLA_HV), F32)],
        compiler_params=_cparams(("parallel", "arbitrary")), name=name)(proj, proj, proj, proj, wgpad, bg, cmat, mask, ssave, do)


def _block_diag(width, hd):
    i = np.arange(width)
    return ((i[:, None] // hd) == (i[None, :] // hd)).astype(np.float32) / hd


def _norm_heads(t, hd):
    outs = []
    for h in range(t.shape[1] // hd):
        th = t[:, h * hd:(h + 1) * hd]
        outs.append(th * lax.rsqrt(jnp.mean(th * th, axis=-1, keepdims=True) + RMS_EPS))
    return jnp.concatenate(outs, axis=1)


def _norm_bd(t, bd):
    return t * lax.rsqrt(jnp.dot(t * t, bd, preferred_element_type=F32, precision=HI) + RMS_EPS)


def _pre_fn(nq, nk, mq, gq, gk, gm, bd):
    return _norm_bd(nq, bd) * gq, _norm_bd(nk, bd) * gk, _norm_heads(mq, MEM_HD) * gm


def pre_fwd(proj, gq, gk, gm, name):
    T = proj.shape[0]
    tm = _fit(T, 512, 8)
    bd = _block_diag(NA_W, NA_HD)

    def body(nq_ref, nk_ref, nv_ref, mq_ref, gq_ref, gk_ref, gm_ref, bd_ref, q_ref, k_ref, v_ref, m_ref):
        qn, kn, mn = _pre_fn(nq_ref[...].astype(F32), nk_ref[...].astype(F32), mq_ref[...].astype(F32),
                             gq_ref[...], gk_ref[...], gm_ref[...], bd_ref[...])
        q_ref[...] = qn.astype(BF16)
        k_ref[...] = kn.astype(BF16)
        v_ref[...] = nv_ref[...].astype(BF16)
        m_ref[...] = mn.astype(BF16)

    col = lambda c0: pl.BlockSpec((tm, 512), lambda i: (i, c0 // 512))
    vec = pl.BlockSpec((1, 512), lambda i: (0, 0))
    row = pl.BlockSpec((tm, 512), lambda i: (i, 0))
    return pl.pallas_call(
        body, grid=(T // tm,),
        in_specs=[col(COL_NQ), col(COL_NK), col(COL_NV), col(COL_MQ), vec, vec, vec, pl.BlockSpec((NA_W, NA_W), lambda i: (0, 0))],
        out_specs=[row] * 4, out_shape=[SDS((T, 512), BF16)] * 4,
        compiler_params=_cparams(("parallel",)), name=name)(proj, proj, proj, proj, gq, gk, gm, bd)


def pre_bwd(proj, gq, gk, gm, d_qn, d_kn, d_nv, d_mn, dq_f, dq_b, dk_f, dk_b, dv_f, dv_b, d_r, dt_f, dt_b, name):
    T = proj.shape[0]
    tm = _fit(T, 256, 8)
    bd = _block_diag(NA_W, NA_HD)

    def body(nq_ref, nk_ref, mq_ref, gq_ref, gk_ref, gm_ref, bd_ref, dqn_ref, dkn_ref, dnv_ref, dmn_ref,
             dqf_ref, dqb_ref, dkf_ref, dkb_ref, dvf_ref, dvb_ref, dr_ref, dtf_ref, dtb_ref,
             o_ref, dgq_ref, dgk_ref, dgm_ref):
        bdv = bd_ref[...]
        fn = lambda a, b, c, x, y, z: _pre_fn(a, b, c, x, y, z, bdv)
        _, vjp = jax.vjp(fn, nq_ref[...].astype(F32), nk_ref[...].astype(F32), mq_ref[...].astype(F32),
                         gq_ref[...], gk_ref[...], gm_ref[...])
        d_nq, d_nk, d_mq, dgq, dgk, dgm = vjp((dqn_ref[...], dkn_ref[...], dmn_ref[...]))
        both = lambda f_ref, b_ref: (f_ref[...].astype(F32) + b_ref[...].astype(F32)).astype(BF16)
        o_ref[:, 0:512] = both(dqf_ref, dqb_ref)
        o_ref[:, 512:1024] = both(dkf_ref, dkb_ref)
        o_ref[:, 1024:2048] = both(dvf_ref, dvb_ref)
        o_ref[:, COL_R:COL_R + 1024] = dr_ref[...].astype(BF16)
        o_ref[:, COL_NQ:COL_NQ + 512] = d_nq.astype(BF16)
        o_ref[:, COL_NK:COL_NK + 512] = d_nk.astype(BF16)
        o_ref[:, COL_NV:COL_NV + 512] = dnv_ref[...].astype(BF16)
        o_ref[:, COL_MQ:COL_MQ + 512] = d_mq.astype(BF16)
        dt = dtf_ref[0].astype(F32) + dtb_ref[0].astype(F32)
        for h in range(1, GLA_H):
            dt = dt + dtf_ref[h].astype(F32) + dtb_ref[h].astype(F32)
        o_ref[:, COL_TAIL:COL_TAIL + 128] = dt.astype(BF16)
        o_ref[:, COL_TAIL + 128:PC] = jnp.zeros((tm, PC - COL_TAIL - 128), BF16)

        @pl.when(pl.program_id(0) == 0)
        def _():
            dgq_ref[...] = jnp.zeros_like(dgq_ref)
            dgk_ref[...] = jnp.zeros_like(dgk_ref)
            dgm_ref[...] = jnp.zeros_like(dgm_ref)

        dgq_ref[...] += dgq
        dgk_ref[...] += dgk
        dgm_ref[...] += dgm

    col = lambda c0: pl.BlockSpec((tm, 512), lambda i: (i, c0 // 512))
    vec = pl.BlockSpec((1, 512), lambda i: (0, 0))
    r512 = pl.BlockSpec((tm, 512), lambda i: (i, 0))
    r1024 = pl.BlockSpec((tm, 1024), lambda i: (i, 0))
    tl = pl.BlockSpec((GLA_H, tm, 128), lambda i: (0, i, 0))
    in_specs = [col(COL_NQ), col(COL_NK), col(COL_MQ), vec, vec, vec, pl.BlockSpec((NA_W, NA_W), lambda i: (0, 0)),
                r512, r512, r512, r512, r512, r512, r512, r512, r1024, r1024, r1024, tl, tl]
    return pl.pallas_call(
        body, grid=(T // tm,), in_specs=in_specs,
        out_specs=[pl.BlockSpec((tm, PC), lambda i: (i, 0)), vec, vec, vec],
        out_shape=[SDS((T, PC), BF16), SDS((1, 512), F32), SDS((1, 512), F32), SDS((1, 512), F32)],
        compiler_params=_cparams(("arbitrary",)), name=name)(
            proj, proj, proj, gq, gk, gm, bd, d_qn, d_kn, d_nv, d_mn, dq_f, dq_b, dk_f, dk_b, dv_f, dv_b, d_r, dt_f, dt_b)


def _post_fn(o_f, o_b, r, o_na, o_mem, g_gla, g_na, g_mem, bd):
    y_gla = _norm_heads(o_f + o_b, GLA_HV) * g_gla * (r * jax.nn.sigmoid(r))
    y_na = _norm_bd(o_na, bd) * g_na
    y_mem = _norm_heads(o_mem, MEM_HD) * g_mem
    return jnp.concatenate([y_gla, y_na, y_mem], axis=1)


def post_fwd(o_f, o_b, proj, o_na, o_mem, g_gla, g_na, g_mem, name):
    T = proj.shape[0]
    tm = _fit(T, 256, 8)
    bd = _block_diag(NA_W, NA_HD)

    def body(of_ref, ob_ref, r_ref, ona_ref, omem_ref, gg_ref, gn_ref, gm_ref, bd_ref, y_ref):
        y_ref[...] = _post_fn(of_ref[...], ob_ref[...], r_ref[...].astype(F32), ona_ref[...], omem_ref[...],
                              gg_ref[...], gn_ref[...], gm_ref[...], bd_ref[...]).astype(BF16)

    r1024 = pl.BlockSpec((tm, 1024), lambda i: (i, 0))
    r512 = pl.BlockSpec((tm, 512), lambda i: (i, 0))
    in_specs = [r1024, r1024, pl.BlockSpec((tm, 1024), lambda i: (i, COL_R // 1024)), r512, r512,
                pl.BlockSpec((1, 1024), lambda i: (0, 0)), pl.BlockSpec((1, 512), lambda i: (0, 0)),
                pl.BlockSpec((1, 512), lambda i: (0, 0)), pl.BlockSpec((NA_W, NA_W), lambda i: (0, 0))]
    return pl.pallas_call(
        body, grid=(T // tm,), in_specs=in_specs, out_specs=pl.BlockSpec((tm, D_MODEL), lambda i: (i, 0)),
        out_shape=SDS((T, D_MODEL), BF16), compiler_params=_cparams(("parallel",)), name=name)(
            o_f, o_b, proj, o_na, o_mem, g_gla, g_na, g_mem, bd)


def post_bwd(o_f, o_b, proj, o_na, o_mem, g_gla, g_na, g_mem, dy, name):
    T = proj.shape[0]
    tm = _fit(T, 256, 8)
    bd = _block_diag(NA_W, NA_HD)

    def body(of_ref, ob_ref, r_ref, ona_ref, omem_ref, gg_ref, gn_ref, gm_ref, bd_ref, dy_ref,
             do_ref, dr_ref, dna_ref, dmem_ref, dgg_ref, dgn_ref, dgm_ref):
        bdv = bd_ref[...]
        fn = lambda o, r, a, m, x, y, z: _post_fn(o, 0.0, r, a, m, x, y, z, bdv)
        _, vjp = jax.vjp(fn, of_ref[...] + ob_ref[...], r_ref[...].astype(F32), ona_ref[...], omem_ref[...],
                         gg_ref[...], gn_ref[...], gm_ref[...])
        d_o, d_r, d_na, d_mem, dgg, dgn, dgm = vjp(dy_ref[...])
        do_ref[...] = d_o.astype(BF16)
        dr_ref[...] = d_r.astype(BF16)
        dna_ref[...] = d_na.astype(BF16)
        dmem_ref[...] = d_mem.astype(BF16)

        @pl.when(pl.program_id(0) == 0)
        def _():
            dgg_ref[...] = jnp.zeros_like(dgg_ref)
            dgn_ref[...] = jnp.zeros_like(dgn_ref)
            dgm_ref[...] = jnp.zeros_like(dgm_ref)

        dgg_ref[...] += dgg
        dgn_ref[...] += dgn
        dgm_ref[...] += dgm

    r1024 = pl.BlockSpec((tm, 1024), lambda i: (i, 0))
    r512 = pl.BlockSpec((tm, 512), lambda i: (i, 0))
    v1024 = pl.BlockSpec((1, 1024), lambda i: (0, 0))
    v512 = pl.BlockSpec((1, 512), lambda i: (0, 0))
    in_specs = [r1024, r1024, pl.BlockSpec((tm, 1024), lambda i: (i, COL_R // 1024)), r512, r512, v1024, v512, v512,
                pl.BlockSpec((NA_W, NA_W), lambda i: (0, 0)), pl.BlockSpec((tm, D_MODEL), lambda i: (i, 0))]
    return pl.pallas_call(
        body, grid=(T // tm,), in_specs=in_specs, out_specs=[r1024, r1024, r512, r512, v1024, v512, v512],
        out_shape=[SDS((T, 1024), BF16), SDS((T, 1024), BF16), SDS((T, 512), BF16), SDS((T, 512), BF16),
                   SDS((1, 1024), F32), SDS((1, 512), F32), SDS((1, 512), F32)],
        compiler_params=_cparams(("arbitrary",)), name=name)(o_f, o_b, proj, o_na, o_mem, g_gla, g_na, g_mem, bd, dy)


NA_RB = 8


def _na_row_scores(q_ref, k_ref, v_ref, tb_ref, rb, j, n_rows):
    r = rb * NA_RB + j
    rs = jnp.clip(r - NA_ROWS // 2, 0, n_rows - NA_ROWS)
    dr0 = rs - r + (NA_ROWS - 1)
    tok = pl.ds(pl.multiple_of(rs * GRID_W, GRID_W), NA_ROWS * GRID_W)
    q = q_ref[j * GRID_W:(j + 1) * GRID_W, :]
    kk, vv = k_ref[tok, :], v_ref[tok, :]
    bias = jnp.concatenate([tb_ref[dr0 + 2 * i] for i in range(NA_ROWS // 2)], axis=1)
    s = _bdot(q, kk, NT) * (NA_HD ** -0.5) + bias
    m = jnp.max(s, axis=1, keepdims=True)
    p = jnp.exp(s - m)
    l = jnp.sum(p, axis=1, keepdims=True)
    return q, kk, vv, p, l, tok, dr0


def natten_fwd(q, k, v, tb2, name):
    H, T, hd = q.shape
    n_rows = T // GRID_W
    rbt = NA_RB * GRID_W

    def body(q_ref, k_ref, v_ref, tb_ref, o_ref):
        rb = pl.program_id(1)
        for j in range(NA_RB):
            _, _, vv, p, l, _, _ = _na_row_scores(q_ref, k_ref, v_ref, tb_ref, rb, j, n_rows)
            o_ref[j * GRID_W:(j + 1) * GRID_W, :] = _bdot(p, vv, NN) / l

    whole = pl.BlockSpec((None, T, hd), lambda h, r: (h, 0, 0))
    blk = pl.BlockSpec((None, rbt, hd), lambda h, r: (h, r, 0))
    return pl.pallas_call(
        body, grid=(H, n_rows // NA_RB),
        in_specs=[blk, whole, whole, pl.BlockSpec((None, 2 * NA_ROWS - 2, GRID_W, 2 * GRID_W), lambda h, r: (h, 0, 0, 0))],
        out_specs=blk, out_shape=SDS((H, T, hd), F32),
        compiler_params=_cparams(("parallel", "arbitrary")), name=name)(q, k, v, tb2)


def natten_bwd(q, k, v, tb2, do, name):
    H, T, hd = q.shape
    n_rows = T // GRID_W
    rbt = NA_RB * GRID_W
    scale = NA_HD ** -0.5

    def body(q_ref, k_ref, v_ref, tb_ref, do_ref, dq_ref, dk_ref, dv_ref, dtb_ref):
        rb = pl.program_id(1)

        @pl.when(rb == 0)
        def _():
            dk_ref[...] = jnp.zeros_like(dk_ref)
            dv_ref[...] = jnp.zeros_like(dv_ref)
            dtb_ref[...] = jnp.zeros_like(dtb_ref)

        for j in range(NA_RB):
            qv, kk, vv, p, l, tok, dr0 = _na_row_scores(q_ref, k_ref, v_ref, tb_ref, rb, j, n_rows)
            p = p / l
            dov = do_ref[j * GRID_W:(j + 1) * GRID_W, :]
            dp = _bdot(dov, vv, NT)
            ds = p * (dp - jnp.sum(dp * p, axis=1, keepdims=True))
            dq_ref[j * GRID_W:(j + 1) * GRID_W, :] = _bdot(ds, kk, NN) * scale
            dk_ref[tok, :] += _bdot(ds, qv, TN) * scale
            dv_ref[tok, :] += _bdot(p, dov, TN)
            for i in range(NA_ROWS // 2):
                dtb_ref[dr0 + 2 * i] += ds[:, 2 * GRID_W * i:2 * GRID_W * (i + 1)]

    whole = pl.BlockSpec((None, T, hd), lambda h, r: (h, 0, 0))
    blk = pl.BlockSpec((None, rbt, hd), lambda h, r: (h, r, 0))
    tbs = pl.BlockSpec((None, 2 * NA_ROWS - 2, GRID_W, 2 * GRID_W), lambda h, r: (h, 0, 0, 0))
    return pl.pallas_call(
        body, grid=(H, n_rows // NA_RB), in_specs=[blk, whole, whole, tbs, blk],
        out_specs=[blk, whole, whole, tbs],
        out_shape=[SDS((H, T, hd), F32), SDS((H, T, hd), F32), SDS((H, T, hd), F32), SDS(tb2.shape, F32)],
        compiler_params=_cparams(("parallel", "arbitrary")), name=name)(q, k, v, tb2, do)


def _rpb_expand_consts():
    qc = np.arange(GRID_W)[:, None]
    kc = np.arange(GRID_W)[None, :]
    cs = np.clip(qc - NA_COLS // 2, 0, GRID_W - NA_COLS)
    inside = (kc >= cs) & (kc < cs + NA_COLS)
    dc = np.clip(kc - qc, -(NA_COLS - 1), NA_COLS - 1) + (NA_COLS - 1)
    e = np.zeros((128, GRID_W * GRID_W), np.float32)
    flat = (qc * GRID_W + kc)
    e[dc[inside], flat[inside]] = 1.0
    neg = np.where(inside, 0.0, NEG).astype(np.float32).reshape(1, -1)
    return e, neg


def _rpb_fold_consts():
    sa = np.zeros((NA_H * 15, NA_H * 14), np.float32)
    sb = np.zeros((NA_H * 15, NA_H * 14), np.float32)
    for h in range(NA_H):
        for d in range(14):
            sa[h * 15 + d, h * 14 + d] = 1.0
            sb[h * 15 + d + 1, h * 14 + d] = 1.0
    return sa, sb


def rpb_table(rpb, name):
    e, neg = _rpb_expand_consts()
    rp = jnp.pad(rpb.reshape(NA_H * 15, 31), ((0, 0), (0, 128 - 31)))

    def body(r_ref, e_ref, n_ref, o_ref):
        o_ref[...] = jnp.dot(r_ref[...], e_ref[...], preferred_element_type=F32, precision=HI) + n_ref[...]

    t = pl.pallas_call(body, out_shape=SDS((NA_H * 15, GRID_W * GRID_W), F32), name=name)(rp, e, neg)
    t = t.reshape(NA_H, 15, GRID_W, GRID_W)
    return jnp.concatenate([t[:, :14], t[:, 1:]], axis=-1)


def rpb_table_bwd(dtb2, name):
    e, _ = _rpb_expand_consts()
    sa, sb = _rpb_fold_consts()
    a = dtb2[..., :GRID_W].reshape(NA_H * 14, GRID_W * GRID_W)
    b = dtb2[..., GRID_W:].reshape(NA_H * 14, GRID_W * GRID_W)

    def body(a_ref, b_ref, e_ref, sa_ref, sb_ref, o_ref):
        ev = e_ref[...]
        pa = lax.dot_general(a_ref[...], ev, (NT, ((), ())), preferred_element_type=F32, precision=HI)
        pb = lax.dot_general(b_ref[...], ev, (NT, ((), ())), preferred_element_type=F32, precision=HI)
        o_ref[...] = (jnp.dot(sa_ref[...], pa, preferred_element_type=F32, precision=HI)
                      + jnp.dot(sb_ref[...], pb, preferred_element_type=F32, precision=HI))

    d = pl.pallas_call(body, out_shape=SDS((NA_H * 15, 128), F32), name=name)(a, b, e, sa, sb)
    return d[:, :31].reshape(NA_H, 15, 31)


def _kprep_fn(kv, gk):
    return _norm_heads(kv[:, :MEM_W], MEM_HD) * gk, kv[:, MEM_W:]


def mem_kprep(kv, gk, name):
    def body(kv_ref, g_ref, k_ref, v_ref):
        kn, vv = _kprep_fn(kv_ref[...], g_ref[...])
        k_ref[...] = kn.astype(BF16)
        v_ref[...] = vv.astype(BF16)

    return pl.pallas_call(body, out_shape=[SDS((MEM_TOK, MEM_W), BF16)] * 2, name=name)(kv, gk)


def mem_kprep_bwd(kv, gk, dk, dv, name):
    def body(kv_ref, g_ref, dk_ref, dv_ref, dkv_ref, dg_ref):
        _, vjp = jax.vjp(_kprep_fn, kv_ref[...], g_ref[...])
        dkv, dg = vjp((dk_ref[...], dv_ref[...]))
        dkv_ref[...] = dkv.astype(BF16)
        dg_ref[...] = dg

    return pl.pallas_call(body, out_shape=[SDS((MEM_TOK, 2 * MEM_W), BF16), SDS((1, MEM_W), F32)], name=name)(kv, gk, dk, dv)


def _mem_probs(q_ref, k_ref, h):
    hs = slice(h * MEM_HD, (h + 1) * MEM_HD)
    qh, kh = q_ref[:, hs], k_ref[:, hs]
    s = _bdot(qh, kh, NT) * (MEM_HD ** -0.5)
    p = jnp.exp(s - jnp.max(s, axis=1, keepdims=True))
    return hs, qh, kh, p, jnp.sum(p, axis=1, keepdims=True)


def mem_attn_fwd(q, km, vm, name):
    T = q.shape[0]
    tm = _fit(T, 512, 8)

    def body(q_ref, k_ref, v_ref, o_ref):
        for h in range(MEM_H):
            hs, _, _, p, l = _mem_probs(q_ref, k_ref, h)
            o_ref[:, hs] = _bdot(p, v_ref[:, hs], NN) / l

    row = pl.BlockSpec((tm, MEM_W), lambda i: (i, 0))
    full = pl.BlockSpec((MEM_TOK, MEM_W), lambda i: (0, 0))
    return pl.pallas_call(body, grid=(T // tm,), in_specs=[row, full, full], out_specs=row, out_shape=SDS((T, MEM_W), F32),
                          compiler_params=_cparams(("parallel",)), name=name)(q, km, vm)


def mem_attn_bwd(q, km, vm, do, name):
    T = q.shape[0]
    tm = _fit(T, 512, 8)
    scale = MEM_HD ** -0.5

    def body(q_ref, k_ref, v_ref, do_ref, dq_ref, dk_ref, dv_ref):
        @pl.when(pl.program_id(0) == 0)
        def _():
            dk_ref[...] = jnp.zeros_like(dk_ref)
            dv_ref[...] = jnp.zeros_like(dv_ref)

        for h in range(MEM_H):
            hs, qh, kh, p, l = _mem_probs(q_ref, k_ref, h)
            p = p / l
            dov = do_ref[:, hs]
            dp = _bdot(dov, v_ref[:, hs], NT)
            ds = p * (dp - jnp.sum(dp * p, axis=1, keepdims=True))
            dq_ref[:, hs] = _bdot(ds, kh, NN) * scale
            dk_ref[:, hs] += _bdot(ds, qh, TN) * scale
            dv_ref[:, hs] += _bdot(p, dov, TN)

    row = pl.BlockSpec((tm, MEM_W), lambda i: (i, 0))
    full = pl.BlockSpec((MEM_TOK, MEM_W), lambda i: (0, 0))
    return pl.pallas_call(
        body, grid=(T // tm,), in_specs=[row, full, full, row], out_specs=[row, full, full],
        out_shape=[SDS((T, MEM_W), F32), SDS((MEM_TOK, MEM_W), F32), SDS((MEM_TOK, MEM_W), F32)],
        compiler_params=_cparams(("arbitrary",)), name=name)(q, km, vm, do)


SLOT_BLOCK_ELEMS = 512 * 1024


def _slot_rows(R, Cc):
    return _fit(R, max(16, SLOT_BLOCK_ELEMS // Cc // 16 * 16), 16)


def sum_slots(a, name):
    n, R, Cc = a.shape
    tr = _slot_rows(R, Cc)

    def body(a_ref, o_ref):
        s = a_ref[0].astype(F32)
        for i in range(1, n):
            s = s + a_ref[i].astype(F32)
        o_ref[...] = s

    return pl.pallas_call(body, grid=(R // tr,), in_specs=[pl.BlockSpec((n, tr, Cc), lambda i: (0, i, 0))],
                          out_specs=pl.BlockSpec((tr, Cc), lambda i: (i, 0)), out_shape=SDS((R, Cc), F32),
                          compiler_params=_cparams(("parallel",)), name=name)(a)


def pair_sum(g, land, c_idx, name):
    _, R, Cc = g.shape
    tr = _slot_rows(R, Cc)

    def body(c_ref, g_ref, l_ref, o_ref):
        o_ref[...] = (g_ref[...].astype(F32) + l_ref[...].astype(F32)).astype(o_ref.dtype)

    blk = pl.BlockSpec((None, tr, Cc), lambda k, i, c: (k, i, 0))
    gs = pltpu.PrefetchScalarGridSpec(
        num_scalar_prefetch=1, grid=(4, R // tr),
        in_specs=[pl.BlockSpec((None, tr, Cc), lambda k, i, c: (2 * k + c[0], i, 0)), blk], out_specs=blk)
    return pl.pallas_call(body, grid_spec=gs, out_shape=SDS((4, R, Cc), g.dtype),
                          compiler_params=_cparams(("parallel", "parallel")), name=name)(c_idx, g, land)


def adamw(w, g, m, v, name):
    R, Cc = w.shape
    tr = _fit(R, max(8, (262144 // max(Cc, 128)) // 8 * 8), 8)
    c1 = 1.0 - ADAM_B1 ** ADAM_STEP
    c2 = 1.0 - ADAM_B2 ** ADAM_STEP

    def body(w_ref, g_ref, m_ref, v_ref, d_ref, mo_ref, vo_ref):
        gv = g_ref[...]
        m2 = ADAM_B1 * m_ref[...] + (1.0 - ADAM_B1) * gv
        v2 = ADAM_B2 * v_ref[...] + (1.0 - ADAM_B2) * (gv * gv)
        d_ref[...] = -ADAM_LR * ((m2 / c1) / (jnp.sqrt(v2 / c2) + ADAM_EPS) + ADAM_WD * w_ref[...])
        mo_ref[...] = m2
        vo_ref[...] = v2

    blk = pl.BlockSpec((tr, Cc), lambda i: (i, 0))
    return pl.pallas_call(body, grid=(R // tr,), in_specs=[blk] * 4, out_specs=[blk] * 3, out_shape=[SDS((R, Cc), F32)] * 3,
                          compiler_params=_cparams(("parallel",)), name=name)(w, g, m, v)


ANY = pl.BlockSpec(memory_space=pl.ANY)


def _my_pos():
    return lax.axis_index("x"), lax.axis_index("y"), lax.axis_index("c")


class Rider:
    def __init__(self, inputs, out_shapes, scratch, start, finish, aliases=None):
        self.inputs, self.out_shapes, self.scratch = list(inputs), list(out_shapes), list(scratch)
        self.start, self.finish, self.aliases = start, finish, dict(aliases or {})


def combine_riders(r1, r2):
    ni, no, ns = len(r1.inputs), len(r1.out_shapes), len(r1.scratch)

    def both(f1, f2):
        def run(i, o, s):
            f1(i[:ni], o[:no], s[:ns])
            f2(i[ni:], o[no:], s[ns:])
        return run

    aliases = {**r1.aliases, **{ni + i: no + j for i, j in r2.aliases.items()}}
    return Rider(r1.inputs + r2.inputs, r1.out_shapes + r2.out_shapes, r1.scratch + r2.scratch,
                 both(r1.start, r2.start), both(r1.finish, r2.finish), aliases)


def _gather_phases(x_refs, out_refs, sems):
    send_sems, recv_sems, local_sems = sems
    n = len(out_refs)
    x, y, c = _my_pos()
    me, sibling = (x, y, c), (x, y, 1 - c)
    chips = [(1 - x, y), (x, 1 - y), (1 - x, 1 - y)]

    def slot(a, px, py, pc):
        return out_refs[a].at[4 * px + 2 * py + pc]

    def copy(a, k, block, to, src=None):
        return pltpu.make_async_remote_copy(
            src_ref=slot(a, *block) if src is None else src, dst_ref=slot(a, *block),
            send_sem=send_sems.at[7 * a + k], recv_sem=recv_sems.at[7 * a + k], device_id=to, device_id_type=MESH)

    def mine(a):
        return pltpu.make_async_copy(x_refs[a], slot(a, *me), local_sems.at[a])

    def spread(a):
        return [copy(a, 0, me, sibling, src=x_refs[a])] + [copy(a, 1 + j, me, (*chip, c), src=x_refs[a])
                                                           for j, chip in enumerate(chips)]

    def spread_start():
        for a in range(n):
            mine(a).start()
        for a in range(n):
            for cp in spread(a):
                cp.start()

    def spread_finish():
        for j, chip in enumerate(chips):
            for a in range(n):
                copy(a, 1 + j, (*chip, c), me).wait_recv()
        for a in range(n):
            copy(a, 0, sibling, me).wait_recv()
        for a in range(n):
            for cp in spread(a):
                cp.wait_send()
            mine(a).wait()

    def forward_start():
        for j, chip in enumerate(chips):
            for a in range(n):
                copy(a, 4 + j, (*chip, c), sibling).start()

    def forward_finish():
        for j, chip in enumerate(chips):
            for a in range(n):
                copy(a, 4 + j, (*chip, 1 - c), me).wait_recv()
        for j, chip in enumerate(chips):
            for a in range(n):
                copy(a, 4 + j, (*chip, c), sibling).wait_send()

    return spread_start, spread_finish, forward_start, forward_finish


def _gather_scratch(n):
    return [pltpu.SemaphoreType.DMA((7 * n,)), pltpu.SemaphoreType.DMA((7 * n,)), pltpu.SemaphoreType.DMA((n,))]


def all_gather(shards, name):
    n = len(shards)

    def body(*refs):
        phases = _gather_phases(refs[:n], refs[n:2 * n], refs[2 * n:])
        for phase in phases:
            phase()

    return pl.pallas_call(
        body, out_shape=[SDS((N_DEV,) + s.shape, s.dtype) for s in shards], in_specs=[ANY] * n, out_specs=[ANY] * n,
        scratch_shapes=_gather_scratch(n), name=name)(*shards)


def gather_spread_rider(shards):
    return Rider(shards, [SDS((N_DEV,) + s.shape, s.dtype) for s in shards], _gather_scratch(len(shards)),
                 lambda i, o, s: _gather_phases(i, o, s)[0](), lambda i, o, s: _gather_phases(i, o, s)[1]())


def gather_forward_rider(bufs):
    n = len(bufs)
    return Rider(bufs, [SDS(b.shape, b.dtype) for b in bufs], _gather_scratch(n),
                 lambda i, o, s: _gather_phases(None, o, s)[2](), lambda i, o, s: _gather_phases(None, o, s)[3](),
                 aliases={a: a for a in range(n)})


def pair_exchange(gs, name):
    n = len(gs)

    def body(*refs):
        g_refs, land_refs = refs[:n], refs[n:2 * n]
        send_sems, recv_sems = refs[2 * n:]
        x, y, c = _my_pos()
        sibling = (x, y, 1 - c)
        copies = [pltpu.make_async_remote_copy(
            src_ref=g_refs[a].at[2 * k + (1 - c)], dst_ref=land_refs[a].at[k], send_sem=send_sems.at[4 * a + k],
            recv_sem=recv_sems.at[4 * a + k], device_id=sibling, device_id_type=MESH) for a in range(n) for k in range(4)]
        for cp in copies:
            cp.start()
        for cp in copies:
            cp.wait_recv()
        for cp in copies:
            cp.wait_send()

    return pl.pallas_call(
        body, out_shape=[SDS((4,) + g.shape[1:], g.dtype) for g in gs], in_specs=[ANY] * n, out_specs=[ANY] * n,
        scratch_shapes=[pltpu.SemaphoreType.DMA((4 * n,)), pltpu.SemaphoreType.DMA((4 * n,))], name=name)(*gs)


def chip_exchange(ss, name):
    n = len(ss)

    def body(*refs):
        start, finish = _chip_exchange_phases(refs[:n], refs[n:2 * n], refs[2 * n:])
        start()
        finish()

    return pl.pallas_call(
        body, out_shape=[SDS(s.shape, s.dtype) for s in ss], in_specs=[ANY] * n, out_specs=[ANY] * n,
        scratch_shapes=_chip_exchange_scratch(n), name=name)(*ss)


def _chip_exchange_scratch(n):
    return [pltpu.SemaphoreType.DMA((3 * n,)), pltpu.SemaphoreType.DMA((3 * n,)), pltpu.SemaphoreType.DMA((n,))]


def _chip_exchange_phases(s_refs, land_refs, sems):
    send_sems, recv_sems, local_sems = sems
    n = len(s_refs)
    x, y, c = _my_pos()
    my_chip = 2 * x + y
    chips = [(1 - x, y), (x, 1 - y), (1 - x, 1 - y)]

    def own(a):
        return pltpu.make_async_copy(s_refs[a].at[my_chip], land_refs[a].at[my_chip], local_sems.at[a])

    def copy(a, j, src_chip, dst_chip):
        px, py = chips[j]
        return pltpu.make_async_remote_copy(
            src_ref=s_refs[a].at[src_chip], dst_ref=land_refs[a].at[dst_chip], send_sem=send_sems.at[3 * a + j],
            recv_sem=recv_sems.at[3 * a + j], device_id=(px, py, c), device_id_type=MESH)

    def start():
        for a in range(n):
            own(a).start()
            for j, (px, py) in enumerate(chips):
                copy(a, j, 2 * px + py, my_chip).start()

    def finish():
        for a in range(n):
            for j, (px, py) in enumerate(chips):
                copy(a, j, my_chip, 2 * px + py).wait_recv()
        for a in range(n):
            for j, (px, py) in enumerate(chips):
                copy(a, j, 2 * px + py, my_chip).wait_send()
            own(a).wait()

    return start, finish


def chip_exchange_rider(ss):
    return Rider(ss, [SDS(s.shape, s.dtype) for s in ss], _chip_exchange_scratch(len(ss)),
                 lambda i, o, s: _chip_exchange_phases(i, o, s)[0](), lambda i, o, s: _chip_exchange_phases(i, o, s)[1]())


W_IN_SHARD, W_IN_PACKED = 644, 768
BIG = ("ffn_w13", "ffn_w2", "w_out", "mem_wkv", "w_in")
GATE_W = ("gla_wg2_f", "gla_wg2_b")
SHARD_NAMES = BIG + GATE_W


def _permuted_ranges(c0, c1):
    res = []
    for o0, o1, p0 in ((0, ORIG_GATE0, 0), (ORIG_GATE0, ORIG_AFTER_GATE, COL_TAIL), (ORIG_AFTER_GATE, IN_COLS, ORIG_GATE0)):
        lo, hi = max(c0, o0), min(c1, o1)
        if lo < hi:
            res.append((p0 + lo - o0, p0 + hi - o0))
    return res


def assemble_w_in(blocks):
    placed = []
    for d in range(N_DEV):
        c = d * W_IN_SHARD
        for p0, p1 in _permuted_ranges(c, c + W_IN_SHARD):
            placed.append((p0, blocks[d][:, c - d * W_IN_SHARD:c - d * W_IN_SHARD + (p1 - p0)]))
            c += p1 - p0
    placed.sort(key=lambda t: t[0])
    return jnp.concatenate([t[1] for t in placed] + [jnp.zeros((blocks.shape[1], PC - IN_COLS), blocks.dtype)], axis=1)


def split_w_in_grad(g):
    pad = jnp.zeros((g.shape[0], W_IN_PACKED - W_IN_SHARD), g.dtype)
    return jnp.stack([jnp.concatenate([g[:, p0:p1] for p0, p1 in _permuted_ranges(d * W_IN_SHARD, (d + 1) * W_IN_SHARD)]
                                      + [pad], axis=1) for d in range(N_DEV)])


def wire_shards(shards):
    return [jnp.pad(shards[n], ((0, 0), (0, W_IN_PACKED - W_IN_SHARD))) if n == "w_in" else shards[n] for n in BIG]


def weights_from_gathered(names, gathered):
    full = {}
    for n, g in zip(names, gathered):
        full[n] = g if n == "ffn_w13" else assemble_w_in(g) if n == "w_in" else g.reshape(-1, g.shape[-1])
    return full


def reduce_scatter_begin(names, gw, c_idx, tag):
    g = [split_w_in_grad(gw[n]) if n == "w_in" else gw[n].reshape((N_DEV, -1, gw[n].shape[-1])) for n in names]
    land = pair_exchange(g, tag + "rs_pair_exchange_" + names[0])
    return [pair_sum(g[i], land[i], c_idx, tag + "rs_pair_sum_" + n) for i, n in enumerate(names)]


def reduce_scatter_end(names, landed, tag):
    out = {n: sum_slots(landed[i], tag + "rs_chip_sum_" + n) for i, n in enumerate(names)}
    if "w_in" in out:
        out["w_in"] = out["w_in"][:, :W_IN_SHARD]
    return out


def gather_gate_weights(w, depth):
    mine = jnp.concatenate([w[n].reshape(-1) for n in GATE_W]).reshape(-1, 128)
    got = all_gather([mine], "gather_gate_weights")[0].reshape(N_DEV, len(GATE_W), depth, GLA_RANK, GLA_DK // N_DEV)
    return [{n: got[:, i, l].transpose(1, 0, 2).reshape(GLA_RANK, GLA_DK) for i, n in enumerate(GATE_W)} for l in range(depth)]


def pad_gate_weight(wg2, backward_dir):
    r0 = GLA_RANK if backward_dir else 0
    w = wg2.astype(F32).reshape(GLA_RANK, GLA_H, GLA_HK).transpose(1, 0, 2)
    return jnp.pad(w, ((0, 0), (r0, 128 - GLA_RANK - r0), (0, 0)))


def unpad_gate_grad(dw, backward_dir):
    r0 = GLA_RANK if backward_dir else 0
    return dw[:, r0:r0 + GLA_RANK, :].transpose(1, 0, 2).reshape(GLA_RANK, GLA_DK)


def to_heads(t):
    T = t.shape[0]
    return t.reshape(T, NA_H, NA_HD).transpose(1, 0, 2)


def from_heads(t):
    return t.transpose(1, 0, 2).reshape(t.shape[1], NA_W)


REPLICATED = ("attn_norm", "gla_bg_f", "gla_bg_b", "gla_out_norm", "na_q_norm", "na_k_norm", "na_rpb", "na_out_norm",
              "mem_norm", "mem_q_norm", "mem_k_norm", "mem_out_norm", "ffn_norm")
WEIGHTS = ("attn_norm", "w_in", "gla_wg2_f", "gla_bg_f", "gla_wg2_b", "gla_bg_b", "gla_out_norm", "na_q_norm", "na_k_norm",
           "na_rpb", "na_out_norm", "mem_norm", "mem_wkv", "mem_q_norm", "mem_k_norm", "mem_out_norm", "w_out", "ffn_norm",
           "ffn_w13", "ffn_w2")


def fold_heads(dg, n_heads, name):
    hd = dg.shape[1] // n_heads
    fold = (np.arange(dg.shape[1])[:, None] % hd == np.arange(128)[None, :]).astype(np.float32)
    out = matmul(jnp.pad(dg, ((0, 7), (0, 0))), fold, "nn", name, exact=True)
    return out[0, :hd]


def layer_fwd(x, mem_n_in, p, W, l, arriving=None, next_shards=None):
    tag = f"l{l}_"
    row = lambda v: v.reshape(1, -1)
    sv = {"x": x}
    if arriving is None:
        sv["xn"] = rms_fwd(x, row(p["attn_norm"]), tag + "attn_rms")
    else:
        sv["xn"], rest = rms_fwd(x, row(p["attn_norm"]), tag + "attn_rms", rider=gather_forward_rider(arriving))
        W = {**W, **weights_from_gathered(BIG[1:], rest)}
    proj = matmul(sv["xn"], W["w_in"], "nn", tag + "proj", out_dtype=BF16, tn=768, tk=2048)
    sv["proj"] = proj
    sv["wg_f"], sv["wg_b"] = pad_gate_weight(W["gla_wg2_f"], False), pad_gate_weight(W["gla_wg2_b"], True)
    sv["bg_f"], sv["bg_b"] = p["gla_bg_f"].reshape(GLA_H, 1, GLA_HK), p["gla_bg_b"].reshape(GLA_H, 1, GLA_HK)
    sv["o_f"], sv["s_f"] = gla_fwd(proj, sv["wg_f"], sv["bg_f"], False, tag + "gla_f")
    sv["o_b"], sv["s_b"] = gla_fwd(proj, sv["wg_b"], sv["bg_b"], True, tag + "gla_b")
    sv["gq"], sv["gk"] = jnp.tile(row(p["na_q_norm"]), (1, NA_H)), jnp.tile(row(p["na_k_norm"]), (1, NA_H))
    sv["gmq"], sv["gmk"] = jnp.tile(row(p["mem_q_norm"]), (1, MEM_H)), jnp.tile(row(p["mem_k_norm"]), (1, MEM_H))
    qn, kn, vn, mqn = pre_fwd(proj, sv["gq"], sv["gk"], sv["gmq"], tag + "pre")
    sv["q_hm"], sv["k_hm"], sv["v_hm"], sv["mqn"] = to_heads(qn), to_heads(kn), to_heads(vn), mqn
    sv["tb2"] = rpb_table(p["na_rpb"], tag + "rpb_table")
    sv["o_na"] = from_heads(natten_fwd(sv["q_hm"], sv["k_hm"], sv["v_hm"], sv["tb2"], tag + "natten"))
    sv["mem_n"] = rms_fwd(mem_n_in, row(p["mem_norm"]), tag + "mem_rms")
    sv["kv"] = matmul(sv["mem_n"], W["mem_wkv"], "nn", tag + "mem_kv", tn=512, tk=2048)
    sv["km"], sv["vm"] = mem_kprep(sv["kv"], sv["gmk"], tag + "mem_kprep")
    sv["o_mem"] = mem_attn_fwd(mqn, sv["km"], sv["vm"], tag + "mem_attn")
    sv["ycat"] = post_fwd(sv["o_f"], sv["o_b"], proj, sv["o_na"], sv["o_mem"], row(p["gla_out_norm"]),
                          row(p["na_out_norm"]), row(p["mem_out_norm"]), tag + "post")
    x1 = matmul(sv["ycat"], W["w_out"], "nn", tag + "out_proj", res=x, tk=2048)
    sv["x1"] = x1
    sv["h"] = rms_fwd(x1, row(p["ffn_norm"]), tag + "ffn_rms")
    if next_shards is None:
        sv["gu"], sv["a"], _ = ffn_up_swiglu(sv["h"], W["ffn_w13"], tag + "ffn_up")
        return matmul(sv["a"], W["ffn_w2"], "nn", tag + "ffn_down", res=x1, tk=2816), sv, W, None, None
    sv["gu"], sv["a"], spread13 = ffn_up_swiglu(sv["h"], W["ffn_w13"], tag + "ffn_up",
                                                rider=gather_spread_rider(next_shards[:1]))
    x2, carried = matmul(sv["a"], W["ffn_w2"], "nn", tag + "ffn_down", res=x1, tk=2816,
                         rider=combine_riders(gather_forward_rider(spread13), gather_spread_rider(next_shards[1:])))
    return x2, sv, W, weights_from_gathered(BIG[:1], carried[:1]), carried[1:]


RS_EARLY, RS_LATE = ("ffn_w13", "ffn_w2", "w_out"), ("mem_wkv", "w_in")


def layer_bwd(dx2, dx2_b, mem_n_in, p, W, sv, l, c_idx, rider=None):
    tag = f"l{l}_b_"
    row = lambda v: v.reshape(1, -1)
    gw, gs = {}, {}
    gw["ffn_w2"] = matmul(sv["a"], dx2_b, "tn", tag + "dw2", out_dtype=BF16, tm=1408, tn=1024, tk=2048)
    dgu = ffn_down_bwd(dx2_b, W["ffn_w2"], sv["gu"], tag + "d_swiglu")
    dh = matmul(dgu, W["ffn_w13"], "nt", tag + "d_h", a_halves=True, b_blocked=True, rider=rider)
    dh, carried = dh if rider is not None else (dh, None)
    gw["ffn_w13"] = matmul(sv["h"], dgu, "tn", tag + "dw13", out_dtype=BF16, tm=2048, tn=1408, tk=1024,
                           b_halves=True, out_blocked=True)
    dx1, dx1_b, dg = rms_bwd(sv["x1"], row(p["ffn_norm"]), dh, dx2, tag + "ffn_rms")
    gs["ffn_norm"] = dg[0]
    dycat = matmul(dx1_b, W["w_out"], "nt", tag + "d_ycat", tk=2048)
    gw["w_out"] = matmul(sv["ycat"], dx1_b, "tn", tag + "dw_out", out_dtype=BF16, tm=2048, tn=1024, tk=2048)
    early = reduce_scatter_begin(RS_EARLY, gw, c_idx, tag)
    d_o, d_r, d_ona, d_omem, dgg, dgn, dgm = post_bwd(
        sv["o_f"], sv["o_b"], sv["proj"], sv["o_na"], sv["o_mem"], row(p["gla_out_norm"]), row(p["na_out_norm"]),
        row(p["mem_out_norm"]), dycat, tag + "post")
    gs["gla_out_norm"], gs["na_out_norm"], gs["mem_out_norm"] = dgg[0], dgn[0], dgm[0]
    dq_f, dk_f, dv_f, dt_f, dwg_f, dbg_f = gla_bwd(sv["proj"], sv["wg_f"], sv["bg_f"], sv["s_f"], d_o, False, tag + "gla_f")
    dq_b, dk_b, dv_b, dt_b, dwg_b, dbg_b = gla_bwd(sv["proj"], sv["wg_b"], sv["bg_b"], sv["s_b"], d_o, True, tag + "gla_b")
    gs["gla_wg2_f"], gs["gla_wg2_b"] = unpad_gate_grad(dwg_f, False), unpad_gate_grad(dwg_b, True)
    gs["gla_bg_f"], gs["gla_bg_b"] = dbg_f.reshape(-1), dbg_b.reshape(-1)
    dq_hm, dk_hm, dv_hm, dtb2 = natten_bwd(sv["q_hm"], sv["k_hm"], sv["v_hm"], sv["tb2"], to_heads(d_ona), tag + "natten")
    gs["na_rpb"] = rpb_table_bwd(dtb2, tag + "rpb_table")
    d_mqn, dkm, dvm = mem_attn_bwd(sv["mqn"], sv["km"], sv["vm"], d_omem, tag + "mem_attn")
    dkv, dgmk = mem_kprep_bwd(sv["kv"], sv["gmk"], dkm, dvm, tag + "mem_kprep")
    gs["mem_k_norm"] = fold_heads(dgmk, MEM_H, tag + "fold_mk")
    gw["mem_wkv"] = matmul(sv["mem_n"], dkv, "tn", tag + "dw_kv", out_dtype=BF16, tm=2048, tn=1024, tk=256)
    d_memn = matmul(dkv, W["mem_wkv"], "nt", tag + "d_memn", tk=1024)
    _, _, dg = rms_bwd(mem_n_in, row(p["mem_norm"]), d_memn, None, tag + "mem_rms")
    gs["mem_norm"] = dg[0]
    dproj, dgq, dgk, dgmq = pre_bwd(sv["proj"], sv["gq"], sv["gk"], sv["gmq"], from_heads(dq_hm), from_heads(dk_hm),
                                    from_heads(dv_hm), d_mqn, dq_f, dq_b, dk_f, dk_b, dv_f, dv_b, d_r, dt_f, dt_b, tag + "pre")
    gs["na_q_norm"] = fold_heads(dgq, NA_H, tag + "fold_q")
    gs["na_k_norm"] = fold_heads(dgk, NA_H, tag + "fold_k")
    gs["mem_q_norm"] = fold_heads(dgmq, MEM_H, tag + "fold_mq")
    dxn, landed13 = matmul(dproj, W["w_in"], "nt", tag + "d_xn", tk=1792, rider=chip_exchange_rider(early[:1]))
    gw["w_in"], landed_rest = matmul(sv["xn"], dproj, "tn", tag + "dw_in", out_dtype=BF16, tm=2048, tn=768, tk=2048,
                                     rider=chip_exchange_rider(early[1:]))
    shard_grads = reduce_scatter_end(RS_EARLY, landed13 + landed_rest, tag)
    late = reduce_scatter_begin(RS_LATE, gw, c_idx, tag)
    dx, dx_b, dg = rms_bwd(sv["x"], row(p["attn_norm"]), dxn, dx1, tag + "attn_rms")
    gs["attn_norm"] = dg[0]
    return dx, dx_b, shard_grads, gs, carried, late


def kernel(x, mem, attn_norm, w_in, gla_wg2_f, gla_bg_f, gla_wg2_b, gla_bg_b, gla_out_norm, na_q_norm, na_k_norm, na_rpb, na_out_norm, mem_norm, mem_wkv, mem_q_norm, mem_k_norm, mem_out_norm, w_out, ffn_norm, ffn_w13, ffn_w2, loss_target, m_attn_norm, m_w_in, m_gla_wg2_f, m_gla_bg_f, m_gla_wg2_b, m_gla_bg_b, m_gla_out_norm, m_na_q_norm, m_na_k_norm, m_na_rpb, m_na_out_norm, m_mem_norm, m_mem_wkv, m_mem_q_norm, m_mem_k_norm, m_mem_out_norm, m_w_out, m_ffn_norm, m_ffn_w13, m_ffn_w2, v_attn_norm, v_w_in, v_gla_wg2_f, v_gla_bg_f, v_gla_wg2_b, v_gla_bg_b, v_gla_out_norm, v_na_q_norm, v_na_k_norm, v_na_rpb, v_na_out_norm, v_mem_norm, v_mem_wkv, v_mem_q_norm, v_mem_k_norm, v_mem_out_norm, v_w_out, v_ffn_norm, v_ffn_w13, v_ffn_w2):
    w = dict(attn_norm=attn_norm, w_in=w_in, gla_wg2_f=gla_wg2_f, gla_bg_f=gla_bg_f, gla_wg2_b=gla_wg2_b, gla_bg_b=gla_bg_b,
             gla_out_norm=gla_out_norm, na_q_norm=na_q_norm, na_k_norm=na_k_norm, na_rpb=na_rpb, na_out_norm=na_out_norm,
             mem_norm=mem_norm, mem_wkv=mem_wkv, mem_q_norm=mem_q_norm, mem_k_norm=mem_k_norm, mem_out_norm=mem_out_norm,
             w_out=w_out, ffn_norm=ffn_norm, ffn_w13=ffn_w13, ffn_w2=ffn_w2)
    mom = dict(attn_norm=m_attn_norm, w_in=m_w_in, gla_wg2_f=m_gla_wg2_f, gla_bg_f=m_gla_bg_f, gla_wg2_b=m_gla_wg2_b,
               gla_bg_b=m_gla_bg_b, gla_out_norm=m_gla_out_norm, na_q_norm=m_na_q_norm, na_k_norm=m_na_k_norm, na_rpb=m_na_rpb,
               na_out_norm=m_na_out_norm, mem_norm=m_mem_norm, mem_wkv=m_mem_wkv, mem_q_norm=m_mem_q_norm,
               mem_k_norm=m_mem_k_norm, mem_out_norm=m_mem_out_norm, w_out=m_w_out, ffn_norm=m_ffn_norm, ffn_w13=m_ffn_w13,
               ffn_w2=m_ffn_w2)
    var = dict(attn_norm=v_attn_norm, w_in=v_w_in, gla_wg2_f=v_gla_wg2_f, gla_bg_f=v_gla_bg_f, gla_wg2_b=v_gla_wg2_b,
               gla_bg_b=v_gla_bg_b, gla_out_norm=v_gla_out_norm, na_q_norm=v_na_q_norm, na_k_norm=v_na_k_norm, na_rpb=v_na_rpb,
               na_out_norm=v_na_out_norm, mem_norm=v_mem_norm, mem_wkv=v_mem_wkv, mem_q_norm=v_mem_q_norm,
               mem_k_norm=v_mem_k_norm, mem_out_norm=v_mem_out_norm, w_out=v_w_out, ffn_norm=v_ffn_norm, ffn_w13=v_ffn_w13,
               ffn_w2=v_ffn_w2)
    depth = attn_norm.shape[0]
    T = x.shape[1]
    xs, mem0, tgt = x.reshape(T, D_MODEL), mem.reshape(MEM_TOK, D_MODEL), loss_target.reshape(T, D_MODEL)
    c_idx = lax.axis_index("c").astype(jnp.int32).reshape(1)

    gates = gather_gate_weights(w, depth)
    send = [wire_shards({n: w[n][l].astype(BF16) for n in BIG}) for l in range(depth)]
    P = [{n: w[n][l] for n in REPLICATED} for l in range(depth)]

    ready = {**weights_from_gathered(BIG, all_gather(send[0], "l0_gather_weights")), **gates[0]}
    arriving = None
    W, saved = [], []
    h = xs
    for l in range(depth):
        h, sv, w_l, ready, arriving = layer_fwd(h, mem0, P[l], ready, l, arriving, send[l + 1] if l + 1 < depth else None)
        W.append(w_l)
        saved.append(sv)
        if ready is not None:
            ready = {**ready, **gates[l + 1]}
    dy, dy_b, lsum = loss_bwd(h, tgt, "loss")
    loss = lax.psum(lsum[0, 0], ("x", "y", "c")) * (0.5 / D_MODEL)

    g_shard, g_small = [None] * depth, [None] * depth
    late = None
    for l in range(depth - 1, -1, -1):
        rider = chip_exchange_rider(late) if late is not None else None
        dy, dy_b, g_shard[l], g_small[l], landed, late = layer_bwd(dy, dy_b, mem0, P[l], W[l], saved[l], l, c_idx, rider)
        if landed is not None:
            g_shard[l + 1].update(reduce_scatter_end(RS_LATE, landed, f"l{l + 1}_b_"))
        saved[l] = None
    g_shard[0].update(reduce_scatter_end(RS_LATE, chip_exchange(late, "l0_rs_chip_exchange"), "l0_b_"))
    grad_x = dy.reshape(x.shape)

    small_names = REPLICATED + GATE_W
    small = jnp.concatenate([g_small[l][n].reshape(-1) for l in range(depth) for n in small_names])
    n_small = small.shape[0]
    rows = -(-n_small // 1024) * 8
    small = jnp.pad(small, (0, rows * 128 - n_small)).reshape(rows, 128)
    small = sum_slots(all_gather([small], "gather_small_grads")[0], "sum_small_grads").reshape(-1)
    grads, off = {}, 0
    per_layer = {n: [] for n in small_names}
    my_cols = (4 * lax.axis_index("x") + 2 * lax.axis_index("y") + lax.axis_index("c")) * (GLA_DK // N_DEV)
    for l in range(depth):
        for n in small_names:
            shp = (GLA_RANK, GLA_DK) if n in GATE_W else w[n].shape[1:]
            g = small[off:off + int(np.prod(shp))].reshape(shp)
            off += int(np.prod(shp))
            per_layer[n].append(lax.dynamic_slice_in_dim(g, my_cols, GLA_DK // N_DEV, axis=1) if n in GATE_W else g)
    for n in small_names:
        grads[n] = jnp.stack(per_layer[n])
    for n in BIG:
        grads[n] = jnp.stack([g_shard[l][n] for l in range(depth)])

    delta, new_m, new_v = {}, {}, {}
    for n in WEIGHTS:
        shp = w[n].shape
        two_d = (shp[0], int(np.prod(shp[1:]))) if n in REPLICATED else (int(np.prod(shp[:-1])), shp[-1])
        d_, m_, v_ = adamw(w[n].reshape(two_d), grads[n].reshape(two_d), mom[n].reshape(two_d), var[n].reshape(two_d),
                           "adamw_" + n)
        delta[n], new_m[n], new_v[n] = d_.reshape(shp), m_.reshape(shp), v_.reshape(shp)

    return (loss, grad_x, *[grads[n] for n in WEIGHTS], *[delta[n] for n in WEIGHTS], *[new_m[n] for n in WEIGHTS],
            *[new_v[n] for n in WEIGHTS])
```

```python
import functools

import numpy as np
import jax
import jax.numpy as jnp
from jax import lax
from jax.experimental import pallas as pl
from jax.experimental.pallas import tpu as pltpu

F32, BF16 = jnp.float32, jnp.bfloat16
HI = lax.Precision.HIGHEST
SDS = jax.ShapeDtypeStruct
MESH = pl.DeviceIdType.MESH

D_MODEL = 2048
GRID_W = 64
GLA_H, GLA_HK, GLA_HV, GLA_RANK, GLA_TAU, GLA_C = 4, 128, 256, 16, 16.0, 64
GLA_DK, GLA_DV = GLA_H * GLA_HK, GLA_H * GLA_HV
NA_H, NA_HD, NA_ROWS, NA_COLS = 8, 64, 8, 16
NA_W = NA_H * NA_HD
MEM_H, MEM_HD, MEM_TOK = 4, 128, 256
MEM_W = MEM_H * MEM_HD
D_FF = 5632
IN_COLS = 5152
RMS_EPS = 1e-6
ADAM_LR, ADAM_B1, ADAM_B2, ADAM_EPS, ADAM_WD, ADAM_STEP = 0.001, 0.9, 0.999, 1e-08, 0.01, 10
N_DEV = 8

PC = 5376
COL_R, COL_NQ, COL_NK, COL_NV, COL_MQ, COL_TAIL = 2048, 3072, 3584, 4096, 4608, 5120
ORIG_GATE0 = 3072
ORIG_AFTER_GATE = 3104

GLA_G = 8
VMEM_LIMIT = 56 * 1024 * 1024
NEG = -1e30


def _cparams(sem):
    return pltpu.CompilerParams(dimension_semantics=sem, vmem_limit_bytes=VMEM_LIMIT)


def _fit(n, pref, unit=128):
    if n <= pref:
        return n
    t = (pref // unit) * unit
    while t >= unit:
        if n % t == 0:
            return t
        t -= unit
    return n


def _bdot(a, b, dims):
    return lax.dot_general(a.astype(BF16), b.astype(BF16), (dims, ((), ())), preferred_element_type=F32)


NN, NT, TN = ((1,), (0,)), ((1,), (1,)), ((0,), (0,))


@functools.partial(jax.custom_vjp, nondiff_argnums=(2,))
def _bdot_vjp(a, b, dims):
    return _bdot(a, b, dims)


def _bdot_vjp_fwd(a, b, dims):
    return _bdot(a, b, dims), (a, b)


def _bdot_vjp_bwd(dims, res, ct):
    a, b = res
    if dims == NN:
        return _bdot(ct, b, NT), _bdot(a, ct, TN)
    if dims == NT:
        return _bdot(ct, b, NN), _bdot(ct, a, TN)
    return _bdot(b, ct, NT), _bdot(a, ct, NN)


_bdot_vjp.defvjp(_bdot_vjp_fwd, _bdot_vjp_bwd)


def _split_dot(c, x, dims):
    hi = x.astype(BF16)
    lo = (x - hi.astype(F32)).astype(BF16)
    cb = c.astype(BF16)
    return (lax.dot_general(cb, hi, (dims, ((), ())), preferred_element_type=F32)
            + lax.dot_general(cb, lo, (dims, ((), ())), preferred_element_type=F32))


@jax.custom_vjp
def _tri_sum(cmat, x):
    return _split_dot(cmat, x, NN)


def _tri_sum_fwd(cmat, x):
    return _split_dot(cmat, x, NN), cmat


def _tri_sum_bwd(cmat, ct):
    return jnp.zeros_like(cmat), _split_dot(cmat, ct, TN)


_tri_sum.defvjp(_tri_sum_fwd, _tri_sum_bwd)


def _call_with_rider(body, grid, in_specs, out_specs, out_shape, scratch, sem, name, args, rider):
    if rider is None:
        return pl.pallas_call(body, grid=grid, in_specs=in_specs, out_specs=out_specs, out_shape=out_shape,
                              scratch_shapes=scratch, compiler_params=_cparams(sem), name=name)(*args), None
    outs_l = list(out_shape) if isinstance(out_shape, (list, tuple)) else [out_shape]
    specs_l = list(out_specs) if isinstance(out_specs, (list, tuple)) else [out_specs]
    ni, no, ns = len(in_specs), len(outs_l), len(scratch)
    nri, nro = len(rider.inputs), len(rider.out_shapes)
    any_spec = pl.BlockSpec(memory_space=pl.ANY)

    def wrapped(*refs):
        ins, rin = refs[:ni], refs[ni:ni + nri]
        o0 = ni + nri
        outs, rout = refs[o0:o0 + no], refs[o0 + no:o0 + no + nro]
        s0 = o0 + no + nro
        scr, rsem = refs[s0:s0 + ns], refs[s0 + ns:]
        ids = [pl.program_id(d) for d in range(len(grid))]
        first = functools.reduce(jnp.logical_and, [i == 0 for i in ids])
        last = functools.reduce(jnp.logical_and, [i == g - 1 for i, g in zip(ids, grid)])

        @pl.when(first)
        def _():
            rider.start(rin, rout, rsem)

        body(*ins, *outs, *scr)

        @pl.when(last)
        def _():
            rider.finish(rin, rout, rsem)

    res = pl.pallas_call(
        wrapped, grid=grid, in_specs=list(in_specs) + [any_spec] * nri, out_specs=specs_l + [any_spec] * nro,
        out_shape=outs_l + rider.out_shapes, scratch_shapes=list(scratch) + rider.scratch,
        input_output_aliases={ni + i: no + j for i, j in rider.aliases.items()},
        compiler_params=_cparams(("arbitrary",) * len(grid)), name=name)(*args, *rider.inputs)
    main = res[:no]
    return (main if isinstance(out_shape, (list, tuple)) else main[0]), list(res[no:])


def matmul(a, b, mode, name, out_dtype=F32, res=None, tm=1024, tn=1024, tk=512, exact=False,
           a_halves=False, b_halves=False, b_blocked=False, out_blocked=False, rider=None):
    if mode == "nn":
        M, K = a.shape
        N = b.shape[0] * b.shape[2] if b_blocked else b.shape[1]
        if b_blocked:
            tn = b.shape[2]
    elif mode == "nt":
        M, K = (a.shape[1], 2 * a.shape[2]) if a_halves else a.shape
        N = b.shape[1] if b_blocked else b.shape[0]
        if b_blocked:
            tk = b.shape[2]
    else:
        K, M = a.shape
        N = 2 * b.shape[2] if b_halves else b.shape[1]
    tm, tn, tk = _fit(M, tm, 8 if mode != "tn" else 128), _fit(N, tn), _fit(K, tk, 128 if mode != "tn" else 16)
    nk = K // tk
    dims = {"nn": NN, "nt": NT, "tn": TN}[mode]

    def body(*refs):
        if res is None:
            a_ref, b_ref, o_ref = refs[:3]
            r_ref = None
            acc = refs[3] if nk > 1 else None
        else:
            a_ref, b_ref, r_ref, o_ref = refs[:4]
            acc = refs[4] if nk > 1 else None
        def product():
            if exact:
                return lax.dot_general(a_ref[...], b_ref[...], (dims, ((), ())), preferred_element_type=F32, precision=HI)
            return _bdot(a_ref[...], b_ref[...], dims)

        def finish(val):
            if r_ref is not None:
                val = val + r_ref[...]
            o_ref[...] = val.astype(out_dtype)

        if nk == 1:
            finish(product())
        else:
            kk = pl.program_id(2)

            @pl.when(kk == 0)
            def _():
                acc[...] = jnp.zeros_like(acc)

            acc[...] += product()

            @pl.when(kk == nk - 1)
            def _():
                finish(acc[...])

    if mode == "tn":
        a_spec = pl.BlockSpec((tk, tm), lambda i, j, k: (k, i))
    elif a_halves:
        nh = K // 2 // tk
        a_spec = pl.BlockSpec((None, tm, tk), lambda i, j, k: (k // nh, i, k % nh))
    else:
        a_spec = pl.BlockSpec((tm, tk), lambda i, j, k: (i, k))
    if b_blocked:
        b_spec = (pl.BlockSpec((None, tk, tn), lambda i, j, k: (j, k, 0)) if mode == "nn"
                  else pl.BlockSpec((None, tn, tk), lambda i, j, k: (k, j, 0)))
    elif b_halves:
        nh = N // 2 // tn
        b_spec = pl.BlockSpec((None, tk, tn), lambda i, j, k: (j // nh, k, j % nh))
    elif mode == "nt":
        b_spec = pl.BlockSpec((tn, tk), lambda i, j, k: (j, k))
    else:
        b_spec = pl.BlockSpec((tk, tn), lambda i, j, k: (k, j))
    if out_blocked:
        o_spec, o_shape = pl.BlockSpec((None, tm, tn), lambda i, j, k: (j, i, 0)), SDS((N // tn, M, tn), out_dtype)
    else:
        o_spec, o_shape = pl.BlockSpec((tm, tn), lambda i, j, k: (i, j)), SDS((M, N), out_dtype)
    in_specs, args = [a_spec, b_spec], [a, b]
    if res is not None:
        in_specs.append(o_spec)
        args.append(res)
    out, carried = _call_with_rider(
        body, (M // tm, N // tn, nk), in_specs, o_spec, o_shape, [pltpu.VMEM((tm, tn), F32)] if nk > 1 else [],
        ("parallel", "parallel", "arbitrary"), name, args, rider)
    return out if rider is None else (out, carried)


def rms_fwd(x, g, name, rider=None):
    T, D = x.shape
    tm = _fit(T, 512, 8)

    def body(x_ref, g_ref, o_ref):
        xv = x_ref[...]
        r = lax.rsqrt(jnp.mean(xv * xv, axis=-1, keepdims=True) + RMS_EPS)
        o_ref[...] = (xv * r * g_ref[...]).astype(BF16)

    out, carried = _call_with_rider(
        body, (T // tm,), [pl.BlockSpec((tm, D), lambda i: (i, 0)), pl.BlockSpec((1, D), lambda i: (0, 0))],
        pl.BlockSpec((tm, D), lambda i: (i, 0)), SDS((T, D), BF16), [], ("parallel",), name, [x, g], rider)
    return out if rider is None else (out, carried)


def rms_bwd(x, g, dy, dres, name):
    T, D = x.shape
    tm = _fit(T, 256, 8)
    has_res = dres is not None

    def body(*refs):
        if has_res:
            x_ref, g_ref, dy_ref, dres_ref, dx_ref, dxb_ref, dg_ref = refs
        else:
            x_ref, g_ref, dy_ref, dx_ref, dxb_ref, dg_ref = refs
        xv, dyv = x_ref[...], dy_ref[...]
        r = lax.rsqrt(jnp.mean(xv * xv, axis=-1, keepdims=True) + RMS_EPS)
        xh = xv * r
        dxh = dyv * g_ref[...]
        dx = r * (dxh - xh * jnp.mean(dxh * xh, axis=-1, keepdims=True))
        if has_res:
            dx = dx + dres_ref[...]
        dx_ref[...] = dx
        dxb_ref[...] = dx.astype(BF16)

        @pl.when(pl.program_id(0) == 0)
        def _():
            dg_ref[...] = jnp.zeros_like(dg_ref)

        dg_ref[...] += jnp.sum(dyv * xh, axis=0, keepdims=True)

    row = pl.BlockSpec((tm, D), lambda i: (i, 0))
    vec = pl.BlockSpec((1, D), lambda i: (0, 0))
    args = [x, g, dy] + ([dres] if has_res else [])
    return pl.pallas_call(
        body, grid=(T // tm,), in_specs=[row, vec, row] + ([row] if has_res else []),
        out_specs=[row, row, vec], out_shape=[SDS((T, D), F32), SDS((T, D), BF16), SDS((1, D), F32)],
        compiler_params=_cparams(("arbitrary",)), name=name)(*args)


def ffn_up_swiglu(h, w13b, name, rider=None):
    T, D = h.shape
    nb, _, tb = w13b.shape
    nh = nb // 2
    tm = _fit(T, 512, 8)

    def body(h_ref, wg_ref, wu_ref, gu_ref, a_ref):
        hv = h_ref[...]
        gv = _bdot(hv, wg_ref[...], NN)
        uv = _bdot(hv, wu_ref[...], NN)
        gu_ref[0] = gv.astype(BF16)
        gu_ref[1] = uv.astype(BF16)
        a_ref[...] = (gv * jax.nn.sigmoid(gv) * uv).astype(BF16)

    (gu, act), carried = _call_with_rider(
        body, (nh, T // tm),
        [pl.BlockSpec((tm, D), lambda j, i: (i, 0)), pl.BlockSpec((None, D, tb), lambda j, i: (j, 0, 0)),
         pl.BlockSpec((None, D, tb), lambda j, i: (j + nh, 0, 0))],
        [pl.BlockSpec((2, tm, tb), lambda j, i: (0, i, j)), pl.BlockSpec((tm, tb), lambda j, i: (i, j))],
        [SDS((2, T, nh * tb), BF16), SDS((T, nh * tb), BF16)], [], ("parallel", "parallel"), name, [h, w13b, w13b], rider)
    return gu, act, carried


def ffn_down_bwd(dy_b, w2, gu, name):
    T, D = dy_b.shape
    Fh = w2.shape[0]
    tm, tf = _fit(T, 1024, 8), _fit(Fh, 512)
    nf = Fh // tf

    def body(dy_ref, w_ref, g_ref, u_ref, o_ref):
        dav = _bdot(dy_ref[...], w_ref[...], NT)
        gv, uv = g_ref[...].astype(F32), u_ref[...].astype(F32)
        sg = jax.nn.sigmoid(gv)
        o_ref[0] = (dav * uv * (sg * (1.0 + gv * (1.0 - sg)))).astype(BF16)
        o_ref[1] = (dav * gv * sg).astype(BF16)

    return pl.pallas_call(
        body, grid=(T // tm, nf),
        in_specs=[pl.BlockSpec((tm, D), lambda i, j: (i, 0)), pl.BlockSpec((tf, D), lambda i, j: (j, 0)),
                  pl.BlockSpec((None, tm, tf), lambda i, j: (0, i, j)), pl.BlockSpec((None, tm, tf), lambda i, j: (1, i, j))],
        out_specs=pl.BlockSpec((2, tm, tf), lambda i, j: (0, i, j)), out_shape=SDS((2, T, Fh), BF16),
        compiler_params=_cparams(("parallel", "parallel")), name=name)(dy_b, w2, gu, gu)


def loss_bwd(y, tgt, name):
    T, D = y.shape
    tm = _fit(T, 512, 8)

    def body(y_ref, t_ref, dy_ref, dyb_ref, l_ref):
        e = y_ref[...] - t_ref[...]
        dy = e * (1.0 / D)
        dy_ref[...] = dy
        dyb_ref[...] = dy.astype(BF16)

        @pl.when(pl.program_id(0) == 0)
        def _():
            l_ref[...] = jnp.zeros_like(l_ref)

        l_ref[...] += jnp.sum(jnp.sum(e * e, axis=1, keepdims=True), axis=0, keepdims=True)

    row = pl.BlockSpec((tm, D), lambda i: (i, 0))
    return pl.pallas_call(
        body, grid=(T // tm,), in_specs=[row, row], out_specs=[row, row, pl.BlockSpec((8, 128), lambda i: (0, 0))],
        out_shape=[SDS((T, D), F32), SDS((T, D), BF16), SDS((8, 128), F32)],
        compiler_params=_cparams(("arbitrary",)), name=name)(y, tgt)


def _log_sigmoid(z):
    return jnp.minimum(z, 0.0) - jnp.log(1.0 + jnp.exp(-jnp.abs(z)))


def _gla_group(q, k, v, tail, wg, bg, s, cmat, mask, backward_dir):
    n = q.shape[0] // GLA_C
    z = _bdot_vjp(tail, wg, NN) + bg
    la = _log_sigmoid(z) * (1.0 / GLA_TAU)
    cum = _tri_sum(cmat, la)
    chunks = [slice(g * GLA_C, (g + 1) * GLA_C) for g in range(n)]
    last = [jnp.sum(la[sl], axis=0, keepdims=True) for sl in chunks]
    last_rows = jnp.concatenate([jnp.broadcast_to(t, (GLA_C, t.shape[1])) for t in last], axis=0)
    q_e = q * jnp.exp(cum) * (GLA_HK ** -0.5)
    k_e = k * jnp.exp(-cum)
    k_end = k * jnp.exp(last_rows - cum)
    sc = jnp.where(mask > 0.5, _bdot_vjp(q_e, k_e, NT), 0.0)
    o = _bdot_vjp(sc, v, NN)
    o_inter = [None] * n
    for g in (range(n - 1, -1, -1) if backward_dir else range(n)):
        o_inter[g] = _bdot_vjp(q_e[chunks[g]], s, NN)
        s = s * jnp.transpose(jnp.exp(last[g])) + _bdot_vjp(k_end[chunks[g]], v[chunks[g]], TN)
    return o + jnp.concatenate(o_inter, axis=0), s


def _gla_consts(backward_dir, GC):
    i = np.arange(GC)
    same = (i[:, None] // GLA_C) == (i[None, :] // GLA_C)
    if backward_dir:
        return (same & (i[None, :] >= i[:, None])).astype(np.float32), (same & (i[None, :] > i[:, None])).astype(np.float32)
    tri = (same & (i[None, :] <= i[:, None])).astype(np.float32)
    return tri, tri


GLA_HB = 2


def _gla_in_specs(GC, nmap):
    return [
        pl.BlockSpec((GC, GLA_HB * GLA_HK), lambda h, n: (nmap(n), h)),
        pl.BlockSpec((GC, GLA_HB * GLA_HK), lambda h, n: (nmap(n), GLA_H // GLA_HB + h)),
        pl.BlockSpec((GC, GLA_HB * GLA_HV), lambda h, n: (nmap(n), 2 * GLA_DK // (GLA_HB * GLA_HV) + h)),
        pl.BlockSpec((GC, 128), lambda h, n: (nmap(n), COL_TAIL // 128)),
        pl.BlockSpec((GLA_HB, 128, 128), lambda h, n: (h, 0, 0)),
        pl.BlockSpec((GLA_HB, 1, 128), lambda h, n: (h, 0, 0)),
        pl.BlockSpec((GC, GC), lambda h, n: (0, 0)),
        pl.BlockSpec((GC, GC), lambda h, n: (0, 0)),
    ]


def _head_cols(ref, hh, width):
    return ref[:, hh * width:(hh + 1) * width].astype(F32)


def gla_fwd(proj, wgpad, bg, backward_dir, name, rider=None):
    T = proj.shape[0]
    GC = min(GLA_G * GLA_C, T)
    NG = T // GC
    cmat, mask = _gla_consts(backward_dir, GC)
    nmap = (lambda n: NG - 1 - n) if backward_dir else (lambda n: n)

    def body(q_ref, k_ref, v_ref, t_ref, wg_ref, bg_ref, c_ref, m_ref, o_ref, ss_ref, s_scr):
        @pl.when(pl.program_id(1) == 0)
        def _():
            s_scr[...] = jnp.zeros_like(s_scr)

        tail, cm, mk = t_ref[...].astype(F32), c_ref[...], m_ref[...]
        for hh in range(GLA_HB):
            s0 = s_scr[hh]
            ss_ref[hh] = s0
            o, s1 = _gla_group(_head_cols(q_ref, hh, GLA_HK), _head_cols(k_ref, hh, GLA_HK), _head_cols(v_ref, hh, GLA_HV),
                               tail, wg_ref[hh], bg_ref[hh], s0, cm, mk, backward_dir)
            o_ref[:, hh * GLA_HV:(hh + 1) * GLA_HV] = o
            s_scr[hh] = s1

    (o, states), carried = _call_with_rider(
        body, (GLA_H // GLA_HB, NG), _gla_in_specs(GC, nmap),
        [pl.BlockSpec((GC, GLA_HB * GLA_HV), lambda h, n: (nmap(n), h)),
         pl.BlockSpec((GLA_HB, None, GLA_HK, GLA_HV), lambda h, n: (h, nmap(n), 0, 0))],
        [SDS((T, GLA_DV), F32), SDS((GLA_H, NG, GLA_HK, GLA_HV), F32)], [pltpu.VMEM((GLA_HB, GLA_HK, GLA_HV), F32)],
        ("parallel", "arbitrary"), name, [proj, proj, proj, proj, wgpad, bg, cmat, mask], rider)
    return (o, states) if rider is None else (o, states, carried)


def gla_bwd(proj, wgpad, bg, ssave, do, backward_dir, name):
    T = proj.shape[0]
    GC = min(GLA_G * GLA_C, T)
    NG = T // GC
    cmat, mask = _gla_consts(backward_dir, GC)
    nmap = (lambda n: n) if backward_dir else (lambda n: NG - 1 - n)

    def body(q_ref, k_ref, v_ref, t_ref, wg_ref, bg_ref, c_ref, m_ref, ss_ref, do_ref,
             dq_ref, dk_ref, dv_ref, dt_ref, dwg_ref, dbg_ref, ds_scr):
        @pl.when(pl.program_id(1) == 0)
        def _():
            ds_scr[...] = jnp.zeros_like(ds_scr)
            dwg_ref[...] = jnp.zeros_like(dwg_ref)
            dbg_ref[...] = jnp.zeros_like(dbg_ref)

        tail, cm, mk = t_ref[...].astype(F32), c_ref[...], m_ref[...]
        fn = lambda q, k, v, t, wg, b, s: _gla_group(q, k, v, t, wg, b, s, cm, mk, backward_dir)
        for hh in range(GLA_HB):
            _, vjp = jax.vjp(fn, _head_cols(q_ref, hh, GLA_HK), _head_cols(k_ref, hh, GLA_HK), _head_cols(v_ref, hh, GLA_HV),
                             tail, wg_ref[hh], bg_ref[hh], ss_ref[hh])
            dq, dk, dv, dt, dwg, dbg, ds = vjp((_head_cols(do_ref, hh, GLA_HV), ds_scr[hh]))
            dq_ref[:, hh * GLA_HK:(hh + 1) * GLA_HK] = dq.astype(BF16)
            dk_ref[:, hh * GLA_HK:(hh + 1) * GLA_HK] = dk.astype(BF16)
            dv_ref[:, hh * GLA_HV:(hh + 1) * GLA_HV] = dv.astype(BF16)
            dt_ref[hh] = dt.astype(BF16)
            dwg_ref[hh] += dwg
            dbg_ref[hh] += dbg
            ds_scr[hh] = ds

    in_specs = _gla_in_specs(GC, nmap) + [
        pl.BlockSpec((GLA_HB, None, GLA_HK, GLA_HV), lambda h, n: (h, nmap(n), 0, 0)),
        pl.BlockSpec((GC, GLA_HB * GLA_HV), lambda h, n: (nmap(n), h)),
    ]
    out_specs = [
        pl.BlockSpec((GC, GLA_HB * GLA_HK), lambda h, n: (nmap(n), h)),
        pl.BlockSpec((GC, GLA_HB * GLA_HK), lambda h, n: (nmap(n), h)),
        pl.BlockSpec((GC, GLA_HB * GLA_HV), lambda h, n: (nmap(n), h)),
        pl.BlockSpec((GLA_HB, GC, 128), lambda h, n: (h, nmap(n), 0)),
        pl.BlockSpec((GLA_HB, 128, 128), lambda h, n: (h, 0, 0)),
        pl.BlockSpec((GLA_HB, 1, 128), lambda h, n: (h, 0, 0)),
    ]
    out_shape = [SDS((T, GLA_DK), BF16), SDS((T, GLA_DK), BF16), SDS((T, GLA_DV), BF16), SDS((GLA_H, T, 128), BF16),
                 SDS((GLA_H, 128, 128), F32), SDS((GLA_H, 1, 128), F32)]
    return pl.pallas_call(
        body, grid=(GLA_H // GLA_HB, NG), in_specs=in_specs, out_specs=out_specs, out_shape=out_shape,
        scratch_shapes=[pltpu.VMEM((GLA_HB, GLA_HK, GLA_HV), F32)],
        compiler_params=_cparams(("parallel", "arbitrary")), name=name)(proj, proj, proj, proj, wgpad, bg, cmat, mask, ssave, do)


def _block_diag(width, hd):
    i = np.arange(width)
    return ((i[:, None] // hd) == (i[None, :] // hd)).astype(np.float32) / hd


def _norm_heads(t, hd):
    outs = []
    for h in range(t.shape[1] // hd):
        th = t[:, h * hd:(h + 1) * hd]
        outs.append(th * lax.rsqrt(jnp.mean(th * th, axis=-1, keepdims=True) + RMS_EPS))
    return jnp.concatenate(outs, axis=1)


def _split_dot_right(x, c):
    hi = x.astype(BF16)
    lo = (x - hi.astype(F32)).astype(BF16)
    cb = c.astype(BF16)
    return jnp.dot(hi, cb, preferred_element_type=F32) + jnp.dot(lo, cb, preferred_element_type=F32)


@jax.custom_vjp
def _head_means(x, bd):
    return _split_dot_right(x, bd)


def _head_means_fwd(x, bd):
    return _split_dot_right(x, bd), bd


def _head_means_bwd(bd, ct):
    return _split_dot_right(ct, bd), jnp.zeros_like(bd)


_head_means.defvjp(_head_means_fwd, _head_means_bwd)


def _norm_bd(t, bd):
    return t * lax.rsqrt(_head_means(t * t, bd) + RMS_EPS)


def _pre_fn(nq, nk, mq, gq, gk, gm, bd):
    return _norm_bd(nq, bd) * gq, _norm_bd(nk, bd) * gk, _norm_heads(mq, MEM_HD) * gm


def pre_fwd(proj, gq, gk, gm, name):
    T = proj.shape[0]
    tm = _fit(T, 512, 8)
    bd = _block_diag(NA_W, NA_HD)

    def body(nq_ref, nk_ref, nv_ref, mq_ref, gq_ref, gk_ref, gm_ref, bd_ref, q_ref, k_ref, v_ref, m_ref):
        qn, kn, mn = _pre_fn(nq_ref[...].astype(F32), nk_ref[...].astype(F32), mq_ref[...].astype(F32),
                             gq_ref[...], gk_ref[...], gm_ref[...], bd_ref[...])
        q_ref[...] = qn.astype(BF16)
        k_ref[...] = kn.astype(BF16)
        v_ref[...] = nv_ref[...].astype(BF16)
        m_ref[...] = mn.astype(BF16)

    col = lambda c0: pl.BlockSpec((tm, 512), lambda i: (i, c0 // 512))
    vec = pl.BlockSpec((1, 512), lambda i: (0, 0))
    row = pl.BlockSpec((tm, 512), lambda i: (i, 0))
    return pl.pallas_call(
        body, grid=(T // tm,),
        in_specs=[col(COL_NQ), col(COL_NK), col(COL_NV), col(COL_MQ), vec, vec, vec, pl.BlockSpec((NA_W, NA_W), lambda i: (0, 0))],
        out_specs=[row] * 4, out_shape=[SDS((T, 512), BF16)] * 4,
        compiler_params=_cparams(("parallel",)), name=name)(proj, proj, proj, proj, gq, gk, gm, bd)


def pre_bwd(proj, gq, gk, gm, d_qn, d_kn, d_nv, d_mn, dq_f, dq_b, dk_f, dk_b, dv_f, dv_b, d_r, dt_f, dt_b, name):
    T = proj.shape[0]
    tm = _fit(T, 256, 8)
    bd = _block_diag(NA_W, NA_HD)

    def body(nq_ref, nk_ref, mq_ref, gq_ref, gk_ref, gm_ref, bd_ref, dqn_ref, dkn_ref, dnv_ref, dmn_ref,
             dqf_ref, dqb_ref, dkf_ref, dkb_ref, dvf_ref, dvb_ref, dr_ref, dtf_ref, dtb_ref,
             o_ref, dgq_ref, dgk_ref, dgm_ref):
        bdv = bd_ref[...]
        fn = lambda a, b, c, x, y, z: _pre_fn(a, b, c, x, y, z, bdv)
        _, vjp = jax.vjp(fn, nq_ref[...].astype(F32), nk_ref[...].astype(F32), mq_ref[...].astype(F32),
                         gq_ref[...], gk_ref[...], gm_ref[...])
        d_nq, d_nk, d_mq, dgq, dgk, dgm = vjp((dqn_ref[...], dkn_ref[...], dmn_ref[...]))
        both = lambda f_ref, b_ref: (f_ref[...].astype(F32) + b_ref[...].astype(F32)).astype(BF16)
        o_ref[:, 0:512] = both(dqf_ref, dqb_ref)
        o_ref[:, 512:1024] = both(dkf_ref, dkb_ref)
        o_ref[:, 1024:2048] = both(dvf_ref, dvb_ref)
        o_ref[:, COL_R:COL_R + 1024] = dr_ref[...].astype(BF16)
        o_ref[:, COL_NQ:COL_NQ + 512] = d_nq.astype(BF16)
        o_ref[:, COL_NK:COL_NK + 512] = d_nk.astype(BF16)
        o_ref[:, COL_NV:COL_NV + 512] = dnv_ref[...].astype(BF16)
        o_ref[:, COL_MQ:COL_MQ + 512] = d_mq.astype(BF16)
        dt = dtf_ref[0].astype(F32) + dtb_ref[0].astype(F32)
        for h in range(1, GLA_H):
            dt = dt + dtf_ref[h].astype(F32) + dtb_ref[h].astype(F32)
        o_ref[:, COL_TAIL:COL_TAIL + 128] = dt.astype(BF16)
        o_ref[:, COL_TAIL + 128:PC] = jnp.zeros((tm, PC - COL_TAIL - 128), BF16)

        @pl.when(pl.program_id(0) == 0)
        def _():
            dgq_ref[...] = jnp.zeros_like(dgq_ref)
            dgk_ref[...] = jnp.zeros_like(dgk_ref)
            dgm_ref[...] = jnp.zeros_like(dgm_ref)

        dgq_ref[...] += dgq
        dgk_ref[...] += dgk
        dgm_ref[...] += dgm

    col = lambda c0: pl.BlockSpec((tm, 512), lambda i: (i, c0 // 512))
    vec = pl.BlockSpec((1, 512), lambda i: (0, 0))
    r512 = pl.BlockSpec((tm, 512), lambda i: (i, 0))
    r1024 = pl.BlockSpec((tm, 1024), lambda i: (i, 0))
    tl = pl.BlockSpec((GLA_H, tm, 128), lambda i: (0, i, 0))
    in_specs = [col(COL_NQ), col(COL_NK), col(COL_MQ), vec, vec, vec, pl.BlockSpec((NA_W, NA_W), lambda i: (0, 0)),
                r512, r512, r512, r512, r512, r512, r512, r512, r1024, r1024, r1024, tl, tl]
    return pl.pallas_call(
        body, grid=(T // tm,), in_specs=in_specs,
        out_specs=[pl.BlockSpec((tm, PC), lambda i: (i, 0)), vec, vec, vec],
        out_shape=[SDS((T, PC), BF16), SDS((1, 512), F32), SDS((1, 512), F32), SDS((1, 512), F32)],
        compiler_params=_cparams(("arbitrary",)), name=name)(
            proj, proj, proj, gq, gk, gm, bd, d_qn, d_kn, d_nv, d_mn, dq_f, dq_b, dk_f, dk_b, dv_f, dv_b, d_r, dt_f, dt_b)


def _post_fn(o_f, o_b, r, o_na, o_mem, g_gla, g_na, g_mem, bd):
    y_gla = _norm_heads(o_f + o_b, GLA_HV) * g_gla * (r * jax.nn.sigmoid(r))
    y_na = _norm_bd(o_na, bd) * g_na
    y_mem = _norm_heads(o_mem, MEM_HD) * g_mem
    return jnp.concatenate([y_gla, y_na, y_mem], axis=1)


def post_fwd(o_f, o_b, proj, o_na, o_mem, g_gla, g_na, g_mem, name):
    T = proj.shape[0]
    tm = _fit(T, 256, 8)
    bd = _block_diag(NA_W, NA_HD)

    def body(of_ref, ob_ref, r_ref, ona_ref, omem_ref, gg_ref, gn_ref, gm_ref, bd_ref, y_ref):
        y_ref[...] = _post_fn(of_ref[...], ob_ref[...], r_ref[...].astype(F32), ona_ref[...], omem_ref[...],
                              gg_ref[...], gn_ref[...], gm_ref[...], bd_ref[...]).astype(BF16)

    r1024 = pl.BlockSpec((tm, 1024), lambda i: (i, 0))
    r512 = pl.BlockSpec((tm, 512), lambda i: (i, 0))
    in_specs = [r1024, r1024, pl.BlockSpec((tm, 1024), lambda i: (i, COL_R // 1024)), r512, r512,
                pl.BlockSpec((1, 1024), lambda i: (0, 0)), pl.BlockSpec((1, 512), lambda i: (0, 0)),
                pl.BlockSpec((1, 512), lambda i: (0, 0)), pl.BlockSpec((NA_W, NA_W), lambda i: (0, 0))]
    return pl.pallas_call(
        body, grid=(T // tm,), in_specs=in_specs, out_specs=pl.BlockSpec((tm, D_MODEL), lambda i: (i, 0)),
        out_shape=SDS((T, D_MODEL), BF16), compiler_params=_cparams(("parallel",)), name=name)(
            o_f, o_b, proj, o_na, o_mem, g_gla, g_na, g_mem, bd)


def post_bwd(o_f, o_b, proj, o_na, o_mem, g_gla, g_na, g_mem, dy, name):
    T = proj.shape[0]
    tm = _fit(T, 256, 8)
    bd = _block_diag(NA_W, NA_HD)

    def body(of_ref, ob_ref, r_ref, ona_ref, omem_ref, gg_ref, gn_ref, gm_ref, bd_ref, dy_ref,
             do_ref, dr_ref, dna_ref, dmem_ref, dgg_ref, dgn_ref, dgm_ref):
        bdv = bd_ref[...]
        fn = lambda o, r, a, m, x, y, z: _post_fn(o, 0.0, r, a, m, x, y, z, bdv)
        _, vjp = jax.vjp(fn, of_ref[...] + ob_ref[...], r_ref[...].astype(F32), ona_ref[...], omem_ref[...],
                         gg_ref[...], gn_ref[...], gm_ref[...])
        d_o, d_r, d_na, d_mem, dgg, dgn, dgm = vjp(dy_ref[...])
        do_ref[...] = d_o.astype(BF16)
        dr_ref[...] = d_r.astype(BF16)
        dna_ref[...] = d_na.astype(BF16)
        dmem_ref[...] = d_mem.astype(BF16)

        @pl.when(pl.program_id(0) == 0)
        def _():
            dgg_ref[...] = jnp.zeros_like(dgg_ref)
            dgn_ref[...] = jnp.zeros_like(dgn_ref)
            dgm_ref[...] = jnp.zeros_like(dgm_ref)

        dgg_ref[...] += dgg
        dgn_ref[...] += dgn
        dgm_ref[...] += dgm

    r1024 = pl.BlockSpec((tm, 1024), lambda i: (i, 0))
    r512 = pl.BlockSpec((tm, 512), lambda i: (i, 0))
    v1024 = pl.BlockSpec((1, 1024), lambda i: (0, 0))
    v512 = pl.BlockSpec((1, 512), lambda i: (0, 0))
    in_specs = [r1024, r1024, pl.BlockSpec((tm, 1024), lambda i: (i, COL_R // 1024)), r512, r512, v1024, v512, v512,
                pl.BlockSpec((NA_W, NA_W), lambda i: (0, 0)), pl.BlockSpec((tm, D_MODEL), lambda i: (i, 0))]
    return pl.pallas_call(
        body, grid=(T // tm,), in_specs=in_specs, out_specs=[r1024, r1024, r512, r512, v1024, v512, v512],
        out_shape=[SDS((T, 1024), BF16), SDS((T, 1024), BF16), SDS((T, 512), BF16), SDS((T, 512), BF16),
                   SDS((1, 1024), F32), SDS((1, 512), F32), SDS((1, 512), F32)],
        compiler_params=_cparams(("arbitrary",)), name=name)(o_f, o_b, proj, o_na, o_mem, g_gla, g_na, g_mem, bd, dy)


NA_RB = 8


def _na_row_scores(q_ref, k_ref, v_ref, tb_ref, rb, j, n_rows):
    r = rb * NA_RB + j
    rs = jnp.clip(r - NA_ROWS // 2, 0, n_rows - NA_ROWS)
    dr0 = rs - r + (NA_ROWS - 1)
    tok = pl.ds(pl.multiple_of(rs * GRID_W, GRID_W), NA_ROWS * GRID_W)
    q = q_ref[j * GRID_W:(j + 1) * GRID_W, :]
    kk, vv = k_ref[tok, :], v_ref[tok, :]
    bias = jnp.concatenate([tb_ref[dr0 + 2 * i] for i in range(NA_ROWS // 2)], axis=1)
    s = _bdot(q, kk, NT) * (NA_HD ** -0.5) + bias
    m = jnp.max(s, axis=1, keepdims=True)
    p = jnp.exp(s - m)
    l = jnp.sum(p, axis=1, keepdims=True)
    return q, kk, vv, p, l, tok, dr0


def natten_fwd(q, k, v, tb2, name):
    H, T, hd = q.shape
    n_rows = T // GRID_W
    rbt = NA_RB * GRID_W

    def body(q_ref, k_ref, v_ref, tb_ref, o_ref):
        rb = pl.program_id(1)
        for j in range(NA_RB):
            _, _, vv, p, l, _, _ = _na_row_scores(q_ref, k_ref, v_ref, tb_ref, rb, j, n_rows)
            o_ref[j * GRID_W:(j + 1) * GRID_W, :] = _bdot(p, vv, NN) / l

    whole = pl.BlockSpec((None, T, hd), lambda h, r: (h, 0, 0))
    blk = pl.BlockSpec((None, rbt, hd), lambda h, r: (h, r, 0))
    return pl.pallas_call(
        body, grid=(H, n_rows // NA_RB),
        in_specs=[blk, whole, whole, pl.BlockSpec((None, 2 * NA_ROWS - 2, GRID_W, 2 * GRID_W), lambda h, r: (h, 0, 0, 0))],
        out_specs=blk, out_shape=SDS((H, T, hd), F32),
        compiler_params=_cparams(("parallel", "arbitrary")), name=name)(q, k, v, tb2)


def natten_bwd(q, k, v, tb2, do, name):
    H, T, hd = q.shape
    n_rows = T // GRID_W
    rbt = NA_RB * GRID_W
    scale = NA_HD ** -0.5

    def body(q_ref, k_ref, v_ref, tb_ref, do_ref, dq_ref, dk_ref, dv_ref, dtb_ref):
        rb = pl.program_id(1)

        @pl.when(rb == 0)
        def _():
            dk_ref[...] = jnp.zeros_like(dk_ref)
            dv_ref[...] = jnp.zeros_like(dv_ref)
            dtb_ref[...] = jnp.zeros_like(dtb_ref)

        for j in range(NA_RB):
            qv, kk, vv, p, l, tok, dr0 = _na_row_scores(q_ref, k_ref, v_ref, tb_ref, rb, j, n_rows)
            p = p / l
            dov = do_ref[j * GRID_W:(j + 1) * GRID_W, :]
            dp = _bdot(dov, vv, NT)
            ds = p * (dp - jnp.sum(dp * p, axis=1, keepdims=True))
            dq_ref[j * GRID_W:(j + 1) * GRID_W, :] = _bdot(ds, kk, NN) * scale
            dk_ref[tok, :] += _bdot(ds, qv, TN) * scale
            dv_ref[tok, :] += _bdot(p, dov, TN)
            for i in range(NA_ROWS // 2):
                dtb_ref[dr0 + 2 * i] += ds[:, 2 * GRID_W * i:2 * GRID_W * (i + 1)]

    whole = pl.BlockSpec((None, T, hd), lambda h, r: (h, 0, 0))
    blk = pl.BlockSpec((None, rbt, hd), lambda h, r: (h, r, 0))
    tbs = pl.BlockSpec((None, 2 * NA_ROWS - 2, GRID_W, 2 * GRID_W), lambda h, r: (h, 0, 0, 0))
    return pl.pallas_call(
        body, grid=(H, n_rows // NA_RB), in_specs=[blk, whole, whole, tbs, blk],
        out_specs=[blk, whole, whole, tbs],
        out_shape=[SDS((H, T, hd), F32), SDS((H, T, hd), F32), SDS((H, T, hd), F32), SDS(tb2.shape, F32)],
        compiler_params=_cparams(("parallel", "arbitrary")), name=name)(q, k, v, tb2, do)


def _rpb_expand_consts():
    qc = np.arange(GRID_W)[:, None]
    kc = np.arange(GRID_W)[None, :]
    cs = np.clip(qc - NA_COLS // 2, 0, GRID_W - NA_COLS)
    inside = (kc >= cs) & (kc < cs + NA_COLS)
    dc = np.clip(kc - qc, -(NA_COLS - 1), NA_COLS - 1) + (NA_COLS - 1)
    e = np.zeros((128, GRID_W * GRID_W), np.float32)
    flat = (qc * GRID_W + kc)
    e[dc[inside], flat[inside]] = 1.0
    neg = np.where(inside, 0.0, NEG).astype(np.float32).reshape(1, -1)
    return e, neg


def _rpb_fold_consts():
    sa = np.zeros((NA_H * 15, NA_H * 14), np.float32)
    sb = np.zeros((NA_H * 15, NA_H * 14), np.float32)
    for h in range(NA_H):
        for d in range(14):
            sa[h * 15 + d, h * 14 + d] = 1.0
            sb[h * 15 + d + 1, h * 14 + d] = 1.0
    return sa, sb


def rpb_table(rpb, name):
    e, neg = _rpb_expand_consts()
    rp = jnp.pad(rpb.reshape(NA_H * 15, 31), ((0, 0), (0, 128 - 31)))

    def body(r_ref, e_ref, n_ref, o_ref):
        o_ref[...] = jnp.dot(r_ref[...], e_ref[...], preferred_element_type=F32, precision=HI) + n_ref[...]

    t = pl.pallas_call(body, out_shape=SDS((NA_H * 15, GRID_W * GRID_W), F32), name=name)(rp, e, neg)
    t = t.reshape(NA_H, 15, GRID_W, GRID_W)
    return jnp.concatenate([t[:, :14], t[:, 1:]], axis=-1)


def rpb_table_bwd(dtb2, name):
    e, _ = _rpb_expand_consts()
    sa, sb = _rpb_fold_consts()
    a = dtb2[..., :GRID_W].reshape(NA_H * 14, GRID_W * GRID_W)
    b = dtb2[..., GRID_W:].reshape(NA_H * 14, GRID_W * GRID_W)

    def body(a_ref, b_ref, e_ref, sa_ref, sb_ref, o_ref):
        ev = e_ref[...]
        pa = lax.dot_general(a_ref[...], ev, (NT, ((), ())), preferred_element_type=F32, precision=HI)
        pb = lax.dot_general(b_ref[...], ev, (NT, ((), ())), preferred_element_type=F32, precision=HI)
        o_ref[...] = (jnp.dot(sa_ref[...], pa, preferred_element_type=F32, precision=HI)
                      + jnp.dot(sb_ref[...], pb, preferred_element_type=F32, precision=HI))

    d = pl.pallas_call(body, out_shape=SDS((NA_H * 15, 128), F32), name=name)(a, b, e, sa, sb)
    return d[:, :31].reshape(NA_H, 15, 31)


def _kprep_fn(kv, gk):
    return _norm_heads(kv[:, :MEM_W], MEM_HD) * gk, kv[:, MEM_W:]


def mem_kprep(kv, gk, name):
    def body(kv_ref, g_ref, k_ref, v_ref):
        kn, vv = _kprep_fn(kv_ref[...], g_ref[...])
        k_ref[...] = kn.astype(BF16)
        v_ref[...] = vv.astype(BF16)

    return pl.pallas_call(body, out_shape=[SDS((MEM_TOK, MEM_W), BF16)] * 2, name=name)(kv, gk)


def mem_kprep_bwd(kv, gk, dk, dv, name):
    def body(kv_ref, g_ref, dk_ref, dv_ref, dkv_ref, dg_ref):
        _, vjp = jax.vjp(_kprep_fn, kv_ref[...], g_ref[...])
        dkv, dg = vjp((dk_ref[...], dv_ref[...]))
        dkv_ref[...] = dkv.astype(BF16)
        dg_ref[...] = dg

    return pl.pallas_call(body, out_shape=[SDS((MEM_TOK, 2 * MEM_W), BF16), SDS((1, MEM_W), F32)], name=name)(kv, gk, dk, dv)


def _mem_probs(q_ref, k_ref, h):
    hs = slice(h * MEM_HD, (h + 1) * MEM_HD)
    qh, kh = q_ref[:, hs], k_ref[:, hs]
    s = _bdot(qh, kh, NT) * (MEM_HD ** -0.5)
    p = jnp.exp(s - jnp.max(s, axis=1, keepdims=True))
    return hs, qh, kh, p, jnp.sum(p, axis=1, keepdims=True)


def mem_attn_fwd(q, km, vm, name):
    T = q.shape[0]
    tm = _fit(T, 512, 8)

    def body(q_ref, k_ref, v_ref, o_ref):
        for h in range(MEM_H):
            hs, _, _, p, l = _mem_probs(q_ref, k_ref, h)
            o_ref[:, hs] = _bdot(p, v_ref[:, hs], NN) / l

    row = pl.BlockSpec((tm, MEM_W), lambda i: (i, 0))
    full = pl.BlockSpec((MEM_TOK, MEM_W), lambda i: (0, 0))
    return pl.pallas_call(body, grid=(T // tm,), in_specs=[row, full, full], out_specs=row, out_shape=SDS((T, MEM_W), F32),
                          compiler_params=_cparams(("parallel",)), name=name)(q, km, vm)


def mem_attn_bwd(q, km, vm, do, name):
    T = q.shape[0]
    tm = _fit(T, 512, 8)
    scale = MEM_HD ** -0.5

    def body(q_ref, k_ref, v_ref, do_ref, dq_ref, dk_ref, dv_ref):
        @pl.when(pl.program_id(0) == 0)
        def _():
            dk_ref[...] = jnp.zeros_like(dk_ref)
            dv_ref[...] = jnp.zeros_like(dv_ref)

        for h in range(MEM_H):
            hs, qh, kh, p, l = _mem_probs(q_ref, k_ref, h)
            p = p / l
            dov = do_ref[:, hs]
            dp = _bdot(dov, v_ref[:, hs], NT)
            ds = p * (dp - jnp.sum(dp * p, axis=1, keepdims=True))
            dq_ref[:, hs] = _bdot(ds, kh, NN) * scale
            dk_ref[:, hs] += _bdot(ds, qh, TN) * scale
            dv_ref[:, hs] += _bdot(p, dov, TN)

    row = pl.BlockSpec((tm, MEM_W), lambda i: (i, 0))
    full = pl.BlockSpec((MEM_TOK, MEM_W), lambda i: (0, 0))
    return pl.pallas_call(
        body, grid=(T // tm,), in_specs=[row, full, full, row], out_specs=[row, full, full],
        out_shape=[SDS((T, MEM_W), F32), SDS((MEM_TOK, MEM_W), F32), SDS((MEM_TOK, MEM_W), F32)],
        compiler_params=_cparams(("arbitrary",)), name=name)(q, km, vm, do)


SLOT_BLOCK_ELEMS = 512 * 1024


def _slot_rows(R, Cc):
    return _fit(R, max(16, SLOT_BLOCK_ELEMS // Cc // 16 * 16), 16)


def sum_slots(a, name):
    n, R, Cc = a.shape
    tr = _slot_rows(R, Cc)

    def body(a_ref, o_ref):
        s = a_ref[0].astype(F32)
        for i in range(1, n):
            s = s + a_ref[i].astype(F32)
        o_ref[...] = s

    return pl.pallas_call(body, grid=(R // tr,), in_specs=[pl.BlockSpec((n, tr, Cc), lambda i: (0, i, 0))],
                          out_specs=pl.BlockSpec((tr, Cc), lambda i: (i, 0)), out_shape=SDS((R, Cc), F32),
                          compiler_params=_cparams(("parallel",)), name=name)(a)


def pair_sum(g, land, c_idx, name):
    _, R, Cc = g.shape
    tr = _slot_rows(R, Cc)

    def body(c_ref, g_ref, l_ref, o_ref):
        o_ref[...] = (g_ref[...].astype(F32) + l_ref[...].astype(F32)).astype(o_ref.dtype)

    blk = pl.BlockSpec((None, tr, Cc), lambda k, i, c: (k, i, 0))
    gs = pltpu.PrefetchScalarGridSpec(
        num_scalar_prefetch=1, grid=(4, R // tr),
        in_specs=[pl.BlockSpec((None, tr, Cc), lambda k, i, c: (2 * k + c[0], i, 0)), blk], out_specs=blk)
    return pl.pallas_call(body, grid_spec=gs, out_shape=SDS((4, R, Cc), g.dtype),
                          compiler_params=_cparams(("parallel", "parallel")), name=name)(c_idx, g, land)


def adamw(w, g, m, v, name):
    R, Cc = w.shape
    tr = _fit(R, max(8, (262144 // max(Cc, 128)) // 8 * 8), 8)
    c1 = 1.0 - ADAM_B1 ** ADAM_STEP
    c2 = 1.0 - ADAM_B2 ** ADAM_STEP

    def body(w_ref, g_ref, m_ref, v_ref, d_ref, mo_ref, vo_ref):
        gv = g_ref[...]
        m2 = ADAM_B1 * m_ref[...] + (1.0 - ADAM_B1) * gv
        v2 = ADAM_B2 * v_ref[...] + (1.0 - ADAM_B2) * (gv * gv)
        d_ref[...] = -ADAM_LR * ((m2 / c1) / (jnp.sqrt(v2 / c2) + ADAM_EPS) + ADAM_WD * w_ref[...])
        mo_ref[...] = m2
        vo_ref[...] = v2

    blk = pl.BlockSpec((tr, Cc), lambda i: (i, 0))
    return pl.pallas_call(body, grid=(R // tr,), in_specs=[blk] * 4, out_specs=[blk] * 3, out_shape=[SDS((R, Cc), F32)] * 3,
                          compiler_params=_cparams(("parallel",)), name=name)(w, g, m, v)


ANY = pl.BlockSpec(memory_space=pl.ANY)


def _my_pos():
    return lax.axis_index("x"), lax.axis_index("y"), lax.axis_index("c")


class Rider:
    def __init__(self, inputs, out_shapes, scratch, start, finish, aliases=None):
        self.inputs, self.out_shapes, self.scratch = list(inputs), list(out_shapes), list(scratch)
        self.start, self.finish, self.aliases = start, finish, dict(aliases or {})


def combine_riders(r1, r2):
    ni, no, ns = len(r1.inputs), len(r1.out_shapes), len(r1.scratch)

    def both(f1, f2):
        def run(i, o, s):
            f1(i[:ni], o[:no], s[:ns])
            f2(i[ni:], o[no:], s[ns:])
        return run

    aliases = {**r1.aliases, **{ni + i: no + j for i, j in r2.aliases.items()}}
    return Rider(r1.inputs + r2.inputs, r1.out_shapes + r2.out_shapes, r1.scratch + r2.scratch,
                 both(r1.start, r2.start), both(r1.finish, r2.finish), aliases)


def _gather_phases(x_refs, out_refs, sems):
    send_sems, recv_sems, local_sems = sems
    n = len(out_refs)
    x, y, c = _my_pos()
    me, sibling = (x, y, c), (x, y, 1 - c)
    chips = [(1 - x, y), (x, 1 - y), (1 - x, 1 - y)]

    def slot(a, px, py, pc):
        return out_refs[a].at[4 * px + 2 * py + pc]

    def copy(a, k, block, to, src=None):
        return pltpu.make_async_remote_copy(
            src_ref=slot(a, *block) if src is None else src, dst_ref=slot(a, *block),
            send_sem=send_sems.at[7 * a + k], recv_sem=recv_sems.at[7 * a + k], device_id=to, device_id_type=MESH)

    def mine(a):
        return pltpu.make_async_copy(x_refs[a], slot(a, *me), local_sems.at[a])

    def spread(a):
        return [copy(a, 0, me, sibling, src=x_refs[a])] + [copy(a, 1 + j, me, (*chip, c), src=x_refs[a])
                                                           for j, chip in enumerate(chips)]

    def spread_start():
        for a in range(n):
            mine(a).start()
        for a in range(n):
            for cp in spread(a):
                cp.start()

    def spread_finish():
        for j, chip in enumerate(chips):
            for a in range(n):
                copy(a, 1 + j, (*chip, c), me).wait_recv()
        for a in range(n):
            copy(a, 0, sibling, me).wait_recv()
        for a in range(n):
            for cp in spread(a):
                cp.wait_send()
            mine(a).wait()

    def forward_start():
        for j, chip in enumerate(chips):
            for a in range(n):
                copy(a, 4 + j, (*chip, c), sibling).start()

    def forward_finish():
        for j, chip in enumerate(chips):
            for a in range(n):
                copy(a, 4 + j, (*chip, 1 - c), me).wait_recv()
        for j, chip in enumerate(chips):
            for a in range(n):
                copy(a, 4 + j, (*chip, c), sibling).wait_send()

    return spread_start, spread_finish, forward_start, forward_finish


def _gather_scratch(n):
    return [pltpu.SemaphoreType.DMA((7 * n,)), pltpu.SemaphoreType.DMA((7 * n,)), pltpu.SemaphoreType.DMA((n,))]


def all_gather(shards, name):
    n = len(shards)

    def body(*refs):
        phases = _gather_phases(refs[:n], refs[n:2 * n], refs[2 * n:])
        for phase in phases:
            phase()

    return pl.pallas_call(
        body, out_shape=[SDS((N_DEV,) + s.shape, s.dtype) for s in shards], in_specs=[ANY] * n, out_specs=[ANY] * n,
        scratch_shapes=_gather_scratch(n), name=name)(*shards)


def gather_spread_rider(shards):
    return Rider(shards, [SDS((N_DEV,) + s.shape, s.dtype) for s in shards], _gather_scratch(len(shards)),
                 lambda i, o, s: _gather_phases(i, o, s)[0](), lambda i, o, s: _gather_phases(i, o, s)[1]())


def gather_forward_rider(bufs):
    n = len(bufs)
    return Rider(bufs, [SDS(b.shape, b.dtype) for b in bufs], _gather_scratch(n),
                 lambda i, o, s: _gather_phases(None, o, s)[2](), lambda i, o, s: _gather_phases(None, o, s)[3](),
                 aliases={a: a for a in range(n)})


def pair_exchange(gs, name):
    n = len(gs)

    def body(*refs):
        g_refs, land_refs = refs[:n], refs[n:2 * n]
        send_sems, recv_sems = refs[2 * n:]
        x, y, c = _my_pos()
        sibling = (x, y, 1 - c)
        copies = [pltpu.make_async_remote_copy(
            src_ref=g_refs[a].at[2 * k + (1 - c)], dst_ref=land_refs[a].at[k], send_sem=send_sems.at[4 * a + k],
            recv_sem=recv_sems.at[4 * a + k], device_id=sibling, device_id_type=MESH) for a in range(n) for k in range(4)]
        for cp in copies:
            cp.start()
        for cp in copies:
            cp.wait_recv()
        for cp in copies:
            cp.wait_send()

    return pl.pallas_call(
        body, out_shape=[SDS((4,) + g.shape[1:], g.dtype) for g in gs], in_specs=[ANY] * n, out_specs=[ANY] * n,
        scratch_shapes=[pltpu.SemaphoreType.DMA((4 * n,)), pltpu.SemaphoreType.DMA((4 * n,))], name=name)(*gs)


def chip_exchange(ss, name):
    n = len(ss)

    def body(*refs):
        start, finish = _chip_exchange_phases(refs[:n], refs[n:2 * n], refs[2 * n:])
        start()
        finish()

    return pl.pallas_call(
        body, out_shape=[SDS(s.shape, s.dtype) for s in ss], in_specs=[ANY] * n, out_specs=[ANY] * n,
        scratch_shapes=_chip_exchange_scratch(n), name=name)(*ss)


def _chip_exchange_scratch(n):
    return [pltpu.SemaphoreType.DMA((3 * n,)), pltpu.SemaphoreType.DMA((3 * n,)), pltpu.SemaphoreType.DMA((n,))]


def _chip_exchange_phases(s_refs, land_refs, sems):
    send_sems, recv_sems, local_sems = sems
    n = len(s_refs)
    x, y, c = _my_pos()
    my_chip = 2 * x + y
    chips = [(1 - x, y), (x, 1 - y), (1 - x, 1 - y)]

    def own(a):
        return pltpu.make_async_copy(s_refs[a].at[my_chip], land_refs[a].at[my_chip], local_sems.at[a])

    def copy(a, j, src_chip, dst_chip):
        px, py = chips[j]
        return pltpu.make_async_remote_copy(
            src_ref=s_refs[a].at[src_chip], dst_ref=land_refs[a].at[dst_chip], send_sem=send_sems.at[3 * a + j],
            recv_sem=recv_sems.at[3 * a + j], device_id=(px, py, c), device_id_type=MESH)

    def start():
        for a in range(n):
            own(a).start()
            for j, (px, py) in enumerate(chips):
                copy(a, j, 2 * px + py, my_chip).start()

    def finish():
        for a in range(n):
            for j, (px, py) in enumerate(chips):
                copy(a, j, my_chip, 2 * px + py).wait_recv()
        for a in range(n):
            for j, (px, py) in enumerate(chips):
                copy(a, j, 2 * px + py, my_chip).wait_send()
            own(a).wait()

    return start, finish


def chip_exchange_rider(ss):
    return Rider(ss, [SDS(s.shape, s.dtype) for s in ss], _chip_exchange_scratch(len(ss)),
                 lambda i, o, s: _chip_exchange_phases(i, o, s)[0](), lambda i, o, s: _chip_exchange_phases(i, o, s)[1]())


W_IN_SHARD, W_IN_PACKED = 644, 768
BIG = ("ffn_w13", "ffn_w2", "w_out", "mem_wkv", "w_in")
GATE_W = ("gla_wg2_f", "gla_wg2_b")
SHARD_NAMES = BIG + GATE_W


def _permuted_ranges(c0, c1):
    res = []
    for o0, o1, p0 in ((0, ORIG_GATE0, 0), (ORIG_GATE0, ORIG_AFTER_GATE, COL_TAIL), (ORIG_AFTER_GATE, IN_COLS, ORIG_GATE0)):
        lo, hi = max(c0, o0), min(c1, o1)
        if lo < hi:
            res.append((p0 + lo - o0, p0 + hi - o0))
    return res


def assemble_w_in(blocks):
    placed = []
    for d in range(N_DEV):
        c = d * W_IN_SHARD
        for p0, p1 in _permuted_ranges(c, c + W_IN_SHARD):
            placed.append((p0, blocks[d][:, c - d * W_IN_SHARD:c - d * W_IN_SHARD + (p1 - p0)]))
            c += p1 - p0
    placed.sort(key=lambda t: t[0])
    return jnp.concatenate([t[1] for t in placed] + [jnp.zeros((blocks.shape[1], PC - IN_COLS), blocks.dtype)], axis=1)


def split_w_in_grad(g):
    pad = jnp.zeros((g.shape[0], W_IN_PACKED - W_IN_SHARD), g.dtype)
    return jnp.stack([jnp.concatenate([g[:, p0:p1] for p0, p1 in _permuted_ranges(d * W_IN_SHARD, (d + 1) * W_IN_SHARD)]
                                      + [pad], axis=1) for d in range(N_DEV)])


def wire_shards(shards):
    return [jnp.pad(shards[n], ((0, 0), (0, W_IN_PACKED - W_IN_SHARD))) if n == "w_in" else shards[n] for n in BIG]


def weights_from_gathered(names, gathered):
    full = {}
    for n, g in zip(names, gathered):
        full[n] = g if n == "ffn_w13" else assemble_w_in(g) if n == "w_in" else g.reshape(-1, g.shape[-1])
    return full


def reduce_scatter_begin(names, gw, c_idx, tag):
    g = [split_w_in_grad(gw[n]) if n == "w_in" else gw[n].reshape((N_DEV, -1, gw[n].shape[-1])) for n in names]
    land = pair_exchange(g, tag + "rs_pair_exchange_" + names[0])
    return [pair_sum(g[i], land[i], c_idx, tag + "rs_pair_sum_" + n) for i, n in enumerate(names)]


def reduce_scatter_end(names, landed, tag):
    out = {n: sum_slots(landed[i], tag + "rs_chip_sum_" + n) for i, n in enumerate(names)}
    if "w_in" in out:
        out["w_in"] = out["w_in"][:, :W_IN_SHARD]
    return out


def gather_gate_weights(w, depth):
    mine = jnp.concatenate([w[n].reshape(-1) for n in GATE_W]).reshape(-1, 128)
    got = all_gather([mine], "gather_gate_weights")[0].reshape(N_DEV, len(GATE_W), depth, GLA_RANK, GLA_DK // N_DEV)
    return [{n: got[:, i, l].transpose(1, 0, 2).reshape(GLA_RANK, GLA_DK) for i, n in enumerate(GATE_W)} for l in range(depth)]


def pad_gate_weight(wg2, backward_dir):
    r0 = GLA_RANK if backward_dir else 0
    w = wg2.astype(F32).reshape(GLA_RANK, GLA_H, GLA_HK).transpose(1, 0, 2)
    return jnp.pad(w, ((0, 0), (r0, 128 - GLA_RANK - r0), (0, 0)))


def unpad_gate_grad(dw, backward_dir):
    r0 = GLA_RANK if backward_dir else 0
    return dw[:, r0:r0 + GLA_RANK, :].transpose(1, 0, 2).reshape(GLA_RANK, GLA_DK)


def to_heads(t):
    T = t.shape[0]
    return t.reshape(T, NA_H, NA_HD).transpose(1, 0, 2)


def from_heads(t):
    return t.transpose(1, 0, 2).reshape(t.shape[1], NA_W)


REPLICATED = ("attn_norm", "gla_bg_f", "gla_bg_b", "gla_out_norm", "na_q_norm", "na_k_norm", "na_rpb", "na_out_norm",
              "mem_norm", "mem_q_norm", "mem_k_norm", "mem_out_norm", "ffn_norm")
WEIGHTS = ("attn_norm", "w_in", "gla_wg2_f", "gla_bg_f", "gla_wg2_b", "gla_bg_b", "gla_out_norm", "na_q_norm", "na_k_norm",
           "na_rpb", "na_out_norm", "mem_norm", "mem_wkv", "mem_q_norm", "mem_k_norm", "mem_out_norm", "w_out", "ffn_norm",
           "ffn_w13", "ffn_w2")


def fold_heads(dg, n_heads, name):
    hd = dg.shape[1] // n_heads
    fold = (np.arange(dg.shape[1])[:, None] % hd == np.arange(128)[None, :]).astype(np.float32)
    out = matmul(jnp.pad(dg, ((0, 7), (0, 0))), fold, "nn", name, exact=True)
    return out[0, :hd]


def layer_fwd(x, mem_n_in, p, W, l, arriving=None, next_shards=None, first_shards=None):
    tag = f"l{l}_"
    row = lambda v: v.reshape(1, -1)
    sv = {"x": x}
    if arriving is None:
        sv["xn"] = rms_fwd(x, row(p["attn_norm"]), tag + "attn_rms")
    else:
        sv["xn"], rest = rms_fwd(x, row(p["attn_norm"]), tag + "attn_rms", rider=gather_forward_rider(arriving))
        W = {**W, **weights_from_gathered(BIG[1:], rest)}
    if first_shards is None:
        proj = matmul(sv["xn"], W["w_in"], "nn", tag + "proj", out_dtype=BF16, tn=768, tk=2048)
    else:
        proj, spread = matmul(sv["xn"], W["w_in"], "nn", tag + "proj", out_dtype=BF16, tn=768, tk=2048,
                              rider=gather_spread_rider(first_shards))
    sv["proj"] = proj
    sv["wg_f"], sv["wg_b"] = pad_gate_weight(W["gla_wg2_f"], False), pad_gate_weight(W["gla_wg2_b"], True)
    sv["bg_f"], sv["bg_b"] = p["gla_bg_f"].reshape(GLA_H, 1, GLA_HK), p["gla_bg_b"].reshape(GLA_H, 1, GLA_HK)
    if first_shards is None:
        sv["o_f"], sv["s_f"] = gla_fwd(proj, sv["wg_f"], sv["bg_f"], False, tag + "gla_f")
    else:
        sv["o_f"], sv["s_f"], done = gla_fwd(proj, sv["wg_f"], sv["bg_f"], False, tag + "gla_f",
                                             rider=gather_forward_rider(spread))
        W = {**W, **weights_from_gathered(BIG[:len(done)], done)}
    sv["o_b"], sv["s_b"] = gla_fwd(proj, sv["wg_b"], sv["bg_b"], True, tag + "gla_b")
    sv["gq"], sv["gk"] = jnp.tile(row(p["na_q_norm"]), (1, NA_H)), jnp.tile(row(p["na_k_norm"]), (1, NA_H))
    sv["gmq"], sv["gmk"] = jnp.tile(row(p["mem_q_norm"]), (1, MEM_H)), jnp.tile(row(p["mem_k_norm"]), (1, MEM_H))
    qn, kn, vn, mqn = pre_fwd(proj, sv["gq"], sv["gk"], sv["gmq"], tag + "pre")
    sv["q_hm"], sv["k_hm"], sv["v_hm"], sv["mqn"] = to_heads(qn), to_heads(kn), to_heads(vn), mqn
    sv["tb2"] = rpb_table(p["na_rpb"], tag + "rpb_table")
    sv["o_na"] = from_heads(natten_fwd(sv["q_hm"], sv["k_hm"], sv["v_hm"], sv["tb2"], tag + "natten"))
    sv["mem_n"] = rms_fwd(mem_n_in, row(p["mem_norm"]), tag + "mem_rms")
    sv["kv"] = matmul(sv["mem_n"], W["mem_wkv"], "nn", tag + "mem_kv", tn=512, tk=2048)
    sv["km"], sv["vm"] = mem_kprep(sv["kv"], sv["gmk"], tag + "mem_kprep")
    sv["o_mem"] = mem_attn_fwd(mqn, sv["km"], sv["vm"], tag + "mem_attn")
    sv["ycat"] = post_fwd(sv["o_f"], sv["o_b"], proj, sv["o_na"], sv["o_mem"], row(p["gla_out_norm"]),
                          row(p["na_out_norm"]), row(p["mem_out_norm"]), tag + "post")
    x1 = matmul(sv["ycat"], W["w_out"], "nn", tag + "out_proj", res=x, tk=2048)
    sv["x1"] = x1
    sv["h"] = rms_fwd(x1, row(p["ffn_norm"]), tag + "ffn_rms")
    if next_shards is None:
        sv["gu"], sv["a"], _ = ffn_up_swiglu(sv["h"], W["ffn_w13"], tag + "ffn_up")
        return matmul(sv["a"], W["ffn_w2"], "nn", tag + "ffn_down", res=x1, tk=2816), sv, W, None, None
    sv["gu"], sv["a"], spread13 = ffn_up_swiglu(sv["h"], W["ffn_w13"], tag + "ffn_up",
                                                rider=gather_spread_rider(next_shards[:1]))
    x2, carried = matmul(sv["a"], W["ffn_w2"], "nn", tag + "ffn_down", res=x1, tk=2816,
                         rider=combine_riders(gather_forward_rider(spread13), gather_spread_rider(next_shards[1:])))
    return x2, sv, W, weights_from_gathered(BIG[:1], carried[:1]), carried[1:]


RS_EARLY, RS_LATE = ("ffn_w13", "ffn_w2", "w_out"), ("mem_wkv", "w_in")


def layer_bwd(dx2, dx2_b, mem_n_in, p, W, sv, l, c_idx, rider=None):
    tag = f"l{l}_b_"
    row = lambda v: v.reshape(1, -1)
    gw, gs = {}, {}
    gw["ffn_w2"] = matmul(sv["a"], dx2_b, "tn", tag + "dw2", out_dtype=BF16, tm=1408, tn=1024, tk=2048)
    dgu = ffn_down_bwd(dx2_b, W["ffn_w2"], sv["gu"], tag + "d_swiglu")
    dh = matmul(dgu, W["ffn_w13"], "nt", tag + "d_h", a_halves=True, b_blocked=True, rider=rider)
    dh, carried = dh if rider is not None else (dh, None)
    gw["ffn_w13"] = matmul(sv["h"], dgu, "tn", tag + "dw13", out_dtype=BF16, tm=2048, tn=1408, tk=1024,
                           b_halves=True, out_blocked=True)
    dx1, dx1_b, dg = rms_bwd(sv["x1"], row(p["ffn_norm"]), dh, dx2, tag + "ffn_rms")
    gs["ffn_norm"] = dg[0]
    dycat = matmul(dx1_b, W["w_out"], "nt", tag + "d_ycat", tk=2048)
    gw["w_out"] = matmul(sv["ycat"], dx1_b, "tn", tag + "dw_out", out_dtype=BF16, tm=2048, tn=1024, tk=2048)
    early = reduce_scatter_begin(RS_EARLY, gw, c_idx, tag)
    d_o, d_r, d_ona, d_omem, dgg, dgn, dgm = post_bwd(
        sv["o_f"], sv["o_b"], sv["proj"], sv["o_na"], sv["o_mem"], row(p["gla_out_norm"]), row(p["na_out_norm"]),
        row(p["mem_out_norm"]), dycat, tag + "post")
    gs["gla_out_norm"], gs["na_out_norm"], gs["mem_out_norm"] = dgg[0], dgn[0], dgm[0]
    dq_f, dk_f, dv_f, dt_f, dwg_f, dbg_f = gla_bwd(sv["proj"], sv["wg_f"], sv["bg_f"], sv["s_f"], d_o, False, tag + "gla_f")
    dq_b, dk_b, dv_b, dt_b, dwg_b, dbg_b = gla_bwd(sv["proj"], sv["wg_b"], sv["bg_b"], sv["s_b"], d_o, True, tag + "gla_b")
    gs["gla_wg2_f"], gs["gla_wg2_b"] = unpad_gate_grad(dwg_f, False), unpad_gate_grad(dwg_b, True)
    gs["gla_bg_f"], gs["gla_bg_b"] = dbg_f.reshape(-1), dbg_b.reshape(-1)
    dq_hm, dk_hm, dv_hm, dtb2 = natten_bwd(sv["q_hm"], sv["k_hm"], sv["v_hm"], sv["tb2"], to_heads(d_ona), tag + "natten")
    gs["na_rpb"] = rpb_table_bwd(dtb2, tag + "rpb_table")
    d_mqn, dkm, dvm = mem_attn_bwd(sv["mqn"], sv["km"], sv["vm"], d_omem, tag + "mem_attn")
    dkv, dgmk = mem_kprep_bwd(sv["kv"], sv["gmk"], dkm, dvm, tag + "mem_kprep")
    gs["mem_k_norm"] = fold_heads(dgmk, MEM_H, tag + "fold_mk")
    gw["mem_wkv"] = matmul(sv["mem_n"], dkv, "tn", tag + "dw_kv", out_dtype=BF16, tm=2048, tn=1024, tk=256)
    d_memn = matmul(dkv, W["mem_wkv"], "nt", tag + "d_memn", tk=1024)
    _, _, dg = rms_bwd(mem_n_in, row(p["mem_norm"]), d_memn, None, tag + "mem_rms")
    gs["mem_norm"] = dg[0]
    dproj, dgq, dgk, dgmq = pre_bwd(sv["proj"], sv["gq"], sv["gk"], sv["gmq"], from_heads(dq_hm), from_heads(dk_hm),
                                    from_heads(dv_hm), d_mqn, dq_f, dq_b, dk_f, dk_b, dv_f, dv_b, d_r, dt_f, dt_b, tag + "pre")
    gs["na_q_norm"] = fold_heads(dgq, NA_H, tag + "fold_q")
    gs["na_k_norm"] = fold_heads(dgk, NA_H, tag + "fold_k")
    gs["mem_q_norm"] = fold_heads(dgmq, MEM_H, tag + "fold_mq")
    dxn, landed13 = matmul(dproj, W["w_in"], "nt", tag + "d_xn", tk=1792, rider=chip_exchange_rider(early[:1]))
    gw["w_in"], landed_rest = matmul(sv["xn"], dproj, "tn", tag + "dw_in", out_dtype=BF16, tm=2048, tn=768, tk=2048,
                                     rider=chip_exchange_rider(early[1:]))
    shard_grads = reduce_scatter_end(RS_EARLY, landed13 + landed_rest, tag)
    late = reduce_scatter_begin(RS_LATE, gw, c_idx, tag)
    dx, dx_b, dg = rms_bwd(sv["x"], row(p["attn_norm"]), dxn, dx1, tag + "attn_rms")
    gs["attn_norm"] = dg[0]
    return dx, dx_b, shard_grads, gs, carried, late


def kernel(x, mem, attn_norm, w_in, gla_wg2_f, gla_bg_f, gla_wg2_b, gla_bg_b, gla_out_norm, na_q_norm, na_k_norm, na_rpb, na_out_norm, mem_norm, mem_wkv, mem_q_norm, mem_k_norm, mem_out_norm, w_out, ffn_norm, ffn_w13, ffn_w2, loss_target, m_attn_norm, m_w_in, m_gla_wg2_f, m_gla_bg_f, m_gla_wg2_b, m_gla_bg_b, m_gla_out_norm, m_na_q_norm, m_na_k_norm, m_na_rpb, m_na_out_norm, m_mem_norm, m_mem_wkv, m_mem_q_norm, m_mem_k_norm, m_mem_out_norm, m_w_out, m_ffn_norm, m_ffn_w13, m_ffn_w2, v_attn_norm, v_w_in, v_gla_wg2_f, v_gla_bg_f, v_gla_wg2_b, v_gla_bg_b, v_gla_out_norm, v_na_q_norm, v_na_k_norm, v_na_rpb, v_na_out_norm, v_mem_norm, v_mem_wkv, v_mem_q_norm, v_mem_k_norm, v_mem_out_norm, v_w_out, v_ffn_norm, v_ffn_w13, v_ffn_w2):
    w = dict(attn_norm=attn_norm, w_in=w_in, gla_wg2_f=gla_wg2_f, gla_bg_f=gla_bg_f, gla_wg2_b=gla_wg2_b, gla_bg_b=gla_bg_b,
             gla_out_norm=gla_out_norm, na_q_norm=na_q_norm, na_k_norm=na_k_norm, na_rpb=na_rpb, na_out_norm=na_out_norm,
             mem_norm=mem_norm, mem_wkv=mem_wkv, mem_q_norm=mem_q_norm, mem_k_norm=mem_k_norm, mem_out_norm=mem_out_norm,
             w_out=w_out, ffn_norm=ffn_norm, ffn_w13=ffn_w13, ffn_w2=ffn_w2)
    mom = dict(attn_norm=m_attn_norm, w_in=m_w_in, gla_wg2_f=m_gla_wg2_f, gla_bg_f=m_gla_bg_f, gla_wg2_b=m_gla_wg2_b,
               gla_bg_b=m_gla_bg_b, gla_out_norm=m_gla_out_norm, na_q_norm=m_na_q_norm, na_k_norm=m_na_k_norm, na_rpb=m_na_rpb,
               na_out_norm=m_na_out_norm, mem_norm=m_mem_norm, mem_wkv=m_mem_wkv, mem_q_norm=m_mem_q_norm,
               mem_k_norm=m_mem_k_norm, mem_out_norm=m_mem_out_norm, w_out=m_w_out, ffn_norm=m_ffn_norm, ffn_w13=m_ffn_w13,
               ffn_w2=m_ffn_w2)
    var = dict(attn_norm=v_attn_norm, w_in=v_w_in, gla_wg2_f=v_gla_wg2_f, gla_bg_f=v_gla_bg_f, gla_wg2_b=v_gla_wg2_b,
               gla_bg_b=v_gla_bg_b, gla_out_norm=v_gla_out_norm, na_q_norm=v_na_q_norm, na_k_norm=v_na_k_norm, na_rpb=v_na_rpb,
               na_out_norm=v_na_out_norm, mem_norm=v_mem_norm, mem_wkv=v_mem_wkv, mem_q_norm=v_mem_q_norm,
               mem_k_norm=v_mem_k_norm, mem_out_norm=v_mem_out_norm, w_out=v_w_out, ffn_norm=v_ffn_norm, ffn_w13=v_ffn_w13,
               ffn_w2=v_ffn_w2)
    depth = attn_norm.shape[0]
    T = x.shape[1]
    xs, mem0, tgt = x.reshape(T, D_MODEL), mem.reshape(MEM_TOK, D_MODEL), loss_target.reshape(T, D_MODEL)
    c_idx = lax.axis_index("c").astype(jnp.int32).reshape(1)

    gates = gather_gate_weights(w, depth)
    send = [wire_shards({n: w[n][l].astype(BF16) for n in BIG}) for l in range(depth)]
    P = [{n: w[n][l] for n in REPLICATED} for l in range(depth)]

    ready = {**weights_from_gathered(BIG[4:], all_gather(send[0][4:], "l0_gather_w_in")), **gates[0]}
    arriving = None
    W, saved = [], []
    h = xs
    for l in range(depth):
        h, sv, w_l, ready, arriving = layer_fwd(h, mem0, P[l], ready, l, arriving, send[l + 1] if l + 1 < depth else None,
                                                send[0][:4] if l == 0 else None)
        W.append(w_l)
        saved.append(sv)
        if ready is not None:
            ready = {**ready, **gates[l + 1]}
    dy, dy_b, lsum = loss_bwd(h, tgt, "loss")
    loss = lax.psum(lsum[0, 0], ("x", "y", "c")) * (0.5 / D_MODEL)

    g_shard, g_small = [None] * depth, [None] * depth
    late = None
    for l in range(depth - 1, -1, -1):
        rider = chip_exchange_rider(late) if late is not None else None
        dy, dy_b, g_shard[l], g_small[l], landed, late = layer_bwd(dy, dy_b, mem0, P[l], W[l], saved[l], l, c_idx, rider)
        if landed is not None:
            g_shard[l + 1].update(reduce_scatter_end(RS_LATE, landed, f"l{l + 1}_b_"))
        saved[l] = None
    g_shard[0].update(reduce_scatter_end(RS_LATE, chip_exchange(late, "l0_rs_chip_exchange"), "l0_b_"))
    grad_x = dy.reshape(x.shape)

    small_names = REPLICATED + GATE_W
    small = jnp.concatenate([g_small[l][n].reshape(-1) for l in range(depth) for n in small_names])
    n_small = small.shape[0]
    rows = -(-n_small // 1024) * 8
    small = jnp.pad(small, (0, rows * 128 - n_small)).reshape(rows, 128)
    small = sum_slots(all_gather([small], "gather_small_grads")[0], "sum_small_grads").reshape(-1)
    grads, off = {}, 0
    per_layer = {n: [] for n in small_names}
    my_cols = (4 * lax.axis_index("x") + 2 * lax.axis_index("y") + lax.axis_index("c")) * (GLA_DK // N_DEV)
    for l in range(depth):
        for n in small_names:
            shp = (GLA_RANK, GLA_DK) if n in GATE_W else w[n].shape[1:]
            g = small[off:off + int(np.prod(shp))].reshape(shp)
            off += int(np.prod(shp))
            per_layer[n].append(lax.dynamic_slice_in_dim(g, my_cols, GLA_DK // N_DEV, axis=1) if n in GATE_W else g)
    for n in small_names:
        grads[n] = jnp.stack(per_layer[n])
    for n in BIG:
        grads[n] = jnp.stack([g_shard[l][n] for l in range(depth)])

    delta, new_m, new_v = {}, {}, {}
    for n in WEIGHTS:
        shp = w[n].shape
        two_d = (shp[0], int(np.prod(shp[1:]))) if n in REPLICATED else (int(np.prod(shp[:-1])), shp[-1])
        d_, m_, v_ = adamw(w[n].reshape(two_d), grads[n].reshape(two_d), mom[n].reshape(two_d), var[n].reshape(two_d),
                           "adamw_" + n)
        delta[n], new_m[n], new_v[n] = d_.reshape(shp), m_.reshape(shp), v_.reshape(shp)

    return (loss, grad_x, *[grads[n] for n in WEIGHTS], *[delta[n] for n in WEIGHTS], *[new_m[n] for n in WEIGHTS],
            *[new_v[n] for n in WEIGHTS])
```

```python
import functools

import numpy as np
import jax
import jax.numpy as jnp
from jax import lax
from jax.experimental import pallas as pl
from jax.experimental.pallas import tpu as pltpu

F32, BF16 = jnp.float32, jnp.bfloat16
HI = lax.Precision.HIGHEST
SDS = jax.ShapeDtypeStruct
MESH = pl.DeviceIdType.MESH

D_MODEL = 2048
GRID_W = 64
GLA_H, GLA_HK, GLA_HV, GLA_RANK, GLA_TAU, GLA_C = 4, 128, 256, 16, 16.0, 64
GLA_DK, GLA_DV = GLA_H * GLA_HK, GLA_H * GLA_HV
NA_H, NA_HD, NA_ROWS, NA_COLS = 8, 64, 8, 16
NA_W = NA_H * NA_HD
MEM_H, MEM_HD, MEM_TOK = 4, 128, 256
MEM_W = MEM_H * MEM_HD
D_FF = 5632
IN_COLS = 5152
RMS_EPS = 1e-6
ADAM_LR, ADAM_B1, ADAM_B2, ADAM_EPS, ADAM_WD, ADAM_STEP = 0.001, 0.9, 0.999, 1e-08, 0.01, 10
N_DEV = 8

PC = 5376
COL_R, COL_NQ, COL_NK, COL_NV, COL_MQ, COL_TAIL = 2048, 3072, 3584, 4096, 4608, 5120
ORIG_GATE0 = 3072
ORIG_AFTER_GATE = 3104

GLA_G = 8
VMEM_LIMIT = 56 * 1024 * 1024
NEG = -1e30


def _cparams(sem):
    return pltpu.CompilerParams(dimension_semantics=sem, vmem_limit_bytes=VMEM_LIMIT)


def _fit(n, pref, unit=128):
    if n <= pref:
        return n
    t = (pref // unit) * unit
    while t >= unit:
        if n % t == 0:
            return t
        t -= unit
    return n


def _bdot(a, b, dims):
    return lax.dot_general(a.astype(BF16), b.astype(BF16), (dims, ((), ())), preferred_element_type=F32)


NN, NT, TN = ((1,), (0,)), ((1,), (1,)), ((0,), (0,))


@functools.partial(jax.custom_vjp, nondiff_argnums=(2,))
def _bdot_vjp(a, b, dims):
    return _bdot(a, b, dims)


def _bdot_vjp_fwd(a, b, dims):
    return _bdot(a, b, dims), (a, b)


def _bdot_vjp_bwd(dims, res, ct):
    a, b = res
    if dims == NN:
        return _bdot(ct, b, NT), _bdot(a, ct, TN)
    if dims == NT:
        return _bdot(ct, b, NN), _bdot(ct, a, TN)
    return _bdot(b, ct, NT), _bdot(a, ct, NN)


_bdot_vjp.defvjp(_bdot_vjp_fwd, _bdot_vjp_bwd)


def _split_dot(c, x, dims):
    hi = x.astype(BF16)
    lo = (x - hi.astype(F32)).astype(BF16)
    cb = c.astype(BF16)
    return (lax.dot_general(cb, hi, (dims, ((), ())), preferred_element_type=F32)
            + lax.dot_general(cb, lo, (dims, ((), ())), preferred_element_type=F32))


@jax.custom_vjp
def _tri_sum(cmat, x):
    return _split_dot(cmat, x, NN)


def _tri_sum_fwd(cmat, x):
    return _split_dot(cmat, x, NN), cmat


def _tri_sum_bwd(cmat, ct):
    return jnp.zeros_like(cmat), _split_dot(cmat, ct, TN)


_tri_sum.defvjp(_tri_sum_fwd, _tri_sum_bwd)


def _call_with_rider(body, grid, in_specs, out_specs, out_shape, scratch, sem, name, args, rider):
    if rider is None:
        return pl.pallas_call(body, grid=grid, in_specs=in_specs, out_specs=out_specs, out_shape=out_shape,
                              scratch_shapes=scratch, compiler_params=_cparams(sem), name=name)(*args), None
    outs_l = list(out_shape) if isinstance(out_shape, (list, tuple)) else [out_shape]
    specs_l = list(out_specs) if isinstance(out_specs, (list, tuple)) else [out_specs]
    ni, no, ns = len(in_specs), len(outs_l), len(scratch)
    nri, nro = len(rider.inputs), len(rider.out_shapes)
    any_spec = pl.BlockSpec(memory_space=pl.ANY)

    def wrapped(*refs):
        ins, rin = refs[:ni], refs[ni:ni + nri]
        o0 = ni + nri
        outs, rout = refs[o0:o0 + no], refs[o0 + no:o0 + no + nro]
        s0 = o0 + no + nro
        scr, rsem = refs[s0:s0 + ns], refs[s0 + ns:]
        ids = [pl.program_id(d) for d in range(len(grid))]
        first = functools.reduce(jnp.logical_and, [i == 0 for i in ids])
        last = functools.reduce(jnp.logical_and, [i == g - 1 for i, g in zip(ids, grid)])

        @pl.when(first)
        def _():
            rider.start(rin, rout, rsem)

        body(*ins, *outs, *scr)

        @pl.when(last)
        def _():
            rider.finish(rin, rout, rsem)

    res = pl.pallas_call(
        wrapped, grid=grid, in_specs=list(in_specs) + [any_spec] * nri, out_specs=specs_l + [any_spec] * nro,
        out_shape=outs_l + rider.out_shapes, scratch_shapes=list(scratch) + rider.scratch,
        input_output_aliases={ni + i: no + j for i, j in rider.aliases.items()},
        compiler_params=_cparams(("arbitrary",) * len(grid)), name=name)(*args, *rider.inputs)
    main = res[:no]
    return (main if isinstance(out_shape, (list, tuple)) else main[0]), list(res[no:])


def matmul(a, b, mode, name, out_dtype=F32, res=None, tm=1024, tn=1024, tk=512, exact=False,
           a_halves=False, b_halves=False, b_blocked=False, out_blocked=False, rider=None):
    if mode == "nn":
        M, K = a.shape
        N = b.shape[0] * b.shape[2] if b_blocked else b.shape[1]
        if b_blocked:
            tn = b.shape[2]
    elif mode == "nt":
        M, K = (a.shape[1], 2 * a.shape[2]) if a_halves else a.shape
        N = b.shape[1] if b_blocked else b.shape[0]
        if b_blocked:
            tk = b.shape[2]
    else:
        K, M = a.shape
        N = 2 * b.shape[2] if b_halves else b.shape[1]
    tm, tn, tk = _fit(M, tm, 8 if mode != "tn" else 128), _fit(N, tn), _fit(K, tk, 128 if mode != "tn" else 16)
    nk = K // tk
    dims = {"nn": NN, "nt": NT, "tn": TN}[mode]

    def body(*refs):
        if res is None:
            a_ref, b_ref, o_ref = refs[:3]
            r_ref = None
            acc = refs[3] if nk > 1 else None
        else:
            a_ref, b_ref, r_ref, o_ref = refs[:4]
            acc = refs[4] if nk > 1 else None
        def product():
            if exact:
                return lax.dot_general(a_ref[...], b_ref[...], (dims, ((), ())), preferred_element_type=F32, precision=HI)
            return _bdot(a_ref[...], b_ref[...], dims)

        def finish(val):
            if r_ref is not None:
                val = val + r_ref[...]
            o_ref[...] = val.astype(out_dtype)

        if nk == 1:
            finish(product())
        else:
            kk = pl.program_id(2)

            @pl.when(kk == 0)
            def _():
                acc[...] = jnp.zeros_like(acc)

            acc[...] += product()

            @pl.when(kk == nk - 1)
            def _():
                finish(acc[...])

    if mode == "tn":
        a_spec = pl.BlockSpec((tk, tm), lambda i, j, k: (k, i))
    elif a_halves:
        nh = K // 2 // tk
        a_spec = pl.BlockSpec((None, tm, tk), lambda i, j, k: (k // nh, i, k % nh))
    else:
        a_spec = pl.BlockSpec((tm, tk), lambda i, j, k: (i, k))
    if b_blocked:
        b_spec = (pl.BlockSpec((None, tk, tn), lambda i, j, k: (j, k, 0)) if mode == "nn"
                  else pl.BlockSpec((None, tn, tk), lambda i, j, k: (k, j, 0)))
    elif b_halves:
        nh = N // 2 // tn
        b_spec = pl.BlockSpec((None, tk, tn), lambda i, j, k: (j // nh, k, j % nh))
    elif mode == "nt":
        b_spec = pl.BlockSpec((tn, tk), lambda i, j, k: (j, k))
    else:
        b_spec = pl.BlockSpec((tk, tn), lambda i, j, k: (k, j))
    if out_blocked:
        o_spec, o_shape = pl.BlockSpec((None, tm, tn), lambda i, j, k: (j, i, 0)), SDS((N // tn, M, tn), out_dtype)
    else:
        o_spec, o_shape = pl.BlockSpec((tm, tn), lambda i, j, k: (i, j)), SDS((M, N), out_dtype)
    in_specs, args = [a_spec, b_spec], [a, b]
    if res is not None:
        in_specs.append(o_spec)
        args.append(res)
    out, carried = _call_with_rider(
        body, (M // tm, N // tn, nk), in_specs, o_spec, o_shape, [pltpu.VMEM((tm, tn), F32)] if nk > 1 else [],
        ("parallel", "parallel", "arbitrary"), name, args, rider)
    return out if rider is None else (out, carried)


def rms_fwd(x, g, name, rider=None):
    T, D = x.shape
    tm = _fit(T, 512, 8)

    def body(x_ref, g_ref, o_ref):
        xv = x_ref[...]
        r = lax.rsqrt(jnp.mean(xv * xv, axis=-1, keepdims=True) + RMS_EPS)
        o_ref[...] = (xv * r * g_ref[...]).astype(BF16)

    out, carried = _call_with_rider(
        body, (T // tm,), [pl.BlockSpec((tm, D), lambda i: (i, 0)), pl.BlockSpec((1, D), lambda i: (0, 0))],
        pl.BlockSpec((tm, D), lambda i: (i, 0)), SDS((T, D), BF16), [], ("parallel",), name, [x, g], rider)
    return out if rider is None else (out, carried)


def rms_bwd(x, g, dy, dres, name):
    T, D = x.shape
    tm = _fit(T, 256, 8)
    has_res = dres is not None

    def body(*refs):
        if has_res:
            x_ref, g_ref, dy_ref, dres_ref, dx_ref, dxb_ref, dg_ref = refs
        else:
            x_ref, g_ref, dy_ref, dx_ref, dxb_ref, dg_ref = refs
        xv, dyv = x_ref[...], dy_ref[...]
        r = lax.rsqrt(jnp.mean(xv * xv, axis=-1, keepdims=True) + RMS_EPS)
        xh = xv * r
        dxh = dyv * g_ref[...]
        dx = r * (dxh - xh * jnp.mean(dxh * xh, axis=-1, keepdims=True))
        if has_res:
            dx = dx + dres_ref[...]
        dx_ref[...] = dx
        dxb_ref[...] = dx.astype(BF16)

        @pl.when(pl.program_id(0) == 0)
        def _():
            dg_ref[...] = jnp.zeros_like(dg_ref)

        dg_ref[...] += jnp.sum(dyv * xh, axis=0, keepdims=True)

    row = pl.BlockSpec((tm, D), lambda i: (i, 0))
    vec = pl.BlockSpec((1, D), lambda i: (0, 0))
    args = [x, g, dy] + ([dres] if has_res else [])
    return pl.pallas_call(
        body, grid=(T // tm,), in_specs=[row, vec, row] + ([row] if has_res else []),
        out_specs=[row, row, vec], out_shape=[SDS((T, D), F32), SDS((T, D), BF16), SDS((1, D), F32)],
        compiler_params=_cparams(("arbitrary",)), name=name)(*args)


SUB_TILE_ROWS = 512


def _sub_tiles(tm):
    step = min(SUB_TILE_ROWS, tm)
    return [slice(r, r + step) for r in range(0, tm, step)]


def ffn_up_swiglu(h, w13b, name, rider=None):
    T, D = h.shape
    nb, _, tb = w13b.shape
    nh = nb // 2
    tm = _fit(T, 512, 8)

    def body(h_ref, wg_ref, wu_ref, gu_ref, a_ref):
        for rows in _sub_tiles(tm):
            hv = h_ref[rows, :]
            gv = _bdot(hv, wg_ref[...], NN)
            uv = _bdot(hv, wu_ref[...], NN)
            gu_ref[0, rows, :] = gv.astype(BF16)
            gu_ref[1, rows, :] = uv.astype(BF16)
            a_ref[rows, :] = (gv * jax.nn.sigmoid(gv) * uv).astype(BF16)

    (gu, act), carried = _call_with_rider(
        body, (nh, T // tm),
        [pl.BlockSpec((tm, D), lambda j, i: (i, 0)), pl.BlockSpec((None, D, tb), lambda j, i: (j, 0, 0)),
         pl.BlockSpec((None, D, tb), lambda j, i: (j + nh, 0, 0))],
        [pl.BlockSpec((2, tm, tb), lambda j, i: (0, i, j)), pl.BlockSpec((tm, tb), lambda j, i: (i, j))],
        [SDS((2, T, nh * tb), BF16), SDS((T, nh * tb), BF16)], [], ("parallel", "parallel"), name, [h, w13b, w13b], rider)
    return gu, act, carried


def ffn_down_bwd(dy_b, w2, gu, name):
    T, D = dy_b.shape
    Fh = w2.shape[0]
    tm, tf = _fit(T, 1024, 8), _fit(Fh, 512)
    nf = Fh // tf

    def body(dy_ref, w_ref, g_ref, u_ref, o_ref):
        for rows in _sub_tiles(tm):
            dav = _bdot(dy_ref[rows, :], w_ref[...], NT)
            gv, uv = g_ref[rows, :].astype(F32), u_ref[rows, :].astype(F32)
            sg = jax.nn.sigmoid(gv)
            o_ref[0, rows, :] = (dav * uv * (sg * (1.0 + gv * (1.0 - sg)))).astype(BF16)
            o_ref[1, rows, :] = (dav * gv * sg).astype(BF16)

    return pl.pallas_call(
        body, grid=(T // tm, nf),
        in_specs=[pl.BlockSpec((tm, D), lambda i, j: (i, 0)), pl.BlockSpec((tf, D), lambda i, j: (j, 0)),
                  pl.BlockSpec((None, tm, tf), lambda i, j: (0, i, j)), pl.BlockSpec((None, tm, tf), lambda i, j: (1, i, j))],
        out_specs=pl.BlockSpec((2, tm, tf), lambda i, j: (0, i, j)), out_shape=SDS((2, T, Fh), BF16),
        compiler_params=_cparams(("parallel", "parallel")), name=name)(dy_b, w2, gu, gu)


def loss_bwd(y, tgt, name):
    T, D = y.shape
    tm = _fit(T, 512, 8)

    def body(y_ref, t_ref, dy_ref, dyb_ref, l_ref):
        e = y_ref[...] - t_ref[...]
        dy = e * (1.0 / D)
        dy_ref[...] = dy
        dyb_ref[...] = dy.astype(BF16)

        @pl.when(pl.program_id(0) == 0)
        def _():
            l_ref[...] = jnp.zeros_like(l_ref)

        l_ref[...] += jnp.sum(jnp.sum(e * e, axis=1, keepdims=True), axis=0, keepdims=True)

    row = pl.BlockSpec((tm, D), lambda i: (i, 0))
    return pl.pallas_call(
        body, grid=(T // tm,), in_specs=[row, row], out_specs=[row, row, pl.BlockSpec((8, 128), lambda i: (0, 0))],
        out_shape=[SDS((T, D), F32), SDS((T, D), BF16), SDS((8, 128), F32)],
        compiler_params=_cparams(("arbitrary",)), name=name)(y, tgt)


def _log_sigmoid(z):
    return jnp.minimum(z, 0.0) - jnp.log(1.0 + jnp.exp(-jnp.abs(z)))


def _gla_group(q, k, v, tail, wg, bg, s, cmat, mask, backward_dir):
    n = q.shape[0] // GLA_C
    z = _bdot_vjp(tail, wg, NN) + bg
    la = _log_sigmoid(z) * (1.0 / GLA_TAU)
    cum = _tri_sum(cmat, la)
    chunks = [slice(g * GLA_C, (g + 1) * GLA_C) for g in range(n)]
    last = [jnp.sum(la[sl], axis=0, keepdims=True) for sl in chunks]
    last_rows = jnp.concatenate([jnp.broadcast_to(t, (GLA_C, t.shape[1])) for t in last], axis=0)
    q_e = q * jnp.exp(cum) * (GLA_HK ** -0.5)
    k_e = k * jnp.exp(-cum)
    k_end = k * jnp.exp(last_rows - cum)
    sc = jnp.where(mask > 0.5, _bdot_vjp(q_e, k_e, NT), 0.0)
    o = _bdot_vjp(sc, v, NN)
    o_inter = [None] * n
    for g in (range(n - 1, -1, -1) if backward_dir else range(n)):
        o_inter[g] = _bdot_vjp(q_e[chunks[g]], s, NN)
        s = s * jnp.transpose(jnp.exp(last[g])) + _bdot_vjp(k_end[chunks[g]], v[chunks[g]], TN)
    return o + jnp.concatenate(o_inter, axis=0), s


def _gla_consts(backward_dir, GC):
    i = np.arange(GC)
    same = (i[:, None] // GLA_C) == (i[None, :] // GLA_C)
    if backward_dir:
        return (same & (i[None, :] >= i[:, None])).astype(np.float32), (same & (i[None, :] > i[:, None])).astype(np.float32)
    tri = (same & (i[None, :] <= i[:, None])).astype(np.float32)
    return tri, tri


GLA_HB = 2


def _gla_in_specs(GC, nmap):
    return [
        pl.BlockSpec((GC, GLA_HB * GLA_HK), lambda h, n: (nmap(n), h)),
        pl.BlockSpec((GC, GLA_HB * GLA_HK), lambda h, n: (nmap(n), GLA_H // GLA_HB + h)),
        pl.BlockSpec((GC, GLA_HB * GLA_HV), lambda h, n: (nmap(n), 2 * GLA_DK // (GLA_HB * GLA_HV) + h)),
        pl.BlockSpec((GC, 128), lambda h, n: (nmap(n), COL_TAIL // 128)),
        pl.BlockSpec((GLA_HB, 128, 128), lambda h, n: (h, 0, 0)),
        pl.BlockSpec((GLA_HB, 1, 128), lambda h, n: (h, 0, 0)),
        pl.BlockSpec((GC, GC), lambda h, n: (0, 0)),
        pl.BlockSpec((GC, GC), lambda h, n: (0, 0)),
    ]


def _head_cols(ref, hh, width):
    return ref[:, hh * width:(hh + 1) * width].astype(F32)


def gla_fwd(proj, wgpad, bg, backward_dir, name, rider=None):
    T = proj.shape[0]
    GC = min(GLA_G * GLA_C, T)
    NG = T // GC
    cmat, mask = _gla_consts(backward_dir, GC)
    nmap = (lambda n: NG - 1 - n) if backward_dir else (lambda n: n)

    def body(q_ref, k_ref, v_ref, t_ref, wg_ref, bg_ref, c_ref, m_ref, o_ref, ss_ref, s_scr):
        @pl.when(pl.program_id(1) == 0)
        def _():
            s_scr[...] = jnp.zeros_like(s_scr)

        tail, cm, mk = t_ref[...].astype(F32), c_ref[...], m_ref[...]
        for hh in range(GLA_HB):
            s0 = s_scr[hh]
            ss_ref[hh] = s0
            o, s1 = _gla_group(_head_cols(q_ref, hh, GLA_HK), _head_cols(k_ref, hh, GLA_HK), _head_cols(v_ref, hh, GLA_HV),
                               tail, wg_ref[hh], bg_ref[hh], s0, cm, mk, backward_dir)
            o_ref[:, hh * GLA_HV:(hh + 1) * GLA_HV] = o
            s_scr[hh] = s1

    (o, states), carried = _call_with_rider(
        body, (GLA_H // GLA_HB, NG), _gla_in_specs(GC, nmap),
        [pl.BlockSpec((GC, GLA_HB * GLA_HV), lambda h, n: (nmap(n), h)),
         pl.BlockSpec((GLA_HB, None, GLA_HK, GLA_HV), lambda h, n: (h, nmap(n), 0, 0))],
        [SDS((T, GLA_DV), F32), SDS((GLA_H, NG, GLA_HK, GLA_HV), F32)], [pltpu.VMEM((GLA_HB, GLA_HK, GLA_HV), F32)],
        ("parallel", "arbitrary"), name, [proj, proj, proj, proj, wgpad, bg, cmat, mask], rider)
    return (o, states) if rider is None else (o, states, carried)


def gla_bwd(proj, wgpad, bg, ssave, do, backward_dir, name):
    T = proj.shape[0]
    GC = min(GLA_G * GLA_C, T)
    NG = T // GC
    cmat, mask = _gla_consts(backward_dir, GC)
    nmap = (lambda n: n) if backward_dir else (lambda n: NG - 1 - n)

    def body(q_ref, k_ref, v_ref, t_ref, wg_ref, bg_ref, c_ref, m_ref, ss_ref, do_ref,
             dq_ref, dk_ref, dv_ref, dt_ref, dwg_ref, dbg_ref, ds_scr):
        @pl.when(pl.program_id(1) == 0)
        def _():
            ds_scr[...] = jnp.zeros_like(ds_scr)
            dwg_ref[...] = jnp.zeros_like(dwg_ref)
            dbg_ref[...] = jnp.zeros_like(dbg_ref)

        tail, cm, mk = t_ref[...].astype(F32), c_ref[...], m_ref[...]
        fn = lambda q, k, v, t, wg, b, s: _gla_group(q, k, v, t, wg, b, s, cm, mk, backward_dir)
        for hh in range(GLA_HB):
            _, vjp = jax.vjp(fn, _head_cols(q_ref, hh, GLA_HK), _head_cols(k_ref, hh, GLA_HK), _head_cols(v_ref, hh, GLA_HV),
                             tail, wg_ref[hh], bg_ref[hh], ss_ref[hh])
            dq, dk, dv, dt, dwg, dbg, ds = vjp((_head_cols(do_ref, hh, GLA_HV), ds_scr[hh]))
            dq_ref[:, hh * GLA_HK:(hh + 1) * GLA_HK] = dq.astype(BF16)
            dk_ref[:, hh * GLA_HK:(hh + 1) * GLA_HK] = dk.astype(BF16)
            dv_ref[:, hh * GLA_HV:(hh + 1) * GLA_HV] = dv.astype(BF16)
            dt_ref[hh] = dt.astype(BF16)
            dwg_ref[hh] += dwg
            dbg_ref[hh] += dbg
            ds_scr[hh] = ds

    in_specs = _gla_in_specs(GC, nmap) + [
        pl.BlockSpec((GLA_HB, None, GLA_HK, GLA_HV), lambda h, n: (h, nmap(n), 0, 0)),
        pl.BlockSpec((GC, GLA_HB * GLA_HV), lambda h, n: (nmap(n), h)),
    ]
    out_specs = [
        pl.BlockSpec((GC, GLA_HB * GLA_HK), lambda h, n: (nmap(n), h)),
        pl.BlockSpec((GC, GLA_HB * GLA_HK), lambda h, n: (nmap(n), h)),
        pl.BlockSpec((GC, GLA_HB * GLA_HV), lambda h, n: (nmap(n), h)),
        pl.BlockSpec((GLA_HB, GC, 128), lambda h, n: (h, nmap(n), 0)),
        pl.BlockSpec((GLA_HB, 128, 128), lambda h, n: (h, 0, 0)),
        pl.BlockSpec((GLA_HB, 1, 128), lambda h, n: (h, 0, 0)),
    ]
    out_shape = [SDS((T, GLA_DK), BF16), SDS((T, GLA_DK), BF16), SDS((T, GLA_DV), BF16), SDS((GLA_H, T, 128), BF16),
                 SDS((GLA_H, 128, 128), F32), SDS((GLA_H, 1, 128), F32)]
    return pl.pallas_call(
        body, grid=(GLA_H // GLA_HB, NG), in_specs=in_specs, out_specs=out_specs, out_shape=out_shape,
        scratch_shapes=[pltpu.VMEM((GLA_HB, GLA_HK, GLA_HV), F32)],
        compiler_params=_cparams(("parallel", "arbitrary")), name=name)(proj, proj, proj, proj, wgpad, bg, cmat, mask, ssave, do)


def _block_diag(width, hd):
    i = np.arange(width)
    return ((i[:, None] // hd) == (i[None, :] // hd)).astype(np.float32) / hd


def _norm_heads(t, hd):
    outs = []
    for h in range(t.shape[1] // hd):
        th = t[:, h * hd:(h + 1) * hd]
        outs.append(th * lax.rsqrt(jnp.mean(th * th, axis=-1, keepdims=True) + RMS_EPS))
    return jnp.concatenate(outs, axis=1)


def _split_dot_right(x, c):
    hi = x.astype(BF16)
    lo = (x - hi.astype(F32)).astype(BF16)
    cb = c.astype(BF16)
    return jnp.dot(hi, cb, preferred_element_type=F32) + jnp.dot(lo, cb, preferred_element_type=F32)


@jax.custom_vjp
def _head_means(x, bd):
    return _split_dot_right(x, bd)


def _head_means_fwd(x, bd):
    return _split_dot_right(x, bd), bd


def _head_means_bwd(bd, ct):
    return _split_dot_right(ct, bd), jnp.zeros_like(bd)


_head_means.defvjp(_head_means_fwd, _head_means_bwd)


def _norm_bd(t, bd):
    return t * lax.rsqrt(_head_means(t * t, bd) + RMS_EPS)


def _pre_fn(nq, nk, mq, gq, gk, gm, bd):
    return _norm_bd(nq, bd) * gq, _norm_bd(nk, bd) * gk, _norm_heads(mq, MEM_HD) * gm


def pre_fwd(proj, gq, gk, gm, name):
    T = proj.shape[0]
    tm = _fit(T, 512, 8)
    bd = _block_diag(NA_W, NA_HD)

    def body(nq_ref, nk_ref, nv_ref, mq_ref, gq_ref, gk_ref, gm_ref, bd_ref, q_ref, k_ref, v_ref, m_ref):
        qn, kn, mn = _pre_fn(nq_ref[...].astype(F32), nk_ref[...].astype(F32), mq_ref[...].astype(F32),
                             gq_ref[...], gk_ref[...], gm_ref[...], bd_ref[...])
        q_ref[...] = qn.astype(BF16)
        k_ref[...] = kn.astype(BF16)
        v_ref[...] = nv_ref[...].astype(BF16)
        m_ref[...] = mn.astype(BF16)

    col = lambda c0: pl.BlockSpec((tm, 512), lambda i: (i, c0 // 512))
    vec = pl.BlockSpec((1, 512), lambda i: (0, 0))
    row = pl.BlockSpec((tm, 512), lambda i: (i, 0))
    return pl.pallas_call(
        body, grid=(T // tm,),
        in_specs=[col(COL_NQ), col(COL_NK), col(COL_NV), col(COL_MQ), vec, vec, vec, pl.BlockSpec((NA_W, NA_W), lambda i: (0, 0))],
        out_specs=[row] * 4, out_shape=[SDS((T, 512), BF16)] * 4,
        compiler_params=_cparams(("parallel",)), name=name)(proj, proj, proj, proj, gq, gk, gm, bd)


def pre_bwd(proj, gq, gk, gm, d_qn, d_kn, d_nv, d_mn, dq_f, dq_b, dk_f, dk_b, dv_f, dv_b, d_r, dt_f, dt_b, name):
    T = proj.shape[0]
    tm = _fit(T, 256, 8)
    bd = _block_diag(NA_W, NA_HD)

    def body(nq_ref, nk_ref, mq_ref, gq_ref, gk_ref, gm_ref, bd_ref, dqn_ref, dkn_ref, dnv_ref, dmn_ref,
             dqf_ref, dqb_ref, dkf_ref, dkb_ref, dvf_ref, dvb_ref, dr_ref, dtf_ref, dtb_ref,
             o_ref, dgq_ref, dgk_ref, dgm_ref):
        bdv = bd_ref[...]
        fn = lambda a, b, c, x, y, z: _pre_fn(a, b, c, x, y, z, bdv)
        _, vjp = jax.vjp(fn, nq_ref[...].astype(F32), nk_ref[...].astype(F32), mq_ref[...].astype(F32),
                         gq_ref[...], gk_ref[...], gm_ref[...])
        d_nq, d_nk, d_mq, dgq, dgk, dgm = vjp((dqn_ref[...], dkn_ref[...], dmn_ref[...]))
        both = lambda f_ref, b_ref: (f_ref[...].astype(F32) + b_ref[...].astype(F32)).astype(BF16)
        o_ref[:, 0:512] = both(dqf_ref, dqb_ref)
        o_ref[:, 512:1024] = both(dkf_ref, dkb_ref)
        o_ref[:, 1024:2048] = both(dvf_ref, dvb_ref)
        o_ref[:, COL_R:COL_R + 1024] = dr_ref[...].astype(BF16)
        o_ref[:, COL_NQ:COL_NQ + 512] = d_nq.astype(BF16)
        o_ref[:, COL_NK:COL_NK + 512] = d_nk.astype(BF16)
        o_ref[:, COL_NV:COL_NV + 512] = dnv_ref[...].astype(BF16)
        o_ref[:, COL_MQ:COL_MQ + 512] = d_mq.astype(BF16)
        dt = dtf_ref[0].astype(F32) + dtb_ref[0].astype(F32)
        for h in range(1, GLA_H):
            dt = dt + dtf_ref[h].astype(F32) + dtb_ref[h].astype(F32)
        o_ref[:, COL_TAIL:COL_TAIL + 128] = dt.astype(BF16)
        o_ref[:, COL_TAIL + 128:PC] = jnp.zeros((tm, PC - COL_TAIL - 128), BF16)

        @pl.when(pl.program_id(0) == 0)
        def _():
            dgq_ref[...] = jnp.zeros_like(dgq_ref)
            dgk_ref[...] = jnp.zeros_like(dgk_ref)
            dgm_ref[...] = jnp.zeros_like(dgm_ref)

        dgq_ref[...] += dgq
        dgk_ref[...] += dgk
        dgm_ref[...] += dgm

    col = lambda c0: pl.BlockSpec((tm, 512), lambda i: (i, c0 // 512))
    vec = pl.BlockSpec((1, 512), lambda i: (0, 0))
    r512 = pl.BlockSpec((tm, 512), lambda i: (i, 0))
    r1024 = pl.BlockSpec((tm, 1024), lambda i: (i, 0))
    tl = pl.BlockSpec((GLA_H, tm, 128), lambda i: (0, i, 0))
    in_specs = [col(COL_NQ), col(COL_NK), col(COL_MQ), vec, vec, vec, pl.BlockSpec((NA_W, NA_W), lambda i: (0, 0)),
                r512, r512, r512, r512, r512, r512, r512, r512, r1024, r1024, r1024, tl, tl]
    return pl.pallas_call(
        body, grid=(T // tm,), in_specs=in_specs,
        out_specs=[pl.BlockSpec((tm, PC), lambda i: (i, 0)), vec, vec, vec],
        out_shape=[SDS((T, PC), BF16), SDS((1, 512), F32), SDS((1, 512), F32), SDS((1, 512), F32)],
        compiler_params=_cparams(("arbitrary",)), name=name)(
            proj, proj, proj, gq, gk, gm, bd, d_qn, d_kn, d_nv, d_mn, dq_f, dq_b, dk_f, dk_b, dv_f, dv_b, d_r, dt_f, dt_b)


def _post_fn(o_f, o_b, r, o_na, o_mem, g_gla, g_na, g_mem, bd):
    y_gla = _norm_heads(o_f + o_b, GLA_HV) * g_gla * (r * jax.nn.sigmoid(r))
    y_na = _norm_bd(o_na, bd) * g_na
    y_mem = _norm_heads(o_mem, MEM_HD) * g_mem
    return jnp.concatenate([y_gla, y_na, y_mem], axis=1)


def post_fwd(o_f, o_b, proj, o_na, o_mem, g_gla, g_na, g_mem, name):
    T = proj.shape[0]
    tm = _fit(T, 256, 8)
    bd = _block_diag(NA_W, NA_HD)

    def body(of_ref, ob_ref, r_ref, ona_ref, omem_ref, gg_ref, gn_ref, gm_ref, bd_ref, y_ref):
        y_ref[...] = _post_fn(of_ref[...], ob_ref[...], r_ref[...].astype(F32), ona_ref[...], omem_ref[...],
                              gg_ref[...], gn_ref[...], gm_ref[...], bd_ref[...]).astype(BF16)

    r1024 = pl.BlockSpec((tm, 1024), lambda i: (i, 0))
    r512 = pl.BlockSpec((tm, 512), lambda i: (i, 0))
    in_specs = [r1024, r1024, pl.BlockSpec((tm, 1024), lambda i: (i, COL_R // 1024)), r512, r512,
                pl.BlockSpec((1, 1024), lambda i: (0, 0)), pl.BlockSpec((1, 512), lambda i: (0, 0)),
                pl.BlockSpec((1, 512), lambda i: (0, 0)), pl.BlockSpec((NA_W, NA_W), lambda i: (0, 0))]
    return pl.pallas_call(
        body, grid=(T // tm,), in_specs=in_specs, out_specs=pl.BlockSpec((tm, D_MODEL), lambda i: (i, 0)),
        out_shape=SDS((T, D_MODEL), BF16), compiler_params=_cparams(("parallel",)), name=name)(
            o_f, o_b, proj, o_na, o_mem, g_gla, g_na, g_mem, bd)


def post_bwd(o_f, o_b, proj, o_na, o_mem, g_gla, g_na, g_mem, dy, name):
    T = proj.shape[0]
    tm = _fit(T, 256, 8)
    bd = _block_diag(NA_W, NA_HD)

    def body(of_ref, ob_ref, r_ref, ona_ref, omem_ref, gg_ref, gn_ref, gm_ref, bd_ref, dy_ref,
             do_ref, dr_ref, dna_ref, dmem_ref, dgg_ref, dgn_ref, dgm_ref):
        bdv = bd_ref[...]
        fn = lambda o, r, a, m, x, y, z: _post_fn(o, 0.0, r, a, m, x, y, z, bdv)
        _, vjp = jax.vjp(fn, of_ref[...] + ob_ref[...], r_ref[...].astype(F32), ona_ref[...], omem_ref[...],
                         gg_ref[...], gn_ref[...], gm_ref[...])
        d_o, d_r, d_na, d_mem, dgg, dgn, dgm = vjp(dy_ref[...])
        do_ref[...] = d_o.astype(BF16)
        dr_ref[...] = d_r.astype(BF16)
        dna_ref[...] = d_na.astype(BF16)
        dmem_ref[...] = d_mem.astype(BF16)

        @pl.when(pl.program_id(0) == 0)
        def _():
            dgg_ref[...] = jnp.zeros_like(dgg_ref)
            dgn_ref[...] = jnp.zeros_like(dgn_ref)
            dgm_ref[...] = jnp.zeros_like(dgm_ref)

        dgg_ref[...] += dgg
        dgn_ref[...] += dgn
        dgm_ref[...] += dgm

    r1024 = pl.BlockSpec((tm, 1024), lambda i: (i, 0))
    r512 = pl.BlockSpec((tm, 512), lambda i: (i, 0))
    v1024 = pl.BlockSpec((1, 1024), lambda i: (0, 0))
    v512 = pl.BlockSpec((1, 512), lambda i: (0, 0))
    in_specs = [r1024, r1024, pl.BlockSpec((tm, 1024), lambda i: (i, COL_R // 1024)), r512, r512, v1024, v512, v512,
                pl.BlockSpec((NA_W, NA_W), lambda i: (0, 0)), pl.BlockSpec((tm, D_MODEL), lambda i: (i, 0))]
    return pl.pallas_call(
        body, grid=(T // tm,), in_specs=in_specs, out_specs=[r1024, r1024, r512, r512, v1024, v512, v512],
        out_shape=[SDS((T, 1024), BF16), SDS((T, 1024), BF16), SDS((T, 512), BF16), SDS((T, 512), BF16),
                   SDS((1, 1024), F32), SDS((1, 512), F32), SDS((1, 512), F32)],
        compiler_params=_cparams(("arbitrary",)), name=name)(o_f, o_b, proj, o_na, o_mem, g_gla, g_na, g_mem, bd, dy)


NA_RB = 8


def _na_row_scores(q_ref, k_ref, v_ref, tb_ref, rb, j, n_rows):
    r = rb * NA_RB + j
    rs = jnp.clip(r - NA_ROWS // 2, 0, n_rows - NA_ROWS)
    dr0 = rs - r + (NA_ROWS - 1)
    tok = pl.ds(pl.multiple_of(rs * GRID_W, GRID_W), NA_ROWS * GRID_W)
    q = q_ref[j * GRID_W:(j + 1) * GRID_W, :]
    kk, vv = k_ref[tok, :], v_ref[tok, :]
    bias = jnp.concatenate([tb_ref[dr0 + 2 * i] for i in range(NA_ROWS // 2)], axis=1)
    s = _bdot(q, kk, NT) * (NA_HD ** -0.5) + bias
    m = jnp.max(s, axis=1, keepdims=True)
    p = jnp.exp(s - m)
    l = jnp.sum(p, axis=1, keepdims=True)
    return q, kk, vv, p, l, tok, dr0


def natten_fwd(q, k, v, tb2, name):
    H, T, hd = q.shape
    n_rows = T // GRID_W
    rbt = NA_RB * GRID_W

    def body(q_ref, k_ref, v_ref, tb_ref, o_ref):
        rb = pl.program_id(1)
        for j in range(NA_RB):
            _, _, vv, p, l, _, _ = _na_row_scores(q_ref, k_ref, v_ref, tb_ref, rb, j, n_rows)
            o_ref[j * GRID_W:(j + 1) * GRID_W, :] = _bdot(p, vv, NN) / l

    whole = pl.BlockSpec((None, T, hd), lambda h, r: (h, 0, 0))
    blk = pl.BlockSpec((None, rbt, hd), lambda h, r: (h, r, 0))
    return pl.pallas_call(
        body, grid=(H, n_rows // NA_RB),
        in_specs=[blk, whole, whole, pl.BlockSpec((None, 2 * NA_ROWS - 2, GRID_W, 2 * GRID_W), lambda h, r: (h, 0, 0, 0))],
        out_specs=blk, out_shape=SDS((H, T, hd), F32),
        compiler_params=_cparams(("parallel", "arbitrary")), name=name)(q, k, v, tb2)


def natten_bwd(q, k, v, tb2, do, name):
    H, T, hd = q.shape
    n_rows = T // GRID_W
    rbt = NA_RB * GRID_W
    scale = NA_HD ** -0.5

    def body(q_ref, k_ref, v_ref, tb_ref, do_ref, dq_ref, dk_ref, dv_ref, dtb_ref):
        rb = pl.program_id(1)

        @pl.when(rb == 0)
        def _():
            dk_ref[...] = jnp.zeros_like(dk_ref)
            dv_ref[...] = jnp.zeros_like(dv_ref)
            dtb_ref[...] = jnp.zeros_like(dtb_ref)

        for j in range(NA_RB):
            qv, kk, vv, p, l, tok, dr0 = _na_row_scores(q_ref, k_ref, v_ref, tb_ref, rb, j, n_rows)
            p = p / l
            dov = do_ref[j * GRID_W:(j + 1) * GRID_W, :]
            dp = _bdot(dov, vv, NT)
            ds = p * (dp - jnp.sum(dp * p, axis=1, keepdims=True))
            dq_ref[j * GRID_W:(j + 1) * GRID_W, :] = _bdot(ds, kk, NN) * scale
            dk_ref[tok, :] += _bdot(ds, qv, TN) * scale
            dv_ref[tok, :] += _bdot(p, dov, TN)
            for i in range(NA_ROWS // 2):
                dtb_ref[dr0 + 2 * i] += ds[:, 2 * GRID_W * i:2 * GRID_W * (i + 1)]

    whole = pl.BlockSpec((None, T, hd), lambda h, r: (h, 0, 0))
    blk = pl.BlockSpec((None, rbt, hd), lambda h, r: (h, r, 0))
    tbs = pl.BlockSpec((None, 2 * NA_ROWS - 2, GRID_W, 2 * GRID_W), lambda h, r: (h, 0, 0, 0))
    return pl.pallas_call(
        body, grid=(H, n_rows // NA_RB), in_specs=[blk, whole, whole, tbs, blk],
        out_specs=[blk, whole, whole, tbs],
        out_shape=[SDS((H, T, hd), F32), SDS((H, T, hd), F32), SDS((H, T, hd), F32), SDS(tb2.shape, F32)],
        compiler_params=_cparams(("parallel", "arbitrary")), name=name)(q, k, v, tb2, do)


def _rpb_expand_consts():
    qc = np.arange(GRID_W)[:, None]
    kc = np.arange(GRID_W)[None, :]
    cs = np.clip(qc - NA_COLS // 2, 0, GRID_W - NA_COLS)
    inside = (kc >= cs) & (kc < cs + NA_COLS)
    dc = np.clip(kc - qc, -(NA_COLS - 1), NA_COLS - 1) + (NA_COLS - 1)
    e = np.zeros((128, GRID_W * GRID_W), np.float32)
    flat = (qc * GRID_W + kc)
    e[dc[inside], flat[inside]] = 1.0
    neg = np.where(inside, 0.0, NEG).astype(np.float32).reshape(1, -1)
    return e, neg


def _rpb_fold_consts():
    sa = np.zeros((NA_H * 15, NA_H * 14), np.float32)
    sb = np.zeros((NA_H * 15, NA_H * 14), np.float32)
    for h in range(NA_H):
        for d in range(14):
            sa[h * 15 + d, h * 14 + d] = 1.0
            sb[h * 15 + d + 1, h * 14 + d] = 1.0
    return sa, sb


def rpb_table(rpb, name):
    e, neg = _rpb_expand_consts()
    rp = jnp.pad(rpb.reshape(NA_H * 15, 31), ((0, 0), (0, 128 - 31)))

    def body(r_ref, e_ref, n_ref, o_ref):
        o_ref[...] = jnp.dot(r_ref[...], e_ref[...], preferred_element_type=F32, precision=HI) + n_ref[...]

    t = pl.pallas_call(body, out_shape=SDS((NA_H * 15, GRID_W * GRID_W), F32), name=name)(rp, e, neg)
    t = t.reshape(NA_H, 15, GRID_W, GRID_W)
    return jnp.concatenate([t[:, :14], t[:, 1:]], axis=-1)


def rpb_table_bwd(dtb2, name):
    e, _ = _rpb_expand_consts()
    sa, sb = _rpb_fold_consts()
    a = dtb2[..., :GRID_W].reshape(NA_H * 14, GRID_W * GRID_W)
    b = dtb2[..., GRID_W:].reshape(NA_H * 14, GRID_W * GRID_W)

    def body(a_ref, b_ref, e_ref, sa_ref, sb_ref, o_ref):
        ev = e_ref[...]
        pa = lax.dot_general(a_ref[...], ev, (NT, ((), ())), preferred_element_type=F32, precision=HI)
        pb = lax.dot_general(b_ref[...], ev, (NT, ((), ())), preferred_element_type=F32, precision=HI)
        o_ref[...] = (jnp.dot(sa_ref[...], pa, preferred_element_type=F32, precision=HI)
                      + jnp.dot(sb_ref[...], pb, preferred_element_type=F32, precision=HI))

    d = pl.pallas_call(body, out_shape=SDS((NA_H * 15, 128), F32), name=name)(a, b, e, sa, sb)
    return d[:, :31].reshape(NA_H, 15, 31)


def _kprep_fn(kv, gk):
    return _norm_heads(kv[:, :MEM_W], MEM_HD) * gk, kv[:, MEM_W:]


def mem_kprep(kv, gk, name):
    def body(kv_ref, g_ref, k_ref, v_ref):
        kn, vv = _kprep_fn(kv_ref[...], g_ref[...])
        k_ref[...] = kn.astype(BF16)
        v_ref[...] = vv.astype(BF16)

    return pl.pallas_call(body, out_shape=[SDS((MEM_TOK, MEM_W), BF16)] * 2, name=name)(kv, gk)


def mem_kprep_bwd(kv, gk, dk, dv, name):
    def body(kv_ref, g_ref, dk_ref, dv_ref, dkv_ref, dg_ref):
        _, vjp = jax.vjp(_kprep_fn, kv_ref[...], g_ref[...])
        dkv, dg = vjp((dk_ref[...], dv_ref[...]))
        dkv_ref[...] = dkv.astype(BF16)
        dg_ref[...] = dg

    return pl.pallas_call(body, out_shape=[SDS((MEM_TOK, 2 * MEM_W), BF16), SDS((1, MEM_W), F32)], name=name)(kv, gk, dk, dv)


def _mem_probs(q_ref, k_ref, h):
    hs = slice(h * MEM_HD, (h + 1) * MEM_HD)
    qh, kh = q_ref[:, hs], k_ref[:, hs]
    s = _bdot(qh, kh, NT) * (MEM_HD ** -0.5)
    p = jnp.exp(s - jnp.max(s, axis=1, keepdims=True))
    return hs, qh, kh, p, jnp.sum(p, axis=1, keepdims=True)


def mem_attn_fwd(q, km, vm, name):
    T = q.shape[0]
    tm = _fit(T, 512, 8)

    def body(q_ref, k_ref, v_ref, o_ref):
        for h in range(MEM_H):
            hs, _, _, p, l = _mem_probs(q_ref, k_ref, h)
            o_ref[:, hs] = _bdot(p, v_ref[:, hs], NN) / l

    row = pl.BlockSpec((tm, MEM_W), lambda i: (i, 0))
    full = pl.BlockSpec((MEM_TOK, MEM_W), lambda i: (0, 0))
    return pl.pallas_call(body, grid=(T // tm,), in_specs=[row, full, full], out_specs=row, out_shape=SDS((T, MEM_W), F32),
                          compiler_params=_cparams(("parallel",)), name=name)(q, km, vm)


def mem_attn_bwd(q, km, vm, do, name):
    T = q.shape[0]
    tm = _fit(T, 512, 8)
    scale = MEM_HD ** -0.5

    def body(q_ref, k_ref, v_ref, do_ref, dq_ref, dk_ref, dv_ref):
        @pl.when(pl.program_id(0) == 0)
        def _():
            dk_ref[...] = jnp.zeros_like(dk_ref)
            dv_ref[...] = jnp.zeros_like(dv_ref)

        for h in range(MEM_H):
            hs, qh, kh, p, l = _mem_probs(q_ref, k_ref, h)
            p = p / l
            dov = do_ref[:, hs]
            dp = _bdot(dov, v_ref[:, hs], NT)
            ds = p * (dp - jnp.sum(dp * p, axis=1, keepdims=True))
            dq_ref[:, hs] = _bdot(ds, kh, NN) * scale
            dk_ref[:, hs] += _bdot(ds, qh, TN) * scale
            dv_ref[:, hs] += _bdot(p, dov, TN)

    row = pl.BlockSpec((tm, MEM_W), lambda i: (i, 0))
    full = pl.BlockSpec((MEM_TOK, MEM_W), lambda i: (0, 0))
    return pl.pallas_call(
        body, grid=(T // tm,), in_specs=[row, full, full, row], out_specs=[row, full, full],
        out_shape=[SDS((T, MEM_W), F32), SDS((MEM_TOK, MEM_W), F32), SDS((MEM_TOK, MEM_W), F32)],
        compiler_params=_cparams(("arbitrary",)), name=name)(q, km, vm, do)


SLOT_BLOCK_ELEMS = 512 * 1024


def _slot_rows(R, Cc):
    return _fit(R, max(16, SLOT_BLOCK_ELEMS // Cc // 16 * 16), 16)


def sum_slots(a, name):
    n, R, Cc = a.shape
    tr = _slot_rows(R, Cc)

    def body(a_ref, o_ref):
        s = a_ref[0].astype(F32)
        for i in range(1, n):
            s = s + a_ref[i].astype(F32)
        o_ref[...] = s

    return pl.pallas_call(body, grid=(R // tr,), in_specs=[pl.BlockSpec((n, tr, Cc), lambda i: (0, i, 0))],
                          out_specs=pl.BlockSpec((tr, Cc), lambda i: (i, 0)), out_shape=SDS((R, Cc), F32),
                          compiler_params=_cparams(("parallel",)), name=name)(a)


def pair_sum(g, land, c_idx, name):
    _, R, Cc = g.shape
    tr = _slot_rows(R, Cc)

    def body(c_ref, g_ref, l_ref, o_ref):
        o_ref[...] = (g_ref[...].astype(F32) + l_ref[...].astype(F32)).astype(o_ref.dtype)

    blk = pl.BlockSpec((None, tr, Cc), lambda k, i, c: (k, i, 0))
    gs = pltpu.PrefetchScalarGridSpec(
        num_scalar_prefetch=1, grid=(4, R // tr),
        in_specs=[pl.BlockSpec((None, tr, Cc), lambda k, i, c: (2 * k + c[0], i, 0)), blk], out_specs=blk)
    return pl.pallas_call(body, grid_spec=gs, out_shape=SDS((4, R, Cc), g.dtype),
                          compiler_params=_cparams(("parallel", "parallel")), name=name)(c_idx, g, land)


def adamw(w, g, m, v, name):
    R, Cc = w.shape
    tr = _fit(R, max(8, (262144 // max(Cc, 128)) // 8 * 8), 8)
    c1 = 1.0 - ADAM_B1 ** ADAM_STEP
    c2 = 1.0 - ADAM_B2 ** ADAM_STEP

    def body(w_ref, g_ref, m_ref, v_ref, d_ref, mo_ref, vo_ref):
        gv = g_ref[...]
        m2 = ADAM_B1 * m_ref[...] + (1.0 - ADAM_B1) * gv
        v2 = ADAM_B2 * v_ref[...] + (1.0 - ADAM_B2) * (gv * gv)
        d_ref[...] = -ADAM_LR * ((m2 / c1) / (jnp.sqrt(v2 / c2) + ADAM_EPS) + ADAM_WD * w_ref[...])
        mo_ref[...] = m2
        vo_ref[...] = v2

    blk = pl.BlockSpec((tr, Cc), lambda i: (i, 0))
    return pl.pallas_call(body, grid=(R // tr,), in_specs=[blk] * 4, out_specs=[blk] * 3, out_shape=[SDS((R, Cc), F32)] * 3,
                          compiler_params=_cparams(("parallel",)), name=name)(w, g, m, v)


ANY = pl.BlockSpec(memory_space=pl.ANY)


def _my_pos():
    return lax.axis_index("x"), lax.axis_index("y"), lax.axis_index("c")


class Rider:
    def __init__(self, inputs, out_shapes, scratch, start, finish, aliases=None):
        self.inputs, self.out_shapes, self.scratch = list(inputs), list(out_shapes), list(scratch)
        self.start, self.finish, self.aliases = start, finish, dict(aliases or {})


def combine_riders(r1, r2):
    ni, no, ns = len(r1.inputs), len(r1.out_shapes), len(r1.scratch)

    def both(f1, f2):
        def run(i, o, s):
            f1(i[:ni], o[:no], s[:ns])
            f2(i[ni:], o[no:], s[ns:])
        return run

    aliases = {**r1.aliases, **{ni + i: no + j for i, j in r2.aliases.items()}}
    return Rider(r1.inputs + r2.inputs, r1.out_shapes + r2.out_shapes, r1.scratch + r2.scratch,
                 both(r1.start, r2.start), both(r1.finish, r2.finish), aliases)


def _gather_phases(x_refs, out_refs, sems):
    send_sems, recv_sems, local_sems = sems
    n = len(out_refs)
    x, y, c = _my_pos()
    me, sibling = (x, y, c), (x, y, 1 - c)
    chips = [(1 - x, y), (x, 1 - y), (1 - x, 1 - y)]

    def slot(a, px, py, pc):
        return out_refs[a].at[4 * px + 2 * py + pc]

    def copy(a, k, block, to, src=None):
        return pltpu.make_async_remote_copy(
            src_ref=slot(a, *block) if src is None else src, dst_ref=slot(a, *block),
            send_sem=send_sems.at[7 * a + k], recv_sem=recv_sems.at[7 * a + k], device_id=to, device_id_type=MESH)

    def mine(a):
        return pltpu.make_async_copy(x_refs[a], slot(a, *me), local_sems.at[a])

    def spread(a):
        return [copy(a, 0, me, sibling, src=x_refs[a])] + [copy(a, 1 + j, me, (*chip, c), src=x_refs[a])
                                                           for j, chip in enumerate(chips)]

    def spread_start():
        for a in range(n):
            mine(a).start()
        for a in range(n):
            for cp in spread(a):
                cp.start()

    def spread_finish():
        for j, chip in enumerate(chips):
            for a in range(n):
                copy(a, 1 + j, (*chip, c), me).wait_recv()
        for a in range(n):
            copy(a, 0, sibling, me).wait_recv()
        for a in range(n):
            for cp in spread(a):
                cp.wait_send()
            mine(a).wait()

    def forward_start():
        for j, chip in enumerate(chips):
            for a in range(n):
                copy(a, 4 + j, (*chip, c), sibling).start()

    def forward_finish():
        for j, chip in enumerate(chips):
            for a in range(n):
                copy(a, 4 + j, (*chip, 1 - c), me).wait_recv()
        for j, chip in enumerate(chips):
            for a in range(n):
                copy(a, 4 + j, (*chip, c), sibling).wait_send()

    return spread_start, spread_finish, forward_start, forward_finish


def _gather_scratch(n):
    return [pltpu.SemaphoreType.DMA((7 * n,)), pltpu.SemaphoreType.DMA((7 * n,)), pltpu.SemaphoreType.DMA((n,))]


def all_gather(shards, name):
    n = len(shards)

    def body(*refs):
        phases = _gather_phases(refs[:n], refs[n:2 * n], refs[2 * n:])
        for phase in phases:
            phase()

    return pl.pallas_call(
        body, out_shape=[SDS((N_DEV,) + s.shape, s.dtype) for s in shards], in_specs=[ANY] * n, out_specs=[ANY] * n,
        scratch_shapes=_gather_scratch(n), name=name)(*shards)


def gather_spread_rider(shards):
    return Rider(shards, [SDS((N_DEV,) + s.shape, s.dtype) for s in shards], _gather_scratch(len(shards)),
                 lambda i, o, s: _gather_phases(i, o, s)[0](), lambda i, o, s: _gather_phases(i, o, s)[1]())


def gather_forward_rider(bufs):
    n = len(bufs)
    return Rider(bufs, [SDS(b.shape, b.dtype) for b in bufs], _gather_scratch(n),
                 lambda i, o, s: _gather_phases(None, o, s)[2](), lambda i, o, s: _gather_phases(None, o, s)[3](),
                 aliases={a: a for a in range(n)})


def pair_exchange(gs, name):
    n = len(gs)

    def body(*refs):
        _pair_exchange_copies(refs[:n], refs[n:2 * n], refs[2 * n:], "start")
        _pair_exchange_copies(refs[:n], refs[n:2 * n], refs[2 * n:], "finish")

    return pl.pallas_call(
        body, out_shape=[SDS((4,) + g.shape[1:], g.dtype) for g in gs], in_specs=[ANY] * n, out_specs=[ANY] * n,
        scratch_shapes=_pair_exchange_scratch(n), name=name)(*gs)


def _pair_exchange_scratch(n):
    return [pltpu.SemaphoreType.DMA((4 * n,)), pltpu.SemaphoreType.DMA((4 * n,))]


def _pair_exchange_copies(g_refs, land_refs, sems, phase):
    send_sems, recv_sems = sems
    x, y, c = _my_pos()
    sibling = (x, y, 1 - c)
    copies = [pltpu.make_async_remote_copy(
        src_ref=g_refs[a].at[2 * k + (1 - c)], dst_ref=land_refs[a].at[k], send_sem=send_sems.at[4 * a + k],
        recv_sem=recv_sems.at[4 * a + k], device_id=sibling, device_id_type=MESH) for a in range(len(g_refs)) for k in range(4)]
    for cp in copies:
        if phase == "start":
            cp.start()
        else:
            cp.wait_recv()
    if phase == "finish":
        for cp in copies:
            cp.wait_send()


def pair_exchange_rider(gs):
    return Rider(gs, [SDS((4,) + g.shape[1:], g.dtype) for g in gs], _pair_exchange_scratch(len(gs)),
                 lambda i, o, s: _pair_exchange_copies(i, o, s, "start"), lambda i, o, s: _pair_exchange_copies(i, o, s, "finish"))


def chip_exchange(ss, name):
    n = len(ss)

    def body(*refs):
        start, finish = _chip_exchange_phases(refs[:n], refs[n:2 * n], refs[2 * n:])
        start()
        finish()

    return pl.pallas_call(
        body, out_shape=[SDS(s.shape, s.dtype) for s in ss], in_specs=[ANY] * n, out_specs=[ANY] * n,
        scratch_shapes=_chip_exchange_scratch(n), name=name)(*ss)


def _chip_exchange_scratch(n):
    return [pltpu.SemaphoreType.DMA((3 * n,)), pltpu.SemaphoreType.DMA((3 * n,)), pltpu.SemaphoreType.DMA((n,))]


def _chip_exchange_phases(s_refs, land_refs, sems):
    send_sems, recv_sems, local_sems = sems
    n = len(s_refs)
    x, y, c = _my_pos()
    my_chip = 2 * x + y
    chips = [(1 - x, y), (x, 1 - y), (1 - x, 1 - y)]

    def own(a):
        return pltpu.make_async_copy(s_refs[a].at[my_chip], land_refs[a].at[my_chip], local_sems.at[a])

    def copy(a, j, src_chip, dst_chip):
        px, py = chips[j]
        return pltpu.make_async_remote_copy(
            src_ref=s_refs[a].at[src_chip], dst_ref=land_refs[a].at[dst_chip], send_sem=send_sems.at[3 * a + j],
            recv_sem=recv_sems.at[3 * a + j], device_id=(px, py, c), device_id_type=MESH)

    def start():
        for a in range(n):
            own(a).start()
            for j, (px, py) in enumerate(chips):
                copy(a, j, 2 * px + py, my_chip).start()

    def finish():
        for a in range(n):
            for j, (px, py) in enumerate(chips):
                copy(a, j, my_chip, 2 * px + py).wait_recv()
        for a in range(n):
            for j, (px, py) in enumerate(chips):
                copy(a, j, 2 * px + py, my_chip).wait_send()
            own(a).wait()

    return start, finish


def chip_exchange_rider(ss):
    return Rider(ss, [SDS(s.shape, s.dtype) for s in ss], _chip_exchange_scratch(len(ss)),
                 lambda i, o, s: _chip_exchange_phases(i, o, s)[0](), lambda i, o, s: _chip_exchange_phases(i, o, s)[1]())


W_IN_SHARD, W_IN_PACKED = 644, 768
BIG = ("ffn_w13", "ffn_w2", "w_out", "mem_wkv", "w_in")
GATE_W = ("gla_wg2_f", "gla_wg2_b")
SHARD_NAMES = BIG + GATE_W


def _permuted_ranges(c0, c1):
    res = []
    for o0, o1, p0 in ((0, ORIG_GATE0, 0), (ORIG_GATE0, ORIG_AFTER_GATE, COL_TAIL), (ORIG_AFTER_GATE, IN_COLS, ORIG_GATE0)):
        lo, hi = max(c0, o0), min(c1, o1)
        if lo < hi:
            res.append((p0 + lo - o0, p0 + hi - o0))
    return res


def assemble_w_in(blocks):
    placed = []
    for d in range(N_DEV):
        c = d * W_IN_SHARD
        for p0, p1 in _permuted_ranges(c, c + W_IN_SHARD):
            placed.append((p0, blocks[d][:, c - d * W_IN_SHARD:c - d * W_IN_SHARD + (p1 - p0)]))
            c += p1 - p0
    placed.sort(key=lambda t: t[0])
    return jnp.concatenate([t[1] for t in placed] + [jnp.zeros((blocks.shape[1], PC - IN_COLS), blocks.dtype)], axis=1)


def split_w_in_grad(g):
    pad = jnp.zeros((g.shape[0], W_IN_PACKED - W_IN_SHARD), g.dtype)
    return jnp.stack([jnp.concatenate([g[:, p0:p1] for p0, p1 in _permuted_ranges(d * W_IN_SHARD, (d + 1) * W_IN_SHARD)]
                                      + [pad], axis=1) for d in range(N_DEV)])


def wire_shards(shards):
    return [jnp.pad(shards[n], ((0, 0), (0, W_IN_PACKED - W_IN_SHARD))) if n == "w_in" else shards[n] for n in BIG]


def weights_from_gathered(names, gathered):
    full = {}
    for n, g in zip(names, gathered):
        full[n] = g if n == "ffn_w13" else assemble_w_in(g) if n == "w_in" else g.reshape(-1, g.shape[-1])
    return full


def destination_slots(names, gw):
    return [split_w_in_grad(gw[n]) if n == "w_in" else gw[n].reshape((N_DEV, -1, gw[n].shape[-1])) for n in names]


def chip_sums(names, g, land, c_idx, tag):
    return [pair_sum(g[i], land[i], c_idx, tag + "rs_pair_sum_" + n) for i, n in enumerate(names)]


def reduce_scatter_begin(names, gw, c_idx, tag):
    g = destination_slots(names, gw)
    return chip_sums(names, g, pair_exchange(g, tag + "rs_pair_exchange_" + names[0]), c_idx, tag)


def reduce_scatter_end(names, landed, tag):
    out = {n: sum_slots(landed[i], tag + "rs_chip_sum_" + n) for i, n in enumerate(names)}
    if "w_in" in out:
        out["w_in"] = out["w_in"][:, :W_IN_SHARD]
    return out


def gather_gate_weights(w, depth):
    mine = jnp.concatenate([w[n].reshape(-1) for n in GATE_W]).reshape(-1, 128)
    got = all_gather([mine], "gather_gate_weights")[0].reshape(N_DEV, len(GATE_W), depth, GLA_RANK, GLA_DK // N_DEV)
    return [{n: got[:, i, l].transpose(1, 0, 2).reshape(GLA_RANK, GLA_DK) for i, n in enumerate(GATE_W)} for l in range(depth)]


def pad_gate_weight(wg2, backward_dir):
    r0 = GLA_RANK if backward_dir else 0
    w = wg2.astype(F32).reshape(GLA_RANK, GLA_H, GLA_HK).transpose(1, 0, 2)
    return jnp.pad(w, ((0, 0), (r0, 128 - GLA_RANK - r0), (0, 0)))


def unpad_gate_grad(dw, backward_dir):
    r0 = GLA_RANK if backward_dir else 0
    return dw[:, r0:r0 + GLA_RANK, :].transpose(1, 0, 2).reshape(GLA_RANK, GLA_DK)


def to_heads(t):
    T = t.shape[0]
    return t.reshape(T, NA_H, NA_HD).transpose(1, 0, 2)


def from_heads(t):
    return t.transpose(1, 0, 2).reshape(t.shape[1], NA_W)


REPLICATED = ("attn_norm", "gla_bg_f", "gla_bg_b", "gla_out_norm", "na_q_norm", "na_k_norm", "na_rpb", "na_out_norm",
              "mem_norm", "mem_q_norm", "mem_k_norm", "mem_out_norm", "ffn_norm")
WEIGHTS = ("attn_norm", "w_in", "gla_wg2_f", "gla_bg_f", "gla_wg2_b", "gla_bg_b", "gla_out_norm", "na_q_norm", "na_k_norm",
           "na_rpb", "na_out_norm", "mem_norm", "mem_wkv", "mem_q_norm", "mem_k_norm", "mem_out_norm", "w_out", "ffn_norm",
           "ffn_w13", "ffn_w2")


def fold_heads(dg, n_heads, name):
    hd = dg.shape[1] // n_heads
    fold = (np.arange(dg.shape[1])[:, None] % hd == np.arange(128)[None, :]).astype(np.float32)
    out = matmul(jnp.pad(dg, ((0, 7), (0, 0))), fold, "nn", name, exact=True)
    return out[0, :hd]


def layer_fwd(x, mem_n_in, p, W, l, arriving=None, next_shards=None, first_shards=None):
    tag = f"l{l}_"
    row = lambda v: v.reshape(1, -1)
    sv = {"x": x}
    if arriving is None:
        sv["xn"] = rms_fwd(x, row(p["attn_norm"]), tag + "attn_rms")
    else:
        sv["xn"], rest = rms_fwd(x, row(p["attn_norm"]), tag + "attn_rms", rider=gather_forward_rider(arriving))
        W = {**W, **weights_from_gathered(BIG[1:], rest)}
    if first_shards is None:
        proj = matmul(sv["xn"], W["w_in"], "nn", tag + "proj", out_dtype=BF16, tn=768, tk=2048)
    else:
        proj, spread = matmul(sv["xn"], W["w_in"], "nn", tag + "proj", out_dtype=BF16, tn=768, tk=2048,
                              rider=gather_spread_rider(first_shards))
    sv["proj"] = proj
    sv["wg_f"], sv["wg_b"] = pad_gate_weight(W["gla_wg2_f"], False), pad_gate_weight(W["gla_wg2_b"], True)
    sv["bg_f"], sv["bg_b"] = p["gla_bg_f"].reshape(GLA_H, 1, GLA_HK), p["gla_bg_b"].reshape(GLA_H, 1, GLA_HK)
    if first_shards is None:
        sv["o_f"], sv["s_f"] = gla_fwd(proj, sv["wg_f"], sv["bg_f"], False, tag + "gla_f")
    else:
        sv["o_f"], sv["s_f"], done = gla_fwd(proj, sv["wg_f"], sv["bg_f"], False, tag + "gla_f",
                                             rider=gather_forward_rider(spread))
        W = {**W, **weights_from_gathered(BIG[:len(done)], done)}
    sv["o_b"], sv["s_b"] = gla_fwd(proj, sv["wg_b"], sv["bg_b"], True, tag + "gla_b")
    sv["gq"], sv["gk"] = jnp.tile(row(p["na_q_norm"]), (1, NA_H)), jnp.tile(row(p["na_k_norm"]), (1, NA_H))
    sv["gmq"], sv["gmk"] = jnp.tile(row(p["mem_q_norm"]), (1, MEM_H)), jnp.tile(row(p["mem_k_norm"]), (1, MEM_H))
    qn, kn, vn, mqn = pre_fwd(proj, sv["gq"], sv["gk"], sv["gmq"], tag + "pre")
    sv["q_hm"], sv["k_hm"], sv["v_hm"], sv["mqn"] = to_heads(qn), to_heads(kn), to_heads(vn), mqn
    sv["tb2"] = rpb_table(p["na_rpb"], tag + "rpb_table")
    sv["o_na"] = from_heads(natten_fwd(sv["q_hm"], sv["k_hm"], sv["v_hm"], sv["tb2"], tag + "natten"))
    sv["mem_n"] = rms_fwd(mem_n_in, row(p["mem_norm"]), tag + "mem_rms")
    sv["kv"] = matmul(sv["mem_n"], W["mem_wkv"], "nn", tag + "mem_kv", tn=512, tk=2048)
    sv["km"], sv["vm"] = mem_kprep(sv["kv"], sv["gmk"], tag + "mem_kprep")
    sv["o_mem"] = mem_attn_fwd(mqn, sv["km"], sv["vm"], tag + "mem_attn")
    sv["ycat"] = post_fwd(sv["o_f"], sv["o_b"], proj, sv["o_na"], sv["o_mem"], row(p["gla_out_norm"]),
                          row(p["na_out_norm"]), row(p["mem_out_norm"]), tag + "post")
    x1 = matmul(sv["ycat"], W["w_out"], "nn", tag + "out_proj", res=x, tk=2048)
    sv["x1"] = x1
    sv["h"] = rms_fwd(x1, row(p["ffn_norm"]), tag + "ffn_rms")
    if next_shards is None:
        sv["gu"], sv["a"], _ = ffn_up_swiglu(sv["h"], W["ffn_w13"], tag + "ffn_up")
        return matmul(sv["a"], W["ffn_w2"], "nn", tag + "ffn_down", res=x1, tk=2816), sv, W, None, None
    sv["gu"], sv["a"], spread13 = ffn_up_swiglu(sv["h"], W["ffn_w13"], tag + "ffn_up",
                                                rider=gather_spread_rider(next_shards[:1]))
    x2, carried = matmul(sv["a"], W["ffn_w2"], "nn", tag + "ffn_down", res=x1, tk=2816,
                         rider=combine_riders(gather_forward_rider(spread13), gather_spread_rider(next_shards[1:])))
    return x2, sv, W, weights_from_gathered(BIG[:1], carried[:1]), carried[1:]


RS_EARLY, RS_LATE = ("ffn_w13", "ffn_w2", "w_out"), ("mem_wkv", "w_in")


def layer_bwd(dx2, dx2_b, mem_n_in, p, W, sv, l, c_idx, rider=None):
    tag = f"l{l}_b_"
    row = lambda v: v.reshape(1, -1)
    gw, gs = {}, {}
    gw["ffn_w2"] = matmul(sv["a"], dx2_b, "tn", tag + "dw2", out_dtype=BF16, tm=1408, tn=1024, tk=2048)
    dgu = ffn_down_bwd(dx2_b, W["ffn_w2"], sv["gu"], tag + "d_swiglu")
    dh = matmul(dgu, W["ffn_w13"], "nt", tag + "d_h", a_halves=True, b_blocked=True, rider=rider)
    dh, carried = dh if rider is not None else (dh, None)
    gw["ffn_w13"] = matmul(sv["h"], dgu, "tn", tag + "dw13", out_dtype=BF16, tm=2048, tn=1408, tk=1024,
                           b_halves=True, out_blocked=True)
    dx1, dx1_b, dg = rms_bwd(sv["x1"], row(p["ffn_norm"]), dh, dx2, tag + "ffn_rms")
    gs["ffn_norm"] = dg[0]
    gw["w_out"] = matmul(sv["ycat"], dx1_b, "tn", tag + "dw_out", out_dtype=BF16, tm=2048, tn=1024, tk=2048)
    slots = destination_slots(RS_EARLY, gw)
    dycat, landed_pair = matmul(dx1_b, W["w_out"], "nt", tag + "d_ycat", tk=2048, rider=pair_exchange_rider(slots))
    early = chip_sums(RS_EARLY, slots, landed_pair, c_idx, tag)
    d_o, d_r, d_ona, d_omem, dgg, dgn, dgm = post_bwd(
        sv["o_f"], sv["o_b"], sv["proj"], sv["o_na"], sv["o_mem"], row(p["gla_out_norm"]), row(p["na_out_norm"]),
        row(p["mem_out_norm"]), dycat, tag + "post")
    gs["gla_out_norm"], gs["na_out_norm"], gs["mem_out_norm"] = dgg[0], dgn[0], dgm[0]
    dq_f, dk_f, dv_f, dt_f, dwg_f, dbg_f = gla_bwd(sv["proj"], sv["wg_f"], sv["bg_f"], sv["s_f"], d_o, False, tag + "gla_f")
    dq_b, dk_b, dv_b, dt_b, dwg_b, dbg_b = gla_bwd(sv["proj"], sv["wg_b"], sv["bg_b"], sv["s_b"], d_o, True, tag + "gla_b")
    gs["gla_wg2_f"], gs["gla_wg2_b"] = unpad_gate_grad(dwg_f, False), unpad_gate_grad(dwg_b, True)
    gs["gla_bg_f"], gs["gla_bg_b"] = dbg_f.reshape(-1), dbg_b.reshape(-1)
    dq_hm, dk_hm, dv_hm, dtb2 = natten_bwd(sv["q_hm"], sv["k_hm"], sv["v_hm"], sv["tb2"], to_heads(d_ona), tag + "natten")
    gs["na_rpb"] = rpb_table_bwd(dtb2, tag + "rpb_table")
    d_mqn, dkm, dvm = mem_attn_bwd(sv["mqn"], sv["km"], sv["vm"], d_omem, tag + "mem_attn")
    dkv, dgmk = mem_kprep_bwd(sv["kv"], sv["gmk"], dkm, dvm, tag + "mem_kprep")
    gs["mem_k_norm"] = fold_heads(dgmk, MEM_H, tag + "fold_mk")
    gw["mem_wkv"] = matmul(sv["mem_n"], dkv, "tn", tag + "dw_kv", out_dtype=BF16, tm=2048, tn=1024, tk=256)
    d_memn = matmul(dkv, W["mem_wkv"], "nt", tag + "d_memn", tk=1024)
    _, _, dg = rms_bwd(mem_n_in, row(p["mem_norm"]), d_memn, None, tag + "mem_rms")
    gs["mem_norm"] = dg[0]
    dproj, dgq, dgk, dgmq = pre_bwd(sv["proj"], sv["gq"], sv["gk"], sv["gmq"], from_heads(dq_hm), from_heads(dk_hm),
                                    from_heads(dv_hm), d_mqn, dq_f, dq_b, dk_f, dk_b, dv_f, dv_b, d_r, dt_f, dt_b, tag + "pre")
    gs["na_q_norm"] = fold_heads(dgq, NA_H, tag + "fold_q")
    gs["na_k_norm"] = fold_heads(dgk, NA_H, tag + "fold_k")
    gs["mem_q_norm"] = fold_heads(dgmq, MEM_H, tag + "fold_mq")
    dxn, landed13 = matmul(dproj, W["w_in"], "nt", tag + "d_xn", tk=1792, rider=chip_exchange_rider(early[:1]))
    gw["w_in"], landed_rest = matmul(sv["xn"], dproj, "tn", tag + "dw_in", out_dtype=BF16, tm=2048, tn=768, tk=2048,
                                     rider=chip_exchange_rider(early[1:]))
    shard_grads = reduce_scatter_end(RS_EARLY, landed13 + landed_rest, tag)
    late = reduce_scatter_begin(RS_LATE, gw, c_idx, tag)
    dx, dx_b, dg = rms_bwd(sv["x"], row(p["attn_norm"]), dxn, dx1, tag + "attn_rms")
    gs["attn_norm"] = dg[0]
    return dx, dx_b, shard_grads, gs, carried, late


def kernel(x, mem, attn_norm, w_in, gla_wg2_f, gla_bg_f, gla_wg2_b, gla_bg_b, gla_out_norm, na_q_norm, na_k_norm, na_rpb, na_out_norm, mem_norm, mem_wkv, mem_q_norm, mem_k_norm, mem_out_norm, w_out, ffn_norm, ffn_w13, ffn_w2, loss_target, m_attn_norm, m_w_in, m_gla_wg2_f, m_gla_bg_f, m_gla_wg2_b, m_gla_bg_b, m_gla_out_norm, m_na_q_norm, m_na_k_norm, m_na_rpb, m_na_out_norm, m_mem_norm, m_mem_wkv, m_mem_q_norm, m_mem_k_norm, m_mem_out_norm, m_w_out, m_ffn_norm, m_ffn_w13, m_ffn_w2, v_attn_norm, v_w_in, v_gla_wg2_f, v_gla_bg_f, v_gla_wg2_b, v_gla_bg_b, v_gla_out_norm, v_na_q_norm, v_na_k_norm, v_na_rpb, v_na_out_norm, v_mem_norm, v_mem_wkv, v_mem_q_norm, v_mem_k_norm, v_mem_out_norm, v_w_out, v_ffn_norm, v_ffn_w13, v_ffn_w2):
    w = dict(attn_norm=attn_norm, w_in=w_in, gla_wg2_f=gla_wg2_f, gla_bg_f=gla_bg_f, gla_wg2_b=gla_wg2_b, gla_bg_b=gla_bg_b,
             gla_out_norm=gla_out_norm, na_q_norm=na_q_norm, na_k_norm=na_k_norm, na_rpb=na_rpb, na_out_norm=na_out_norm,
             mem_norm=mem_norm, mem_wkv=mem_wkv, mem_q_norm=mem_q_norm, mem_k_norm=mem_k_norm, mem_out_norm=mem_out_norm,
             w_out=w_out, ffn_norm=ffn_norm, ffn_w13=ffn_w13, ffn_w2=ffn_w2)
    mom = dict(attn_norm=m_attn_norm, w_in=m_w_in, gla_wg2_f=m_gla_wg2_f, gla_bg_f=m_gla_bg_f, gla_wg2_b=m_gla_wg2_b,
               gla_bg_b=m_gla_bg_b, gla_out_norm=m_gla_out_norm, na_q_norm=m_na_q_norm, na_k_norm=m_na_k_norm, na_rpb=m_na_rpb,
               na_out_norm=m_na_out_norm, mem_norm=m_mem_norm, mem_wkv=m_mem_wkv, mem_q_norm=m_mem_q_norm,
               mem_k_norm=m_mem_k_norm, mem_out_norm=m_mem_out_norm, w_out=m_w_out, ffn_norm=m_ffn_norm, ffn_w13=m_ffn_w13,
               ffn_w2=m_ffn_w2)
    var = dict(attn_norm=v_attn_norm, w_in=v_w_in, gla_wg2_f=v_gla_wg2_f, gla_bg_f=v_gla_bg_f, gla_wg2_b=v_gla_wg2_b,
               gla_bg_b=v_gla_bg_b, gla_out_norm=v_gla_out_norm, na_q_norm=v_na_q_norm, na_k_norm=v_na_k_norm, na_rpb=v_na_rpb,
               na_out_norm=v_na_out_norm, mem_norm=v_mem_norm, mem_wkv=v_mem_wkv, mem_q_norm=v_mem_q_norm,
               mem_k_norm=v_mem_k_norm, mem_out_norm=v_mem_out_norm, w_out=v_w_out, ffn_norm=v_ffn_norm, ffn_w13=v_ffn_w13,
               ffn_w2=v_ffn_w2)
    depth = attn_norm.shape[0]
    T = x.shape[1]
    xs, mem0, tgt = x.reshape(T, D_MODEL), mem.reshape(MEM_TOK, D_MODEL), loss_target.reshape(T, D_MODEL)
    c_idx = lax.axis_index("c").astype(jnp.int32).reshape(1)

    gates = gather_gate_weights(w, depth)
    send = [wire_shards({n: w[n][l].astype(BF16) for n in BIG}) for l in range(depth)]
    P = [{n: w[n][l] for n in REPLICATED} for l in range(depth)]

    ready = {**weights_from_gathered(BIG[4:], all_gather(send[0][4:], "l0_gather_w_in")), **gates[0]}
    arriving = None
    W, saved = [], []
    h = xs
    for l in range(depth):
        h, sv, w_l, ready, arriving = layer_fwd(h, mem0, P[l], ready, l, arriving, send[l + 1] if l + 1 < depth else None,
                                                send[0][:4] if l == 0 else None)
        W.append(w_l)
        saved.append(sv)
        if ready is not None:
            ready = {**ready, **gates[l + 1]}
    dy, dy_b, lsum = loss_bwd(h, tgt, "loss")
    loss = lax.psum(lsum[0, 0], ("x", "y", "c")) * (0.5 / D_MODEL)

    g_shard, g_small = [None] * depth, [None] * depth
    late = None
    for l in range(depth - 1, -1, -1):
        rider = chip_exchange_rider(late) if late is not None else None
        dy, dy_b, g_shard[l], g_small[l], landed, late = layer_bwd(dy, dy_b, mem0, P[l], W[l], saved[l], l, c_idx, rider)
        if landed is not None:
            g_shard[l + 1].update(reduce_scatter_end(RS_LATE, landed, f"l{l + 1}_b_"))
        saved[l] = None
    g_shard[0].update(reduce_scatter_end(RS_LATE, chip_exchange(late, "l0_rs_chip_exchange"), "l0_b_"))
    grad_x = dy.reshape(x.shape)

    small_names = REPLICATED + GATE_W
    small = jnp.concatenate([g_small[l][n].reshape(-1) for l in range(depth) for n in small_names])
    n_small = small.shape[0]
    rows = -(-n_small // 1024) * 8
    small = jnp.pad(small, (0, rows * 128 - n_small)).reshape(rows, 128)
    small = sum_slots(all_gather([small], "gather_small_grads")[0], "sum_small_grads").reshape(-1)
    grads, off = {}, 0
    per_layer = {n: [] for n in small_names}
    my_cols = (4 * lax.axis_index("x") + 2 * lax.axis_index("y") + lax.axis_index("c")) * (GLA_DK // N_DEV)
    for l in range(depth):
        for n in small_names:
            shp = (GLA_RANK, GLA_DK) if n in GATE_W else w[n].shape[1:]
            g = small[off:off + int(np.prod(shp))].reshape(shp)
            off += int(np.prod(shp))
            per_layer[n].append(lax.dynamic_slice_in_dim(g, my_cols, GLA_DK // N_DEV, axis=1) if n in GATE_W else g)
    for n in small_names:
        grads[n] = jnp.stack(per_layer[n])
    for n in BIG:
        grads[n] = jnp.stack([g_shard[l][n] for l in range(depth)])

    delta, new_m, new_v = {}, {}, {}
    for n in WEIGHTS:
        shp = w[n].shape
        two_d = (shp[0], int(np.prod(shp[1:]))) if n in REPLICATED else (int(np.prod(shp[:-1])), shp[-1])
        d_, m_, v_ = adamw(w[n].reshape(two_d), grads[n].reshape(two_d), mom[n].reshape(two_d), var[n].reshape(two_d),
                           "adamw_" + n)
        delta[n], new_m[n], new_v[n] = d_.reshape(shp), m_.reshape(shp), v_.reshape(shp)

    return (loss, grad_x, *[grads[n] for n in WEIGHTS], *[delta[n] for n in WEIGHTS], *[new_m[n] for n in WEIGHTS],
            *[new_v[n] for n in WEIGHTS])
```

```python
import functools

import numpy as np
import jax
import jax.numpy as jnp
from jax import lax
from jax.experimental import pallas as pl
from jax.experimental.pallas import tpu as pltpu

F32, BF16 = jnp.float32, jnp.bfloat16
HI = lax.Precision.HIGHEST
SDS = jax.ShapeDtypeStruct
MESH = pl.DeviceIdType.MESH

D_MODEL = 2048
GRID_W = 64
GLA_H, GLA_HK, GLA_HV, GLA_RANK, GLA_TAU, GLA_C = 4, 128, 256, 16, 16.0, 64
GLA_DK, GLA_DV = GLA_H * GLA_HK, GLA_H * GLA_HV
NA_H, NA_HD, NA_ROWS, NA_COLS = 8, 64, 8, 16
NA_W = NA_H * NA_HD
MEM_H, MEM_HD, MEM_TOK = 4, 128, 256
MEM_W = MEM_H * MEM_HD
D_FF = 5632
IN_COLS = 5152
RMS_EPS = 1e-6
ADAM_LR, ADAM_B1, ADAM_B2, ADAM_EPS, ADAM_WD, ADAM_STEP = 0.001, 0.9, 0.999, 1e-08, 0.01, 10
N_DEV = 8

PC = 5376
COL_R, COL_NQ, COL_NK, COL_NV, COL_MQ, COL_TAIL = 2048, 3072, 3584, 4096, 4608, 5120
ORIG_GATE0 = 3072
ORIG_AFTER_GATE = 3104

GLA_G = 4
VMEM_LIMIT = 56 * 1024 * 1024
NEG = -1e30


def _cparams(sem):
    return pltpu.CompilerParams(dimension_semantics=sem, vmem_limit_bytes=VMEM_LIMIT)


def _fit(n, pref, unit=128):
    if n <= pref:
        return n
    t = (pref // unit) * unit
    while t >= unit:
        if n % t == 0:
            return t
        t -= unit
    return n


def _bdot(a, b, dims):
    return lax.dot_general(a.astype(BF16), b.astype(BF16), (dims, ((), ())), preferred_element_type=F32)


NN, NT, TN = ((1,), (0,)), ((1,), (1,)), ((0,), (0,))


@functools.partial(jax.custom_vjp, nondiff_argnums=(2,))
def _bdot_vjp(a, b, dims):
    return _bdot(a, b, dims)


def _bdot_vjp_fwd(a, b, dims):
    return _bdot(a, b, dims), (a, b)


def _bdot_vjp_bwd(dims, res, ct):
    a, b = res
    if dims == NN:
        return _bdot(ct, b, NT), _bdot(a, ct, TN)
    if dims == NT:
        return _bdot(ct, b, NN), _bdot(ct, a, TN)
    return _bdot(b, ct, NT), _bdot(a, ct, NN)


_bdot_vjp.defvjp(_bdot_vjp_fwd, _bdot_vjp_bwd)


def _split_dot(c, x, dims):
    hi = x.astype(BF16)
    lo = (x - hi.astype(F32)).astype(BF16)
    cb = c.astype(BF16)
    return (lax.dot_general(cb, hi, (dims, ((), ())), preferred_element_type=F32)
            + lax.dot_general(cb, lo, (dims, ((), ())), preferred_element_type=F32))


@jax.custom_vjp
def _tri_sum(cmat, x):
    return _split_dot(cmat, x, NN)


def _tri_sum_fwd(cmat, x):
    return _split_dot(cmat, x, NN), cmat


def _tri_sum_bwd(cmat, ct):
    return jnp.zeros_like(cmat), _split_dot(cmat, ct, TN)


_tri_sum.defvjp(_tri_sum_fwd, _tri_sum_bwd)


def _call_with_rider(body, grid, in_specs, out_specs, out_shape, scratch, sem, name, args, rider):
    if rider is None:
        return pl.pallas_call(body, grid=grid, in_specs=in_specs, out_specs=out_specs, out_shape=out_shape,
                              scratch_shapes=scratch, compiler_params=_cparams(sem), name=name)(*args), None
    outs_l = list(out_shape) if isinstance(out_shape, (list, tuple)) else [out_shape]
    specs_l = list(out_specs) if isinstance(out_specs, (list, tuple)) else [out_specs]
    ni, no, ns = len(in_specs), len(outs_l), len(scratch)
    nri, nro = len(rider.inputs), len(rider.out_shapes)
    any_spec = pl.BlockSpec(memory_space=pl.ANY)

    def wrapped(*refs):
        ins, rin = refs[:ni], refs[ni:ni + nri]
        o0 = ni + nri
        outs, rout = refs[o0:o0 + no], refs[o0 + no:o0 + no + nro]
        s0 = o0 + no + nro
        scr, rsem = refs[s0:s0 + ns], refs[s0 + ns:]
        ids = [pl.program_id(d) for d in range(len(grid))]
        first = functools.reduce(jnp.logical_and, [i == 0 for i in ids])
        last = functools.reduce(jnp.logical_and, [i == g - 1 for i, g in zip(ids, grid)])

        @pl.when(first)
        def _():
            rider.start(rin, rout, rsem)

        body(*ins, *outs, *scr)

        @pl.when(last)
        def _():
            rider.finish(rin, rout, rsem)

    res = pl.pallas_call(
        wrapped, grid=grid, in_specs=list(in_specs) + [any_spec] * nri, out_specs=specs_l + [any_spec] * nro,
        out_shape=outs_l + rider.out_shapes, scratch_shapes=list(scratch) + rider.scratch,
        input_output_aliases={ni + i: no + j for i, j in rider.aliases.items()},
        compiler_params=_cparams(("arbitrary",) * len(grid)), name=name)(*args, *rider.inputs)
    main = res[:no]
    return (main if isinstance(out_shape, (list, tuple)) else main[0]), list(res[no:])


def matmul(a, b, mode, name, out_dtype=F32, res=None, tm=1024, tn=1024, tk=512, exact=False,
           a_halves=False, b_halves=False, b_blocked=False, out_blocked=False, rider=None):
    if mode == "nn":
        M, K = a.shape
        N = b.shape[0] * b.shape[2] if b_blocked else b.shape[1]
        if b_blocked:
            tn = b.shape[2]
    elif mode == "nt":
        M, K = (a.shape[1], 2 * a.shape[2]) if a_halves else a.shape
        N = b.shape[1] if b_blocked else b.shape[0]
        if b_blocked:
            tk = b.shape[2]
    else:
        K, M = a.shape
        N = 2 * b.shape[2] if b_halves else b.shape[1]
    tm, tn, tk = _fit(M, tm, 8 if mode != "tn" else 128), _fit(N, tn), _fit(K, tk, 128 if mode != "tn" else 16)
    nk = K // tk
    dims = {"nn": NN, "nt": NT, "tn": TN}[mode]

    def body(*refs):
        if res is None:
            a_ref, b_ref, o_ref = refs[:3]
            r_ref = None
            acc = refs[3] if nk > 1 else None
        else:
            a_ref, b_ref, r_ref, o_ref = refs[:4]
            acc = refs[4] if nk > 1 else None
        def product():
            if exact:
                return lax.dot_general(a_ref[...], b_ref[...], (dims, ((), ())), preferred_element_type=F32, precision=HI)
            return _bdot(a_ref[...], b_ref[...], dims)

        def finish(val):
            if r_ref is not None:
                val = val + r_ref[...]
            o_ref[...] = val.astype(out_dtype)

        if nk == 1:
            finish(product())
        else:
            kk = pl.program_id(2)

            @pl.when(kk == 0)
            def _():
                acc[...] = jnp.zeros_like(acc)

            acc[...] += product()

            @pl.when(kk == nk - 1)
            def _():
                finish(acc[...])

    if mode == "tn":
        a_spec = pl.BlockSpec((tk, tm), lambda i, j, k: (k, i))
    elif a_halves:
        nh = K // 2 // tk
        a_spec = pl.BlockSpec((None, tm, tk), lambda i, j, k: (k // nh, i, k % nh))
    else:
        a_spec = pl.BlockSpec((tm, tk), lambda i, j, k: (i, k))
    if b_blocked:
        b_spec = (pl.BlockSpec((None, tk, tn), lambda i, j, k: (j, k, 0)) if mode == "nn"
                  else pl.BlockSpec((None, tn, tk), lambda i, j, k: (k, j, 0)))
    elif b_halves:
        nh = N // 2 // tn
        b_spec = pl.BlockSpec((None, tk, tn), lambda i, j, k: (j // nh, k, j % nh))
    elif mode == "nt":
        b_spec = pl.BlockSpec((tn, tk), lambda i, j, k: (j, k))
    else:
        b_spec = pl.BlockSpec((tk, tn), lambda i, j, k: (k, j))
    if out_blocked:
        o_spec, o_shape = pl.BlockSpec((None, tm, tn), lambda i, j, k: (j, i, 0)), SDS((N // tn, M, tn), out_dtype)
    else:
        o_spec, o_shape = pl.BlockSpec((tm, tn), lambda i, j, k: (i, j)), SDS((M, N), out_dtype)
    in_specs, args = [a_spec, b_spec], [a, b]
    if res is not None:
        in_specs.append(o_spec)
        args.append(res)
    out, carried = _call_with_rider(
        body, (M // tm, N // tn, nk), in_specs, o_spec, o_shape, [pltpu.VMEM((tm, tn), F32)] if nk > 1 else [],
        ("parallel", "parallel", "arbitrary"), name, args, rider)
    return out if rider is None else (out, carried)


def rms_fwd(x, g, name, rider=None):
    T, D = x.shape
    tm = _fit(T, 512, 8)

    def body(x_ref, g_ref, o_ref):
        xv = x_ref[...]
        r = lax.rsqrt(jnp.mean(xv * xv, axis=-1, keepdims=True) + RMS_EPS)
        o_ref[...] = (xv * r * g_ref[...]).astype(BF16)

    out, carried = _call_with_rider(
        body, (T // tm,), [pl.BlockSpec((tm, D), lambda i: (i, 0)), pl.BlockSpec((1, D), lambda i: (0, 0))],
        pl.BlockSpec((tm, D), lambda i: (i, 0)), SDS((T, D), BF16), [], ("parallel",), name, [x, g], rider)
    return out if rider is None else (out, carried)


def rms_bwd(x, g, dy, dres, name):
    T, D = x.shape
    tm = _fit(T, 256, 8)
    has_res = dres is not None

    def body(*refs):
        if has_res:
            x_ref, g_ref, dy_ref, dres_ref, dx_ref, dxb_ref, dg_ref = refs
        else:
            x_ref, g_ref, dy_ref, dx_ref, dxb_ref, dg_ref = refs
        xv, dyv = x_ref[...], dy_ref[...]
        r = lax.rsqrt(jnp.mean(xv * xv, axis=-1, keepdims=True) + RMS_EPS)
        xh = xv * r
        dxh = dyv * g_ref[...]
        dx = r * (dxh - xh * jnp.mean(dxh * xh, axis=-1, keepdims=True))
        if has_res:
            dx = dx + dres_ref[...]
        dx_ref[...] = dx
        dxb_ref[...] = dx.astype(BF16)

        @pl.when(pl.program_id(0) == 0)
        def _():
            dg_ref[...] = jnp.zeros_like(dg_ref)

        dg_ref[...] += jnp.sum(dyv * xh, axis=0, keepdims=True)

    row = pl.BlockSpec((tm, D), lambda i: (i, 0))
    vec = pl.BlockSpec((1, D), lambda i: (0, 0))
    args = [x, g, dy] + ([dres] if has_res else [])
    return pl.pallas_call(
        body, grid=(T // tm,), in_specs=[row, vec, row] + ([row] if has_res else []),
        out_specs=[row, row, vec], out_shape=[SDS((T, D), F32), SDS((T, D), BF16), SDS((1, D), F32)],
        compiler_params=_cparams(("arbitrary",)), name=name)(*args)


def ffn_up_swiglu(h, w13b, name, rider=None):
    T, D = h.shape
    nb, _, tb = w13b.shape
    nh = nb // 2
    tm = _fit(T, 512, 8)

    def body(h_ref, wg_ref, wu_ref, gu_ref, a_ref):
        hv = h_ref[...]
        gv = _bdot(hv, wg_ref[...], NN)
        uv = _bdot(hv, wu_ref[...], NN)
        gu_ref[0] = gv.astype(BF16)
        gu_ref[1] = uv.astype(BF16)
        a_ref[...] = (gv * jax.nn.sigmoid(gv) * uv).astype(BF16)

    (gu, act), carried = _call_with_rider(
        body, (nh, T // tm),
        [pl.BlockSpec((tm, D), lambda j, i: (i, 0)), pl.BlockSpec((None, D, tb), lambda j, i: (j, 0, 0)),
         pl.BlockSpec((None, D, tb), lambda j, i: (j + nh, 0, 0))],
        [pl.BlockSpec((2, tm, tb), lambda j, i: (0, i, j)), pl.BlockSpec((tm, tb), lambda j, i: (i, j))],
        [SDS((2, T, nh * tb), BF16), SDS((T, nh * tb), BF16)], [], ("parallel", "parallel"), name, [h, w13b, w13b], rider)
    return gu, act, carried


def ffn_down_bwd(dy_b, w2, gu, name):
    T, D = dy_b.shape
    Fh = w2.shape[0]
    tm, tf = _fit(T, 1024, 8), _fit(Fh, 512)
    nf = Fh // tf

    def body(dy_ref, w_ref, g_ref, u_ref, o_ref):
        dav = _bdot(dy_ref[...], w_ref[...], NT)
        gv, uv = g_ref[...].astype(F32), u_ref[...].astype(F32)
        sg = jax.nn.sigmoid(gv)
        o_ref[0] = (dav * uv * (sg * (1.0 + gv * (1.0 - sg)))).astype(BF16)
        o_ref[1] = (dav * gv * sg).astype(BF16)

    return pl.pallas_call(
        body, grid=(T // tm, nf),
        in_specs=[pl.BlockSpec((tm, D), lambda i, j: (i, 0)), pl.BlockSpec((tf, D), lambda i, j: (j, 0)),
                  pl.BlockSpec((None, tm, tf), lambda i, j: (0, i, j)), pl.BlockSpec((None, tm, tf), lambda i, j: (1, i, j))],
        out_specs=pl.BlockSpec((2, tm, tf), lambda i, j: (0, i, j)), out_shape=SDS((2, T, Fh), BF16),
        compiler_params=_cparams(("parallel", "parallel")), name=name)(dy_b, w2, gu, gu)


def loss_bwd(y, tgt, name):
    T, D = y.shape
    tm = _fit(T, 512, 8)

    def body(y_ref, t_ref, dy_ref, dyb_ref, l_ref):
        e = y_ref[...] - t_ref[...]
        dy = e * (1.0 / D)
        dy_ref[...] = dy
        dyb_ref[...] = dy.astype(BF16)

        @pl.when(pl.program_id(0) == 0)
        def _():
            l_ref[...] = jnp.zeros_like(l_ref)

        l_ref[...] += jnp.sum(jnp.sum(e * e, axis=1, keepdims=True), axis=0, keepdims=True)

    row = pl.BlockSpec((tm, D), lambda i: (i, 0))
    return pl.pallas_call(
        body, grid=(T // tm,), in_specs=[row, row], out_specs=[row, row, pl.BlockSpec((8, 128), lambda i: (0, 0))],
        out_shape=[SDS((T, D), F32), SDS((T, D), BF16), SDS((8, 128), F32)],
        compiler_params=_cparams(("arbitrary",)), name=name)(y, tgt)


def _log_sigmoid(z):
    return jnp.minimum(z, 0.0) - jnp.log(1.0 + jnp.exp(-jnp.abs(z)))


def _gla_group(q, k, v, tail, wg, bg, s, cmat, mask, backward_dir):
    n = q.shape[0] // GLA_C
    z = _bdot_vjp(tail, wg, NN) + bg
    la = _log_sigmoid(z) * (1.0 / GLA_TAU)
    cum = _tri_sum(cmat, la)
    chunks = [slice(g * GLA_C, (g + 1) * GLA_C) for g in range(n)]
    last = [jnp.sum(la[sl], axis=0, keepdims=True) for sl in chunks]
    last_rows = jnp.concatenate([jnp.broadcast_to(t, (GLA_C, t.shape[1])) for t in last], axis=0)
    q_e = q * jnp.exp(cum) * (GLA_HK ** -0.5)
    k_e = k * jnp.exp(-cum)
    k_end = k * jnp.exp(last_rows - cum)
    sc = jnp.where(mask > 0.5, _bdot_vjp(q_e, k_e, NT), 0.0)
    o = _bdot_vjp(sc, v, NN)
    o_inter = [None] * n
    for g in (range(n - 1, -1, -1) if backward_dir else range(n)):
        o_inter[g] = _bdot_vjp(q_e[chunks[g]], s, NN)
        s = s * jnp.transpose(jnp.exp(last[g])) + _bdot_vjp(k_end[chunks[g]], v[chunks[g]], TN)
    return o + jnp.concatenate(o_inter, axis=0), s


def _gla_consts(backward_dir, GC):
    i = np.arange(GC)
    same = (i[:, None] // GLA_C) == (i[None, :] // GLA_C)
    if backward_dir:
        return (same & (i[None, :] >= i[:, None])).astype(np.float32), (same & (i[None, :] > i[:, None])).astype(np.float32)
    tri = (same & (i[None, :] <= i[:, None])).astype(np.float32)
    return tri, tri


GLA_HB = 2


def _gla_in_specs(GC, nmap):
    return [
        pl.BlockSpec((GC, GLA_HB * GLA_HK), lambda h, n: (nmap(n), h)),
        pl.BlockSpec((GC, GLA_HB * GLA_HK), lambda h, n: (nmap(n), GLA_H // GLA_HB + h)),
        pl.BlockSpec((GC, GLA_HB * GLA_HV), lambda h, n: (nmap(n), 2 * GLA_DK // (GLA_HB * GLA_HV) + h)),
        pl.BlockSpec((GC, 128), lambda h, n: (nmap(n), COL_TAIL // 128)),
        pl.BlockSpec((GLA_HB, 128, 128), lambda h, n: (h, 0, 0)),
        pl.BlockSpec((GLA_HB, 1, 128), lambda h, n: (h, 0, 0)),
        pl.BlockSpec((GC, GC), lambda h, n: (0, 0)),
        pl.BlockSpec((GC, GC), lambda h, n: (0, 0)),
    ]


def _head_cols(ref, hh, width):
    return ref[:, hh * width:(hh + 1) * width].astype(F32)


def gla_fwd(proj, wgpad, bg, backward_dir, name, rider=None):
    T = proj.shape[0]
    GC = min(GLA_G * GLA_C, T)
    NG = T // GC
    cmat, mask = _gla_consts(backward_dir, GC)
    nmap = (lambda n: NG - 1 - n) if backward_dir else (lambda n: n)

    def body(q_ref, k_ref, v_ref, t_ref, wg_ref, bg_ref, c_ref, m_ref, o_ref, ss_ref, s_scr):
        @pl.when(pl.program_id(1) == 0)
        def _():
            s_scr[...] = jnp.zeros_like(s_scr)

        tail, cm, mk = t_ref[...].astype(F32), c_ref[...], m_ref[...]
        for hh in range(GLA_HB):
            s0 = s_scr[hh]
            ss_ref[hh] = s0
            o, s1 = _gla_group(_head_cols(q_ref, hh, GLA_HK), _head_cols(k_ref, hh, GLA_HK), _head_cols(v_ref, hh, GLA_HV),
                               tail, wg_ref[hh], bg_ref[hh], s0, cm, mk, backward_dir)
            o_ref[:, hh * GLA_HV:(hh + 1) * GLA_HV] = o
            s_scr[hh] = s1

    (o, states), carried = _call_with_rider(
        body, (GLA_H // GLA_HB, NG), _gla_in_specs(GC, nmap),
        [pl.BlockSpec((GC, GLA_HB * GLA_HV), lambda h, n: (nmap(n), h)),
         pl.BlockSpec((GLA_HB, None, GLA_HK, GLA_HV), lambda h, n: (h, nmap(n), 0, 0))],
        [SDS((T, GLA_DV), F32), SDS((GLA_H, NG, GLA_HK, GLA_HV), F32)], [pltpu.VMEM((GLA_HB, GLA_HK, GLA_HV), F32)],
        ("parallel", "arbitrary"), name, [proj, proj, proj, proj, wgpad, bg, cmat, mask], rider)
    return (o, states) if rider is None else (o, states, carried)


def gla_bwd(proj, wgpad, bg, ssave, do, backward_dir, name):
    T = proj.shape[0]
    GC = min(GLA_G * GLA_C, T)
    NG = T // GC
    cmat, mask = _gla_consts(backward_dir, GC)
    nmap = (lambda n: n) if backward_dir else (lambda n: NG - 1 - n)

    def body(q_ref, k_ref, v_ref, t_ref, wg_ref, bg_ref, c_ref, m_ref, ss_ref, do_ref,
             dq_ref, dk_ref, dv_ref, dt_ref, dwg_ref, dbg_ref, ds_scr):
        @pl.when(pl.program_id(1) == 0)
        def _():
            ds_scr[...] = jnp.zeros_like(ds_scr)
            dwg_ref[...] = jnp.zeros_like(dwg_ref)
            dbg_ref[...] = jnp.zeros_like(dbg_ref)

        tail, cm, mk = t_ref[...].astype(F32), c_ref[...], m_ref[...]
        fn = lambda q, k, v, t, wg, b, s: _gla_group(q, k, v, t, wg, b, s, cm, mk, backward_dir)
        for hh in range(GLA_HB):
            _, vjp = jax.vjp(fn, _head_cols(q_ref, hh, GLA_HK), _head_cols(k_ref, hh, GLA_HK), _head_cols(v_ref, hh, GLA_HV),
                             tail, wg_ref[hh], bg_ref[hh], ss_ref[hh])
            dq, dk, dv, dt, dwg, dbg, ds = vjp((_head_cols(do_ref, hh, GLA_HV), ds_scr[hh]))
            dq_ref[:, hh * GLA_HK:(hh + 1) * GLA_HK] = dq.astype(BF16)
            dk_ref[:, hh * GLA_HK:(hh + 1) * GLA_HK] = dk.astype(BF16)
            dv_ref[:, hh * GLA_HV:(hh + 1) * GLA_HV] = dv.astype(BF16)
            dt_ref[hh] = dt.astype(BF16)
            dwg_ref[hh] += dwg
            dbg_ref[hh] += dbg
            ds_scr[hh] = ds

    in_specs = _gla_in_specs(GC, nmap) + [
        pl.BlockSpec((GLA_HB, None, GLA_HK, GLA_HV), lambda h, n: (h, nmap(n), 0, 0)),
        pl.BlockSpec((GC, GLA_HB * GLA_HV), lambda h, n: (nmap(n), h)),
    ]
    out_specs = [
        pl.BlockSpec((GC, GLA_HB * GLA_HK), lambda h, n: (nmap(n), h)),
        pl.BlockSpec((GC, GLA_HB * GLA_HK), lambda h, n: (nmap(n), h)),
        pl.BlockSpec((GC, GLA_HB * GLA_HV), lambda h, n: (nmap(n), h)),
        pl.BlockSpec((GLA_HB, GC, 128), lambda h, n: (h, nmap(n), 0)),
        pl.BlockSpec((GLA_HB, 128, 128), lambda h, n: (h, 0, 0)),
        pl.BlockSpec((GLA_HB, 1, 128), lambda h, n: (h, 0, 0)),
    ]
    out_shape = [SDS((T, GLA_DK), BF16), SDS((T, GLA_DK), BF16), SDS((T, GLA_DV), BF16), SDS((GLA_H, T, 128), BF16),
                 SDS((GLA_H, 128, 128), F32), SDS((GLA_H, 1, 128), F32)]
    return pl.pallas_call(
        body, grid=(GLA_H // GLA_HB, NG), in_specs=in_specs, out_specs=out_specs, out_shape=out_shape,
        scratch_shapes=[pltpu.VMEM((GLA_HB, GLA_HK, GLA_HV), F32)],
        compiler_params=_cparams(("parallel", "arbitrary")), name=name)(proj, proj, proj, proj, wgpad, bg, cmat, mask, ssave, do)


def _block_diag(width, hd):
    i = np.arange(width)
    return ((i[:, None] // hd) == (i[None, :] // hd)).astype(np.float32) / hd


def _norm_heads(t, hd):
    outs = []
    for h in range(t.shape[1] // hd):
        th = t[:, h * hd:(h + 1) * hd]
        outs.append(th * lax.rsqrt(jnp.mean(th * th, axis=-1, keepdims=True) + RMS_EPS))
    return jnp.concatenate(outs, axis=1)


def _split_dot_right(x, c):
    hi = x.astype(BF16)
    lo = (x - hi.astype(F32)).astype(BF16)
    cb = c.astype(BF16)
    return jnp.dot(hi, cb, preferred_element_type=F32) + jnp.dot(lo, cb, preferred_element_type=F32)


@jax.custom_vjp
def _head_means(x, bd):
    return _split_dot_right(x, bd)


def _head_means_fwd(x, bd):
    return _split_dot_right(x, bd), bd


def _head_means_bwd(bd, ct):
    return _split_dot_right(ct, bd), jnp.zeros_like(bd)


_head_means.defvjp(_head_means_fwd, _head_means_bwd)


def _norm_bd(t, bd):
    return t * lax.rsqrt(_head_means(t * t, bd) + RMS_EPS)


def _pre_fn(nq, nk, mq, gq, gk, gm, bd):
    return _norm_bd(nq, bd) * gq, _norm_bd(nk, bd) * gk, _norm_heads(mq, MEM_HD) * gm


def pre_fwd(proj, gq, gk, gm, name):
    T = proj.shape[0]
    tm = _fit(T, 512, 8)
    bd = _block_diag(NA_W, NA_HD)

    def body(nq_ref, nk_ref, nv_ref, mq_ref, gq_ref, gk_ref, gm_ref, bd_ref, q_ref, k_ref, v_ref, m_ref):
        qn, kn, mn = _pre_fn(nq_ref[...].astype(F32), nk_ref[...].astype(F32), mq_ref[...].astype(F32),
                             gq_ref[...], gk_ref[...], gm_ref[...], bd_ref[...])
        q_ref[...] = qn.astype(BF16)
        k_ref[...] = kn.astype(BF16)
        v_ref[...] = nv_ref[...].astype(BF16)
        m_ref[...] = mn.astype(BF16)

    col = lambda c0: pl.BlockSpec((tm, 512), lambda i: (i, c0 // 512))
    vec = pl.BlockSpec((1, 512), lambda i: (0, 0))
    row = pl.BlockSpec((tm, 512), lambda i: (i, 0))
    return pl.pallas_call(
        body, grid=(T // tm,),
        in_specs=[col(COL_NQ), col(COL_NK), col(COL_NV), col(COL_MQ), vec, vec, vec, pl.BlockSpec((NA_W, NA_W), lambda i: (0, 0))],
        out_specs=[row] * 4, out_shape=[SDS((T, 512), BF16)] * 4,
        compiler_params=_cparams(("parallel",)), name=name)(proj, proj, proj, proj, gq, gk, gm, bd)


def pre_bwd(proj, gq, gk, gm, d_qn, d_kn, d_nv, d_mn, dq_f, dq_b, dk_f, dk_b, dv_f, dv_b, d_r, dt_f, dt_b, name):
    T = proj.shape[0]
    tm = _fit(T, 256, 8)
    bd = _block_diag(NA_W, NA_HD)

    def body(nq_ref, nk_ref, mq_ref, gq_ref, gk_ref, gm_ref, bd_ref, dqn_ref, dkn_ref, dnv_ref, dmn_ref,
             dqf_ref, dqb_ref, dkf_ref, dkb_ref, dvf_ref, dvb_ref, dr_ref, dtf_ref, dtb_ref,
             o_ref, dgq_ref, dgk_ref, dgm_ref):
        bdv = bd_ref[...]
        fn = lambda a, b, c, x, y, z: _pre_fn(a, b, c, x, y, z, bdv)
        _, vjp = jax.vjp(fn, nq_ref[...].astype(F32), nk_ref[...].astype(F32), mq_ref[...].astype(F32),
                         gq_ref[...], gk_ref[...], gm_ref[...])
        d_nq, d_nk, d_mq, dgq, dgk, dgm = vjp((dqn_ref[...], dkn_ref[...], dmn_ref[...]))
        both = lambda f_ref, b_ref: (f_ref[...].astype(F32) + b_ref[...].astype(F32)).astype(BF16)
        o_ref[:, 0:512] = both(dqf_ref, dqb_ref)
        o_ref[:, 512:1024] = both(dkf_ref, dkb_ref)
        o_ref[:, 1024:2048] = both(dvf_ref, dvb_ref)
        o_ref[:, COL_R:COL_R + 1024] = dr_ref[...].astype(BF16)
        o_ref[:, COL_NQ:COL_NQ + 512] = d_nq.astype(BF16)
        o_ref[:, COL_NK:COL_NK + 512] = d_nk.astype(BF16)
        o_ref[:, COL_NV:COL_NV + 512] = dnv_ref[...].astype(BF16)
        o_ref[:, COL_MQ:COL_MQ + 512] = d_mq.astype(BF16)
        dt = dtf_ref[0].astype(F32) + dtb_ref[0].astype(F32)
        for h in range(1, GLA_H):
            dt = dt + dtf_ref[h].astype(F32) + dtb_ref[h].astype(F32)
        o_ref[:, COL_TAIL:COL_TAIL + 128] = dt.astype(BF16)
        o_ref[:, COL_TAIL + 128:PC] = jnp.zeros((tm, PC - COL_TAIL - 128), BF16)

        @pl.when(pl.program_id(0) == 0)
        def _():
            dgq_ref[...] = jnp.zeros_like(dgq_ref)
            dgk_ref[...] = jnp.zeros_like(dgk_ref)
            dgm_ref[...] = jnp.zeros_like(dgm_ref)

        dgq_ref[...] += dgq
        dgk_ref[...] += dgk
        dgm_ref[...] += dgm

    col = lambda c0: pl.BlockSpec((tm, 512), lambda i: (i, c0 // 512))
    vec = pl.BlockSpec((1, 512), lambda i: (0, 0))
    r512 = pl.BlockSpec((tm, 512), lambda i: (i, 0))
    r1024 = pl.BlockSpec((tm, 1024), lambda i: (i, 0))
    tl = pl.BlockSpec((GLA_H, tm, 128), lambda i: (0, i, 0))
    in_specs = [col(COL_NQ), col(COL_NK), col(COL_MQ), vec, vec, vec, pl.BlockSpec((NA_W, NA_W), lambda i: (0, 0)),
                r512, r512, r512, r512, r512, r512, r512, r512, r1024, r1024, r1024, tl, tl]
    return pl.pallas_call(
        body, grid=(T // tm,), in_specs=in_specs,
        out_specs=[pl.BlockSpec((tm, PC), lambda i: (i, 0)), vec, vec, vec],
        out_shape=[SDS((T, PC), BF16), SDS((1, 512), F32), SDS((1, 512), F32), SDS((1, 512), F32)],
        compiler_params=_cparams(("arbitrary",)), name=name)(
            proj, proj, proj, gq, gk, gm, bd, d_qn, d_kn, d_nv, d_mn, dq_f, dq_b, dk_f, dk_b, dv_f, dv_b, d_r, dt_f, dt_b)


def _post_fn(o_f, o_b, r, o_na, o_mem, g_gla, g_na, g_mem, bd):
    y_gla = _norm_heads(o_f + o_b, GLA_HV) * g_gla * (r * jax.nn.sigmoid(r))
    y_na = _norm_bd(o_na, bd) * g_na
    y_mem = _norm_heads(o_mem, MEM_HD) * g_mem
    return jnp.concatenate([y_gla, y_na, y_mem], axis=1)


def post_fwd(o_f, o_b, proj, o_na, o_mem, g_gla, g_na, g_mem, name):
    T = proj.shape[0]
    tm = _fit(T, 256, 8)
    bd = _block_diag(NA_W, NA_HD)

    def body(of_ref, ob_ref, r_ref, ona_ref, omem_ref, gg_ref, gn_ref, gm_ref, bd_ref, y_ref):
        y_ref[...] = _post_fn(of_ref[...], ob_ref[...], r_ref[...].astype(F32), ona_ref[...], omem_ref[...],
                              gg_ref[...], gn_ref[...], gm_ref[...], bd_ref[...]).astype(BF16)

    r1024 = pl.BlockSpec((tm, 1024), lambda i: (i, 0))
    r512 = pl.BlockSpec((tm, 512), lambda i: (i, 0))
    in_specs = [r1024, r1024, pl.BlockSpec((tm, 1024), lambda i: (i, COL_R // 1024)), r512, r512,
                pl.BlockSpec((1, 1024), lambda i: (0, 0)), pl.BlockSpec((1, 512), lambda i: (0, 0)),
                pl.BlockSpec((1, 512), lambda i: (0, 0)), pl.BlockSpec((NA_W, NA_W), lambda i: (0, 0))]
    return pl.pallas_call(
        body, grid=(T // tm,), in_specs=in_specs, out_specs=pl.BlockSpec((tm, D_MODEL), lambda i: (i, 0)),
        out_shape=SDS((T, D_MODEL), BF16), compiler_params=_cparams(("parallel",)), name=name)(
            o_f, o_b, proj, o_na, o_mem, g_gla, g_na, g_mem, bd)


def post_bwd(o_f, o_b, proj, o_na, o_mem, g_gla, g_na, g_mem, dy, name):
    T = proj.shape[0]
    tm = _fit(T, 256, 8)
    bd = _block_diag(NA_W, NA_HD)

    def body(of_ref, ob_ref, r_ref, ona_ref, omem_ref, gg_ref, gn_ref, gm_ref, bd_ref, dy_ref,
             do_ref, dr_ref, dna_ref, dmem_ref, dgg_ref, dgn_ref, dgm_ref):
        bdv = bd_ref[...]
        fn = lambda o, r, a, m, x, y, z: _post_fn(o, 0.0, r, a, m, x, y, z, bdv)
        _, vjp = jax.vjp(fn, of_ref[...] + ob_ref[...], r_ref[...].astype(F32), ona_ref[...], omem_ref[...],
                         gg_ref[...], gn_ref[...], gm_ref[...])
        d_o, d_r, d_na, d_mem, dgg, dgn, dgm = vjp(dy_ref[...])
        do_ref[...] = d_o.astype(BF16)
        dr_ref[...] = d_r.astype(BF16)
        dna_ref[...] = d_na.astype(BF16)
        dmem_ref[...] = d_mem.astype(BF16)

        @pl.when(pl.program_id(0) == 0)
        def _():
            dgg_ref[...] = jnp.zeros_like(dgg_ref)
            dgn_ref[...] = jnp.zeros_like(dgn_ref)
            dgm_ref[...] = jnp.zeros_like(dgm_ref)

        dgg_ref[...] += dgg
        dgn_ref[...] += dgn
        dgm_ref[...] += dgm

    r1024 = pl.BlockSpec((tm, 1024), lambda i: (i, 0))
    r512 = pl.BlockSpec((tm, 512), lambda i: (i, 0))
    v1024 = pl.BlockSpec((1, 1024), lambda i: (0, 0))
    v512 = pl.BlockSpec((1, 512), lambda i: (0, 0))
    in_specs = [r1024, r1024, pl.BlockSpec((tm, 1024), lambda i: (i, COL_R // 1024)), r512, r512, v1024, v512, v512,
                pl.BlockSpec((NA_W, NA_W), lambda i: (0, 0)), pl.BlockSpec((tm, D_MODEL), lambda i: (i, 0))]
    return pl.pallas_call(
        body, grid=(T // tm,), in_specs=in_specs, out_specs=[r1024, r1024, r512, r512, v1024, v512, v512],
        out_shape=[SDS((T, 1024), BF16), SDS((T, 1024), BF16), SDS((T, 512), BF16), SDS((T, 512), BF16),
                   SDS((1, 1024), F32), SDS((1, 512), F32), SDS((1, 512), F32)],
        compiler_params=_cparams(("arbitrary",)), name=name)(o_f, o_b, proj, o_na, o_mem, g_gla, g_na, g_mem, bd, dy)


NA_RB = 8


def _na_row_scores(q_ref, k_ref, v_ref, tb_ref, rb, j, n_rows):
    r = rb * NA_RB + j
    rs = jnp.clip(r - NA_ROWS // 2, 0, n_rows - NA_ROWS)
    dr0 = rs - r + (NA_ROWS - 1)
    tok = pl.ds(pl.multiple_of(rs * GRID_W, GRID_W), NA_ROWS * GRID_W)
    q = q_ref[j * GRID_W:(j + 1) * GRID_W, :]
    kk, vv = k_ref[tok, :], v_ref[tok, :]
    bias = jnp.concatenate([tb_ref[dr0 + 2 * i] for i in range(NA_ROWS // 2)], axis=1)
    s = _bdot(q, kk, NT) * (NA_HD ** -0.5) + bias
    m = jnp.max(s, axis=1, keepdims=True)
    p = jnp.exp(s - m)
    l = jnp.sum(p, axis=1, keepdims=True)
    return q, kk, vv, p, l, tok, dr0


def natten_fwd(q, k, v, tb2, name):
    H, T, hd = q.shape
    n_rows = T // GRID_W
    rbt = NA_RB * GRID_W

    def body(q_ref, k_ref, v_ref, tb_ref, o_ref):
        rb = pl.program_id(1)
        for j in range(NA_RB):
            _, _, vv, p, l, _, _ = _na_row_scores(q_ref, k_ref, v_ref, tb_ref, rb, j, n_rows)
            o_ref[j * GRID_W:(j + 1) * GRID_W, :] = _bdot(p, vv, NN) / l

    whole = pl.BlockSpec((None, T, hd), lambda h, r: (h, 0, 0))
    blk = pl.BlockSpec((None, rbt, hd), lambda h, r: (h, r, 0))
    return pl.pallas_call(
        body, grid=(H, n_rows // NA_RB),
        in_specs=[blk, whole, whole, pl.BlockSpec((None, 2 * NA_ROWS - 2, GRID_W, 2 * GRID_W), lambda h, r: (h, 0, 0, 0))],
        out_specs=blk, out_shape=SDS((H, T, hd), F32),
        compiler_params=_cparams(("parallel", "arbitrary")), name=name)(q, k, v, tb2)


def natten_bwd(q, k, v, tb2, do, name):
    H, T, hd = q.shape
    n_rows = T // GRID_W
    rbt = NA_RB * GRID_W
    scale = NA_HD ** -0.5

    def body(q_ref, k_ref, v_ref, tb_ref, do_ref, dq_ref, dk_ref, dv_ref, dtb_ref):
        rb = pl.program_id(1)

        @pl.when(rb == 0)
        def _():
            dk_ref[...] = jnp.zeros_like(dk_ref)
            dv_ref[...] = jnp.zeros_like(dv_ref)
            dtb_ref[...] = jnp.zeros_like(dtb_ref)

        for j in range(NA_RB):
            qv, kk, vv, p, l, tok, dr0 = _na_row_scores(q_ref, k_ref, v_ref, tb_ref, rb, j, n_rows)
            p = p / l
            dov = do_ref[j * GRID_W:(j + 1) * GRID_W, :]
            dp = _bdot(dov, vv, NT)
            ds = p * (dp - jnp.sum(dp * p, axis=1, keepdims=True))
            dq_ref[j * GRID_W:(j + 1) * GRID_W, :] = _bdot(ds, kk, NN) * scale
            dk_ref[tok, :] += _bdot(ds, qv, TN) * scale
            dv_ref[tok, :] += _bdot(p, dov, TN)
            for i in range(NA_ROWS // 2):
                dtb_ref[dr0 + 2 * i] += ds[:, 2 * GRID_W * i:2 * GRID_W * (i + 1)]

    whole = pl.BlockSpec((None, T, hd), lambda h, r: (h, 0, 0))
    blk = pl.BlockSpec((None, rbt, hd), lambda h, r: (h, r, 0))
    tbs = pl.BlockSpec((None, 2 * NA_ROWS - 2, GRID_W, 2 * GRID_W), lambda h, r: (h, 0, 0, 0))
    return pl.pallas_call(
        body, grid=(H, n_rows // NA_RB), in_specs=[blk, whole, whole, tbs, blk],
        out_specs=[blk, whole, whole, tbs],
        out_shape=[SDS((H, T, hd), F32), SDS((H, T, hd), F32), SDS((H, T, hd), F32), SDS(tb2.shape, F32)],
        compiler_params=_cparams(("parallel", "arbitrary")), name=name)(q, k, v, tb2, do)


def _rpb_expand_consts():
    qc = np.arange(GRID_W)[:, None]
    kc = np.arange(GRID_W)[None, :]
    cs = np.clip(qc - NA_COLS // 2, 0, GRID_W - NA_COLS)
    inside = (kc >= cs) & (kc < cs + NA_COLS)
    dc = np.clip(kc - qc, -(NA_COLS - 1), NA_COLS - 1) + (NA_COLS - 1)
    e = np.zeros((128, GRID_W * GRID_W), np.float32)
    flat = (qc * GRID_W + kc)
    e[dc[inside], flat[inside]] = 1.0
    neg = np.where(inside, 0.0, NEG).astype(np.float32).reshape(1, -1)
    return e, neg


def _rpb_fold_consts():
    sa = np.zeros((NA_H * 15, NA_H * 14), np.float32)
    sb = np.zeros((NA_H * 15, NA_H * 14), np.float32)
    for h in range(NA_H):
        for d in range(14):
            sa[h * 15 + d, h * 14 + d] = 1.0
            sb[h * 15 + d + 1, h * 14 + d] = 1.0
    return sa, sb


def rpb_table(rpb, name):
    e, neg = _rpb_expand_consts()
    rp = jnp.pad(rpb.reshape(NA_H * 15, 31), ((0, 0), (0, 128 - 31)))

    def body(r_ref, e_ref, n_ref, o_ref):
        o_ref[...] = jnp.dot(r_ref[...], e_ref[...], preferred_element_type=F32, precision=HI) + n_ref[...]

    t = pl.pallas_call(body, out_shape=SDS((NA_H * 15, GRID_W * GRID_W), F32), name=name)(rp, e, neg)
    t = t.reshape(NA_H, 15, GRID_W, GRID_W)
    return jnp.concatenate([t[:, :14], t[:, 1:]], axis=-1)


def rpb_table_bwd(dtb2, name):
    e, _ = _rpb_expand_consts()
    sa, sb = _rpb_fold_consts()
    a = dtb2[..., :GRID_W].reshape(NA_H * 14, GRID_W * GRID_W)
    b = dtb2[..., GRID_W:].reshape(NA_H * 14, GRID_W * GRID_W)

    def body(a_ref, b_ref, e_ref, sa_ref, sb_ref, o_ref):
        ev = e_ref[...]
        pa = lax.dot_general(a_ref[...], ev, (NT, ((), ())), preferred_element_type=F32, precision=HI)
        pb = lax.dot_general(b_ref[...], ev, (NT, ((), ())), preferred_element_type=F32, precision=HI)
        o_ref[...] = (jnp.dot(sa_ref[...], pa, preferred_element_type=F32, precision=HI)
                      + jnp.dot(sb_ref[...], pb, preferred_element_type=F32, precision=HI))

    d = pl.pallas_call(body, out_shape=SDS((NA_H * 15, 128), F32), name=name)(a, b, e, sa, sb)
    return d[:, :31].reshape(NA_H, 15, 31)


def _kprep_fn(kv, gk):
    return _norm_heads(kv[:, :MEM_W], MEM_HD) * gk, kv[:, MEM_W:]


def mem_kprep(kv, gk, name):
    def body(kv_ref, g_ref, k_ref, v_ref):
        kn, vv = _kprep_fn(kv_ref[...], g_ref[...])
        k_ref[...] = kn.astype(BF16)
        v_ref[...] = vv.astype(BF16)

    return pl.pallas_call(body, out_shape=[SDS((MEM_TOK, MEM_W), BF16)] * 2, name=name)(kv, gk)


def mem_kprep_bwd(kv, gk, dk, dv, name):
    def body(kv_ref, g_ref, dk_ref, dv_ref, dkv_ref, dg_ref):
        _, vjp = jax.vjp(_kprep_fn, kv_ref[...], g_ref[...])
        dkv, dg = vjp((dk_ref[...], dv_ref[...]))
        dkv_ref[...] = dkv.astype(BF16)
        dg_ref[...] = dg

    return pl.pallas_call(body, out_shape=[SDS((MEM_TOK, 2 * MEM_W), BF16), SDS((1, MEM_W), F32)], name=name)(kv, gk, dk, dv)


def _mem_probs(q_ref, k_ref, h):
    hs = slice(h * MEM_HD, (h + 1) * MEM_HD)
    qh, kh = q_ref[:, hs], k_ref[:, hs]
    s = _bdot(qh, kh, NT) * (MEM_HD ** -0.5)
    p = jnp.exp(s - jnp.max(s, axis=1, keepdims=True))
    return hs, qh, kh, p, jnp.sum(p, axis=1, keepdims=True)


def mem_attn_fwd(q, km, vm, name):
    T = q.shape[0]
    tm = _fit(T, 512, 8)

    def body(q_ref, k_ref, v_ref, o_ref):
        for h in range(MEM_H):
            hs, _, _, p, l = _mem_probs(q_ref, k_ref, h)
            o_ref[:, hs] = _bdot(p, v_ref[:, hs], NN) / l

    row = pl.BlockSpec((tm, MEM_W), lambda i: (i, 0))
    full = pl.BlockSpec((MEM_TOK, MEM_W), lambda i: (0, 0))
    return pl.pallas_call(body, grid=(T // tm,), in_specs=[row, full, full], out_specs=row, out_shape=SDS((T, MEM_W), F32),
                          compiler_params=_cparams(("parallel",)), name=name)(q, km, vm)


def mem_attn_bwd(q, km, vm, do, name):
    T = q.shape[0]
    tm = _fit(T, 512, 8)
    scale = MEM_HD ** -0.5

    def body(q_ref, k_ref, v_ref, do_ref, dq_ref, dk_ref, dv_ref):
        @pl.when(pl.program_id(0) == 0)
        def _():
            dk_ref[...] = jnp.zeros_like(dk_ref)
            dv_ref[...] = jnp.zeros_like(dv_ref)

        for h in range(MEM_H):
            hs, qh, kh, p, l = _mem_probs(q_ref, k_ref, h)
            p = p / l
            dov = do_ref[:, hs]
            dp = _bdot(dov, v_ref[:, hs], NT)
            ds = p * (dp - jnp.sum(dp * p, axis=1, keepdims=True))
            dq_ref[:, hs] = _bdot(ds, kh, NN) * scale
            dk_ref[:, hs] += _bdot(ds, qh, TN) * scale
            dv_ref[:, hs] += _bdot(p, dov, TN)

    row = pl.BlockSpec((tm, MEM_W), lambda i: (i, 0))
    full = pl.BlockSpec((MEM_TOK, MEM_W), lambda i: (0, 0))
    return pl.pallas_call(
        body, grid=(T // tm,), in_specs=[row, full, full, row], out_specs=[row, full, full],
        out_shape=[SDS((T, MEM_W), F32), SDS((MEM_TOK, MEM_W), F32), SDS((MEM_TOK, MEM_W), F32)],
        compiler_params=_cparams(("arbitrary",)), name=name)(q, km, vm, do)


SLOT_BLOCK_ELEMS = 512 * 1024


def _slot_rows(R, Cc):
    return _fit(R, max(16, SLOT_BLOCK_ELEMS // Cc // 16 * 16), 16)


def sum_slots(a, name):
    n, R, Cc = a.shape
    tr = _slot_rows(R, Cc)

    def body(a_ref, o_ref):
        s = a_ref[0].astype(F32)
        for i in range(1, n):
            s = s + a_ref[i].astype(F32)
        o_ref[...] = s

    return pl.pallas_call(body, grid=(R // tr,), in_specs=[pl.BlockSpec((n, tr, Cc), lambda i: (0, i, 0))],
                          out_specs=pl.BlockSpec((tr, Cc), lambda i: (i, 0)), out_shape=SDS((R, Cc), F32),
                          compiler_params=_cparams(("parallel",)), name=name)(a)


def pair_sum(g, land, c_idx, name):
    _, R, Cc = g.shape
    tr = _slot_rows(R, Cc)

    def body(c_ref, g_ref, l_ref, o_ref):
        o_ref[...] = (g_ref[...].astype(F32) + l_ref[...].astype(F32)).astype(o_ref.dtype)

    blk = pl.BlockSpec((None, tr, Cc), lambda k, i, c: (k, i, 0))
    gs = pltpu.PrefetchScalarGridSpec(
        num_scalar_prefetch=1, grid=(4, R // tr),
        in_specs=[pl.BlockSpec((None, tr, Cc), lambda k, i, c: (2 * k + c[0], i, 0)), blk], out_specs=blk)
    return pl.pallas_call(body, grid_spec=gs, out_shape=SDS((4, R, Cc), g.dtype),
                          compiler_params=_cparams(("parallel", "parallel")), name=name)(c_idx, g, land)


def adamw(w, g, m, v, name):
    R, Cc = w.shape
    tr = _fit(R, max(8, (262144 // max(Cc, 128)) // 8 * 8), 8)
    c1 = 1.0 - ADAM_B1 ** ADAM_STEP
    c2 = 1.0 - ADAM_B2 ** ADAM_STEP

    def body(w_ref, g_ref, m_ref, v_ref, d_ref, mo_ref, vo_ref):
        gv = g_ref[...]
        m2 = ADAM_B1 * m_ref[...] + (1.0 - ADAM_B1) * gv
        v2 = ADAM_B2 * v_ref[...] + (1.0 - ADAM_B2) * (gv * gv)
        d_ref[...] = -ADAM_LR * ((m2 / c1) / (jnp.sqrt(v2 / c2) + ADAM_EPS) + ADAM_WD * w_ref[...])
        mo_ref[...] = m2
        vo_ref[...] = v2

    blk = pl.BlockSpec((tr, Cc), lambda i: (i, 0))
    return pl.pallas_call(body, grid=(R // tr,), in_specs=[blk] * 4, out_specs=[blk] * 3, out_shape=[SDS((R, Cc), F32)] * 3,
                          compiler_params=_cparams(("parallel",)), name=name)(w, g, m, v)


ANY = pl.BlockSpec(memory_space=pl.ANY)


def _my_pos():
    return lax.axis_index("x"), lax.axis_index("y"), lax.axis_index("c")


class Rider:
    def __init__(self, inputs, out_shapes, scratch, start, finish, aliases=None):
        self.inputs, self.out_shapes, self.scratch = list(inputs), list(out_shapes), list(scratch)
        self.start, self.finish, self.aliases = start, finish, dict(aliases or {})


def combine_riders(r1, r2):
    ni, no, ns = len(r1.inputs), len(r1.out_shapes), len(r1.scratch)

    def both(f1, f2):
        def run(i, o, s):
            f1(i[:ni], o[:no], s[:ns])
            f2(i[ni:], o[no:], s[ns:])
        return run

    aliases = {**r1.aliases, **{ni + i: no + j for i, j in r2.aliases.items()}}
    return Rider(r1.inputs + r2.inputs, r1.out_shapes + r2.out_shapes, r1.scratch + r2.scratch,
                 both(r1.start, r2.start), both(r1.finish, r2.finish), aliases)


def _gather_phases(x_refs, out_refs, sems):
    send_sems, recv_sems, local_sems = sems
    n = len(out_refs)
    x, y, c = _my_pos()
    me, sibling = (x, y, c), (x, y, 1 - c)
    chips = [(1 - x, y), (x, 1 - y), (1 - x, 1 - y)]

    def slot(a, px, py, pc):
        return out_refs[a].at[4 * px + 2 * py + pc]

    def copy(a, k, block, to, src=None):
        return pltpu.make_async_remote_copy(
            src_ref=slot(a, *block) if src is None else src, dst_ref=slot(a, *block),
            send_sem=send_sems.at[7 * a + k], recv_sem=recv_sems.at[7 * a + k], device_id=to, device_id_type=MESH)

    def mine(a):
        return pltpu.make_async_copy(x_refs[a], slot(a, *me), local_sems.at[a])

    def spread(a):
        return [copy(a, 0, me, sibling, src=x_refs[a])] + [copy(a, 1 + j, me, (*chip, c), src=x_refs[a])
                                                           for j, chip in enumerate(chips)]

    def spread_start():
        for a in range(n):
            mine(a).start()
        for a in range(n):
            for cp in spread(a):
                cp.start()

    def spread_finish():
        for j, chip in enumerate(chips):
            for a in range(n):
                copy(a, 1 + j, (*chip, c), me).wait_recv()
        for a in range(n):
            copy(a, 0, sibling, me).wait_recv()
        for a in range(n):
            for cp in spread(a):
                cp.wait_send()
            mine(a).wait()

    def forward_start():
        for j, chip in enumerate(chips):
            for a in range(n):
                copy(a, 4 + j, (*chip, c), sibling).start()

    def forward_finish():
        for j, chip in enumerate(chips):
            for a in range(n):
                copy(a, 4 + j, (*chip, 1 - c), me).wait_recv()
        for j, chip in enumerate(chips):
            for a in range(n):
                copy(a, 4 + j, (*chip, c), sibling).wait_send()

    return spread_start, spread_finish, forward_start, forward_finish


def _gather_scratch(n):
    return [pltpu.SemaphoreType.DMA((7 * n,)), pltpu.SemaphoreType.DMA((7 * n,)), pltpu.SemaphoreType.DMA((n,))]


def all_gather(shards, name):
    n = len(shards)

    def body(*refs):
        phases = _gather_phases(refs[:n], refs[n:2 * n], refs[2 * n:])
        for phase in phases:
            phase()

    return pl.pallas_call(
        body, out_shape=[SDS((N_DEV,) + s.shape, s.dtype) for s in shards], in_specs=[ANY] * n, out_specs=[ANY] * n,
        scratch_shapes=_gather_scratch(n), name=name)(*shards)


def gather_spread_rider(shards):
    return Rider(shards, [SDS((N_DEV,) + s.shape, s.dtype) for s in shards], _gather_scratch(len(shards)),
                 lambda i, o, s: _gather_phases(i, o, s)[0](), lambda i, o, s: _gather_phases(i, o, s)[1]())


def gather_forward_rider(bufs):
    n = len(bufs)
    return Rider(bufs, [SDS(b.shape, b.dtype) for b in bufs], _gather_scratch(n),
                 lambda i, o, s: _gather_phases(None, o, s)[2](), lambda i, o, s: _gather_phases(None, o, s)[3](),
                 aliases={a: a for a in range(n)})


def pair_exchange(gs, name):
    n = len(gs)

    def body(*refs):
        _pair_exchange_copies(refs[:n], refs[n:2 * n], refs[2 * n:], "start")
        _pair_exchange_copies(refs[:n], refs[n:2 * n], refs[2 * n:], "finish")

    return pl.pallas_call(
        body, out_shape=[SDS((4,) + g.shape[1:], g.dtype) for g in gs], in_specs=[ANY] * n, out_specs=[ANY] * n,
        scratch_shapes=_pair_exchange_scratch(n), name=name)(*gs)


def _pair_exchange_scratch(n):
    return [pltpu.SemaphoreType.DMA((4 * n,)), pltpu.SemaphoreType.DMA((4 * n,))]


def _pair_exchange_copies(g_refs, land_refs, sems, phase):
    send_sems, recv_sems = sems
    x, y, c = _my_pos()
    sibling = (x, y, 1 - c)
    copies = [pltpu.make_async_remote_copy(
        src_ref=g_refs[a].at[2 * k + (1 - c)], dst_ref=land_refs[a].at[k], send_sem=send_sems.at[4 * a + k],
        recv_sem=recv_sems.at[4 * a + k], device_id=sibling, device_id_type=MESH) for a in range(len(g_refs)) for k in range(4)]
    for cp in copies:
        if phase == "start":
            cp.start()
        else:
            cp.wait_recv()
    if phase == "finish":
        for cp in copies:
            cp.wait_send()


def pair_exchange_rider(gs):
    return Rider(gs, [SDS((4,) + g.shape[1:], g.dtype) for g in gs], _pair_exchange_scratch(len(gs)),
                 lambda i, o, s: _pair_exchange_copies(i, o, s, "start"), lambda i, o, s: _pair_exchange_copies(i, o, s, "finish"))


def chip_exchange(ss, name):
    n = len(ss)

    def body(*refs):
        start, finish = _chip_exchange_phases(refs[:n], refs[n:2 * n], refs[2 * n:])
        start()
        finish()

    return pl.pallas_call(
        body, out_shape=[SDS(s.shape, s.dtype) for s in ss], in_specs=[ANY] * n, out_specs=[ANY] * n,
        scratch_shapes=_chip_exchange_scratch(n), name=name)(*ss)


def _chip_exchange_scratch(n):
    return [pltpu.SemaphoreType.DMA((3 * n,)), pltpu.SemaphoreType.DMA((3 * n,)), pltpu.SemaphoreType.DMA((n,))]


def _chip_exchange_phases(s_refs, land_refs, sems):
    send_sems, recv_sems, local_sems = sems
    n = len(s_refs)
    x, y, c = _my_pos()
    my_chip = 2 * x + y
    chips = [(1 - x, y), (x, 1 - y), (1 - x, 1 - y)]

    def own(a):
        return pltpu.make_async_copy(s_refs[a].at[my_chip], land_refs[a].at[my_chip], local_sems.at[a])

    def copy(a, j, src_chip, dst_chip):
        px, py = chips[j]
        return pltpu.make_async_remote_copy(
            src_ref=s_refs[a].at[src_chip], dst_ref=land_refs[a].at[dst_chip], send_sem=send_sems.at[3 * a + j],
            recv_sem=recv_sems.at[3 * a + j], device_id=(px, py, c), device_id_type=MESH)

    def start():
        for a in range(n):
            own(a).start()
            for j, (px, py) in enumerate(chips):
                copy(a, j, 2 * px + py, my_chip).start()

    def finish():
        for a in range(n):
            for j, (px, py) in enumerate(chips):
                copy(a, j, my_chip, 2 * px + py).wait_recv()
        for a in range(n):
            for j, (px, py) in enumerate(chips):
                copy(a, j, 2 * px + py, my_chip).wait_send()
            own(a).wait()

    return start, finish


def chip_exchange_rider(ss):
    return Rider(ss, [SDS(s.shape, s.dtype) for s in ss], _chip_exchange_scratch(len(ss)),
                 lambda i, o, s: _chip_exchange_phases(i, o, s)[0](), lambda i, o, s: _chip_exchange_phases(i, o, s)[1]())


W_IN_SHARD, W_IN_PACKED = 644, 768
BIG = ("ffn_w13", "ffn_w2", "w_out", "mem_wkv", "w_in")
GATE_W = ("gla_wg2_f", "gla_wg2_b")
SHARD_NAMES = BIG + GATE_W


def _permuted_ranges(c0, c1):
    res = []
    for o0, o1, p0 in ((0, ORIG_GATE0, 0), (ORIG_GATE0, ORIG_AFTER_GATE, COL_TAIL), (ORIG_AFTER_GATE, IN_COLS, ORIG_GATE0)):
        lo, hi = max(c0, o0), min(c1, o1)
        if lo < hi:
            res.append((p0 + lo - o0, p0 + hi - o0))
    return res


def assemble_w_in(blocks):
    placed = []
    for d in range(N_DEV):
        c = d * W_IN_SHARD
        for p0, p1 in _permuted_ranges(c, c + W_IN_SHARD):
            placed.append((p0, blocks[d][:, c - d * W_IN_SHARD:c - d * W_IN_SHARD + (p1 - p0)]))
            c += p1 - p0
    placed.sort(key=lambda t: t[0])
    return jnp.concatenate([t[1] for t in placed] + [jnp.zeros((blocks.shape[1], PC - IN_COLS), blocks.dtype)], axis=1)


def split_w_in_grad(g):
    pad = jnp.zeros((g.shape[0], W_IN_PACKED - W_IN_SHARD), g.dtype)
    return jnp.stack([jnp.concatenate([g[:, p0:p1] for p0, p1 in _permuted_ranges(d * W_IN_SHARD, (d + 1) * W_IN_SHARD)]
                                      + [pad], axis=1) for d in range(N_DEV)])


def wire_shards(shards):
    return [jnp.pad(shards[n], ((0, 0), (0, W_IN_PACKED - W_IN_SHARD))) if n == "w_in" else shards[n] for n in BIG]


def weights_from_gathered(names, gathered):
    full = {}
    for n, g in zip(names, gathered):
        full[n] = g if n == "ffn_w13" else assemble_w_in(g) if n == "w_in" else g.reshape(-1, g.shape[-1])
    return full


def destination_slots(names, gw):
    return [split_w_in_grad(gw[n]) if n == "w_in" else gw[n].reshape((N_DEV, -1, gw[n].shape[-1])) for n in names]


def chip_sums(names, g, land, c_idx, tag):
    return [pair_sum(g[i], land[i], c_idx, tag + "rs_pair_sum_" + n) for i, n in enumerate(names)]


def reduce_scatter_begin(names, gw, c_idx, tag):
    g = destination_slots(names, gw)
    return chip_sums(names, g, pair_exchange(g, tag + "rs_pair_exchange_" + names[0]), c_idx, tag)


def reduce_scatter_end(names, landed, tag):
    out = {n: sum_slots(landed[i], tag + "rs_chip_sum_" + n) for i, n in enumerate(names)}
    if "w_in" in out:
        out["w_in"] = out["w_in"][:, :W_IN_SHARD]
    return out


def gather_gate_weights(w, depth):
    mine = jnp.concatenate([w[n].reshape(-1) for n in GATE_W]).reshape(-1, 128)
    got = all_gather([mine], "gather_gate_weights")[0].reshape(N_DEV, len(GATE_W), depth, GLA_RANK, GLA_DK // N_DEV)
    return [{n: got[:, i, l].transpose(1, 0, 2).reshape(GLA_RANK, GLA_DK) for i, n in enumerate(GATE_W)} for l in range(depth)]


def pad_gate_weight(wg2, backward_dir):
    r0 = GLA_RANK if backward_dir else 0
    w = wg2.astype(F32).reshape(GLA_RANK, GLA_H, GLA_HK).transpose(1, 0, 2)
    return jnp.pad(w, ((0, 0), (r0, 128 - GLA_RANK - r0), (0, 0)))


def unpad_gate_grad(dw, backward_dir):
    r0 = GLA_RANK if backward_dir else 0
    return dw[:, r0:r0 + GLA_RANK, :].transpose(1, 0, 2).reshape(GLA_RANK, GLA_DK)


def to_heads(t):
    T = t.shape[0]
    return t.reshape(T, NA_H, NA_HD).transpose(1, 0, 2)


def from_heads(t):
    return t.transpose(1, 0, 2).reshape(t.shape[1], NA_W)


REPLICATED = ("attn_norm", "gla_bg_f", "gla_bg_b", "gla_out_norm", "na_q_norm", "na_k_norm", "na_rpb", "na_out_norm",
              "mem_norm", "mem_q_norm", "mem_k_norm", "mem_out_norm", "ffn_norm")
WEIGHTS = ("attn_norm", "w_in", "gla_wg2_f", "gla_bg_f", "gla_wg2_b", "gla_bg_b", "gla_out_norm", "na_q_norm", "na_k_norm",
           "na_rpb", "na_out_norm", "mem_norm", "mem_wkv", "mem_q_norm", "mem_k_norm", "mem_out_norm", "w_out", "ffn_norm",
           "ffn_w13", "ffn_w2")


def fold_heads(dg, n_heads, name):
    hd = dg.shape[1] // n_heads
    fold = (np.arange(dg.shape[1])[:, None] % hd == np.arange(128)[None, :]).astype(np.float32)
    out = matmul(jnp.pad(dg, ((0, 7), (0, 0))), fold, "nn", name, exact=True)
    return out[0, :hd]


def layer_fwd(x, mem_n_in, p, W, l, arriving=None, next_shards=None, first_shards=None):
    tag = f"l{l}_"
    row = lambda v: v.reshape(1, -1)
    sv = {"x": x}
    if arriving is None:
        sv["xn"] = rms_fwd(x, row(p["attn_norm"]), tag + "attn_rms")
    else:
        sv["xn"], rest = rms_fwd(x, row(p["attn_norm"]), tag + "attn_rms", rider=gather_forward_rider(arriving))
        W = {**W, **weights_from_gathered(BIG[1:], rest)}
    if first_shards is None:
        proj = matmul(sv["xn"], W["w_in"], "nn", tag + "proj", out_dtype=BF16, tn=768, tk=2048)
    else:
        proj, spread = matmul(sv["xn"], W["w_in"], "nn", tag + "proj", out_dtype=BF16, tn=768, tk=2048,
                              rider=gather_spread_rider(first_shards))
    sv["proj"] = proj
    sv["wg_f"], sv["wg_b"] = pad_gate_weight(W["gla_wg2_f"], False), pad_gate_weight(W["gla_wg2_b"], True)
    sv["bg_f"], sv["bg_b"] = p["gla_bg_f"].reshape(GLA_H, 1, GLA_HK), p["gla_bg_b"].reshape(GLA_H, 1, GLA_HK)
    if first_shards is None:
        sv["o_f"], sv["s_f"] = gla_fwd(proj, sv["wg_f"], sv["bg_f"], False, tag + "gla_f")
    else:
        sv["o_f"], sv["s_f"], done = gla_fwd(proj, sv["wg_f"], sv["bg_f"], False, tag + "gla_f",
                                             rider=gather_forward_rider(spread))
        W = {**W, **weights_from_gathered(BIG[:len(done)], done)}
    sv["o_b"], sv["s_b"] = gla_fwd(proj, sv["wg_b"], sv["bg_b"], True, tag + "gla_b")
    sv["gq"], sv["gk"] = jnp.tile(row(p["na_q_norm"]), (1, NA_H)), jnp.tile(row(p["na_k_norm"]), (1, NA_H))
    sv["gmq"], sv["gmk"] = jnp.tile(row(p["mem_q_norm"]), (1, MEM_H)), jnp.tile(row(p["mem_k_norm"]), (1, MEM_H))
    qn, kn, vn, mqn = pre_fwd(proj, sv["gq"], sv["gk"], sv["gmq"], tag + "pre")
    sv["q_hm"], sv["k_hm"], sv["v_hm"], sv["mqn"] = to_heads(qn), to_heads(kn), to_heads(vn), mqn
    sv["tb2"] = rpb_table(p["na_rpb"], tag + "rpb_table")
    sv["o_na"] = from_heads(natten_fwd(sv["q_hm"], sv["k_hm"], sv["v_hm"], sv["tb2"], tag + "natten"))
    sv["mem_n"] = rms_fwd(mem_n_in, row(p["mem_norm"]), tag + "mem_rms")
    sv["kv"] = matmul(sv["mem_n"], W["mem_wkv"], "nn", tag + "mem_kv", tn=512, tk=2048)
    sv["km"], sv["vm"] = mem_kprep(sv["kv"], sv["gmk"], tag + "mem_kprep")
    sv["o_mem"] = mem_attn_fwd(mqn, sv["km"], sv["vm"], tag + "mem_attn")
    sv["ycat"] = post_fwd(sv["o_f"], sv["o_b"], proj, sv["o_na"], sv["o_mem"], row(p["gla_out_norm"]),
                          row(p["na_out_norm"]), row(p["mem_out_norm"]), tag + "post")
    x1 = matmul(sv["ycat"], W["w_out"], "nn", tag + "out_proj", res=x, tk=2048)
    sv["x1"] = x1
    sv["h"] = rms_fwd(x1, row(p["ffn_norm"]), tag + "ffn_rms")
    if next_shards is None:
        sv["gu"], sv["a"], _ = ffn_up_swiglu(sv["h"], W["ffn_w13"], tag + "ffn_up")
        return matmul(sv["a"], W["ffn_w2"], "nn", tag + "ffn_down", res=x1, tk=2816), sv, W, None, None
    sv["gu"], sv["a"], spread13 = ffn_up_swiglu(sv["h"], W["ffn_w13"], tag + "ffn_up",
                                                rider=gather_spread_rider(next_shards[:1]))
    x2, carried = matmul(sv["a"], W["ffn_w2"], "nn", tag + "ffn_down", res=x1, tk=2816,
                         rider=combine_riders(gather_forward_rider(spread13), gather_spread_rider(next_shards[1:])))
    return x2, sv, W, weights_from_gathered(BIG[:1], carried[:1]), carried[1:]


RS_EARLY, RS_LATE = ("ffn_w13", "ffn_w2", "w_out"), ("mem_wkv", "w_in")


def layer_bwd(dx2, dx2_b, mem_n_in, p, W, sv, l, c_idx, rider=None):
    tag = f"l{l}_b_"
    row = lambda v: v.reshape(1, -1)
    gw, gs = {}, {}
    gw["ffn_w2"] = matmul(sv["a"], dx2_b, "tn", tag + "dw2", out_dtype=BF16, tm=1408, tn=1024, tk=2048)
    dgu = ffn_down_bwd(dx2_b, W["ffn_w2"], sv["gu"], tag + "d_swiglu")
    dh = matmul(dgu, W["ffn_w13"], "nt", tag + "d_h", a_halves=True, b_blocked=True, rider=rider)
    dh, carried = dh if rider is not None else (dh, None)
    gw["ffn_w13"] = matmul(sv["h"], dgu, "tn", tag + "dw13", out_dtype=BF16, tm=2048, tn=1408, tk=1024,
                           b_halves=True, out_blocked=True)
    dx1, dx1_b, dg = rms_bwd(sv["x1"], row(p["ffn_norm"]), dh, dx2, tag + "ffn_rms")
    gs["ffn_norm"] = dg[0]
    gw["w_out"] = matmul(sv["ycat"], dx1_b, "tn", tag + "dw_out", out_dtype=BF16, tm=2048, tn=1024, tk=2048)
    slots = destination_slots(RS_EARLY, gw)
    dycat, landed_pair = matmul(dx1_b, W["w_out"], "nt", tag + "d_ycat", tk=2048, rider=pair_exchange_rider(slots))
    early = chip_sums(RS_EARLY, slots, landed_pair, c_idx, tag)
    d_o, d_r, d_ona, d_omem, dgg, dgn, dgm = post_bwd(
        sv["o_f"], sv["o_b"], sv["proj"], sv["o_na"], sv["o_mem"], row(p["gla_out_norm"]), row(p["na_out_norm"]),
        row(p["mem_out_norm"]), dycat, tag + "post")
    gs["gla_out_norm"], gs["na_out_norm"], gs["mem_out_norm"] = dgg[0], dgn[0], dgm[0]
    dq_f, dk_f, dv_f, dt_f, dwg_f, dbg_f = gla_bwd(sv["proj"], sv["wg_f"], sv["bg_f"], sv["s_f"], d_o, False, tag + "gla_f")
    dq_b, dk_b, dv_b, dt_b, dwg_b, dbg_b = gla_bwd(sv["proj"], sv["wg_b"], sv["bg_b"], sv["s_b"], d_o, True, tag + "gla_b")
    gs["gla_wg2_f"], gs["gla_wg2_b"] = unpad_gate_grad(dwg_f, False), unpad_gate_grad(dwg_b, True)
    gs["gla_bg_f"], gs["gla_bg_b"] = dbg_f.reshape(-1), dbg_b.reshape(-1)
    dq_hm, dk_hm, dv_hm, dtb2 = natten_bwd(sv["q_hm"], sv["k_hm"], sv["v_hm"], sv["tb2"], to_heads(d_ona), tag + "natten")
    gs["na_rpb"] = rpb_table_bwd(dtb2, tag + "rpb_table")
    d_mqn, dkm, dvm = mem_attn_bwd(sv["mqn"], sv["km"], sv["vm"], d_omem, tag + "mem_attn")
    dkv, dgmk = mem_kprep_bwd(sv["kv"], sv["gmk"], dkm, dvm, tag + "mem_kprep")
    gs["mem_k_norm"] = fold_heads(dgmk, MEM_H, tag + "fold_mk")
    gw["mem_wkv"] = matmul(sv["mem_n"], dkv, "tn", tag + "dw_kv", out_dtype=BF16, tm=2048, tn=1024, tk=256)
    d_memn = matmul(dkv, W["mem_wkv"], "nt", tag + "d_memn", tk=1024)
    _, _, dg = rms_bwd(mem_n_in, row(p["mem_norm"]), d_memn, None, tag + "mem_rms")
    gs["mem_norm"] = dg[0]
    dproj, dgq, dgk, dgmq = pre_bwd(sv["proj"], sv["gq"], sv["gk"], sv["gmq"], from_heads(dq_hm), from_heads(dk_hm),
                                    from_heads(dv_hm), d_mqn, dq_f, dq_b, dk_f, dk_b, dv_f, dv_b, d_r, dt_f, dt_b, tag + "pre")
    gs["na_q_norm"] = fold_heads(dgq, NA_H, tag + "fold_q")
    gs["na_k_norm"] = fold_heads(dgk, NA_H, tag + "fold_k")
    gs["mem_q_norm"] = fold_heads(dgmq, MEM_H, tag + "fold_mq")
    dxn, landed13 = matmul(dproj, W["w_in"], "nt", tag + "d_xn", tk=1792, rider=chip_exchange_rider(early[:1]))
    gw["w_in"], landed_rest = matmul(sv["xn"], dproj, "tn", tag + "dw_in", out_dtype=BF16, tm=2048, tn=768, tk=2048,
                                     rider=chip_exchange_rider(early[1:]))
    shard_grads = reduce_scatter_end(RS_EARLY, landed13 + landed_rest, tag)
    late = reduce_scatter_begin(RS_LATE, gw, c_idx, tag)
    dx, dx_b, dg = rms_bwd(sv["x"], row(p["attn_norm"]), dxn, dx1, tag + "attn_rms")
    gs["attn_norm"] = dg[0]
    return dx, dx_b, shard_grads, gs, carried, late


def kernel(x, mem, attn_norm, w_in, gla_wg2_f, gla_bg_f, gla_wg2_b, gla_bg_b, gla_out_norm, na_q_norm, na_k_norm, na_rpb, na_out_norm, mem_norm, mem_wkv, mem_q_norm, mem_k_norm, mem_out_norm, w_out, ffn_norm, ffn_w13, ffn_w2, loss_target, m_attn_norm, m_w_in, m_gla_wg2_f, m_gla_bg_f, m_gla_wg2_b, m_gla_bg_b, m_gla_out_norm, m_na_q_norm, m_na_k_norm, m_na_rpb, m_na_out_norm, m_mem_norm, m_mem_wkv, m_mem_q_norm, m_mem_k_norm, m_mem_out_norm, m_w_out, m_ffn_norm, m_ffn_w13, m_ffn_w2, v_attn_norm, v_w_in, v_gla_wg2_f, v_gla_bg_f, v_gla_wg2_b, v_gla_bg_b, v_gla_out_norm, v_na_q_norm, v_na_k_norm, v_na_rpb, v_na_out_norm, v_mem_norm, v_mem_wkv, v_mem_q_norm, v_mem_k_norm, v_mem_out_norm, v_w_out, v_ffn_norm, v_ffn_w13, v_ffn_w2):
    w = dict(attn_norm=attn_norm, w_in=w_in, gla_wg2_f=gla_wg2_f, gla_bg_f=gla_bg_f, gla_wg2_b=gla_wg2_b, gla_bg_b=gla_bg_b,
             gla_out_norm=gla_out_norm, na_q_norm=na_q_norm, na_k_norm=na_k_norm, na_rpb=na_rpb, na_out_norm=na_out_norm,
             mem_norm=mem_norm, mem_wkv=mem_wkv, mem_q_norm=mem_q_norm, mem_k_norm=mem_k_norm, mem_out_norm=mem_out_norm,
             w_out=w_out, ffn_norm=ffn_norm, ffn_w13=ffn_w13, ffn_w2=ffn_w2)
    mom = dict(attn_norm=m_attn_norm, w_in=m_w_in, gla_wg2_f=m_gla_wg2_f, gla_bg_f=m_gla_bg_f, gla_wg2_b=m_gla_wg2_b,
               gla_bg_b=m_gla_bg_b, gla_out_norm=m_gla_out_norm, na_q_norm=m_na_q_norm, na_k_norm=m_na_k_norm, na_rpb=m_na_rpb,
               na_out_norm=m_na_out_norm, mem_norm=m_mem_norm, mem_wkv=m_mem_wkv, mem_q_norm=m_mem_q_norm,
               mem_k_norm=m_mem_k_norm, mem_out_norm=m_mem_out_norm, w_out=m_w_out, ffn_norm=m_ffn_norm, ffn_w13=m_ffn_w13,
               ffn_w2=m_ffn_w2)
    var = dict(attn_norm=v_attn_norm, w_in=v_w_in, gla_wg2_f=v_gla_wg2_f, gla_bg_f=v_gla_bg_f, gla_wg2_b=v_gla_wg2_b,
               gla_bg_b=v_gla_bg_b, gla_out_norm=v_gla_out_norm, na_q_norm=v_na_q_norm, na_k_norm=v_na_k_norm, na_rpb=v_na_rpb,
               na_out_norm=v_na_out_norm, mem_norm=v_mem_norm, mem_wkv=v_mem_wkv, mem_q_norm=v_mem_q_norm,
               mem_k_norm=v_mem_k_norm, mem_out_norm=v_mem_out_norm, w_out=v_w_out, ffn_norm=v_ffn_norm, ffn_w13=v_ffn_w13,
               ffn_w2=v_ffn_w2)
    depth = attn_norm.shape[0]
    T = x.shape[1]
    xs, mem0, tgt = x.reshape(T, D_MODEL), mem.reshape(MEM_TOK, D_MODEL), loss_target.reshape(T, D_MODEL)
    c_idx = lax.axis_index("c").astype(jnp.int32).reshape(1)

    gates = gather_gate_weights(w, depth)
    send = [wire_shards({n: w[n][l].astype(BF16) for n in BIG}) for l in range(depth)]
    P = [{n: w[n][l] for n in REPLICATED} for l in range(depth)]

    ready = {**weights_from_gathered(BIG[4:], all_gather(send[0][4:], "l0_gather_w_in")), **gates[0]}
    arriving = None
    W, saved = [], []
    h = xs
    for l in range(depth):
        h, sv, w_l, ready, arriving = layer_fwd(h, mem0, P[l], ready, l, arriving, send[l + 1] if l + 1 < depth else None,
                                                send[0][:4] if l == 0 else None)
        W.append(w_l)
        saved.append(sv)
        if ready is not None:
            ready = {**ready, **gates[l + 1]}
    dy, dy_b, lsum = loss_bwd(h, tgt, "loss")
    loss = lax.psum(lsum[0, 0], ("x", "y", "c")) * (0.5 / D_MODEL)

    g_shard, g_small = [None] * depth, [None] * depth
    late = None
    for l in range(depth - 1, -1, -1):
        rider = chip_exchange_rider(late) if late is not None else None
        dy, dy_b, g_shard[l], g_small[l], landed, late = layer_bwd(dy, dy_b, mem0, P[l], W[l], saved[l], l, c_idx, rider)
        if landed is not None:
            g_shard[l + 1].update(reduce_scatter_end(RS_LATE, landed, f"l{l + 1}_b_"))
        saved[l] = None
    g_shard[0].update(reduce_scatter_end(RS_LATE, chip_exchange(late, "l0_rs_chip_exchange"), "l0_b_"))
    grad_x = dy.reshape(x.shape)

    small_names = REPLICATED + GATE_W
    small = jnp.concatenate([g_small[l][n].reshape(-1) for l in range(depth) for n in small_names])
    n_small = small.shape[0]
    rows = -(-n_small // 1024) * 8
    small = jnp.pad(small, (0, rows * 128 - n_small)).reshape(rows, 128)
    small = sum_slots(all_gather([small], "gather_small_grads")[0], "sum_small_grads").reshape(-1)
    grads, off = {}, 0
    per_layer = {n: [] for n in small_names}
    my_cols = (4 * lax.axis_index("x") + 2 * lax.axis_index("y") + lax.axis_index("c")) * (GLA_DK // N_DEV)
    for l in range(depth):
        for n in small_names:
            shp = (GLA_RANK, GLA_DK) if n in GATE_W else w[n].shape[1:]
            g = small[off:off + int(np.prod(shp))].reshape(shp)
            off += int(np.prod(shp))
            per_layer[n].append(lax.dynamic_slice_in_dim(g, my_cols, GLA_DK // N_DEV, axis=1) if n in GATE_W else g)
    for n in small_names:
        grads[n] = jnp.stack(per_layer[n])
    for n in BIG:
        grads[n] = jnp.stack([g_shard[l][n] for l in range(depth)])

    delta, new_m, new_v = {}, {}, {}
    for n in WEIGHTS:
        shp = w[n].shape
        two_d = (shp[0], int(np.prod(shp[1:]))) if n in REPLICATED else (int(np.prod(shp[:-1])), shp[-1])
        d_, m_, v_ = adamw(w[n].reshape(two_d), grads[n].reshape(two_d), mom[n].reshape(two_d), var[n].reshape(two_d),
                           "adamw_" + n)
        delta[n], new_m[n], new_v[n] = d_.reshape(shp), m_.reshape(shp), v_.reshape(shp)

    return (loss, grad_x, *[grads[n] for n in WEIGHTS], *[delta[n] for n in WEIGHTS], *[new_m[n] for n in WEIGHTS],
            *[new_v[n] for n in WEIGHTS])
```

```python
import functools

import numpy as np
import jax
import jax.numpy as jnp
from jax import lax
from jax.experimental import pallas as pl
from jax.experimental.pallas import tpu as pltpu

F32, BF16 = jnp.float32, jnp.bfloat16
HI = lax.Precision.HIGHEST
SDS = jax.ShapeDtypeStruct
MESH = pl.DeviceIdType.MESH

D_MODEL = 2048
GRID_W = 64
GLA_H, GLA_HK, GLA_HV, GLA_RANK, GLA_TAU, GLA_C = 4, 128, 256, 16, 16.0, 64
GLA_DK, GLA_DV = GLA_H * GLA_HK, GLA_H * GLA_HV
NA_H, NA_HD, NA_ROWS, NA_COLS = 8, 64, 8, 16
NA_W = NA_H * NA_HD
MEM_H, MEM_HD, MEM_TOK = 4, 128, 256
MEM_W = MEM_H * MEM_HD
D_FF = 5632
IN_COLS = 5152
RMS_EPS = 1e-6
ADAM_LR, ADAM_B1, ADAM_B2, ADAM_EPS, ADAM_WD, ADAM_STEP = 0.001, 0.9, 0.999, 1e-08, 0.01, 10
N_DEV = 8

PC = 5376
COL_R, COL_NQ, COL_NK, COL_NV, COL_MQ, COL_TAIL = 2048, 3072, 3584, 4096, 4608, 5120
ORIG_GATE0 = 3072
ORIG_AFTER_GATE = 3104

GLA_G = 4
VMEM_LIMIT = 56 * 1024 * 1024
NEG = -1e30


def _cparams(sem):
    return pltpu.CompilerParams(dimension_semantics=sem, vmem_limit_bytes=VMEM_LIMIT)


def _fit(n, pref, unit=128):
    if n <= pref:
        return n
    t = (pref // unit) * unit
    while t >= unit:
        if n % t == 0:
            return t
        t -= unit
    return n


def _bdot(a, b, dims):
    return lax.dot_general(a.astype(BF16), b.astype(BF16), (dims, ((), ())), preferred_element_type=F32)


NN, NT, TN = ((1,), (0,)), ((1,), (1,)), ((0,), (0,))


@functools.partial(jax.custom_vjp, nondiff_argnums=(2,))
def _bdot_vjp(a, b, dims):
    return _bdot(a, b, dims)


def _bdot_vjp_fwd(a, b, dims):
    return _bdot(a, b, dims), (a, b)


def _bdot_vjp_bwd(dims, res, ct):
    a, b = res
    if dims == NN:
        return _bdot(ct, b, NT), _bdot(a, ct, TN)
    if dims == NT:
        return _bdot(ct, b, NN), _bdot(ct, a, TN)
    return _bdot(b, ct, NT), _bdot(a, ct, NN)


_bdot_vjp.defvjp(_bdot_vjp_fwd, _bdot_vjp_bwd)


def _split_dot(c, x, dims):
    hi = x.astype(BF16)
    lo = (x - hi.astype(F32)).astype(BF16)
    cb = c.astype(BF16)
    return (lax.dot_general(cb, hi, (dims, ((), ())), preferred_element_type=F32)
            + lax.dot_general(cb, lo, (dims, ((), ())), preferred_element_type=F32))


@jax.custom_vjp
def _tri_sum(cmat, x):
    return _split_dot(cmat, x, NN)


def _tri_sum_fwd(cmat, x):
    return _split_dot(cmat, x, NN), cmat


def _tri_sum_bwd(cmat, ct):
    return jnp.zeros_like(cmat), _split_dot(cmat, ct, TN)


_tri_sum.defvjp(_tri_sum_fwd, _tri_sum_bwd)


def _call_with_rider(body, grid, in_specs, out_specs, out_shape, scratch, sem, name, args, rider):
    if rider is None:
        return pl.pallas_call(body, grid=grid, in_specs=in_specs, out_specs=out_specs, out_shape=out_shape,
                              scratch_shapes=scratch, compiler_params=_cparams(sem), name=name)(*args), None
    outs_l = list(out_shape) if isinstance(out_shape, (list, tuple)) else [out_shape]
    specs_l = list(out_specs) if isinstance(out_specs, (list, tuple)) else [out_specs]
    ni, no, ns = len(in_specs), len(outs_l), len(scratch)
    nri, nro = len(rider.inputs), len(rider.out_shapes)
    any_spec = pl.BlockSpec(memory_space=pl.ANY)

    def wrapped(*refs):
        ins, rin = refs[:ni], refs[ni:ni + nri]
        o0 = ni + nri
        outs, rout = refs[o0:o0 + no], refs[o0 + no:o0 + no + nro]
        s0 = o0 + no + nro
        scr, rsem = refs[s0:s0 + ns], refs[s0 + ns:]
        ids = [pl.program_id(d) for d in range(len(grid))]
        first = functools.reduce(jnp.logical_and, [i == 0 for i in ids])
        last = functools.reduce(jnp.logical_and, [i == g - 1 for i, g in zip(ids, grid)])

        @pl.when(first)
        def _():
            rider.start(rin, rout, rsem)

        body(*ins, *outs, *scr)

        @pl.when(last)
        def _():
            rider.finish(rin, rout, rsem)

    res = pl.pallas_call(
        wrapped, grid=grid, in_specs=list(in_specs) + [any_spec] * nri, out_specs=specs_l + [any_spec] * nro,
        out_shape=outs_l + rider.out_shapes, scratch_shapes=list(scratch) + rider.scratch,
        input_output_aliases={ni + i: no + j for i, j in rider.aliases.items()},
        compiler_params=_cparams(("arbitrary",) * len(grid)), name=name)(*args, *rider.inputs)
    main = res[:no]
    return (main if isinstance(out_shape, (list, tuple)) else main[0]), list(res[no:])


def matmul(a, b, mode, name, out_dtype=F32, res=None, tm=1024, tn=1024, tk=512, exact=False,
           a_halves=False, b_halves=False, b_blocked=False, out_blocked=False, rider=None):
    if mode == "nn":
        M, K = a.shape
        N = b.shape[0] * b.shape[2] if b_blocked else b.shape[1]
        if b_blocked:
            tn = b.shape[2]
    elif mode == "nt":
        M, K = (a.shape[1], 2 * a.shape[2]) if a_halves else a.shape
        N = b.shape[1] if b_blocked else b.shape[0]
        if b_blocked:
            tk = b.shape[2]
    else:
        K, M = a.shape
        N = 2 * b.shape[2] if b_halves else b.shape[1]
    tm, tn, tk = _fit(M, tm, 8 if mode != "tn" else 128), _fit(N, tn), _fit(K, tk, 128 if mode != "tn" else 16)
    nk = K // tk
    dims = {"nn": NN, "nt": NT, "tn": TN}[mode]

    def body(*refs):
        if res is None:
            a_ref, b_ref, o_ref = refs[:3]
            r_ref = None
            acc = refs[3] if nk > 1 else None
        else:
            a_ref, b_ref, r_ref, o_ref = refs[:4]
            acc = refs[4] if nk > 1 else None
        def product():
            if exact:
                return lax.dot_general(a_ref[...], b_ref[...], (dims, ((), ())), preferred_element_type=F32, precision=HI)
            return _bdot(a_ref[...], b_ref[...], dims)

        def finish(val):
            if r_ref is not None:
                val = val + r_ref[...]
            o_ref[...] = val.astype(out_dtype)

        if nk == 1:
            finish(product())
        else:
            kk = pl.program_id(2)

            @pl.when(kk == 0)
            def _():
                acc[...] = jnp.zeros_like(acc)

            acc[...] += product()

            @pl.when(kk == nk - 1)
            def _():
                finish(acc[...])

    if mode == "tn":
        a_spec = pl.BlockSpec((tk, tm), lambda i, j, k: (k, i))
    elif a_halves:
        nh = K // 2 // tk
        a_spec = pl.BlockSpec((None, tm, tk), lambda i, j, k: (k // nh, i, k % nh))
    else:
        a_spec = pl.BlockSpec((tm, tk), lambda i, j, k: (i, k))
    if b_blocked:
        b_spec = (pl.BlockSpec((None, tk, tn), lambda i, j, k: (j, k, 0)) if mode == "nn"
                  else pl.BlockSpec((None, tn, tk), lambda i, j, k: (k, j, 0)))
    elif b_halves:
        nh = N // 2 // tn
        b_spec = pl.BlockSpec((None, tk, tn), lambda i, j, k: (j // nh, k, j % nh))
    elif mode == "nt":
        b_spec = pl.BlockSpec((tn, tk), lambda i, j, k: (j, k))
    else:
        b_spec = pl.BlockSpec((tk, tn), lambda i, j, k: (k, j))
    if out_blocked:
        o_spec, o_shape = pl.BlockSpec((None, tm, tn), lambda i, j, k: (j, i, 0)), SDS((N // tn, M, tn), out_dtype)
    else:
        o_spec, o_shape = pl.BlockSpec((tm, tn), lambda i, j, k: (i, j)), SDS((M, N), out_dtype)
    in_specs, args = [a_spec, b_spec], [a, b]
    if res is not None:
        in_specs.append(o_spec)
        args.append(res)
    out, carried = _call_with_rider(
        body, (M // tm, N // tn, nk), in_specs, o_spec, o_shape, [pltpu.VMEM((tm, tn), F32)] if nk > 1 else [],
        ("parallel", "parallel", "arbitrary"), name, args, rider)
    return out if rider is None else (out, carried)


def rms_fwd(x, g, name, rider=None):
    T, D = x.shape
    tm = _fit(T, 512, 8)

    def body(x_ref, g_ref, o_ref):
        xv = x_ref[...]
        r = lax.rsqrt(jnp.mean(xv * xv, axis=-1, keepdims=True) + RMS_EPS)
        o_ref[...] = (xv * r * g_ref[...]).astype(BF16)

    out, carried = _call_with_rider(
        body, (T // tm,), [pl.BlockSpec((tm, D), lambda i: (i, 0)), pl.BlockSpec((1, D), lambda i: (0, 0))],
        pl.BlockSpec((tm, D), lambda i: (i, 0)), SDS((T, D), BF16), [], ("parallel",), name, [x, g], rider)
    return out if rider is None else (out, carried)


def rms_bwd(x, g, dy, dres, name):
    T, D = x.shape
    tm = _fit(T, 256, 8)
    has_res = dres is not None

    def body(*refs):
        if has_res:
            x_ref, g_ref, dy_ref, dres_ref, dx_ref, dxb_ref, dg_ref = refs
        else:
            x_ref, g_ref, dy_ref, dx_ref, dxb_ref, dg_ref = refs
        xv, dyv = x_ref[...], dy_ref[...]
        r = lax.rsqrt(jnp.mean(xv * xv, axis=-1, keepdims=True) + RMS_EPS)
        xh = xv * r
        dxh = dyv * g_ref[...]
        dx = r * (dxh - xh * jnp.mean(dxh * xh, axis=-1, keepdims=True))
        if has_res:
            dx = dx + dres_ref[...]
        dx_ref[...] = dx
        dxb_ref[...] = dx.astype(BF16)

        @pl.when(pl.program_id(0) == 0)
        def _():
            dg_ref[...] = jnp.zeros_like(dg_ref)

        dg_ref[...] += jnp.sum(dyv * xh, axis=0, keepdims=True)

    row = pl.BlockSpec((tm, D), lambda i: (i, 0))
    vec = pl.BlockSpec((1, D), lambda i: (0, 0))
    args = [x, g, dy] + ([dres] if has_res else [])
    return pl.pallas_call(
        body, grid=(T // tm,), in_specs=[row, vec, row] + ([row] if has_res else []),
        out_specs=[row, row, vec], out_shape=[SDS((T, D), F32), SDS((T, D), BF16), SDS((1, D), F32)],
        compiler_params=_cparams(("arbitrary",)), name=name)(*args)


def ffn_up_swiglu(h, w13b, name, rider=None):
    T, D = h.shape
    nb, _, tb = w13b.shape
    nh = nb // 2
    tm = _fit(T, 512, 8)

    def body(h_ref, wg_ref, wu_ref, gu_ref, a_ref):
        hv = h_ref[...]
        gv = _bdot(hv, wg_ref[...], NN)
        uv = _bdot(hv, wu_ref[...], NN)
        gu_ref[0] = gv.astype(BF16)
        gu_ref[1] = uv.astype(BF16)
        a_ref[...] = (gv * jax.nn.sigmoid(gv) * uv).astype(BF16)

    (gu, act), carried = _call_with_rider(
        body, (nh, T // tm),
        [pl.BlockSpec((tm, D), lambda j, i: (i, 0)), pl.BlockSpec((None, D, tb), lambda j, i: (j, 0, 0)),
         pl.BlockSpec((None, D, tb), lambda j, i: (j + nh, 0, 0))],
        [pl.BlockSpec((2, tm, tb), lambda j, i: (0, i, j)), pl.BlockSpec((tm, tb), lambda j, i: (i, j))],
        [SDS((2, T, nh * tb), BF16), SDS((T, nh * tb), BF16)], [], ("parallel", "parallel"), name, [h, w13b, w13b], rider)
    return gu, act, carried


def ffn_down_bwd(dy_b, w2, gu, name):
    T, D = dy_b.shape
    Fh = w2.shape[0]
    tm, tf = _fit(T, 1024, 8), _fit(Fh, 512)
    nf = Fh // tf

    def body(dy_ref, w_ref, g_ref, u_ref, o_ref):
        dav = _bdot(dy_ref[...], w_ref[...], NT)
        gv, uv = g_ref[...].astype(F32), u_ref[...].astype(F32)
        sg = jax.nn.sigmoid(gv)
        o_ref[0] = (dav * uv * (sg * (1.0 + gv * (1.0 - sg)))).astype(BF16)
        o_ref[1] = (dav * gv * sg).astype(BF16)

    return pl.pallas_call(
        body, grid=(T // tm, nf),
        in_specs=[pl.BlockSpec((tm, D), lambda i, j: (i, 0)), pl.BlockSpec((tf, D), lambda i, j: (j, 0)),
                  pl.BlockSpec((None, tm, tf), lambda i, j: (0, i, j)), pl.BlockSpec((None, tm, tf), lambda i, j: (1, i, j))],
        out_specs=pl.BlockSpec((2, tm, tf), lambda i, j: (0, i, j)), out_shape=SDS((2, T, Fh), BF16),
        compiler_params=_cparams(("parallel", "parallel")), name=name)(dy_b, w2, gu, gu)


def loss_bwd(y, tgt, name):
    T, D = y.shape
    tm = _fit(T, 512, 8)

    def body(y_ref, t_ref, dy_ref, dyb_ref, l_ref):
        e = y_ref[...] - t_ref[...]
        dy = e * (1.0 / D)
        dy_ref[...] = dy
        dyb_ref[...] = dy.astype(BF16)

        @pl.when(pl.program_id(0) == 0)
        def _():
            l_ref[...] = jnp.zeros_like(l_ref)

        l_ref[...] += jnp.sum(jnp.sum(e * e, axis=1, keepdims=True), axis=0, keepdims=True)

    row = pl.BlockSpec((tm, D), lambda i: (i, 0))
    return pl.pallas_call(
        body, grid=(T // tm,), in_specs=[row, row], out_specs=[row, row, pl.BlockSpec((8, 128), lambda i: (0, 0))],
        out_shape=[SDS((T, D), F32), SDS((T, D), BF16), SDS((8, 128), F32)],
        compiler_params=_cparams(("arbitrary",)), name=name)(y, tgt)


def _log_sigmoid(z):
    return jnp.minimum(z, 0.0) - jnp.log(1.0 + jnp.exp(-jnp.abs(z)))


def _gla_group(q, k, v, tail, wg, bg, s, cmat, mask, backward_dir):
    n = q.shape[0] // GLA_C
    z = _bdot_vjp(tail, wg, NN) + bg
    la = _log_sigmoid(z) * (1.0 / GLA_TAU)
    cum = _tri_sum(cmat, la)
    chunks = [slice(g * GLA_C, (g + 1) * GLA_C) for g in range(n)]
    last = [jnp.sum(la[sl], axis=0, keepdims=True) for sl in chunks]
    last_rows = jnp.concatenate([jnp.broadcast_to(t, (GLA_C, t.shape[1])) for t in last], axis=0)
    q_e = q * jnp.exp(cum) * (GLA_HK ** -0.5)
    k_e = k * jnp.exp(-cum)
    k_end = k * jnp.exp(last_rows - cum)
    sc = jnp.where(mask > 0.5, _bdot_vjp(q_e, k_e, NT), 0.0)
    o = _bdot_vjp(sc, v, NN)
    o_inter = [None] * n
    for g in (range(n - 1, -1, -1) if backward_dir else range(n)):
        o_inter[g] = _bdot_vjp(q_e[chunks[g]], s, NN)
        s = s * jnp.transpose(jnp.exp(last[g])) + _bdot_vjp(k_end[chunks[g]], v[chunks[g]], TN)
    return o + jnp.concatenate(o_inter, axis=0), s


def _gla_consts(backward_dir, GC):
    i = np.arange(GC)
    same = (i[:, None] // GLA_C) == (i[None, :] // GLA_C)
    if backward_dir:
        return (same & (i[None, :] >= i[:, None])).astype(np.float32), (same & (i[None, :] > i[:, None])).astype(np.float32)
    tri = (same & (i[None, :] <= i[:, None])).astype(np.float32)
    return tri, tri


GLA_HB = 4


def _gla_in_specs(GC, nmap):
    return [
        pl.BlockSpec((GC, GLA_HB * GLA_HK), lambda h, n: (nmap(n), h)),
        pl.BlockSpec((GC, GLA_HB * GLA_HK), lambda h, n: (nmap(n), GLA_H // GLA_HB + h)),
        pl.BlockSpec((GC, GLA_HB * GLA_HV), lambda h, n: (nmap(n), 2 * GLA_DK // (GLA_HB * GLA_HV) + h)),
        pl.BlockSpec((GC, 128), lambda h, n: (nmap(n), COL_TAIL // 128)),
        pl.BlockSpec((GLA_HB, 128, 128), lambda h, n: (h, 0, 0)),
        pl.BlockSpec((GLA_HB, 1, 128), lambda h, n: (h, 0, 0)),
        pl.BlockSpec((GC, GC), lambda h, n: (0, 0)),
        pl.BlockSpec((GC, GC), lambda h, n: (0, 0)),
    ]


def _head_cols(ref, hh, width):
    return ref[:, hh * width:(hh + 1) * width].astype(F32)


def gla_fwd(proj, wgpad, bg, backward_dir, name, rider=None):
    T = proj.shape[0]
    GC = min(GLA_G * GLA_C, T)
    NG = T // GC
    cmat, mask = _gla_consts(backward_dir, GC)
    nmap = (lambda n: NG - 1 - n) if backward_dir else (lambda n: n)

    def body(q_ref, k_ref, v_ref, t_ref, wg_ref, bg_ref, c_ref, m_ref, o_ref, ss_ref, s_scr):
        @pl.when(pl.program_id(1) == 0)
        def _():
            s_scr[...] = jnp.zeros_like(s_scr)

        tail, cm, mk = t_ref[...].astype(F32), c_ref[...], m_ref[...]
        for hh in range(GLA_HB):
            s0 = s_scr[hh]
            ss_ref[hh] = s0
            o, s1 = _gla_group(_head_cols(q_ref, hh, GLA_HK), _head_cols(k_ref, hh, GLA_HK), _head_cols(v_ref, hh, GLA_HV),
                               tail, wg_ref[hh], bg_ref[hh], s0, cm, mk, backward_dir)
            o_ref[:, hh * GLA_HV:(hh + 1) * GLA_HV] = o
            s_scr[hh] = s1

    (o, states), carried = _call_with_rider(
        body, (GLA_H // GLA_HB, NG), _gla_in_specs(GC, nmap),
        [pl.BlockSpec((GC, GLA_HB * GLA_HV), lambda h, n: (nmap(n), h)),
         pl.BlockSpec((GLA_HB, None, GLA_HK, GLA_HV), lambda h, n: (h, nmap(n), 0, 0))],
        [SDS((T, GLA_DV), F32), SDS((GLA_H, NG, GLA_HK, GLA_HV), F32)], [pltpu.VMEM((GLA_HB, GLA_HK, GLA_HV), F32)],
        ("parallel", "arbitrary"), name, [proj, proj, proj, proj, wgpad, bg, cmat, mask], rider)
    return (o, states) if rider is None else (o, states, carried)


def gla_bwd(proj, wgpad, bg, ssave, do, backward_dir, name):
    T = proj.shape[0]
    GC = min(GLA_G * GLA_C, T)
    NG = T // GC
    cmat, mask = _gla_consts(backward_dir, GC)
    nmap = (lambda n: n) if backward_dir else (lambda n: NG - 1 - n)

    def body(q_ref, k_ref, v_ref, t_ref, wg_ref, bg_ref, c_ref, m_ref, ss_ref, do_ref,
             dq_ref, dk_ref, dv_ref, dt_ref, dwg_ref, dbg_ref, ds_scr):
        @pl.when(pl.program_id(1) == 0)
        def _():
            ds_scr[...] = jnp.zeros_like(ds_scr)
            dwg_ref[...] = jnp.zeros_like(dwg_ref)
            dbg_ref[...] = jnp.zeros_like(dbg_ref)

        tail, cm, mk = t_ref[...].astype(F32), c_ref[...], m_ref[...]
        fn = lambda q, k, v, t, wg, b, s: _gla_group(q, k, v, t, wg, b, s, cm, mk, backward_dir)
        for hh in range(GLA_HB):
            _, vjp = jax.vjp(fn, _head_cols(q_ref, hh, GLA_HK), _head_cols(k_ref, hh, GLA_HK), _head_cols(v_ref, hh, GLA_HV),
                             tail, wg_ref[hh], bg_ref[hh], ss_ref[hh])
            dq, dk, dv, dt, dwg, dbg, ds = vjp((_head_cols(do_ref, hh, GLA_HV), ds_scr[hh]))
            dq_ref[:, hh * GLA_HK:(hh + 1) * GLA_HK] = dq.astype(BF16)
            dk_ref[:, hh * GLA_HK:(hh + 1) * GLA_HK] = dk.astype(BF16)
            dv_ref[:, hh * GLA_HV:(hh + 1) * GLA_HV] = dv.astype(BF16)
            dt_ref[hh] = dt.astype(BF16)
            dwg_ref[hh] += dwg
            dbg_ref[hh] += dbg
            ds_scr[hh] = ds

    in_specs = _gla_in_specs(GC, nmap) + [
        pl.BlockSpec((GLA_HB, None, GLA_HK, GLA_HV), lambda h, n: (h, nmap(n), 0, 0)),
        pl.BlockSpec((GC, GLA_HB * GLA_HV), lambda h, n: (nmap(n), h)),
    ]
    out_specs = [
        pl.BlockSpec((GC, GLA_HB * GLA_HK), lambda h, n: (nmap(n), h)),
        pl.BlockSpec((GC, GLA_HB * GLA_HK), lambda h, n: (nmap(n), h)),
        pl.BlockSpec((GC, GLA_HB * GLA_HV), lambda h, n: (nmap(n), h)),
        pl.BlockSpec((GLA_HB, GC, 128), lambda h, n: (h, nmap(n), 0)),
        pl.BlockSpec((GLA_HB, 128, 128), lambda h, n: (h, 0, 0)),
        pl.BlockSpec((GLA_HB, 1, 128), lambda h, n: (h, 0, 0)),
    ]
    out_shape = [SDS((T, GLA_DK), BF16), SDS((T, GLA_DK), BF16), SDS((T, GLA_DV), BF16), SDS((GLA_H, T, 128), BF16),
                 SDS((GLA_H, 128, 128), F32), SDS((GLA_H, 1, 128), F32)]
    return pl.pallas_call(
        body, grid=(GLA_H // GLA_HB, NG), in_specs=in_specs, out_specs=out_specs, out_shape=out_shape,
        scratch_shapes=[pltpu.VMEM((GLA_HB, GLA_HK, GLA_HV), F32)],
        compiler_params=_cparams(("parallel", "arbitrary")), name=name)(proj, proj, proj, proj, wgpad, bg, cmat, mask, ssave, do)


def _block_diag(width, hd):
    i = np.arange(width)
    return ((i[:, None] // hd) == (i[None, :] // hd)).astype(np.float32) / hd


def _norm_heads(t, hd):
    outs = []
    for h in range(t.shape[1] // hd):
        th = t[:, h * hd:(h + 1) * hd]
        outs.append(th * lax.rsqrt(jnp.mean(th * th, axis=-1, keepdims=True) + RMS_EPS))
    return jnp.concatenate(outs, axis=1)


def _split_dot_right(x, c):
    hi = x.astype(BF16)
    lo = (x - hi.astype(F32)).astype(BF16)
    cb = c.astype(BF16)
    return jnp.dot(hi, cb, preferred_element_type=F32) + jnp.dot(lo, cb, preferred_element_type=F32)


@jax.custom_vjp
def _head_means(x, bd):
    return _split_dot_right(x, bd)


def _head_means_fwd(x, bd):
    return _split_dot_right(x, bd), bd


def _head_means_bwd(bd, ct):
    return _split_dot_right(ct, bd), jnp.zeros_like(bd)


_head_means.defvjp(_head_means_fwd, _head_means_bwd)


def _norm_bd(t, bd):
    return t * lax.rsqrt(_head_means(t * t, bd) + RMS_EPS)


def _pre_fn(nq, nk, mq, gq, gk, gm, bd):
    return _norm_bd(nq, bd) * gq, _norm_bd(nk, bd) * gk, _norm_heads(mq, MEM_HD) * gm


def pre_fwd(proj, gq, gk, gm, name):
    T = proj.shape[0]
    tm = _fit(T, 512, 8)
    bd = _block_diag(NA_W, NA_HD)

    def body(nq_ref, nk_ref, nv_ref, mq_ref, gq_ref, gk_ref, gm_ref, bd_ref, q_ref, k_ref, v_ref, m_ref):
        qn, kn, mn = _pre_fn(nq_ref[...].astype(F32), nk_ref[...].astype(F32), mq_ref[...].astype(F32),
                             gq_ref[...], gk_ref[...], gm_ref[...], bd_ref[...])
        q_ref[...] = qn.astype(BF16)
        k_ref[...] = kn.astype(BF16)
        v_ref[...] = nv_ref[...].astype(BF16)
        m_ref[...] = mn.astype(BF16)

    col = lambda c0: pl.BlockSpec((tm, 512), lambda i: (i, c0 // 512))
    vec = pl.BlockSpec((1, 512), lambda i: (0, 0))
    row = pl.BlockSpec((tm, 512), lambda i: (i, 0))
    return pl.pallas_call(
        body, grid=(T // tm,),
        in_specs=[col(COL_NQ), col(COL_NK), col(COL_NV), col(COL_MQ), vec, vec, vec, pl.BlockSpec((NA_W, NA_W), lambda i: (0, 0))],
        out_specs=[row] * 4, out_shape=[SDS((T, 512), BF16)] * 4,
        compiler_params=_cparams(("parallel",)), name=name)(proj, proj, proj, proj, gq, gk, gm, bd)


def pre_bwd(proj, gq, gk, gm, d_qn, d_kn, d_nv, d_mn, dq_f, dq_b, dk_f, dk_b, dv_f, dv_b, d_r, dt_f, dt_b, name):
    T = proj.shape[0]
    tm = _fit(T, 256, 8)
    bd = _block_diag(NA_W, NA_HD)

    def body(nq_ref, nk_ref, mq_ref, gq_ref, gk_ref, gm_ref, bd_ref, dqn_ref, dkn_ref, dnv_ref, dmn_ref,
             dqf_ref, dqb_ref, dkf_ref, dkb_ref, dvf_ref, dvb_ref, dr_ref, dtf_ref, dtb_ref,
             o_ref, dgq_ref, dgk_ref, dgm_ref):
        bdv = bd_ref[...]
        fn = lambda a, b, c, x, y, z: _pre_fn(a, b, c, x, y, z, bdv)
        _, vjp = jax.vjp(fn, nq_ref[...].astype(F32), nk_ref[...].astype(F32), mq_ref[...].astype(F32),
                         gq_ref[...], gk_ref[...], gm_ref[...])
        d_nq, d_nk, d_mq, dgq, dgk, dgm = vjp((dqn_ref[...], dkn_ref[...], dmn_ref[...]))
        both = lambda f_ref, b_ref: (f_ref[...].astype(F32) + b_ref[...].astype(F32)).astype(BF16)
        o_ref[:, 0:512] = both(dqf_ref, dqb_ref)
        o_ref[:, 512:1024] = both(dkf_ref, dkb_ref)
        o_ref[:, 1024:2048] = both(dvf_ref, dvb_ref)
        o_ref[:, COL_R:COL_R + 1024] = dr_ref[...].astype(BF16)
        o_ref[:, COL_NQ:COL_NQ + 512] = d_nq.astype(BF16)
        o_ref[:, COL_NK:COL_NK + 512] = d_nk.astype(BF16)
        o_ref[:, COL_NV:COL_NV + 512] = dnv_ref[...].astype(BF16)
        o_ref[:, COL_MQ:COL_MQ + 512] = d_mq.astype(BF16)
        dt = dtf_ref[0].astype(F32) + dtb_ref[0].astype(F32)
        for h in range(1, GLA_H):
            dt = dt + dtf_ref[h].astype(F32) + dtb_ref[h].astype(F32)
        o_ref[:, COL_TAIL:COL_TAIL + 128] = dt.astype(BF16)
        o_ref[:, COL_TAIL + 128:PC] = jnp.zeros((tm, PC - COL_TAIL - 128), BF16)

        @pl.when(pl.program_id(0) == 0)
        def _():
            dgq_ref[...] = jnp.zeros_like(dgq_ref)
            dgk_ref[...] = jnp.zeros_like(dgk_ref)
            dgm_ref[...] = jnp.zeros_like(dgm_ref)

        dgq_ref[...] += dgq
        dgk_ref[...] += dgk
        dgm_ref[...] += dgm

    col = lambda c0: pl.BlockSpec((tm, 512), lambda i: (i, c0 // 512))
    vec = pl.BlockSpec((1, 512), lambda i: (0, 0))
    r512 = pl.BlockSpec((tm, 512), lambda i: (i, 0))
    r1024 = pl.BlockSpec((tm, 1024), lambda i: (i, 0))
    tl = pl.BlockSpec((GLA_H, tm, 128), lambda i: (0, i, 0))
    in_specs = [col(COL_NQ), col(COL_NK), col(COL_MQ), vec, vec, vec, pl.BlockSpec((NA_W, NA_W), lambda i: (0, 0)),
                r512, r512, r512, r512, r512, r512, r512, r512, r1024, r1024, r1024, tl, tl]
    return pl.pallas_call(
        body, grid=(T // tm,), in_specs=in_specs,
        out_specs=[pl.BlockSpec((tm, PC), lambda i: (i, 0)), vec, vec, vec],
        out_shape=[SDS((T, PC), BF16), SDS((1, 512), F32), SDS((1, 512), F32), SDS((1, 512), F32)],
        compiler_params=_cparams(("arbitrary",)), name=name)(
            proj, proj, proj, gq, gk, gm, bd, d_qn, d_kn, d_nv, d_mn, dq_f, dq_b, dk_f, dk_b, dv_f, dv_b, d_r, dt_f, dt_b)


def _post_fn(o_f, o_b, r, o_na, o_mem, g_gla, g_na, g_mem, bd):
    y_gla = _norm_heads(o_f + o_b, GLA_HV) * g_gla * (r * jax.nn.sigmoid(r))
    y_na = _norm_bd(o_na, bd) * g_na
    y_mem = _norm_heads(o_mem, MEM_HD) * g_mem
    return jnp.concatenate([y_gla, y_na, y_mem], axis=1)


def post_fwd(o_f, o_b, proj, o_na, o_mem, g_gla, g_na, g_mem, name):
    T = proj.shape[0]
    tm = _fit(T, 256, 8)
    bd = _block_diag(NA_W, NA_HD)

    def body(of_ref, ob_ref, r_ref, ona_ref, omem_ref, gg_ref, gn_ref, gm_ref, bd_ref, y_ref):
        y_ref[...] = _post_fn(of_ref[...], ob_ref[...], r_ref[...].astype(F32), ona_ref[...], omem_ref[...],
                              gg_ref[...], gn_ref[...], gm_ref[...], bd_ref[...]).astype(BF16)

    r1024 = pl.BlockSpec((tm, 1024), lambda i: (i, 0))
    r512 = pl.BlockSpec((tm, 512), lambda i: (i, 0))
    in_specs = [r1024, r1024, pl.BlockSpec((tm, 1024), lambda i: (i, COL_R // 1024)), r512, r512,
                pl.BlockSpec((1, 1024), lambda i: (0, 0)), pl.BlockSpec((1, 512), lambda i: (0, 0)),
                pl.BlockSpec((1, 512), lambda i: (0, 0)), pl.BlockSpec((NA_W, NA_W), lambda i: (0, 0))]
    return pl.pallas_call(
        body, grid=(T // tm,), in_specs=in_specs, out_specs=pl.BlockSpec((tm, D_MODEL), lambda i: (i, 0)),
        out_shape=SDS((T, D_MODEL), BF16), compiler_params=_cparams(("parallel",)), name=name)(
            o_f, o_b, proj, o_na, o_mem, g_gla, g_na, g_mem, bd)


def post_bwd(o_f, o_b, proj, o_na, o_mem, g_gla, g_na, g_mem, dy, name):
    T = proj.shape[0]
    tm = _fit(T, 256, 8)
    bd = _block_diag(NA_W, NA_HD)

    def body(of_ref, ob_ref, r_ref, ona_ref, omem_ref, gg_ref, gn_ref, gm_ref, bd_ref, dy_ref,
             do_ref, dr_ref, dna_ref, dmem_ref, dgg_ref, dgn_ref, dgm_ref):
        bdv = bd_ref[...]
        fn = lambda o, r, a, m, x, y, z: _post_fn(o, 0.0, r, a, m, x, y, z, bdv)
        _, vjp = jax.vjp(fn, of_ref[...] + ob_ref[...], r_ref[...].astype(F32), ona_ref[...], omem_ref[...],
                         gg_ref[...], gn_ref[...], gm_ref[...])
        d_o, d_r, d_na, d_mem, dgg, dgn, dgm = vjp(dy_ref[...])
        do_ref[...] = d_o.astype(BF16)
        dr_ref[...] = d_r.astype(BF16)
        dna_ref[...] = d_na.astype(BF16)
        dmem_ref[...] = d_mem.astype(BF16)

        @pl.when(pl.program_id(0) == 0)
        def _():
            dgg_ref[...] = jnp.zeros_like(dgg_ref)
            dgn_ref[...] = jnp.zeros_like(dgn_ref)
            dgm_ref[...] = jnp.zeros_like(dgm_ref)

        dgg_ref[...] += dgg
        dgn_ref[...] += dgn
        dgm_ref[...] += dgm

    r1024 = pl.BlockSpec((tm, 1024), lambda i: (i, 0))
    r512 = pl.BlockSpec((tm, 512), lambda i: (i, 0))
    v1024 = pl.BlockSpec((1, 1024), lambda i: (0, 0))
    v512 = pl.BlockSpec((1, 512), lambda i: (0, 0))
    in_specs = [r1024, r1024, pl.BlockSpec((tm, 1024), lambda i: (i, COL_R // 1024)), r512, r512, v1024, v512, v512,
                pl.BlockSpec((NA_W, NA_W), lambda i: (0, 0)), pl.BlockSpec((tm, D_MODEL), lambda i: (i, 0))]
    return pl.pallas_call(
        body, grid=(T // tm,), in_specs=in_specs, out_specs=[r1024, r1024, r512, r512, v1024, v512, v512],
        out_shape=[SDS((T, 1024), BF16), SDS((T, 1024), BF16), SDS((T, 512), BF16), SDS((T, 512), BF16),
                   SDS((1, 1024), F32), SDS((1, 512), F32), SDS((1, 512), F32)],
        compiler_params=_cparams(("arbitrary",)), name=name)(o_f, o_b, proj, o_na, o_mem, g_gla, g_na, g_mem, bd, dy)


NA_RB = 16


def _na_row_scores(q_ref, k_ref, v_ref, tb_ref, rb, j, n_rows):
    r = rb * NA_RB + j
    rs = jnp.clip(r - NA_ROWS // 2, 0, n_rows - NA_ROWS)
    dr0 = rs - r + (NA_ROWS - 1)
    tok = pl.ds(pl.multiple_of(rs * GRID_W, GRID_W), NA_ROWS * GRID_W)
    q = q_ref[j * GRID_W:(j + 1) * GRID_W, :]
    kk, vv = k_ref[tok, :], v_ref[tok, :]
    bias = jnp.concatenate([tb_ref[dr0 + 2 * i] for i in range(NA_ROWS // 2)], axis=1)
    s = _bdot(q, kk, NT) * (NA_HD ** -0.5) + bias
    m = jnp.max(s, axis=1, keepdims=True)
    p = jnp.exp(s - m)
    l = jnp.sum(p, axis=1, keepdims=True)
    return q, kk, vv, p, l, tok, dr0


def natten_fwd(q, k, v, tb2, name):
    H, T, hd = q.shape
    n_rows = T // GRID_W
    rbt = NA_RB * GRID_W

    def body(q_ref, k_ref, v_ref, tb_ref, o_ref):
        rb = pl.program_id(1)
        for j in range(NA_RB):
            _, _, vv, p, l, _, _ = _na_row_scores(q_ref, k_ref, v_ref, tb_ref, rb, j, n_rows)
            o_ref[j * GRID_W:(j + 1) * GRID_W, :] = _bdot(p, vv, NN) / l

    whole = pl.BlockSpec((None, T, hd), lambda h, r: (h, 0, 0))
    blk = pl.BlockSpec((None, rbt, hd), lambda h, r: (h, r, 0))
    return pl.pallas_call(
        body, grid=(H, n_rows // NA_RB),
        in_specs=[blk, whole, whole, pl.BlockSpec((None, 2 * NA_ROWS - 2, GRID_W, 2 * GRID_W), lambda h, r: (h, 0, 0, 0))],
        out_specs=blk, out_shape=SDS((H, T, hd), F32),
        compiler_params=_cparams(("parallel", "arbitrary")), name=name)(q, k, v, tb2)


def natten_bwd(q, k, v, tb2, do, name):
    H, T, hd = q.shape
    n_rows = T // GRID_W
    rbt = NA_RB * GRID_W
    scale = NA_HD ** -0.5

    def body(q_ref, k_ref, v_ref, tb_ref, do_ref, dq_ref, dk_ref, dv_ref, dtb_ref):
        rb = pl.program_id(1)

        @pl.when(rb == 0)
        def _():
            dk_ref[...] = jnp.zeros_like(dk_ref)
            dv_ref[...] = jnp.zeros_like(dv_ref)
            dtb_ref[...] = jnp.zeros_like(dtb_ref)

        for j in range(NA_RB):
            qv, kk, vv, p, l, tok, dr0 = _na_row_scores(q_ref, k_ref, v_ref, tb_ref, rb, j, n_rows)
            p = p / l
            dov = do_ref[j * GRID_W:(j + 1) * GRID_W, :]
            dp = _bdot(dov, vv, NT)
            ds = p * (dp - jnp.sum(dp * p, axis=1, keepdims=True))
            dq_ref[j * GRID_W:(j + 1) * GRID_W, :] = _bdot(ds, kk, NN) * scale
            dk_ref[tok, :] += _bdot(ds, qv, TN) * scale
            dv_ref[tok, :] += _bdot(p, dov, TN)
            for i in range(NA_ROWS // 2):
                dtb_ref[dr0 + 2 * i] += ds[:, 2 * GRID_W * i:2 * GRID_W * (i + 1)]

    whole = pl.BlockSpec((None, T, hd), lambda h, r: (h, 0, 0))
    blk = pl.BlockSpec((None, rbt, hd), lambda h, r: (h, r, 0))
    tbs = pl.BlockSpec((None, 2 * NA_ROWS - 2, GRID_W, 2 * GRID_W), lambda h, r: (h, 0, 0, 0))
    return pl.pallas_call(
        body, grid=(H, n_rows // NA_RB), in_specs=[blk, whole, whole, tbs, blk],
        out_specs=[blk, whole, whole, tbs],
        out_shape=[SDS((H, T, hd), F32), SDS((H, T, hd), F32), SDS((H, T, hd), F32), SDS(tb2.shape, F32)],
        compiler_params=_cparams(("parallel", "arbitrary")), name=name)(q, k, v, tb2, do)


def _rpb_expand_consts():
    qc = np.arange(GRID_W)[:, None]
    kc = np.arange(GRID_W)[None, :]
    cs = np.clip(qc - NA_COLS // 2, 0, GRID_W - NA_COLS)
    inside = (kc >= cs) & (kc < cs + NA_COLS)
    dc = np.clip(kc - qc, -(NA_COLS - 1), NA_COLS - 1) + (NA_COLS - 1)
    e = np.zeros((128, GRID_W * GRID_W), np.float32)
    flat = (qc * GRID_W + kc)
    e[dc[inside], flat[inside]] = 1.0
    neg = np.where(inside, 0.0, NEG).astype(np.float32).reshape(1, -1)
    return e, neg


def _rpb_fold_consts():
    sa = np.zeros((NA_H * 15, NA_H * 14), np.float32)
    sb = np.zeros((NA_H * 15, NA_H * 14), np.float32)
    for h in range(NA_H):
        for d in range(14):
            sa[h * 15 + d, h * 14 + d] = 1.0
            sb[h * 15 + d + 1, h * 14 + d] = 1.0
    return sa, sb


def rpb_table(rpb, name):
    e, neg = _rpb_expand_consts()
    rp = jnp.pad(rpb.reshape(NA_H * 15, 31), ((0, 0), (0, 128 - 31)))

    def body(r_ref, e_ref, n_ref, o_ref):
        o_ref[...] = jnp.dot(r_ref[...], e_ref[...], preferred_element_type=F32, precision=HI) + n_ref[...]

    t = pl.pallas_call(body, out_shape=SDS((NA_H * 15, GRID_W * GRID_W), F32), name=name)(rp, e, neg)
    t = t.reshape(NA_H, 15, GRID_W, GRID_W)
    return jnp.concatenate([t[:, :14], t[:, 1:]], axis=-1)


def rpb_table_bwd(dtb2, name):
    e, _ = _rpb_expand_consts()
    sa, sb = _rpb_fold_consts()
    a = dtb2[..., :GRID_W].reshape(NA_H * 14, GRID_W * GRID_W)
    b = dtb2[..., GRID_W:].reshape(NA_H * 14, GRID_W * GRID_W)

    def body(a_ref, b_ref, e_ref, sa_ref, sb_ref, o_ref):
        ev = e_ref[...]
        pa = lax.dot_general(a_ref[...], ev, (NT, ((), ())), preferred_element_type=F32, precision=HI)
        pb = lax.dot_general(b_ref[...], ev, (NT, ((), ())), preferred_element_type=F32, precision=HI)
        o_ref[...] = (jnp.dot(sa_ref[...], pa, preferred_element_type=F32, precision=HI)
                      + jnp.dot(sb_ref[...], pb, preferred_element_type=F32, precision=HI))

    d = pl.pallas_call(body, out_shape=SDS((NA_H * 15, 128), F32), name=name)(a, b, e, sa, sb)
    return d[:, :31].reshape(NA_H, 15, 31)


def _kprep_fn(kv, gk):
    return _norm_heads(kv[:, :MEM_W], MEM_HD) * gk, kv[:, MEM_W:]


def mem_kprep(kv, gk, name):
    def body(kv_ref, g_ref, k_ref, v_ref):
        kn, vv = _kprep_fn(kv_ref[...], g_ref[...])
        k_ref[...] = kn.astype(BF16)
        v_ref[...] = vv.astype(BF16)

    return pl.pallas_call(body, out_shape=[SDS((MEM_TOK, MEM_W), BF16)] * 2, name=name)(kv, gk)


def mem_kprep_bwd(kv, gk, dk, dv, name):
    def body(kv_ref, g_ref, dk_ref, dv_ref, dkv_ref, dg_ref):
        _, vjp = jax.vjp(_kprep_fn, kv_ref[...], g_ref[...])
        dkv, dg = vjp((dk_ref[...], dv_ref[...]))
        dkv_ref[...] = dkv.astype(BF16)
        dg_ref[...] = dg

    return pl.pallas_call(body, out_shape=[SDS((MEM_TOK, 2 * MEM_W), BF16), SDS((1, MEM_W), F32)], name=name)(kv, gk, dk, dv)


def _mem_probs(q_ref, k_ref, h):
    hs = slice(h * MEM_HD, (h + 1) * MEM_HD)
    qh, kh = q_ref[:, hs], k_ref[:, hs]
    s = _bdot(qh, kh, NT) * (MEM_HD ** -0.5)
    p = jnp.exp(s - jnp.max(s, axis=1, keepdims=True))
    return hs, qh, kh, p, jnp.sum(p, axis=1, keepdims=True)


def mem_attn_fwd(q, km, vm, name):
    T = q.shape[0]
    tm = _fit(T, 512, 8)

    def body(q_ref, k_ref, v_ref, o_ref):
        for h in range(MEM_H):
            hs, _, _, p, l = _mem_probs(q_ref, k_ref, h)
            o_ref[:, hs] = _bdot(p, v_ref[:, hs], NN) / l

    row = pl.BlockSpec((tm, MEM_W), lambda i: (i, 0))
    full = pl.BlockSpec((MEM_TOK, MEM_W), lambda i: (0, 0))
    return pl.pallas_call(body, grid=(T // tm,), in_specs=[row, full, full], out_specs=row, out_shape=SDS((T, MEM_W), F32),
                          compiler_params=_cparams(("parallel",)), name=name)(q, km, vm)


def mem_attn_bwd(q, km, vm, do, name):
    T = q.shape[0]
    tm = _fit(T, 512, 8)
    scale = MEM_HD ** -0.5

    def body(q_ref, k_ref, v_ref, do_ref, dq_ref, dk_ref, dv_ref):
        @pl.when(pl.program_id(0) == 0)
        def _():
            dk_ref[...] = jnp.zeros_like(dk_ref)
            dv_ref[...] = jnp.zeros_like(dv_ref)

        for h in range(MEM_H):
            hs, qh, kh, p, l = _mem_probs(q_ref, k_ref, h)
            p = p / l
            dov = do_ref[:, hs]
            dp = _bdot(dov, v_ref[:, hs], NT)
            ds = p * (dp - jnp.sum(dp * p, axis=1, keepdims=True))
            dq_ref[:, hs] = _bdot(ds, kh, NN) * scale
            dk_ref[:, hs] += _bdot(ds, qh, TN) * scale
            dv_ref[:, hs] += _bdot(p, dov, TN)

    row = pl.BlockSpec((tm, MEM_W), lambda i: (i, 0))
    full = pl.BlockSpec((MEM_TOK, MEM_W), lambda i: (0, 0))
    return pl.pallas_call(
        body, grid=(T // tm,), in_specs=[row, full, full, row], out_specs=[row, full, full],
        out_shape=[SDS((T, MEM_W), F32), SDS((MEM_TOK, MEM_W), F32), SDS((MEM_TOK, MEM_W), F32)],
        compiler_params=_cparams(("arbitrary",)), name=name)(q, km, vm, do)


SLOT_BLOCK_ELEMS = 512 * 1024


def _slot_rows(R, Cc):
    return _fit(R, max(16, SLOT_BLOCK_ELEMS // Cc // 16 * 16), 16)


def sum_slots(a, name):
    n, R, Cc = a.shape
    tr = _slot_rows(R, Cc)

    def body(a_ref, o_ref):
        s = a_ref[0].astype(F32)
        for i in range(1, n):
            s = s + a_ref[i].astype(F32)
        o_ref[...] = s

    return pl.pallas_call(body, grid=(R // tr,), in_specs=[pl.BlockSpec((n, tr, Cc), lambda i: (0, i, 0))],
                          out_specs=pl.BlockSpec((tr, Cc), lambda i: (i, 0)), out_shape=SDS((R, Cc), F32),
                          compiler_params=_cparams(("parallel",)), name=name)(a)


def pair_sum(g, land, c_idx, name):
    _, R, Cc = g.shape
    tr = _slot_rows(R, Cc)

    def body(c_ref, g_ref, l_ref, o_ref):
        o_ref[...] = (g_ref[...].astype(F32) + l_ref[...].astype(F32)).astype(o_ref.dtype)

    blk = pl.BlockSpec((None, tr, Cc), lambda k, i, c: (k, i, 0))
    gs = pltpu.PrefetchScalarGridSpec(
        num_scalar_prefetch=1, grid=(4, R // tr),
        in_specs=[pl.BlockSpec((None, tr, Cc), lambda k, i, c: (2 * k + c[0], i, 0)), blk], out_specs=blk)
    return pl.pallas_call(body, grid_spec=gs, out_shape=SDS((4, R, Cc), g.dtype),
                          compiler_params=_cparams(("parallel", "parallel")), name=name)(c_idx, g, land)


def adamw(w, g, m, v, name):
    R, Cc = w.shape
    tr = _fit(R, max(8, (262144 // max(Cc, 128)) // 8 * 8), 8)
    c1 = 1.0 - ADAM_B1 ** ADAM_STEP
    c2 = 1.0 - ADAM_B2 ** ADAM_STEP

    def body(w_ref, g_ref, m_ref, v_ref, d_ref, mo_ref, vo_ref):
        gv = g_ref[...]
        m2 = ADAM_B1 * m_ref[...] + (1.0 - ADAM_B1) * gv
        v2 = ADAM_B2 * v_ref[...] + (1.0 - ADAM_B2) * (gv * gv)
        d_ref[...] = -ADAM_LR * ((m2 / c1) / (jnp.sqrt(v2 / c2) + ADAM_EPS) + ADAM_WD * w_ref[...])
        mo_ref[...] = m2
        vo_ref[...] = v2

    blk = pl.BlockSpec((tr, Cc), lambda i: (i, 0))
    return pl.pallas_call(body, grid=(R // tr,), in_specs=[blk] * 4, out_specs=[blk] * 3, out_shape=[SDS((R, Cc), F32)] * 3,
                          compiler_params=_cparams(("parallel",)), name=name)(w, g, m, v)


ANY = pl.BlockSpec(memory_space=pl.ANY)


def _my_pos():
    return lax.axis_index("x"), lax.axis_index("y"), lax.axis_index("c")


class Rider:
    def __init__(self, inputs, out_shapes, scratch, start, finish, aliases=None):
        self.inputs, self.out_shapes, self.scratch = list(inputs), list(out_shapes), list(scratch)
        self.start, self.finish, self.aliases = start, finish, dict(aliases or {})


def combine_riders(r1, r2):
    ni, no, ns = len(r1.inputs), len(r1.out_shapes), len(r1.scratch)

    def both(f1, f2):
        def run(i, o, s):
            f1(i[:ni], o[:no], s[:ns])
            f2(i[ni:], o[no:], s[ns:])
        return run

    aliases = {**r1.aliases, **{ni + i: no + j for i, j in r2.aliases.items()}}
    return Rider(r1.inputs + r2.inputs, r1.out_shapes + r2.out_shapes, r1.scratch + r2.scratch,
                 both(r1.start, r2.start), both(r1.finish, r2.finish), aliases)


def _gather_phases(x_refs, out_refs, sems):
    send_sems, recv_sems, local_sems = sems
    n = len(out_refs)
    x, y, c = _my_pos()
    me, sibling = (x, y, c), (x, y, 1 - c)
    chips = [(1 - x, y), (x, 1 - y), (1 - x, 1 - y)]

    def slot(a, px, py, pc):
        return out_refs[a].at[4 * px + 2 * py + pc]

    def copy(a, k, block, to, src=None):
        return pltpu.make_async_remote_copy(
            src_ref=slot(a, *block) if src is None else src, dst_ref=slot(a, *block),
            send_sem=send_sems.at[7 * a + k], recv_sem=recv_sems.at[7 * a + k], device_id=to, device_id_type=MESH)

    def mine(a):
        return pltpu.make_async_copy(x_refs[a], slot(a, *me), local_sems.at[a])

    def spread(a):
        return [copy(a, 0, me, sibling, src=x_refs[a])] + [copy(a, 1 + j, me, (*chip, c), src=x_refs[a])
                                                           for j, chip in enumerate(chips)]

    def spread_start():
        for a in range(n):
            mine(a).start()
        for a in range(n):
            for cp in spread(a):
                cp.start()

    def spread_finish():
        for j, chip in enumerate(chips):
            for a in range(n):
                copy(a, 1 + j, (*chip, c), me).wait_recv()
        for a in range(n):
            copy(a, 0, sibling, me).wait_recv()
        for a in range(n):
            for cp in spread(a):
                cp.wait_send()
            mine(a).wait()

    def forward_start():
        for j, chip in enumerate(chips):
            for a in range(n):
                copy(a, 4 + j, (*chip, c), sibling).start()

    def forward_finish():
        for j, chip in enumerate(chips):
            for a in range(n):
                copy(a, 4 + j, (*chip, 1 - c), me).wait_recv()
        for j, chip in enumerate(chips):
            for a in range(n):
                copy(a, 4 + j, (*chip, c), sibling).wait_send()

    return spread_start, spread_finish, forward_start, forward_finish


def _gather_scratch(n):
    return [pltpu.SemaphoreType.DMA((7 * n,)), pltpu.SemaphoreType.DMA((7 * n,)), pltpu.SemaphoreType.DMA((n,))]


def all_gather(shards, name):
    n = len(shards)

    def body(*refs):
        phases = _gather_phases(refs[:n], refs[n:2 * n], refs[2 * n:])
        for phase in phases:
            phase()

    return pl.pallas_call(
        body, out_shape=[SDS((N_DEV,) + s.shape, s.dtype) for s in shards], in_specs=[ANY] * n, out_specs=[ANY] * n,
        scratch_shapes=_gather_scratch(n), name=name)(*shards)


def gather_spread_rider(shards):
    return Rider(shards, [SDS((N_DEV,) + s.shape, s.dtype) for s in shards], _gather_scratch(len(shards)),
                 lambda i, o, s: _gather_phases(i, o, s)[0](), lambda i, o, s: _gather_phases(i, o, s)[1]())


def gather_forward_rider(bufs):
    n = len(bufs)
    return Rider(bufs, [SDS(b.shape, b.dtype) for b in bufs], _gather_scratch(n),
                 lambda i, o, s: _gather_phases(None, o, s)[2](), lambda i, o, s: _gather_phases(None, o, s)[3](),
                 aliases={a: a for a in range(n)})


def pair_exchange(gs, name):
    n = len(gs)

    def body(*refs):
        _pair_exchange_copies(refs[:n], refs[n:2 * n], refs[2 * n:], "start")
        _pair_exchange_copies(refs[:n], refs[n:2 * n], refs[2 * n:], "finish")

    return pl.pallas_call(
        body, out_shape=[SDS((4,) + g.shape[1:], g.dtype) for g in gs], in_specs=[ANY] * n, out_specs=[ANY] * n,
        scratch_shapes=_pair_exchange_scratch(n), name=name)(*gs)


def _pair_exchange_scratch(n):
    return [pltpu.SemaphoreType.DMA((4 * n,)), pltpu.SemaphoreType.DMA((4 * n,))]


def _pair_exchange_copies(g_refs, land_refs, sems, phase):
    send_sems, recv_sems = sems
    x, y, c = _my_pos()
    sibling = (x, y, 1 - c)
    copies = [pltpu.make_async_remote_copy(
        src_ref=g_refs[a].at[2 * k + (1 - c)], dst_ref=land_refs[a].at[k], send_sem=send_sems.at[4 * a + k],
        recv_sem=recv_sems.at[4 * a + k], device_id=sibling, device_id_type=MESH) for a in range(len(g_refs)) for k in range(4)]
    for cp in copies:
        if phase == "start":
            cp.start()
        else:
            cp.wait_recv()
    if phase == "finish":
        for cp in copies:
            cp.wait_send()


def pair_exchange_rider(gs):
    return Rider(gs, [SDS((4,) + g.shape[1:], g.dtype) for g in gs], _pair_exchange_scratch(len(gs)),
                 lambda i, o, s: _pair_exchange_copies(i, o, s, "start"), lambda i, o, s: _pair_exchange_copies(i, o, s, "finish"))


def chip_exchange(ss, name):
    n = len(ss)

    def body(*refs):
        start, finish = _chip_exchange_phases(refs[:n], refs[n:2 * n], refs[2 * n:])
        start()
        finish()

    return pl.pallas_call(
        body, out_shape=[SDS(s.shape, s.dtype) for s in ss], in_specs=[ANY] * n, out_specs=[ANY] * n,
        scratch_shapes=_chip_exchange_scratch(n), name=name)(*ss)


def _chip_exchange_scratch(n):
    return [pltpu.SemaphoreType.DMA((3 * n,)), pltpu.SemaphoreType.DMA((3 * n,)), pltpu.SemaphoreType.DMA((n,))]


def _chip_exchange_phases(s_refs, land_refs, sems):
    send_sems, recv_sems, local_sems = sems
    n = len(s_refs)
    x, y, c = _my_pos()
    my_chip = 2 * x + y
    chips = [(1 - x, y), (x, 1 - y), (1 - x, 1 - y)]

    def own(a):
        return pltpu.make_async_copy(s_refs[a].at[my_chip], land_refs[a].at[my_chip], local_sems.at[a])

    def copy(a, j, src_chip, dst_chip):
        px, py = chips[j]
        return pltpu.make_async_remote_copy(
            src_ref=s_refs[a].at[src_chip], dst_ref=land_refs[a].at[dst_chip], send_sem=send_sems.at[3 * a + j],
            recv_sem=recv_sems.at[3 * a + j], device_id=(px, py, c), device_id_type=MESH)

    def start():
        for a in range(n):
            own(a).start()
            for j, (px, py) in enumerate(chips):
                copy(a, j, 2 * px + py, my_chip).start()

    def finish():
        for a in range(n):
            for j, (px, py) in enumerate(chips):
                copy(a, j, my_chip, 2 * px + py).wait_recv()
        for a in range(n):
            for j, (px, py) in enumerate(chips):
                copy(a, j, 2 * px + py, my_chip).wait_send()
            own(a).wait()

    return start, finish


def chip_exchange_rider(ss):
    return Rider(ss, [SDS(s.shape, s.dtype) for s in ss], _chip_exchange_scratch(len(ss)),
                 lambda i, o, s: _chip_exchange_phases(i, o, s)[0](), lambda i, o, s: _chip_exchange_phases(i, o, s)[1]())


W_IN_SHARD, W_IN_PACKED = 644, 768
BIG = ("ffn_w13", "ffn_w2", "w_out", "mem_wkv", "w_in")
GATE_W = ("gla_wg2_f", "gla_wg2_b")
SHARD_NAMES = BIG + GATE_W


def _permuted_ranges(c0, c1):
    res = []
    for o0, o1, p0 in ((0, ORIG_GATE0, 0), (ORIG_GATE0, ORIG_AFTER_GATE, COL_TAIL), (ORIG_AFTER_GATE, IN_COLS, ORIG_GATE0)):
        lo, hi = max(c0, o0), min(c1, o1)
        if lo < hi:
            res.append((p0 + lo - o0, p0 + hi - o0))
    return res


def assemble_w_in(blocks):
    placed = []
    for d in range(N_DEV):
        c = d * W_IN_SHARD
        for p0, p1 in _permuted_ranges(c, c + W_IN_SHARD):
            placed.append((p0, blocks[d][:, c - d * W_IN_SHARD:c - d * W_IN_SHARD + (p1 - p0)]))
            c += p1 - p0
    placed.sort(key=lambda t: t[0])
    return jnp.concatenate([t[1] for t in placed] + [jnp.zeros((blocks.shape[1], PC - IN_COLS), blocks.dtype)], axis=1)


def split_w_in_grad(g):
    pad = jnp.zeros((g.shape[0], W_IN_PACKED - W_IN_SHARD), g.dtype)
    return jnp.stack([jnp.concatenate([g[:, p0:p1] for p0, p1 in _permuted_ranges(d * W_IN_SHARD, (d + 1) * W_IN_SHARD)]
                                      + [pad], axis=1) for d in range(N_DEV)])


def wire_shards(shards):
    return [jnp.pad(shards[n], ((0, 0), (0, W_IN_PACKED - W_IN_SHARD))) if n == "w_in" else shards[n] for n in BIG]


def weights_from_gathered(names, gathered):
    full = {}
    for n, g in zip(names, gathered):
        full[n] = g if n == "ffn_w13" else assemble_w_in(g) if n == "w_in" else g.reshape(-1, g.shape[-1])
    return full


def destination_slots(names, gw):
    return [split_w_in_grad(gw[n]) if n == "w_in" else gw[n].reshape((N_DEV, -1, gw[n].shape[-1])) for n in names]


def chip_sums(names, g, land, c_idx, tag):
    return [pair_sum(g[i], land[i], c_idx, tag + "rs_pair_sum_" + n) for i, n in enumerate(names)]


def reduce_scatter_begin(names, gw, c_idx, tag):
    g = destination_slots(names, gw)
    return chip_sums(names, g, pair_exchange(g, tag + "rs_pair_exchange_" + names[0]), c_idx, tag)


def reduce_scatter_end(names, landed, tag):
    out = {n: sum_slots(landed[i], tag + "rs_chip_sum_" + n) for i, n in enumerate(names)}
    if "w_in" in out:
        out["w_in"] = out["w_in"][:, :W_IN_SHARD]
    return out


def gather_gate_weights(w, depth):
    mine = jnp.concatenate([w[n].reshape(-1) for n in GATE_W]).reshape(-1, 128)
    got = all_gather([mine], "gather_gate_weights")[0].reshape(N_DEV, len(GATE_W), depth, GLA_RANK, GLA_DK // N_DEV)
    return [{n: got[:, i, l].transpose(1, 0, 2).reshape(GLA_RANK, GLA_DK) for i, n in enumerate(GATE_W)} for l in range(depth)]


def pad_gate_weight(wg2, backward_dir):
    r0 = GLA_RANK if backward_dir else 0
    w = wg2.astype(F32).reshape(GLA_RANK, GLA_H, GLA_HK).transpose(1, 0, 2)
    return jnp.pad(w, ((0, 0), (r0, 128 - GLA_RANK - r0), (0, 0)))


def unpad_gate_grad(dw, backward_dir):
    r0 = GLA_RANK if backward_dir else 0
    return dw[:, r0:r0 + GLA_RANK, :].transpose(1, 0, 2).reshape(GLA_RANK, GLA_DK)


def to_heads(t):
    T = t.shape[0]
    return t.reshape(T, NA_H, NA_HD).transpose(1, 0, 2)


def from_heads(t):
    return t.transpose(1, 0, 2).reshape(t.shape[1], NA_W)


REPLICATED = ("attn_norm", "gla_bg_f", "gla_bg_b", "gla_out_norm", "na_q_norm", "na_k_norm", "na_rpb", "na_out_norm",
              "mem_norm", "mem_q_norm", "mem_k_norm", "mem_out_norm", "ffn_norm")
WEIGHTS = ("attn_norm", "w_in", "gla_wg2_f", "gla_bg_f", "gla_wg2_b", "gla_bg_b", "gla_out_norm", "na_q_norm", "na_k_norm",
           "na_rpb", "na_out_norm", "mem_norm", "mem_wkv", "mem_q_norm", "mem_k_norm", "mem_out_norm", "w_out", "ffn_norm",
           "ffn_w13", "ffn_w2")


def fold_heads(dg, n_heads, name):
    hd = dg.shape[1] // n_heads
    fold = (np.arange(dg.shape[1])[:, None] % hd == np.arange(128)[None, :]).astype(np.float32)
    out = matmul(jnp.pad(dg, ((0, 7), (0, 0))), fold, "nn", name, exact=True)
    return out[0, :hd]


def layer_fwd(x, mem_n_in, p, W, l, arriving=None, next_shards=None, first_shards=None):
    tag = f"l{l}_"
    row = lambda v: v.reshape(1, -1)
    sv = {"x": x}
    if arriving is None:
        sv["xn"] = rms_fwd(x, row(p["attn_norm"]), tag + "attn_rms")
    else:
        sv["xn"], rest = rms_fwd(x, row(p["attn_norm"]), tag + "attn_rms", rider=gather_forward_rider(arriving))
        W = {**W, **weights_from_gathered(BIG[1:], rest)}
    if first_shards is None:
        proj = matmul(sv["xn"], W["w_in"], "nn", tag + "proj", out_dtype=BF16, tn=768, tk=2048)
    else:
        proj, spread = matmul(sv["xn"], W["w_in"], "nn", tag + "proj", out_dtype=BF16, tn=768, tk=2048,
                              rider=gather_spread_rider(first_shards))
    sv["proj"] = proj
    sv["wg_f"], sv["wg_b"] = pad_gate_weight(W["gla_wg2_f"], False), pad_gate_weight(W["gla_wg2_b"], True)
    sv["bg_f"], sv["bg_b"] = p["gla_bg_f"].reshape(GLA_H, 1, GLA_HK), p["gla_bg_b"].reshape(GLA_H, 1, GLA_HK)
    if first_shards is None:
        sv["o_f"], sv["s_f"] = gla_fwd(proj, sv["wg_f"], sv["bg_f"], False, tag + "gla_f")
    else:
        sv["o_f"], sv["s_f"], done = gla_fwd(proj, sv["wg_f"], sv["bg_f"], False, tag + "gla_f",
                                             rider=gather_forward_rider(spread))
        W = {**W, **weights_from_gathered(BIG[:len(done)], done)}
    sv["o_b"], sv["s_b"] = gla_fwd(proj, sv["wg_b"], sv["bg_b"], True, tag + "gla_b")
    sv["gq"], sv["gk"] = jnp.tile(row(p["na_q_norm"]), (1, NA_H)), jnp.tile(row(p["na_k_norm"]), (1, NA_H))
    sv["gmq"], sv["gmk"] = jnp.tile(row(p["mem_q_norm"]), (1, MEM_H)), jnp.tile(row(p["mem_k_norm"]), (1, MEM_H))
    qn, kn, vn, mqn = pre_fwd(proj, sv["gq"], sv["gk"], sv["gmq"], tag + "pre")
    sv["q_hm"], sv["k_hm"], sv["v_hm"], sv["mqn"] = to_heads(qn), to_heads(kn), to_heads(vn), mqn
    sv["tb2"] = rpb_table(p["na_rpb"], tag + "rpb_table")
    sv["o_na"] = from_heads(natten_fwd(sv["q_hm"], sv["k_hm"], sv["v_hm"], sv["tb2"], tag + "natten"))
    sv["mem_n"] = rms_fwd(mem_n_in, row(p["mem_norm"]), tag + "mem_rms")
    sv["kv"] = matmul(sv["mem_n"], W["mem_wkv"], "nn", tag + "mem_kv", tn=512, tk=2048)
    sv["km"], sv["vm"] = mem_kprep(sv["kv"], sv["gmk"], tag + "mem_kprep")
    sv["o_mem"] = mem_attn_fwd(mqn, sv["km"], sv["vm"], tag + "mem_attn")
    sv["ycat"] = post_fwd(sv["o_f"], sv["o_b"], proj, sv["o_na"], sv["o_mem"], row(p["gla_out_norm"]),
                          row(p["na_out_norm"]), row(p["mem_out_norm"]), tag + "post")
    x1 = matmul(sv["ycat"], W["w_out"], "nn", tag + "out_proj", res=x, tk=2048)
    sv["x1"] = x1
    sv["h"] = rms_fwd(x1, row(p["ffn_norm"]), tag + "ffn_rms")
    if next_shards is None:
        sv["gu"], sv["a"], _ = ffn_up_swiglu(sv["h"], W["ffn_w13"], tag + "ffn_up")
        return matmul(sv["a"], W["ffn_w2"], "nn", tag + "ffn_down", res=x1, tk=2816), sv, W, None, None
    sv["gu"], sv["a"], spread13 = ffn_up_swiglu(sv["h"], W["ffn_w13"], tag + "ffn_up",
                                                rider=gather_spread_rider(next_shards[:1]))
    x2, carried = matmul(sv["a"], W["ffn_w2"], "nn", tag + "ffn_down", res=x1, tk=2816,
                         rider=combine_riders(gather_forward_rider(spread13), gather_spread_rider(next_shards[1:])))
    return x2, sv, W, weights_from_gathered(BIG[:1], carried[:1]), carried[1:]


RS_EARLY, RS_LATE = ("ffn_w13", "ffn_w2", "w_out"), ("mem_wkv", "w_in")


def layer_bwd(dx2, dx2_b, mem_n_in, p, W, sv, l, c_idx, rider=None):
    tag = f"l{l}_b_"
    row = lambda v: v.reshape(1, -1)
    gw, gs = {}, {}
    gw["ffn_w2"] = matmul(sv["a"], dx2_b, "tn", tag + "dw2", out_dtype=BF16, tm=1408, tn=1024, tk=2048)
    dgu = ffn_down_bwd(dx2_b, W["ffn_w2"], sv["gu"], tag + "d_swiglu")
    dh = matmul(dgu, W["ffn_w13"], "nt", tag + "d_h", a_halves=True, b_blocked=True, rider=rider)
    dh, carried = dh if rider is not None else (dh, None)
    gw["ffn_w13"] = matmul(sv["h"], dgu, "tn", tag + "dw13", out_dtype=BF16, tm=2048, tn=1408, tk=1024,
                           b_halves=True, out_blocked=True)
    dx1, dx1_b, dg = rms_bwd(sv["x1"], row(p["ffn_norm"]), dh, dx2, tag + "ffn_rms")
    gs["ffn_norm"] = dg[0]
    gw["w_out"] = matmul(sv["ycat"], dx1_b, "tn", tag + "dw_out", out_dtype=BF16, tm=2048, tn=1024, tk=2048)
    slots = destination_slots(RS_EARLY, gw)
    dycat, landed_pair = matmul(dx1_b, W["w_out"], "nt", tag + "d_ycat", tk=2048, rider=pair_exchange_rider(slots))
    early = chip_sums(RS_EARLY, slots, landed_pair, c_idx, tag)
    d_o, d_r, d_ona, d_omem, dgg, dgn, dgm = post_bwd(
        sv["o_f"], sv["o_b"], sv["proj"], sv["o_na"], sv["o_mem"], row(p["gla_out_norm"]), row(p["na_out_norm"]),
        row(p["mem_out_norm"]), dycat, tag + "post")
    gs["gla_out_norm"], gs["na_out_norm"], gs["mem_out_norm"] = dgg[0], dgn[0], dgm[0]
    dq_f, dk_f, dv_f, dt_f, dwg_f, dbg_f = gla_bwd(sv["proj"], sv["wg_f"], sv["bg_f"], sv["s_f"], d_o, False, tag + "gla_f")
    dq_b, dk_b, dv_b, dt_b, dwg_b, dbg_b = gla_bwd(sv["proj"], sv["wg_b"], sv["bg_b"], sv["s_b"], d_o, True, tag + "gla_b")
    gs["gla_wg2_f"], gs["gla_wg2_b"] = unpad_gate_grad(dwg_f, False), unpad_gate_grad(dwg_b, True)
    gs["gla_bg_f"], gs["gla_bg_b"] = dbg_f.reshape(-1), dbg_b.reshape(-1)
    dq_hm, dk_hm, dv_hm, dtb2 = natten_bwd(sv["q_hm"], sv["k_hm"], sv["v_hm"], sv["tb2"], to_heads(d_ona), tag + "natten")
    gs["na_rpb"] = rpb_table_bwd(dtb2, tag + "rpb_table")
    d_mqn, dkm, dvm = mem_attn_bwd(sv["mqn"], sv["km"], sv["vm"], d_omem, tag + "mem_attn")
    dkv, dgmk = mem_kprep_bwd(sv["kv"], sv["gmk"], dkm, dvm, tag + "mem_kprep")
    gs["mem_k_norm"] = fold_heads(dgmk, MEM_H, tag + "fold_mk")
    gw["mem_wkv"] = matmul(sv["mem_n"], dkv, "tn", tag + "dw_kv", out_dtype=BF16, tm=2048, tn=1024, tk=256)
    d_memn = matmul(dkv, W["mem_wkv"], "nt", tag + "d_memn", tk=1024)
    _, _, dg = rms_bwd(mem_n_in, row(p["mem_norm"]), d_memn, None, tag + "mem_rms")
    gs["mem_norm"] = dg[0]
    dproj, dgq, dgk, dgmq = pre_bwd(sv["proj"], sv["gq"], sv["gk"], sv["gmq"], from_heads(dq_hm), from_heads(dk_hm),
                                    from_heads(dv_hm), d_mqn, dq_f, dq_b, dk_f, dk_b, dv_f, dv_b, d_r, dt_f, dt_b, tag + "pre")
    gs["na_q_norm"] = fold_heads(dgq, NA_H, tag + "fold_q")
    gs["na_k_norm"] = fold_heads(dgk, NA_H, tag + "fold_k")
    gs["mem_q_norm"] = fold_heads(dgmq, MEM_H, tag + "fold_mq")
    dxn, landed13 = matmul(dproj, W["w_in"], "nt", tag + "d_xn", tk=1792, rider=chip_exchange_rider(early[:1]))
    gw["w_in"], landed_rest = matmul(sv["xn"], dproj, "tn", tag + "dw_in", out_dtype=BF16, tm=2048, tn=768, tk=2048,
                                     rider=chip_exchange_rider(early[1:]))
    shard_grads = reduce_scatter_end(RS_EARLY, landed13 + landed_rest, tag)
    late = reduce_scatter_begin(RS_LATE, gw, c_idx, tag)
    dx, dx_b, dg = rms_bwd(sv["x"], row(p["attn_norm"]), dxn, dx1, tag + "attn_rms")
    gs["attn_norm"] = dg[0]
    return dx, dx_b, shard_grads, gs, carried, late


def kernel(x, mem, attn_norm, w_in, gla_wg2_f, gla_bg_f, gla_wg2_b, gla_bg_b, gla_out_norm, na_q_norm, na_k_norm, na_rpb, na_out_norm, mem_norm, mem_wkv, mem_q_norm, mem_k_norm, mem_out_norm, w_out, ffn_norm, ffn_w13, ffn_w2, loss_target, m_attn_norm, m_w_in, m_gla_wg2_f, m_gla_bg_f, m_gla_wg2_b, m_gla_bg_b, m_gla_out_norm, m_na_q_norm, m_na_k_norm, m_na_rpb, m_na_out_norm, m_mem_norm, m_mem_wkv, m_mem_q_norm, m_mem_k_norm, m_mem_out_norm, m_w_out, m_ffn_norm, m_ffn_w13, m_ffn_w2, v_attn_norm, v_w_in, v_gla_wg2_f, v_gla_bg_f, v_gla_wg2_b, v_gla_bg_b, v_gla_out_norm, v_na_q_norm, v_na_k_norm, v_na_rpb, v_na_out_norm, v_mem_norm, v_mem_wkv, v_mem_q_norm, v_mem_k_norm, v_mem_out_norm, v_w_out, v_ffn_norm, v_ffn_w13, v_ffn_w2):
    w = dict(attn_norm=attn_norm, w_in=w_in, gla_wg2_f=gla_wg2_f, gla_bg_f=gla_bg_f, gla_wg2_b=gla_wg2_b, gla_bg_b=gla_bg_b,
             gla_out_norm=gla_out_norm, na_q_norm=na_q_norm, na_k_norm=na_k_norm, na_rpb=na_rpb, na_out_norm=na_out_norm,
             mem_norm=mem_norm, mem_wkv=mem_wkv, mem_q_norm=mem_q_norm, mem_k_norm=mem_k_norm, mem_out_norm=mem_out_norm,
             w_out=w_out, ffn_norm=ffn_norm, ffn_w13=ffn_w13, ffn_w2=ffn_w2)
    mom = dict(attn_norm=m_attn_norm, w_in=m_w_in, gla_wg2_f=m_gla_wg2_f, gla_bg_f=m_gla_bg_f, gla_wg2_b=m_gla_wg2_b,
               gla_bg_b=m_gla_bg_b, gla_out_norm=m_gla_out_norm, na_q_norm=m_na_q_norm, na_k_norm=m_na_k_norm, na_rpb=m_na_rpb,
               na_out_norm=m_na_out_norm, mem_norm=m_mem_norm, mem_wkv=m_mem_wkv, mem_q_norm=m_mem_q_norm,
               mem_k_norm=m_mem_k_norm, mem_out_norm=m_mem_out_norm, w_out=m_w_out, ffn_norm=m_ffn_norm, ffn_w13=m_ffn_w13,
               ffn_w2=m_ffn_w2)
    var = dict(attn_norm=v_attn_norm, w_in=v_w_in, gla_wg2_f=v_gla_wg2_f, gla_bg_f=v_gla_bg_f, gla_wg2_b=v_gla_wg2_b,
               gla_bg_b=v_gla_bg_b, gla_out_norm=v_gla_out_norm, na_q_norm=v_na_q_norm, na_k_norm=v_na_k_norm, na_rpb=v_na_rpb,
               na_out_norm=v_na_out_norm, mem_norm=v_mem_norm, mem_wkv=v_mem_wkv, mem_q_norm=v_mem_q_norm,
               mem_k_norm=v_mem_k_norm, mem_out_norm=v_mem_out_norm, w_out=v_w_out, ffn_norm=v_ffn_norm, ffn_w13=v_ffn_w13,
               ffn_w2=v_ffn_w2)
    depth = attn_norm.shape[0]
    T = x.shape[1]
    xs, mem0, tgt = x.reshape(T, D_MODEL), mem.reshape(MEM_TOK, D_MODEL), loss_target.reshape(T, D_MODEL)
    c_idx = lax.axis_index("c").astype(jnp.int32).reshape(1)

    gates = gather_gate_weights(w, depth)
    send = [wire_shards({n: w[n][l].astype(BF16) for n in BIG}) for l in range(depth)]
    P = [{n: w[n][l] for n in REPLICATED} for l in range(depth)]

    ready = {**weights_from_gathered(BIG[4:], all_gather(send[0][4:], "l0_gather_w_in")), **gates[0]}
    arriving = None
    W, saved = [], []
    h = xs
    for l in range(depth):
        h, sv, w_l, ready, arriving = layer_fwd(h, mem0, P[l], ready, l, arriving, send[l + 1] if l + 1 < depth else None,
                                                send[0][:4] if l == 0 else None)
        W.append(w_l)
        saved.append(sv)
        if ready is not None:
            ready = {**ready, **gates[l + 1]}
    dy, dy_b, lsum = loss_bwd(h, tgt, "loss")
    loss = lax.psum(lsum[0, 0], ("x", "y", "c")) * (0.5 / D_MODEL)

    g_shard, g_small = [None] * depth, [None] * depth
    late = None
    for l in range(depth - 1, -1, -1):
        rider = chip_exchange_rider(late) if late is not None else None
        dy, dy_b, g_shard[l], g_small[l], landed, late = layer_bwd(dy, dy_b, mem0, P[l], W[l], saved[l], l, c_idx, rider)
        if landed is not None:
            g_shard[l + 1].update(reduce_scatter_end(RS_LATE, landed, f"l{l + 1}_b_"))
        saved[l] = None
    g_shard[0].update(reduce_scatter_end(RS_LATE, chip_exchange(late, "l0_rs_chip_exchange"), "l0_b_"))
    grad_x = dy.reshape(x.shape)

    small_names = REPLICATED + GATE_W
    small = jnp.concatenate([g_small[l][n].reshape(-1) for l in range(depth) for n in small_names])
    n_small = small.shape[0]
    rows = -(-n_small // 1024) * 8
    small = jnp.pad(small, (0, rows * 128 - n_small)).reshape(rows, 128)
    small = sum_slots(all_gather([small], "gather_small_grads")[0], "sum_small_grads").reshape(-1)
    grads, off = {}, 0
    per_layer = {n: [] for n in small_names}
    my_cols = (4 * lax.axis_index("x") + 2 * lax.axis_index("y") + lax.axis_index("c")) * (GLA_DK // N_DEV)
    for l in range(depth):
        for n in small_names:
            shp = (GLA_RANK, GLA_DK) if n in GATE_W else w[n].shape[1:]
            g = small[off:off + int(np.prod(shp))].reshape(shp)
            off += int(np.prod(shp))
            per_layer[n].append(lax.dynamic_slice_in_dim(g, my_cols, GLA_DK // N_DEV, axis=1) if n in GATE_W else g)
    for n in small_names:
        grads[n] = jnp.stack(per_layer[n])
    for n in BIG:
        grads[n] = jnp.stack([g_shard[l][n] for l in range(depth)])

    delta, new_m, new_v = {}, {}, {}
    for n in WEIGHTS:
        shp = w[n].shape
        two_d = (shp[0], int(np.prod(shp[1:]))) if n in REPLICATED else (int(np.prod(shp[:-1])), shp[-1])
        d_, m_, v_ = adamw(w[n].reshape(two_d), grads[n].reshape(two_d), mom[n].reshape(two_d), var[n].reshape(two_d),
                           "adamw_" + n)
        delta[n], new_m[n], new_v[n] = d_.reshape(shp), m_.reshape(shp), v_.reshape(shp)

    return (loss, grad_x, *[grads[n] for n in WEIGHTS], *[delta[n] for n in WEIGHTS], *[new_m[n] for n in WEIGHTS],
            *[new_v[n] for n in WEIGHTS])
```
